```python
import math
import jax, jax.numpy as jnp
from jax import lax
import numpy as np

D_MODEL = 1024
BATCH = 8
SEQ = 2048
DEPTH = 1
DEC_BATCH = 16
DEC_SEQ = 32
PAST_LEN = 2048

CHUNK = 64
D_MIX = D_MODEL
ATT_HEAD_DIM = 64
D_ATT = D_MIX // 2
N_ATT_HEADS = D_ATT // ATT_HEAD_DIM
D_SSM = D_MIX - D_ATT
SSM_HEAD_DIM = 64
N_SSM_HEADS = D_SSM // SSM_HEAD_DIM
SSM_GROUPS = 2
SSM_HEADS_PER_GROUP = N_SSM_HEADS // SSM_GROUPS
SSM_STATE = 64
SSM_CONV = 4
SSM_CHUNK = CHUNK
CONV_DIM = D_SSM + 2 * SSM_GROUPS * SSM_STATE
D_FF = 2816
FFN_CONV = 3
Q_BLOCK = 128
D_IN_PROJ = 3 * D_ATT + D_SSM + CONV_DIM + N_SSM_HEADS
NORM_EPS = 1e-6

kernel_name = 'hybrid_stickbreak_ssd_convffn_stream_step'


def rmsnorm(x, g):
    x32 = x.astype(jnp.float32)
    y = x32 * lax.rsqrt(jnp.mean(x32 * x32, axis=-1, keepdims=True) + NORM_EPS)
    return (y * g.astype(jnp.float32)).astype(x.dtype)


def causal_dwconv(x, prev, w, bias):
    K = w.shape[0]
    l = x.shape[1]
    xp = jnp.concatenate([prev.astype(x.dtype), x], axis=1)
    out = bias.astype(x.dtype)
    for i in range(K):
        out = out + xp[:, i:i + l] * w[i].astype(x.dtype)
    return out, xp[:, l:]


def _stick_breaking_block(q, q_pos, k, v, k_pos):
    z = jnp.einsum('bqhd,bkhd->bhqk', q, k).astype(jnp.float32) * (ATT_HEAD_DIM ** -0.5)
    causal = k_pos[None, :] < q_pos[:, None]
    log_keep = jnp.where(causal, jax.nn.log_sigmoid(-z), 0.0)
    log_after = lax.cumsum(log_keep, axis=3, reverse=True) - log_keep
    w = jnp.where(causal, jnp.exp(jax.nn.log_sigmoid(z) + log_after), 0.0)
    return jnp.einsum('bhqk,bkhd->bqhd', w.astype(v.dtype), v)


def stick_breaking(q, q_pos, k, v, k_pos):
    b, l, h, d = q.shape
    if l > Q_BLOCK and l % Q_BLOCK == 0:
        nb = l // Q_BLOCK
        qb = q.reshape(b, nb, Q_BLOCK, h, d).transpose(1, 0, 2, 3, 4)
        pb = q_pos.reshape(nb, Q_BLOCK)
        ob = lax.map(lambda a: _stick_breaking_block(a[0], a[1], k, v, k_pos), (qb, pb))
        return ob.transpose(1, 0, 2, 3, 4).reshape(b, l, h, d)
    return _stick_breaking_block(q, q_pos, k, v, k_pos)


def ssd_scan(x, dt, a, bmat, cmat, h0):
    bsz, l = x.shape[0], x.shape[1]
    q = min(SSM_CHUNK, l)
    c = l // q
    G, R, P, N = SSM_GROUPS, SSM_HEADS_PER_GROUP, SSM_HEAD_DIM, SSM_STATE
    xs = x.reshape(bsz, c, q, G, R, P)
    dts = dt.reshape(bsz, c, q, G, R)
    bs = bmat.reshape(bsz, c, q, G, N)
    cs = cmat.reshape(bsz, c, q, G, N)
    acum = jnp.cumsum(dts * a.reshape(G, R), axis=2)
    seg = acum[:, :, :, None] - acum[:, :, None]
    tri = jnp.tril(jnp.ones((q, q), dtype=bool))[:, :, None, None]
    decay = jnp.exp(jnp.where(tri, seg, -jnp.inf))
    cb = jnp.einsum('bctgn,bcsgn->bctsg', cs, bs)
    y_diag = jnp.einsum('bctsg,bctsgr,bcsgr,bcsgrp->bctgrp', cb, decay, dts, xs)
    to_end = jnp.exp(acum[:, :, -1:] - acum)
    chunk_states = jnp.einsum('bcsgn,bcsgr,bcsgrp->bcgrpn', bs, to_end * dts, xs)
    chunk_decay = jnp.exp(acum[:, :, -1])

    def step(h, inp):
        st, dec = inp
        return dec[..., None, None] * h + st, h

    h_final, h_in = lax.scan(step, h0.reshape(bsz, G, R, P, N),
                             (jnp.moveaxis(chunk_states, 1, 0), jnp.moveaxis(chunk_decay, 1, 0)))
    h_in = jnp.moveaxis(h_in, 0, 1)
    y_off = jnp.einsum('bctgn,bctgr,bcgrpn->bctgrp', cs, jnp.exp(acum), h_in)
    y = (y_diag + y_off).reshape(bsz, l, N_SSM_HEADS, P)
    return y, h_final.reshape(bsz, N_SSM_HEADS, P, N)


def mixer(h, k_past, v_past, ssm_h0, conv_prev, p):
    b, l, _ = h.shape
    proj = h @ p['w_in']
    o = 0
    q = proj[..., o:o + D_ATT].reshape(b, l, N_ATT_HEADS, ATT_HEAD_DIM); o += D_ATT
    k = proj[..., o:o + D_ATT].reshape(b, l, N_ATT_HEADS, ATT_HEAD_DIM); o += D_ATT
    v = proj[..., o:o + D_ATT].reshape(b, l, N_ATT_HEADS, ATT_HEAD_DIM); o += D_ATT
    z = proj[..., o:o + D_SSM]; o += D_SSM
    xbc = proj[..., o:o + CONV_DIM]; o += CONV_DIM
    dt_raw = proj[..., o:o + N_SSM_HEADS]

    past = k_past.shape[1]
    k_all = jnp.concatenate([k_past.astype(k.dtype), k], axis=1)
    v_all = jnp.concatenate([v_past.astype(v.dtype), v], axis=1)
    q_pos = past + jnp.arange(l)
    k_pos = jnp.arange(past + l)
    attn = stick_breaking(q, q_pos, k_all, v_all, k_pos).reshape(b, l, D_ATT)

    xbc_c, conv_new = causal_dwconv(xbc, conv_prev, p['ssm_conv_w'], p['ssm_conv_b'])
    xbc_c = jax.nn.silu(xbc_c)
    gn = SSM_GROUPS * SSM_STATE
    xh = xbc_c[..., :D_SSM].reshape(b, l, N_SSM_HEADS, SSM_HEAD_DIM).astype(jnp.float32)
    bm = xbc_c[..., D_SSM:D_SSM + gn].reshape(b, l, SSM_GROUPS, SSM_STATE).astype(jnp.float32)
    cm = xbc_c[..., D_SSM + gn:].reshape(b, l, SSM_GROUPS, SSM_STATE).astype(jnp.float32)
    dt = jax.nn.softplus(dt_raw.astype(jnp.float32) + p['dt_bias'].astype(jnp.float32))
    a = -jnp.exp(p['a_log'].astype(jnp.float32))
    y, h_fin = ssd_scan(xh, dt, a, bm, cm, ssm_h0.astype(jnp.float32))
    y = y + p['d_skip'].astype(jnp.float32)[:, None] * xh
    y = y.reshape(b, l, D_SSM).astype(h.dtype)
    y = rmsnorm(y * jax.nn.silu(z), p['g_ssm_out'])

    out = jnp.concatenate([rmsnorm(attn, p['g_attn_out']), y], axis=-1) @ p['w_out']
    return out, k, v, h_fin.astype(ssm_h0.dtype), conv_new


def conv_ffn(h, conv_prev, p):
    gu = h @ p['w_up']
    gate, up = gu[..., :D_FF], gu[..., D_FF:]
    gate_c, conv_new = causal_dwconv(gate, conv_prev, p['ffn_conv_w'], p['ffn_conv_b'])
    act = jax.nn.gelu(gate_c, approximate=True) * up
    return act @ p['w_down'], conv_new


def layer(x, k_past, v_past, ssm_h0, ssm_conv_prev, ffn_conv_prev, p):
    m, k, v, h_fin, ssm_conv_new = mixer(rmsnorm(x, p['g_mix_pre']), k_past, v_past, ssm_h0, ssm_conv_prev, p)
    x = x + rmsnorm(m, p['g_mix_post'])
    f, ffn_conv_new = conv_ffn(rmsnorm(x, p['g_ffn_pre']), ffn_conv_prev, p)
    x = x + rmsnorm(f, p['g_ffn_post'])
    return x, k, v, h_fin, ssm_conv_new, ffn_conv_new


def setup_inputs(seed: int = 0) -> dict:
    key = jax.random.key(seed)
    ks = jax.random.split(key, 24)
    f32 = jnp.float32

    def nrm(k, shape, scale):
        return jax.random.normal(k, shape, f32) * scale

    def gain(k, shape):
        return 1.0 + 0.05 * jax.random.normal(k, shape, f32)

    dt0 = jnp.exp(jax.random.uniform(ks[12], (DEPTH, N_SSM_HEADS), f32)
                  * (math.log(0.1) - math.log(0.001)) + math.log(0.001))
    dt_bias = dt0 + jnp.log(-jnp.expm1(-dt0))
    a_log = jnp.log(jax.random.uniform(ks[13], (DEPTH, N_SSM_HEADS), f32, minval=1.0, maxval=16.0))
    return {
        'x_prompt': nrm(ks[0], (BATCH, SEQ, D_MODEL), 1.0),
        'x_sample': nrm(ks[1], (DEC_BATCH, DEC_SEQ, D_MODEL), 1.0),
        'cache_k': nrm(ks[2], (DEPTH, DEC_BATCH, PAST_LEN, N_ATT_HEADS, ATT_HEAD_DIM), 1.0),
        'cache_v': nrm(ks[3], (DEPTH, DEC_BATCH, PAST_LEN, N_ATT_HEADS, ATT_HEAD_DIM), 1.0),
        'state_ssm': nrm(ks[4], (DEPTH, DEC_BATCH, N_SSM_HEADS, SSM_HEAD_DIM, SSM_STATE), 0.1),
        'state_ssm_conv': nrm(ks[5], (DEPTH, DEC_BATCH, SSM_CONV - 1, CONV_DIM), 1.0),
        'state_ffn_conv': nrm(ks[6], (DEPTH, DEC_BATCH, FFN_CONV - 1, D_FF), 1.0),
        'g_mix_pre': gain(ks[7], (DEPTH, D_MODEL)),
        'g_mix_post': gain(ks[8], (DEPTH, D_MODEL)),
        'w_in': nrm(ks[9], (DEPTH, D_MODEL, D_IN_PROJ), D_MODEL ** -0.5),
        'ssm_conv_w': nrm(ks[10], (DEPTH, SSM_CONV, CONV_DIM), SSM_CONV ** -0.5),
        'ssm_conv_b': nrm(ks[11], (DEPTH, CONV_DIM), 0.02),
        'dt_bias': dt_bias,
        'a_log': a_log,
        'd_skip': 1.0 + 0.1 * jax.random.normal(ks[14], (DEPTH, N_SSM_HEADS), f32),
        'g_ssm_out': gain(ks[15], (DEPTH, D_SSM)),
        'g_attn_out': gain(ks[16], (DEPTH, D_ATT)),
        'w_out': nrm(ks[17], (DEPTH, D_MIX, D_MODEL), D_MIX ** -0.5),
        'g_ffn_pre': gain(ks[18], (DEPTH, D_MODEL)),
        'g_ffn_post': gain(ks[19], (DEPTH, D_MODEL)),
        'w_up': nrm(ks[20], (DEPTH, D_MODEL, 2 * D_FF), D_MODEL ** -0.5),
        'ffn_conv_w': nrm(ks[21], (DEPTH, FFN_CONV, D_FF), FFN_CONV ** -0.5),
        'ffn_conv_b': nrm(ks[22], (DEPTH, D_FF), 0.02),
        'w_down': nrm(ks[23], (DEPTH, D_FF, D_MODEL), D_FF ** -0.5),
    }


def reference(x_prompt, x_sample, cache_k, cache_v, state_ssm, state_ssm_conv, state_ffn_conv,
              g_mix_pre, g_mix_post, w_in, ssm_conv_w, ssm_conv_b, dt_bias, a_log, d_skip,
              g_ssm_out, g_attn_out, w_out, g_ffn_pre, g_ffn_post, w_up, ffn_conv_w, ffn_conv_b, w_down):
    bp = x_prompt.shape[0]
    dtp = x_prompt.dtype
    zk = jnp.zeros((bp, 0, N_ATT_HEADS, ATT_HEAD_DIM), dtp)
    zh = jnp.zeros((bp, N_SSM_HEADS, SSM_HEAD_DIM, SSM_STATE), dtp)
    zcs = jnp.zeros((bp, SSM_CONV - 1, CONV_DIM), dtp)
    zcf = jnp.zeros((bp, FFN_CONV - 1, D_FF), dtp)

    y_p, y_s = x_prompt, x_sample
    kp_l, vp_l, hp_l, csp_l, cfp_l = [], [], [], [], []
    ks_l, vs_l, hs_l, css_l, cfs_l = [], [], [], [], []
    for i in range(DEPTH):
        p = {
            'g_mix_pre': g_mix_pre[i], 'g_mix_post': g_mix_post[i], 'w_in': w_in[i],
            'ssm_conv_w': ssm_conv_w[i], 'ssm_conv_b': ssm_conv_b[i], 'dt_bias': dt_bias[i],
            'a_log': a_log[i], 'd_skip': d_skip[i], 'g_ssm_out': g_ssm_out[i],
            'g_attn_out': g_attn_out[i], 'w_out': w_out[i], 'g_ffn_pre': g_ffn_pre[i],
            'g_ffn_post': g_ffn_post[i], 'w_up': w_up[i], 'ffn_conv_w': ffn_conv_w[i],
            'ffn_conv_b': ffn_conv_b[i], 'w_down': w_down[i],
        }
        y_p, kp, vp, hp, csp, cfp = layer(y_p, zk, zk, zh, zcs, zcf, p)
        y_s, kn, vn, hn, csn, cfn = layer(y_s, cache_k[i], cache_v[i], state_ssm[i],
                                          state_ssm_conv[i], state_ffn_conv[i], p)
        kp_l.append(kp); vp_l.append(vp); hp_l.append(hp); csp_l.append(csp); cfp_l.append(cfp)
        ks_l.append(kn); vs_l.append(vn); hs_l.append(hn); css_l.append(csn); cfs_l.append(cfn)

    return (y_p, y_s,
            jnp.stack(kp_l), jnp.stack(vp_l), jnp.stack(hp_l), jnp.stack(csp_l), jnp.stack(cfp_l),
            jnp.stack(ks_l), jnp.stack(vs_l), jnp.stack(hs_l), jnp.stack(css_l), jnp.stack(cfs_l))
```

```python
import functools
import math

import jax
import jax.numpy as jnp
from jax import lax
from jax.experimental import pallas as pl
from jax.experimental.pallas import tpu as pltpu

F32 = jnp.float32
BF16 = jnp.bfloat16

D_MODEL = 1024
D_ATT = 512
N_ATT_HEADS = 8
ATT_HEAD_DIM = 64
D_SSM = 512
N_SSM_HEADS = 8
SSM_HEAD_DIM = 64
SSM_STATE = 64
SSM_GROUPS = 2
SSM_CONV = 4
CONV_DIM = D_SSM + 2 * SSM_GROUPS * SSM_STATE
D_FF = 2816
FFN_CONV = 3
NORM_EPS = 1e-6
D_MAIN_PROJ = 3 * D_ATT + D_SSM + CONV_DIM
ATT_SCALE = ATT_HEAD_DIM ** -0.5

LANES = 128
SUBLANES = 8
HEADS_PER_LANE_TILE = LANES // ATT_HEAD_DIM
N_PAIRS = N_SSM_HEADS // HEADS_PER_LANE_TILE

PROJ_TM = 512
ATT_TQ = 256
ATT_TK = 128
SSD_Q = 128
FFN_TM = 512
FFN_F = 256
FFN_NC = D_FF // FFN_F
VMEM_LIMIT = 56 * 1024 * 1024


def _rmsnorm(x, g):
    y = x * lax.rsqrt(jnp.mean(x * x, axis=-1, keepdims=True) + NORM_EPS)
    return y * g


def _softplus(x):
    return jnp.maximum(x, 0.0) + jnp.log1p(jnp.exp(-jnp.abs(x)))


def _silu(x):
    return x * jax.nn.sigmoid(x)


def _dot(a, b):
    return jnp.dot(a, b, preferred_element_type=F32)


def _dot_nt(a, b):
    return lax.dot_general(a, b, (((1,), (1,)), ((), ())), preferred_element_type=F32)


def _dot_tn(a, b):
    return lax.dot_general(a, b, (((0,), (0,)), ((), ())), preferred_element_type=F32)


def _split2(x):
    hi = x.astype(BF16)
    lo = (x - hi.astype(F32)).astype(BF16)
    return hi, lo


def _split3(x):
    hi = x.astype(BF16)
    r1 = x - hi.astype(F32)
    mid = r1.astype(BF16)
    lo = (r1 - mid.astype(F32)).astype(BF16)
    return hi, mid, lo


def _in_proj_kernel(x_ref, g_ref, w_ref, wdt_ref, dtb_ref,
                    qb_ref, kb_ref, vb_ref, k_ref, v_ref, z_ref, xbc_ref, dt_ref):
    h = _rmsnorm(x_ref[...], g_ref[...]).astype(BF16)

    def proj(lo, hi):
        return _dot(h, w_ref[:, lo:hi])

    qb_ref[...] = (proj(0, D_ATT) * ATT_SCALE).astype(BF16)
    k = proj(D_ATT, 2 * D_ATT)
    k_ref[...] = k
    kb_ref[...] = k.astype(BF16)
    v = proj(2 * D_ATT, 3 * D_ATT)
    v_ref[...] = v
    vb_ref[...] = v.astype(BF16)
    z_ref[...] = proj(3 * D_ATT, 3 * D_ATT + D_SSM)
    xbc_ref[...] = proj(3 * D_ATT + D_SSM, D_MAIN_PROJ)
    dt_ref[...] = _softplus(_dot(h, wdt_ref[...]) + dtb_ref[...])


def _in_proj(x2d, g, w_main, w_dt, dt_bias):
    t = x2d.shape[0]
    tm = min(PROJ_TM, t)
    assert t % tm == 0
    row = lambda n: pl.BlockSpec((tm, n), lambda i: (i, 0))
    full = lambda a: pl.BlockSpec(a.shape, lambda i: (0,) * a.ndim)
    out_shape = (
        jax.ShapeDtypeStruct((t, D_ATT), BF16), jax.ShapeDtypeStruct((t, D_ATT), BF16),
        jax.ShapeDtypeStruct((t, D_ATT), BF16),
        jax.ShapeDtypeStruct((t, D_ATT), F32), jax.ShapeDtypeStruct((t, D_ATT), F32),
        jax.ShapeDtypeStruct((t, D_SSM), F32), jax.ShapeDtypeStruct((t, CONV_DIM), F32),
        jax.ShapeDtypeStruct((t, LANES), F32),
    )
    return pl.pallas_call(
        _in_proj_kernel,
        grid=(t // tm,),
        in_specs=[row(D_MODEL), full(g), full(w_main), full(w_dt), full(dt_bias)],
        out_specs=(row(D_ATT), row(D_ATT), row(D_ATT), row(D_ATT), row(D_ATT),
                   row(D_SSM), row(CONV_DIM), row(LANES)),
        out_shape=out_shape,
        compiler_params=pltpu.CompilerParams(
            dimension_semantics=("arbitrary",), vmem_limit_bytes=VMEM_LIMIT),
        name="in_proj",
    )(x2d, g, w_main, w_dt, dt_bias)


def _sb_weights(s, c, u2, mask):
    lk = -_softplus(s)
    if mask is not None:
        lk = jnp.where(mask, lk, 0.0)
    hi, lo = _split2(lk)
    r = _dot(jnp.concatenate([hi, lo], axis=1), u2) + c
    w = jnp.exp(s + r)
    if mask is not None:
        w = jnp.where(mask, w, 0.0)
    return w, c + jnp.sum(lk, axis=1, keepdims=True)


def _head_lane_masks(n_lanes, dtype):
    lane = lax.broadcasted_iota(jnp.int32, (1, n_lanes), 1)
    return [jnp.where((lane >= h * ATT_HEAD_DIM) & (lane < (h + 1) * ATT_HEAD_DIM), 1.0, 0.0).astype(dtype)
            for h in range(n_lanes // ATT_HEAD_DIM)]


def _attn_prompt_kernel(q_ref, k_ref, v_ref, u2_ref, o_ref, *, tq, tk):
    i = pl.program_id(2)
    nd = tq // tk
    hm = _head_lane_masks(LANES, BF16)
    q = q_ref[0]
    qh = [q * m for m in hm]
    u2 = u2_ref[...]
    row = lax.broadcasted_iota(jnp.int32, (tq, tk), 0)
    col = lax.broadcasted_iota(jnp.int32, (tq, tk), 1)

    def block(j, carry, mask):
        c0, c1, acc = carry
        off = pl.multiple_of(j * tk, tk)
        kb = k_ref[0, pl.ds(off, tk), :]
        vb = v_ref[0, pl.ds(off, tk), :]
        w0, c0 = _sb_weights(_dot_nt(qh[0], kb), c0, u2, mask)
        w1, c1 = _sb_weights(_dot_nt(qh[1], kb), c1, u2, mask)
        wcat = jnp.concatenate([w0.astype(BF16), w1.astype(BF16)], axis=1)
        vcat = jnp.concatenate([vb * hm[0], vb * hm[1]], axis=0)
        return c0, c1, acc + _dot(wcat, vcat)

    carry = (jnp.zeros((tq, 1), F32), jnp.zeros((tq, 1), F32), jnp.zeros((tq, LANES), F32))
    for d in reversed(range(nd)):
        carry = block(i * nd + d, carry, (col + d * tk) < row)
    n_off = i * nd
    carry = lax.fori_loop(0, n_off, lambda t, cr: block(n_off - 1 - t, cr, None), carry)
    o_ref[0] = carry[2]


def _u2_matrix(tk):
    j = jnp.arange(2 * tk)[:, None] % tk
    s = jnp.arange(tk)[None, :]
    return (j >= s).astype(BF16)


def _attn_prompt(qb, kb, vb):
    b, l, _ = qb.shape
    tq, tk = min(ATT_TQ, l), min(ATT_TK, l)
    assert l % tq == 0 and tq % tk == 0
    u2 = _u2_matrix(tk)
    kernel = functools.partial(_attn_prompt_kernel, tq=tq, tk=tk)
    return pl.pallas_call(
        kernel,
        grid=(b, D_ATT // LANES, l // tq),
        in_specs=[pl.BlockSpec((1, tq, LANES), lambda bi, hp, i: (bi, i, hp)),
                  pl.BlockSpec((1, l, LANES), lambda bi, hp, i: (bi, 0, hp)),
                  pl.BlockSpec((1, l, LANES), lambda bi, hp, i: (bi, 0, hp)),
                  pl.BlockSpec(u2.shape, lambda bi, hp, i: (0, 0))],
        out_specs=pl.BlockSpec((1, tq, LANES), lambda bi, hp, i: (bi, i, hp)),
        out_shape=jax.ShapeDtypeStruct((b, l, D_ATT), F32),
        compiler_params=pltpu.CompilerParams(
            dimension_semantics=("arbitrary", "arbitrary", "arbitrary"), vmem_limit_bytes=VMEM_LIMIT),
        name="attn_prompt",
    )(qb, kb, vb, u2)


def _attn_sample_kernel(q_ref, kn_ref, vn_ref, ck_ref, cv_ref, u2_ref, o_ref, *, l, past, tk):
    m = N_ATT_HEADS * l
    hm = _head_lane_masks(D_ATT, BF16)
    q = q_ref[0]
    qs = jnp.concatenate([q * mk for mk in hm], axis=0)
    u2 = u2_ref[...]

    def block(kb, vb, carry, mask):
        c, acc = carry
        w, c = _sb_weights(_dot_nt(qs, kb), c, u2, mask)
        return c, acc + _dot(w.astype(BF16), vb)

    carry = (jnp.zeros((m, 1), F32), jnp.zeros((m, D_ATT), F32))
    pad = jnp.zeros((tk - l, D_ATT), BF16)
    kn = jnp.concatenate([kn_ref[0], pad], axis=0)
    vn = jnp.concatenate([vn_ref[0], pad], axis=0)
    row = jnp.concatenate([lax.broadcasted_iota(jnp.int32, (l, tk), 0)] * N_ATT_HEADS, axis=0)
    col = lax.broadcasted_iota(jnp.int32, (m, tk), 1)
    carry = block(kn, vn, carry, col < row)

    def cache_block(t, cr):
        off = pl.multiple_of(past - tk - t * tk, tk)
        kb = ck_ref[0, pl.ds(off, tk), :].astype(BF16)
        vb = cv_ref[0, pl.ds(off, tk), :].astype(BF16)
        return block(kb, vb, cr, None)

    _, acc = lax.fori_loop(0, past // tk, cache_block, carry)
    hmf = _head_lane_masks(D_ATT, F32)
    out = acc[0:l] * hmf[0]
    for h in range(1, N_ATT_HEADS):
        out = out + acc[h * l:(h + 1) * l] * hmf[h]
    o_ref[0] = out


def _attn_sample(qb, kb_new, vb_new, cache_k, cache_v):
    b, l, _ = qb.shape
    past = cache_k.shape[1]
    tk = ATT_TK
    assert past % tk == 0 and l <= tk and l % 16 == 0
    u2 = _u2_matrix(tk)
    kernel = functools.partial(_attn_sample_kernel, l=l, past=past, tk=tk)
    new = pl.BlockSpec((1, l, D_ATT), lambda bi: (bi, 0, 0))
    old = pl.BlockSpec((1, past, D_ATT), lambda bi: (bi, 0, 0))
    return pl.pallas_call(
        kernel,
        grid=(b,),
        in_specs=[new, new, new, old, old, pl.BlockSpec(u2.shape, lambda bi: (0, 0))],
        out_specs=new,
        out_shape=jax.ShapeDtypeStruct((b, l, D_ATT), F32),
        compiler_params=pltpu.CompilerParams(
            dimension_semantics=("arbitrary",), vmem_limit_bytes=VMEM_LIMIT),
        name="attn_sample",
    )(qb, kb_new, vb_new, cache_k, cache_v, u2)


def _ssd_kernel(xbc_ref, z_ref, dt_ref, conv0_ref, s0_ref, cw_ref, cb_ref, alog_ref, dskip_ref, g_ref,
                tri_ref, y_ref, sfin_ref, cbuf, state, *, q):
    c = pl.program_id(1)

    @pl.when(c == 0)
    def _():
        cbuf[0:SUBLANES, :] = conv0_ref[0]
        state[...] = s0_ref[0]

    cbuf[SUBLANES:SUBLANES + q, :] = xbc_ref[0]
    xc = cb_ref[...]
    for i in range(SSM_CONV):
        xc = xc + cbuf[pl.ds(SUBLANES - (SSM_CONV - 1) + i, q), :] * cw_ref[i:i + 1, :]
    xc = _silu(xc)
    cbuf[0:SUBLANES, :] = cbuf[q:q + SUBLANES, :]

    lane = lax.broadcasted_iota(jnp.int32, (1, LANES), 1)
    lo_half = lane < SSM_STATE
    bmat = xc[:, D_SSM:D_SSM + LANES]
    cmat = xc[:, D_SSM + LANES:D_SSM + 2 * LANES]
    b_sw = pltpu.roll(bmat, SSM_STATE, axis=1)
    c_sw = pltpu.roll(cmat, SSM_STATE, axis=1)
    bdup = [jnp.where(lo_half, bmat, b_sw), jnp.where(lo_half, b_sw, bmat)]
    cdup = [jnp.where(lo_half, cmat, c_sw), jnp.where(lo_half, c_sw, cmat)]
    gmask = [jnp.where(lo_half, 1.0, 0.0), jnp.where(lo_half, 0.0, 1.0)]
    bmat_b = bmat.astype(BF16)
    cb_g = [_dot_nt((cmat * gmask[g]).astype(BF16), bmat_b) for g in range(SSM_GROUPS)]

    dt = dt_ref[0]
    da = dt * (-jnp.exp(alog_ref[...]))
    tri = tri_ref[...]
    hi, mid, lo = _split3(da)
    acum = _dot(tri, hi) + _dot(tri, mid) + _dot(tri, lo)
    acum_t = acum.T
    dt_t = dt.T
    a_end = acum[q - 1:q, :]
    trow = lax.broadcasted_iota(jnp.int32, (q, q), 0)
    tcol = lax.broadcasted_iota(jnp.int32, (q, q), 1)
    causal = tcol <= trow
    bd_r = lax.broadcasted_iota(jnp.int32, (LANES, LANES), 0) < SSM_STATE
    bd_c = lax.broadcasted_iota(jnp.int32, (LANES, LANES), 1) < SSM_STATE
    block_diag = bd_r == bd_c
    hm_b = [jnp.where(lo_half, 1.0, 0.0).astype(BF16), jnp.where(lo_half, 0.0, 1.0).astype(BF16)]

    ys = []
    ssq = jnp.zeros((q, 1), F32)
    for pr in range(N_PAIRS):
        g = pr // (N_PAIRS // SSM_GROUPS)
        x_pair = xc[:, pr * LANES:(pr + 1) * LANES]
        x_b = x_pair.astype(BF16)
        ms = []
        for hh in range(HEADS_PER_LANE_TILE):
            h = pr * HEADS_PER_LANE_TILE + hh
            seg = acum[:, h:h + 1] - acum_t[h:h + 1, :]
            lmat = jnp.where(causal, jnp.exp(seg), 0.0)
            ms.append((cb_g[g] * lmat * dt_t[h:h + 1, :]).astype(BF16))
        h0 = pr * HEADS_PER_LANE_TILE
        acol = jnp.where(lo_half, acum[:, h0:h0 + 1], acum[:, h0 + 1:h0 + 2])
        dcol = jnp.where(lo_half, dt[:, h0:h0 + 1], dt[:, h0 + 1:h0 + 2])
        aend = jnp.where(lo_half, a_end[:, h0:h0 + 1], a_end[:, h0 + 1:h0 + 2])
        s_pair = state[pr]
        y_diag = _dot(jnp.concatenate(ms, axis=1), jnp.concatenate([x_b * hm_b[0], x_b * hm_b[1]], axis=0))
        y_off = _dot((cdup[g] * jnp.exp(acol)).astype(BF16), s_pair.astype(BF16))
        new = _dot_tn((bdup[g] * (jnp.exp(aend - acol) * dcol)).astype(BF16), x_b)
        state[pr] = s_pair * jnp.exp(aend) + jnp.where(block_diag, new, 0.0)
        y = y_diag + y_off + dskip_ref[:, pr * LANES:(pr + 1) * LANES] * x_pair
        yz = y * _silu(z_ref[0, :, pr * LANES:(pr + 1) * LANES])
        ssq = ssq + jnp.sum(yz * yz, axis=1, keepdims=True)
        ys.append(yz)

    inv = lax.rsqrt(ssq * (1.0 / D_SSM) + NORM_EPS)
    for pr in range(N_PAIRS):
        y_ref[0, :, pr * LANES:(pr + 1) * LANES] = (
            ys[pr] * inv * g_ref[:, pr * LANES:(pr + 1) * LANES]).astype(BF16)
    sfin_ref[0] = state[...]


def _ssd(xbc, z, dt, conv0, s0, conv_w, conv_b, a_log_pad, dskip_lanes, g_ssm):
    b, l, _ = xbc.shape
    q = min(SSD_Q, l)
    assert l % q == 0 and q % SUBLANES == 0
    tri = (jnp.arange(q)[None, :] <= jnp.arange(q)[:, None]).astype(BF16)
    seq = lambda n: pl.BlockSpec((1, q, n), lambda bi, ci: (bi, ci, 0))
    per_b = lambda a: pl.BlockSpec((1,) + a.shape[1:], lambda bi, ci: (bi,) + (0,) * (a.ndim - 1))
    full = lambda a: pl.BlockSpec(a.shape, lambda bi, ci: (0,) * a.ndim)
    kernel = functools.partial(_ssd_kernel, q=q)
    return pl.pallas_call(
        kernel,
        grid=(b, l // q),
        in_specs=[seq(CONV_DIM), seq(D_SSM), seq(LANES), per_b(conv0), per_b(s0),
                  full(conv_w), full(conv_b), full(a_log_pad), full(dskip_lanes), full(g_ssm), full(tri)],
        out_specs=(seq(D_SSM), per_b(s0)),
        out_shape=(jax.ShapeDtypeStruct((b, l, D_SSM), BF16), jax.ShapeDtypeStruct(s0.shape, F32)),
        scratch_shapes=[pltpu.VMEM((SUBLANES + q, CONV_DIM), F32),
                        pltpu.VMEM((N_PAIRS, LANES, LANES), F32)],
        compiler_params=pltpu.CompilerParams(
            dimension_semantics=("arbitrary", "arbitrary"), vmem_limit_bytes=VMEM_LIMIT),
        name="ssd",
    )(xbc, z, dt, conv0, s0, conv_w, conv_b, a_log_pad, dskip_lanes, g_ssm, tri)


def _state_to_pairs(st):
    b = st.shape[0]
    s5 = st.reshape(b, N_PAIRS, HEADS_PER_LANE_TILE, SSM_HEAD_DIM, SSM_STATE)
    eye = jnp.eye(HEADS_PER_LANE_TILE, dtype=st.dtype)
    return jnp.einsum('bqipn,ij->bqinjp', s5, eye).reshape(b, N_PAIRS, LANES, LANES)


def _pairs_to_state(sp):
    b = sp.shape[0]
    s6 = sp.reshape(b, N_PAIRS, HEADS_PER_LANE_TILE, SSM_STATE, HEADS_PER_LANE_TILE, SSM_HEAD_DIM)
    diag = jnp.stack([s6[:, :, i, :, i, :] for i in range(HEADS_PER_LANE_TILE)], axis=2)
    return jnp.swapaxes(diag, -1, -2).reshape(b, N_SSM_HEADS, SSM_HEAD_DIM, SSM_STATE)


def _gelu_tanh(x):
    return 0.5 * x * (1.0 + jnp.tanh(math.sqrt(2.0 / math.pi) * (x + 0.044715 * (x * x * x))))


def _out_ffn_kernel(x_ref, attn_ref, ys_ref, fc0_ref, ga_ref, gpost_ref, gpre_ref, gfpost_ref,
                    woa_ref, wos_ref, wg_ref, wu_ref, wd_ref, cw_ref,
                    y_ref, fcn_ref, h2_buf, acc, carry, *, n_seq, lt):
    t = pl.program_id(1)

    @pl.when(t == 0)
    def _():
        carry[...] = fc0_ref[0]

    an = _rmsnorm(attn_ref[...], ga_ref[...]).astype(BF16)
    m = _dot(an, woa_ref[...]) + _dot(ys_ref[...], wos_ref[...])
    x1 = x_ref[...] + _rmsnorm(m, gpost_ref[...])
    h2_buf[...] = _rmsnorm(x1, gpre_ref[...]).astype(BF16)
    acc[...] = jnp.zeros_like(acc)
    row = lax.broadcasted_iota(jnp.int32, (lt, FFN_F), 0)

    def chunk(ci, _):
        h2 = h2_buf[...]
        gate = _dot(h2, wg_ref[ci])
        up = _dot(h2, wu_ref[ci])
        cw = cw_ref[ci]
        acts = []
        for s in range(n_seq):
            gs = gate[s * lt:(s + 1) * lt]
            prev = carry[ci, s * SUBLANES:(s + 1) * SUBLANES, :]
            p1 = prev[SUBLANES - 1:SUBLANES, :]
            p2 = prev[SUBLANES - 2:SUBLANES - 1, :]
            g1 = jnp.where(row == 0, p1, pltpu.roll(gs, 1, axis=0))
            g2 = jnp.where(row == 0, p2, jnp.where(row == 1, p1, pltpu.roll(gs, 2, axis=0)))
            gc = cw[3:4, :] + g2 * cw[0:1, :] + g1 * cw[1:2, :] + gs * cw[2:3, :]
            carry[ci, s * SUBLANES:(s + 1) * SUBLANES, :] = gs[lt - SUBLANES:lt]
            acts.append(_gelu_tanh(gc))
        act = acts[0] if n_seq == 1 else jnp.concatenate(acts, axis=0)
        acc[...] += _dot((act * up).astype(BF16), wd_ref[ci])
        return 0

    lax.fori_loop(0, FFN_NC, chunk, 0)
    y_ref[...] = x1 + _rmsnorm(acc[...], gfpost_ref[...])
    fcn_ref[0] = carry[...]


def _out_ffn(x2d, attn2d, ys2d, fc0, n_seq, lt, gains, weights):
    t = x2d.shape[0]
    tm = n_seq * lt
    n_groups = fc0.shape[0]
    tiles = t // (tm * n_groups)
    assert tiles * tm * n_groups == t
    row = lambda n: pl.BlockSpec((tm, n), lambda gi, ti: (gi * tiles + ti, 0))
    full = lambda a: pl.BlockSpec(a.shape, lambda gi, ti: (0,) * a.ndim, pipeline_mode=pl.Buffered(1))
    fc_spec = pl.BlockSpec((1,) + fc0.shape[1:], lambda gi, ti: (gi, 0, 0, 0))
    kernel = functools.partial(_out_ffn_kernel, n_seq=n_seq, lt=lt)
    return pl.pallas_call(
        kernel,
        grid=(n_groups, tiles),
        in_specs=[row(D_MODEL), row(D_ATT), row(D_SSM), fc_spec] + [full(a) for a in gains]
                 + [full(a) for a in weights],
        out_specs=(row(D_MODEL), fc_spec),
        out_shape=(jax.ShapeDtypeStruct((t, D_MODEL), F32), jax.ShapeDtypeStruct(fc0.shape, F32)),
        scratch_shapes=[pltpu.VMEM((tm, D_MODEL), BF16), pltpu.VMEM((tm, D_MODEL), F32),
                        pltpu.VMEM(fc0.shape[1:], F32)],
        compiler_params=pltpu.CompilerParams(
            dimension_semantics=("arbitrary", "arbitrary"), vmem_limit_bytes=VMEM_LIMIT),
        name="out_ffn",
    )(x2d, attn2d, ys2d, fc0, *gains, *weights)


def _ffn_state_to_chunks(st, n_seq):
    b = st.shape[0]
    s = st.reshape(b // n_seq, n_seq, FFN_CONV - 1, FFN_NC, FFN_F)
    s = jnp.pad(s, ((0, 0), (0, 0), (SUBLANES - (FFN_CONV - 1), 0), (0, 0), (0, 0)))
    return jnp.transpose(s, (0, 3, 1, 2, 4)).reshape(b // n_seq, FFN_NC, n_seq * SUBLANES, FFN_F)


def _chunks_to_ffn_state(ch, n_seq):
    g = ch.shape[0]
    s = ch.reshape(g, FFN_NC, n_seq, SUBLANES, FFN_F)[:, :, :, SUBLANES - (FFN_CONV - 1):, :]
    return jnp.transpose(s, (0, 2, 3, 1, 4)).reshape(g * n_seq, FFN_CONV - 1, D_FF)


def _layer(x, k_past, v_past, ssm_h0, ssm_conv_prev, ffn_conv_prev, p, is_prompt):
    b, l, _ = x.shape
    t = b * l
    x2d = x.reshape(t, D_MODEL)
    qb, kb, vb, k, v, z, xbc, dt = _in_proj(x2d, p['g_mix_pre'], p['w_main'], p['w_dt'], p['dt_bias'])
    r3 = lambda a: a.reshape(b, l, a.shape[-1])

    if is_prompt:
        attn = _attn_prompt(r3(qb), r3(kb), r3(vb))
    else:
        attn = _attn_sample(r3(qb), r3(kb), r3(vb), k_past, v_past)

    conv0 = jnp.pad(ssm_conv_prev, ((0, 0), (SUBLANES - (SSM_CONV - 1), 0), (0, 0)))
    l_ssd = l if l % SSD_Q == 0 else SSD_Q * pl.cdiv(l, SSD_Q)
    pad_t = lambda a: jnp.pad(r3(a), ((0, 0), (0, l_ssd - l), (0, 0)))
    ys, s_fin = _ssd(pad_t(xbc), pad_t(z), pad_t(dt), conv0, _state_to_pairs(ssm_h0),
                     p['ssm_conv_w'], p['ssm_conv_b'], p['a_log'], p['d_skip'], p['g_ssm_out'])
    ys = ys[:, :l, :]
    ssm_conv_new = r3(xbc)[:, l - (SSM_CONV - 1):, :]

    if is_prompt:
        n_seq, lt = 1, min(FFN_TM, l)
    else:
        n_seq, lt = b, l
    fc0 = _ffn_state_to_chunks(ffn_conv_prev, n_seq)
    gains = (p['g_attn_out'], p['g_mix_post'], p['g_ffn_pre'], p['g_ffn_post'])
    weights = (p['w_out_a'], p['w_out_s'], p['w_gate'], p['w_up'], p['w_down'], p['ffn_cw'])
    y2d, fcn = _out_ffn(x2d, attn.reshape(t, D_ATT), ys.reshape(t, D_SSM), fc0, n_seq, lt, gains, weights)
    return (y2d.reshape(b, l, D_MODEL), k.reshape(b, l, N_ATT_HEADS, ATT_HEAD_DIM),
            v.reshape(b, l, N_ATT_HEADS, ATT_HEAD_DIM), _pairs_to_state(s_fin), ssm_conv_new,
            _chunks_to_ffn_state(fcn, n_seq))


def _prep_params(i, g_mix_pre, g_mix_post, w_in, ssm_conv_w, ssm_conv_b, dt_bias, a_log, d_skip,
                 g_ssm_out, g_attn_out, w_out, g_ffn_pre, g_ffn_post, w_up, ffn_conv_w, ffn_conv_b, w_down):
    row = lambda a: a[i].reshape(1, -1).astype(F32)
    pad_lanes = lambda a: jnp.pad(a, ((0, 0), (0, LANES - a.shape[1])))
    wi = w_in[i]
    ffn_cw = jnp.concatenate([ffn_conv_w[i], ffn_conv_b[i][None, :],
                              jnp.zeros((SUBLANES - FFN_CONV - 1, D_FF), F32)], axis=0)
    return {
        'g_mix_pre': row(g_mix_pre), 'g_mix_post': row(g_mix_post),
        'g_ffn_pre': row(g_ffn_pre), 'g_ffn_post': row(g_ffn_post),
        'g_attn_out': row(g_attn_out), 'g_ssm_out': row(g_ssm_out),
        'w_main': wi[:, :D_MAIN_PROJ].astype(BF16),
        'w_dt': pad_lanes(wi[:, D_MAIN_PROJ:]).astype(BF16),
        'dt_bias': pad_lanes(row(dt_bias)),
        'ssm_conv_w': ssm_conv_w[i].astype(F32), 'ssm_conv_b': row(ssm_conv_b),
        'a_log': pad_lanes(row(a_log)),
        'd_skip': jnp.repeat(d_skip[i].astype(F32), SSM_HEAD_DIM).reshape(1, D_SSM),
        'w_out_a': w_out[i][:D_ATT].astype(BF16), 'w_out_s': w_out[i][D_ATT:].astype(BF16),
        'w_gate': jnp.transpose(w_up[i][:, :D_FF].reshape(D_MODEL, FFN_NC, FFN_F), (1, 0, 2)).astype(BF16),
        'w_up': jnp.transpose(w_up[i][:, D_FF:].reshape(D_MODEL, FFN_NC, FFN_F), (1, 0, 2)).astype(BF16),
        'w_down': w_down[i].reshape(FFN_NC, FFN_F, D_MODEL).astype(BF16),
        'ffn_cw': jnp.transpose(ffn_cw.reshape(SUBLANES, FFN_NC, FFN_F), (1, 0, 2)),
    }


def kernel(x_prompt, x_sample, cache_k, cache_v, state_ssm, state_ssm_conv, state_ffn_conv, g_mix_pre, g_mix_post, w_in, ssm_conv_w, ssm_conv_b, dt_bias, a_log, d_skip, g_ssm_out, g_attn_out, w_out, g_ffn_pre, g_ffn_post, w_up, ffn_conv_w, ffn_conv_b, w_down):
    depth = w_in.shape[0]
    bp = x_prompt.shape[0]
    dtp = x_prompt.dtype
    zh = jnp.zeros((bp, N_SSM_HEADS, SSM_HEAD_DIM, SSM_STATE), dtp)
    zcs = jnp.zeros((bp, SSM_CONV - 1, CONV_DIM), dtp)
    zcf = jnp.zeros((bp, FFN_CONV - 1, D_FF), dtp)
    y_p, y_s = x_prompt, x_sample
    outs_p, outs_s = [], []
    for i in range(depth):
        p = _prep_params(i, g_mix_pre, g_mix_post, w_in, ssm_conv_w, ssm_conv_b, dt_bias, a_log, d_skip,
                         g_ssm_out, g_attn_out, w_out, g_ffn_pre, g_ffn_post, w_up, ffn_conv_w, ffn_conv_b,
                         w_down)
        rp = _layer(y_p, None, None, zh, zcs, zcf, p, True)
        b_s, past = cache_k.shape[1], cache_k.shape[2]
        rs = _layer(y_s, cache_k[i].reshape(b_s, past, D_ATT), cache_v[i].reshape(b_s, past, D_ATT),
                    state_ssm[i], state_ssm_conv[i], state_ffn_conv[i], p, False)
        y_p, y_s = rp[0], rs[0]
        outs_p.append(rp[1:])
        outs_s.append(rs[1:])
    stack = lambda outs, j: jnp.stack([o[j] for o in outs])
    return (y_p, y_s) + tuple(stack(outs_p, j) for j in range(5)) + tuple(stack(outs_s, j) for j in range(5))
```

```python
import functools
import math

import jax
import jax.numpy as jnp
from jax import lax
from jax.experimental import pallas as pl
from jax.experimental.pallas import tpu as pltpu

F32 = jnp.float32
BF16 = jnp.bfloat16

D_MODEL = 1024
D_ATT = 512
N_ATT_HEADS = 8
ATT_HEAD_DIM = 64
D_SSM = 512
N_SSM_HEADS = 8
SSM_HEAD_DIM = 64
SSM_STATE = 64
SSM_GROUPS = 2
SSM_CONV = 4
CONV_DIM = D_SSM + 2 * SSM_GROUPS * SSM_STATE
D_FF = 2816
FFN_CONV = 3
NORM_EPS = 1e-6
D_MAIN_PROJ = 3 * D_ATT + D_SSM + CONV_DIM
ATT_SCALE = ATT_HEAD_DIM ** -0.5

LANES = 128
SUBLANES = 8
HEADS_PER_LANE_TILE = LANES // ATT_HEAD_DIM
N_PAIRS = N_SSM_HEADS // HEADS_PER_LANE_TILE

PROJ_TM = 512
ATT_TQ = 256
ATT_TK = 128
SSD_Q = 128
FFN_TM = 512
FFN_F = 256
FFN_NC = D_FF // FFN_F
VMEM_LIMIT = 56 * 1024 * 1024


def _rmsnorm(x, g):
    y = x * lax.rsqrt(jnp.mean(x * x, axis=-1, keepdims=True) + NORM_EPS)
    return y * g


def _softplus(x):
    return jnp.maximum(x, 0.0) + jnp.log1p(jnp.exp(-jnp.abs(x)))


def _silu(x):
    return x * jax.nn.sigmoid(x)


def _dot(a, b):
    return jnp.dot(a, b, preferred_element_type=F32)


def _dot_nt(a, b):
    return lax.dot_general(a, b, (((1,), (1,)), ((), ())), preferred_element_type=F32)


def _dot_tn(a, b):
    return lax.dot_general(a, b, (((0,), (0,)), ((), ())), preferred_element_type=F32)


def _split2(x):
    hi = x.astype(BF16)
    lo = (x - hi.astype(F32)).astype(BF16)
    return hi, lo


def _split3(x):
    hi = x.astype(BF16)
    r1 = x - hi.astype(F32)
    mid = r1.astype(BF16)
    lo = (r1 - mid.astype(F32)).astype(BF16)
    return hi, mid, lo


def _in_proj_kernel(x_ref, g_ref, w_ref, wdt_ref, dtb_ref,
                    qb_ref, kb_ref, vb_ref, k_ref, v_ref, z_ref, xbc_ref, dt_ref):
    h = _rmsnorm(x_ref[...], g_ref[...]).astype(BF16)

    def proj(lo, hi):
        return _dot(h, w_ref[:, lo:hi])

    qb_ref[...] = (proj(0, D_ATT) * ATT_SCALE).astype(BF16)
    k = proj(D_ATT, 2 * D_ATT)
    k_ref[...] = k
    kb_ref[...] = k.astype(BF16)
    v = proj(2 * D_ATT, 3 * D_ATT)
    v_ref[...] = v
    vb_ref[...] = v.astype(BF16)
    z_ref[...] = proj(3 * D_ATT, 3 * D_ATT + D_SSM)
    xbc_ref[...] = proj(3 * D_ATT + D_SSM, D_MAIN_PROJ)
    dt_ref[...] = _softplus(_dot(h, wdt_ref[...]) + dtb_ref[...])


def _in_proj(x2d, g, w_main, w_dt, dt_bias):
    t = x2d.shape[0]
    tm = min(PROJ_TM, t)
    assert t % tm == 0
    row = lambda n: pl.BlockSpec((tm, n), lambda i: (i, 0))
    full = lambda a: pl.BlockSpec(a.shape, lambda i: (0,) * a.ndim)
    out_shape = (
        jax.ShapeDtypeStruct((t, D_ATT), BF16), jax.ShapeDtypeStruct((t, D_ATT), BF16),
        jax.ShapeDtypeStruct((t, D_ATT), BF16),
        jax.ShapeDtypeStruct((t, D_ATT), F32), jax.ShapeDtypeStruct((t, D_ATT), F32),
        jax.ShapeDtypeStruct((t, D_SSM), F32), jax.ShapeDtypeStruct((t, CONV_DIM), F32),
        jax.ShapeDtypeStruct((t, LANES), F32),
    )
    return pl.pallas_call(
        _in_proj_kernel,
        grid=(t // tm,),
        in_specs=[row(D_MODEL), full(g), full(w_main), full(w_dt), full(dt_bias)],
        out_specs=(row(D_ATT), row(D_ATT), row(D_ATT), row(D_ATT), row(D_ATT),
                   row(D_SSM), row(CONV_DIM), row(LANES)),
        out_shape=out_shape,
        compiler_params=pltpu.CompilerParams(
            dimension_semantics=("arbitrary",), vmem_limit_bytes=VMEM_LIMIT),
        name="in_proj",
    )(x2d, g, w_main, w_dt, dt_bias)


def _sb_weights(s, c, u2, mask):
    lk = -(jnp.maximum(s, 0.0) + jnp.log(1.0 + jnp.exp(-jnp.abs(s))))
    if mask is not None:
        lk = jnp.where(mask, lk, 0.0)
    hi, lo = _split2(lk)
    r = _dot(jnp.concatenate([hi, lo], axis=1), u2) + c
    w = jnp.exp(s + r)
    if mask is not None:
        w = jnp.where(mask, w, 0.0)
    return w, c + jnp.sum(lk, axis=1, keepdims=True)


def _head_lane_masks(n_lanes, dtype):
    lane = lax.broadcasted_iota(jnp.int32, (1, n_lanes), 1)
    return [jnp.where((lane >= h * ATT_HEAD_DIM) & (lane < (h + 1) * ATT_HEAD_DIM), 1.0, 0.0).astype(dtype)
            for h in range(n_lanes // ATT_HEAD_DIM)]


def _attn_prompt_kernel(q_ref, k_ref, v_ref, u2x_ref, o_ref, s_buf, hl_buf, r_buf, *, tq, nq):
    tk = tq // 2
    hm = _head_lane_masks(LANES, BF16)
    n_items = nq * (nq + 1) // 2
    big = 1e30
    delta = (lax.broadcasted_iota(jnp.int32, (tq, tq), 1) - lax.broadcasted_iota(jnp.int32, (tq, tq), 0))

    s_buf[...] = jnp.full(s_buf.shape, big, F32)
    hl_buf[...] = jnp.zeros(hl_buf.shape, BF16)
    r_buf[...] = jnp.zeros(r_buf.shape, F32)

    def advance(i, j):
        last = j == 0
        return jnp.where(last, i + 1, i), jnp.where(last, i + 1, j - 1)

    def row_off(idx):
        return pl.multiple_of(jnp.minimum(idx, nq - 1) * tq, tq)

    def stage1(i, j, slot3, slot2):
        q = q_ref[0, pl.ds(row_off(i), tq), :]
        qn = -q
        qs = jnp.concatenate([qn * hm[0], qn * hm[1]], axis=0)
        kb = k_ref[0, pl.ds(row_off(j), tq), :]
        t = _dot_nt(qs, kb)
        mask = delta < jnp.where(i == j, 0, 2 * tq)
        for h in range(HEADS_PER_LANE_TILE):
            th = jnp.where(mask, t[h * tq:(h + 1) * tq], big)
            s_buf[slot3, h * tq:(h + 1) * tq, :] = th
            lk = jnp.minimum(th, 0.0) - jnp.log(1.0 + jnp.exp(-jnp.abs(th)))
            hi, lo = _split2(lk)
            for half in range(2):
                r0 = half * 2 * tq + h * tq
                hl_buf[slot2, r0:r0 + tq, 0:tk] = hi[:, half * tk:(half + 1) * tk]
                hl_buf[slot2, r0:r0 + tq, tk:tq] = lo[:, half * tk:(half + 1) * tk]

    def stage2(slot2):
        r_buf[slot2] = _dot(hl_buf[slot2], u2x_ref[...])

    def stage3(i, j, slot3, slot2, c, acc):
        first = i == j
        c = jnp.where(first, 0.0, c)
        acc = jnp.where(first, 0.0, acc)
        ws = []
        for half in (1, 0):
            r0 = half * 2 * tq
            r = r_buf[slot2, r0:r0 + 2 * tq, 0:tk] + c
            th = s_buf[slot3, :, half * tk:(half + 1) * tk]
            w = jnp.exp(r - th).astype(BF16)
            c = c + r_buf[slot2, r0:r0 + 2 * tq, tk:tq]
            ws += [w[0:tq], w[tq:2 * tq]]
        vsb = v_ref[0, pl.ds(row_off(j), tq), :]
        vs = []
        for half in (1, 0):
            vb = vsb[half * tk:(half + 1) * tk]
            vs += [vb * hm[0], vb * hm[1]]
        acc = acc + _dot(jnp.concatenate(ws, axis=1), jnp.concatenate(vs, axis=0))
        o_ref[0, pl.ds(row_off(i), tq), :] = acc
        return c, acc

    def body(n, carry):
        i1, j1, i2, j2, i3, j3, slot3, c, acc = carry
        slot2 = n & 1
        slot3_s3 = jnp.where(slot3 == 2, 0, slot3 + 1)
        c, acc = stage3(i3, j3, slot3_s3, slot2, c, acc)
        stage2(1 - slot2)
        stage1(i1, j1, slot3, slot2)
        ni, nj = advance(i1, j1)
        return ni, nj, i1, j1, i2, j2, slot3_s3, c, acc

    z = jnp.int32(0)
    init = (z, z, z, z, z, z, z, jnp.zeros((2 * tq, tk), F32), jnp.zeros((tq, LANES), F32))
    lax.fori_loop(0, n_items + 2, body, init)


def _u2_matrix(tk):
    j = jnp.arange(2 * tk)[:, None] % tk
    s = jnp.arange(tk)[None, :]
    return (j >= s).astype(BF16)


def _attn_prompt(qb, kb, vb):
    b, l, _ = qb.shape
    tq = min(ATT_TQ, l)
    assert l % tq == 0 and tq == 2 * ATT_TK
    u2x = jnp.concatenate([_u2_matrix(ATT_TK), jnp.ones((tq, ATT_TK), BF16)], axis=1)
    kernel = functools.partial(_attn_prompt_kernel, tq=tq, nq=l // tq)
    seq = pl.BlockSpec((1, l, LANES), lambda bi, hp: (bi, 0, hp))
    return pl.pallas_call(
        kernel,
        grid=(b, D_ATT // LANES),
        in_specs=[seq, seq, seq, pl.BlockSpec(u2x.shape, lambda bi, hp: (0, 0))],
        out_specs=seq,
        out_shape=jax.ShapeDtypeStruct((b, l, D_ATT), F32),
        scratch_shapes=[pltpu.VMEM((3, 2 * tq, tq), F32), pltpu.VMEM((2, 4 * tq, tq), BF16),
                        pltpu.VMEM((2, 4 * tq, tq), F32)],
        compiler_params=pltpu.CompilerParams(
            dimension_semantics=("arbitrary", "arbitrary"), vmem_limit_bytes=VMEM_LIMIT),
        name="attn_prompt",
    )(qb, kb, vb, u2x)


def _attn_sample_kernel(q_ref, kn_ref, vn_ref, ck_ref, cv_ref, u2_ref, o_ref, *, l, past, tk):
    m = N_ATT_HEADS * l
    hm = _head_lane_masks(D_ATT, BF16)
    q = q_ref[0]
    qs = jnp.concatenate([q * mk for mk in hm], axis=0)
    u2 = u2_ref[...]

    def block(kb, vb, carry, mask):
        c, acc = carry
        w, c = _sb_weights(_dot_nt(qs, kb), c, u2, mask)
        return c, acc + _dot(w.astype(BF16), vb)

    carry = (jnp.zeros((m, 1), F32), jnp.zeros((m, D_ATT), F32))
    pad = jnp.zeros((tk - l, D_ATT), BF16)
    kn = jnp.concatenate([kn_ref[0], pad], axis=0)
    vn = jnp.concatenate([vn_ref[0], pad], axis=0)
    row = jnp.concatenate([lax.broadcasted_iota(jnp.int32, (l, tk), 0)] * N_ATT_HEADS, axis=0)
    col = lax.broadcasted_iota(jnp.int32, (m, tk), 1)
    carry = block(kn, vn, carry, col < row)

    def cache_block(t, cr):
        off = pl.multiple_of(past - tk - t * tk, tk)
        kb = ck_ref[0, pl.ds(off, tk), :].astype(BF16)
        vb = cv_ref[0, pl.ds(off, tk), :].astype(BF16)
        return block(kb, vb, cr, None)

    _, acc = lax.fori_loop(0, past // tk, cache_block, carry)
    hmf = _head_lane_masks(D_ATT, F32)
    out = acc[0:l] * hmf[0]
    for h in range(1, N_ATT_HEADS):
        out = out + acc[h * l:(h + 1) * l] * hmf[h]
    o_ref[0] = out


def _attn_sample(qb, kb_new, vb_new, cache_k, cache_v):
    b, l, _ = qb.shape
    past = cache_k.shape[1]
    tk = ATT_TK
    assert past % tk == 0 and l <= tk and l % 16 == 0
    u2 = _u2_matrix(tk)
    kernel = functools.partial(_attn_sample_kernel, l=l, past=past, tk=tk)
    new = pl.BlockSpec((1, l, D_ATT), lambda bi: (bi, 0, 0))
    old = pl.BlockSpec((1, past, D_ATT), lambda bi: (bi, 0, 0))
    return pl.pallas_call(
        kernel,
        grid=(b,),
        in_specs=[new, new, new, old, old, pl.BlockSpec(u2.shape, lambda bi: (0, 0))],
        out_specs=new,
        out_shape=jax.ShapeDtypeStruct((b, l, D_ATT), F32),
        compiler_params=pltpu.CompilerParams(
            dimension_semantics=("arbitrary",), vmem_limit_bytes=VMEM_LIMIT),
        name="attn_sample",
    )(qb, kb_new, vb_new, cache_k, cache_v, u2)


def _ssd_kernel(xbc_ref, z_ref, dt_ref, conv0_ref, s0_ref, cw_ref, cb_ref, alog_ref, dskip_ref, g_ref,
                tri_ref, y_ref, sfin_ref, cbuf, state, *, q):
    c = pl.program_id(1)

    @pl.when(c == 0)
    def _():
        cbuf[0:SUBLANES, :] = conv0_ref[0]
        state[...] = s0_ref[0]

    cbuf[SUBLANES:SUBLANES + q, :] = xbc_ref[0]
    xc = cb_ref[...]
    for i in range(SSM_CONV):
        xc = xc + cbuf[pl.ds(SUBLANES - (SSM_CONV - 1) + i, q), :] * cw_ref[i:i + 1, :]
    xc = _silu(xc)
    cbuf[0:SUBLANES, :] = cbuf[q:q + SUBLANES, :]

    lane = lax.broadcasted_iota(jnp.int32, (1, LANES), 1)
    lo_half = lane < SSM_STATE
    bmat = xc[:, D_SSM:D_SSM + LANES]
    cmat = xc[:, D_SSM + LANES:D_SSM + 2 * LANES]
    b_sw = pltpu.roll(bmat, SSM_STATE, axis=1)
    c_sw = pltpu.roll(cmat, SSM_STATE, axis=1)
    bdup = [jnp.where(lo_half, bmat, b_sw), jnp.where(lo_half, b_sw, bmat)]
    cdup = [jnp.where(lo_half, cmat, c_sw), jnp.where(lo_half, c_sw, cmat)]
    gmask = [jnp.where(lo_half, 1.0, 0.0), jnp.where(lo_half, 0.0, 1.0)]
    bmat_b = bmat.astype(BF16)
    cb_g = [_dot_nt((cmat * gmask[g]).astype(BF16), bmat_b) for g in range(SSM_GROUPS)]

    dt = dt_ref[0]
    da = dt * (-jnp.exp(alog_ref[...]))
    tri = tri_ref[...]
    hi, mid, lo = _split3(da)
    acum = _dot(tri, hi) + _dot(tri, mid) + _dot(tri, lo)
    acum_t = acum.T
    dt_t = dt.T
    a_end = acum[q - 1:q, :]
    trow = lax.broadcasted_iota(jnp.int32, (q, q), 0)
    tcol = lax.broadcasted_iota(jnp.int32, (q, q), 1)
    causal = tcol <= trow
    bd_r = lax.broadcasted_iota(jnp.int32, (LANES, LANES), 0) < SSM_STATE
    bd_c = lax.broadcasted_iota(jnp.int32, (LANES, LANES), 1) < SSM_STATE
    block_diag = bd_r == bd_c
    hm_b = [jnp.where(lo_half, 1.0, 0.0).astype(BF16), jnp.where(lo_half, 0.0, 1.0).astype(BF16)]

    ys = []
    ssq = jnp.zeros((q, 1), F32)
    for pr in range(N_PAIRS):
        g = pr // (N_PAIRS // SSM_GROUPS)
        x_pair = xc[:, pr * LANES:(pr + 1) * LANES]
        x_b = x_pair.astype(BF16)
        ms = []
        for hh in range(HEADS_PER_LANE_TILE):
            h = pr * HEADS_PER_LANE_TILE + hh
            seg = acum[:, h:h + 1] - acum_t[h:h + 1, :]
            lmat = jnp.where(causal, jnp.exp(seg), 0.0)
            ms.append((cb_g[g] * lmat * dt_t[h:h + 1, :]).astype(BF16))
        h0 = pr * HEADS_PER_LANE_TILE
        acol = jnp.where(lo_half, acum[:, h0:h0 + 1], acum[:, h0 + 1:h0 + 2])
        dcol = jnp.where(lo_half, dt[:, h0:h0 + 1], dt[:, h0 + 1:h0 + 2])
        aend = jnp.where(lo_half, a_end[:, h0:h0 + 1], a_end[:, h0 + 1:h0 + 2])
        s_pair = state[pr]
        y_diag = _dot(jnp.concatenate(ms, axis=1), jnp.concatenate([x_b * hm_b[0], x_b * hm_b[1]], axis=0))
        y_off = _dot((cdup[g] * jnp.exp(acol)).astype(BF16), s_pair.astype(BF16))
        new = _dot_tn((bdup[g] * (jnp.exp(aend - acol) * dcol)).astype(BF16), x_b)
        state[pr] = s_pair * jnp.exp(aend) + jnp.where(block_diag, new, 0.0)
        y = y_diag + y_off + dskip_ref[:, pr * LANES:(pr + 1) * LANES] * x_pair
        yz = y * _silu(z_ref[0, :, pr * LANES:(pr + 1) * LANES])
        ssq = ssq + jnp.sum(yz * yz, axis=1, keepdims=True)
        ys.append(yz)

    inv = lax.rsqrt(ssq * (1.0 / D_SSM) + NORM_EPS)
    for pr in range(N_PAIRS):
        y_ref[0, :, pr * LANES:(pr + 1) * LANES] = (
            ys[pr] * inv * g_ref[:, pr * LANES:(pr + 1) * LANES]).astype(BF16)
    sfin_ref[0] = state[...]


def _ssd(xbc, z, dt, conv0, s0, conv_w, conv_b, a_log_pad, dskip_lanes, g_ssm):
    b, l, _ = xbc.shape
    q = min(SSD_Q, l)
    assert l % q == 0 and q % SUBLANES == 0
    tri = (jnp.arange(q)[None, :] <= jnp.arange(q)[:, None]).astype(BF16)
    seq = lambda n: pl.BlockSpec((1, q, n), lambda bi, ci: (bi, ci, 0))
    per_b = lambda a: pl.BlockSpec((1,) + a.shape[1:], lambda bi, ci: (bi,) + (0,) * (a.ndim - 1))
    full = lambda a: pl.BlockSpec(a.shape, lambda bi, ci: (0,) * a.ndim)
    kernel = functools.partial(_ssd_kernel, q=q)
    return pl.pallas_call(
        kernel,
        grid=(b, l // q),
        in_specs=[seq(CONV_DIM), seq(D_SSM), seq(LANES), per_b(conv0), per_b(s0),
                  full(conv_w), full(conv_b), full(a_log_pad), full(dskip_lanes), full(g_ssm), full(tri)],
        out_specs=(seq(D_SSM), per_b(s0)),
        out_shape=(jax.ShapeDtypeStruct((b, l, D_SSM), BF16), jax.ShapeDtypeStruct(s0.shape, F32)),
        scratch_shapes=[pltpu.VMEM((SUBLANES + q, CONV_DIM), F32),
                        pltpu.VMEM((N_PAIRS, LANES, LANES), F32)],
        compiler_params=pltpu.CompilerParams(
            dimension_semantics=("arbitrary", "arbitrary"), vmem_limit_bytes=VMEM_LIMIT),
        name="ssd",
    )(xbc, z, dt, conv0, s0, conv_w, conv_b, a_log_pad, dskip_lanes, g_ssm, tri)


def _state_to_pairs(st):
    b = st.shape[0]
    s5 = st.reshape(b, N_PAIRS, HEADS_PER_LANE_TILE, SSM_HEAD_DIM, SSM_STATE)
    eye = jnp.eye(HEADS_PER_LANE_TILE, dtype=st.dtype)
    return jnp.einsum('bqipn,ij->bqinjp', s5, eye).reshape(b, N_PAIRS, LANES, LANES)


def _pairs_to_state(sp):
    b = sp.shape[0]
    s6 = sp.reshape(b, N_PAIRS, HEADS_PER_LANE_TILE, SSM_STATE, HEADS_PER_LANE_TILE, SSM_HEAD_DIM)
    diag = jnp.stack([s6[:, :, i, :, i, :] for i in range(HEADS_PER_LANE_TILE)], axis=2)
    return jnp.swapaxes(diag, -1, -2).reshape(b, N_SSM_HEADS, SSM_HEAD_DIM, SSM_STATE)


def _gelu_tanh(x):
    return 0.5 * x * (1.0 + jnp.tanh(math.sqrt(2.0 / math.pi) * (x + 0.044715 * (x * x * x))))


def _out_ffn_kernel(x_ref, attn_ref, ys_ref, fc0_ref, ga_ref, gpost_ref, gpre_ref, gfpost_ref,
                    woa_ref, wos_ref, wg_ref, wu_ref, wd_ref, cw_ref,
                    y_ref, fcn_ref, h2_buf, acc, carry, *, n_seq, lt):
    t = pl.program_id(1)

    @pl.when(t == 0)
    def _():
        carry[...] = fc0_ref[0]

    an = _rmsnorm(attn_ref[...], ga_ref[...]).astype(BF16)
    m = _dot(an, woa_ref[...]) + _dot(ys_ref[...], wos_ref[...])
    x1 = x_ref[...] + _rmsnorm(m, gpost_ref[...])
    h2_buf[...] = _rmsnorm(x1, gpre_ref[...]).astype(BF16)
    acc[...] = jnp.zeros_like(acc)
    row = lax.broadcasted_iota(jnp.int32, (lt, FFN_F), 0)

    def chunk(ci, _):
        h2 = h2_buf[...]
        gate = _dot(h2, wg_ref[ci])
        up = _dot(h2, wu_ref[ci])
        cw = cw_ref[ci]
        acts = []
        for s in range(n_seq):
            gs = gate[s * lt:(s + 1) * lt]
            prev = carry[ci, s * SUBLANES:(s + 1) * SUBLANES, :]
            p1 = prev[SUBLANES - 1:SUBLANES, :]
            p2 = prev[SUBLANES - 2:SUBLANES - 1, :]
            g1 = jnp.where(row == 0, p1, pltpu.roll(gs, 1, axis=0))
            g2 = jnp.where(row == 0, p2, jnp.where(row == 1, p1, pltpu.roll(gs, 2, axis=0)))
            gc = cw[3:4, :] + g2 * cw[0:1, :] + g1 * cw[1:2, :] + gs * cw[2:3, :]
            carry[ci, s * SUBLANES:(s + 1) * SUBLANES, :] = gs[lt - SUBLANES:lt]
            acts.append(_gelu_tanh(gc))
        act = acts[0] if n_seq == 1 else jnp.concatenate(acts, axis=0)
        acc[...] += _dot((act * up).astype(BF16), wd_ref[ci])
        return 0

    lax.fori_loop(0, FFN_NC, chunk, 0)
    y_ref[...] = x1 + _rmsnorm(acc[...], gfpost_ref[...])
    fcn_ref[0] = carry[...]


def _out_ffn(x2d, attn2d, ys2d, fc0, n_seq, lt, gains, weights):
    t = x2d.shape[0]
    tm = n_seq * lt
    n_groups = fc0.shape[0]
    tiles = t // (tm * n_groups)
    assert tiles * tm * n_groups == t
    row = lambda n: pl.BlockSpec((tm, n), lambda gi, ti: (gi * tiles + ti, 0))
    full = lambda a: pl.BlockSpec(a.shape, lambda gi, ti: (0,) * a.ndim, pipeline_mode=pl.Buffered(1))
    fc_spec = pl.BlockSpec((1,) + fc0.shape[1:], lambda gi, ti: (gi, 0, 0, 0))
    kernel = functools.partial(_out_ffn_kernel, n_seq=n_seq, lt=lt)
    return pl.pallas_call(
        kernel,
        grid=(n_groups, tiles),
        in_specs=[row(D_MODEL), row(D_ATT), row(D_SSM), fc_spec] + [full(a) for a in gains]
                 + [full(a) for a in weights],
        out_specs=(row(D_MODEL), fc_spec),
        out_shape=(jax.ShapeDtypeStruct((t, D_MODEL), F32), jax.ShapeDtypeStruct(fc0.shape, F32)),
        scratch_shapes=[pltpu.VMEM((tm, D_MODEL), BF16), pltpu.VMEM((tm, D_MODEL), F32),
                        pltpu.VMEM(fc0.shape[1:], F32)],
        compiler_params=pltpu.CompilerParams(
            dimension_semantics=("arbitrary", "arbitrary"), vmem_limit_bytes=VMEM_LIMIT),
        name="out_ffn",
    )(x2d, attn2d, ys2d, fc0, *gains, *weights)


def _ffn_state_to_chunks(st, n_seq):
    b = st.shape[0]
    s = st.reshape(b // n_seq, n_seq, FFN_CONV - 1, FFN_NC, FFN_F)
    s = jnp.pad(s, ((0, 0), (0, 0), (SUBLANES - (FFN_CONV - 1), 0), (0, 0), (0, 0)))
    return jnp.transpose(s, (0, 3, 1, 2, 4)).reshape(b // n_seq, FFN_NC, n_seq * SUBLANES, FFN_F)


def _chunks_to_ffn_state(ch, n_seq):
    g = ch.shape[0]
    s = ch.reshape(g, FFN_NC, n_seq, SUBLANES, FFN_F)[:, :, :, SUBLANES - (FFN_CONV - 1):, :]
    return jnp.transpose(s, (0, 2, 3, 1, 4)).reshape(g * n_seq, FFN_CONV - 1, D_FF)


def _layer(x, k_past, v_past, ssm_h0, ssm_conv_prev, ffn_conv_prev, p, is_prompt):
    b, l, _ = x.shape
    t = b * l
    x2d = x.reshape(t, D_MODEL)
    qb, kb, vb, k, v, z, xbc, dt = _in_proj(x2d, p['g_mix_pre'], p['w_main'], p['w_dt'], p['dt_bias'])
    r3 = lambda a: a.reshape(b, l, a.shape[-1])

    if is_prompt:
        attn = _attn_prompt(r3(qb), r3(kb), r3(vb))
    else:
        attn = _attn_sample(r3(qb), r3(kb), r3(vb), k_past, v_past)

    conv0 = jnp.pad(ssm_conv_prev, ((0, 0), (SUBLANES - (SSM_CONV - 1), 0), (0, 0)))
    l_ssd = l if l % SSD_Q == 0 else SSD_Q * pl.cdiv(l, SSD_Q)
    pad_t = lambda a: jnp.pad(r3(a), ((0, 0), (0, l_ssd - l), (0, 0)))
    ys, s_fin = _ssd(pad_t(xbc), pad_t(z), pad_t(dt), conv0, _state_to_pairs(ssm_h0),
                     p['ssm_conv_w'], p['ssm_conv_b'], p['a_log'], p['d_skip'], p['g_ssm_out'])
    ys = ys[:, :l, :]
    ssm_conv_new = r3(xbc)[:, l - (SSM_CONV - 1):, :]

    if is_prompt:
        n_seq, lt = 1, min(FFN_TM, l)
    else:
        n_seq, lt = b, l
    fc0 = _ffn_state_to_chunks(ffn_conv_prev, n_seq)
    gains = (p['g_attn_out'], p['g_mix_post'], p['g_ffn_pre'], p['g_ffn_post'])
    weights = (p['w_out_a'], p['w_out_s'], p['w_gate'], p['w_up'], p['w_down'], p['ffn_cw'])
    y2d, fcn = _out_ffn(x2d, attn.reshape(t, D_ATT), ys.reshape(t, D_SSM), fc0, n_seq, lt, gains, weights)
    return (y2d.reshape(b, l, D_MODEL), k.reshape(b, l, N_ATT_HEADS, ATT_HEAD_DIM),
            v.reshape(b, l, N_ATT_HEADS, ATT_HEAD_DIM), _pairs_to_state(s_fin), ssm_conv_new,
            _chunks_to_ffn_state(fcn, n_seq))


def _prep_params(i, g_mix_pre, g_mix_post, w_in, ssm_conv_w, ssm_conv_b, dt_bias, a_log, d_skip,
                 g_ssm_out, g_attn_out, w_out, g_ffn_pre, g_ffn_post, w_up, ffn_conv_w, ffn_conv_b, w_down):
    row = lambda a: a[i].reshape(1, -1).astype(F32)
    pad_lanes = lambda a: jnp.pad(a, ((0, 0), (0, LANES - a.shape[1])))
    wi = w_in[i]
    ffn_cw = jnp.concatenate([ffn_conv_w[i], ffn_conv_b[i][None, :],
                              jnp.zeros((SUBLANES - FFN_CONV - 1, D_FF), F32)], axis=0)
    return {
        'g_mix_pre': row(g_mix_pre), 'g_mix_post': row(g_mix_post),
        'g_ffn_pre': row(g_ffn_pre), 'g_ffn_post': row(g_ffn_post),
        'g_attn_out': row(g_attn_out), 'g_ssm_out': row(g_ssm_out),
        'w_main': wi[:, :D_MAIN_PROJ].astype(BF16),
        'w_dt': pad_lanes(wi[:, D_MAIN_PROJ:]).astype(BF16),
        'dt_bias': pad_lanes(row(dt_bias)),
        'ssm_conv_w': ssm_conv_w[i].astype(F32), 'ssm_conv_b': row(ssm_conv_b),
        'a_log': pad_lanes(row(a_log)),
        'd_skip': jnp.repeat(d_skip[i].astype(F32), SSM_HEAD_DIM).reshape(1, D_SSM),
        'w_out_a': w_out[i][:D_ATT].astype(BF16), 'w_out_s': w_out[i][D_ATT:].astype(BF16),
        'w_gate': jnp.transpose(w_up[i][:, :D_FF].reshape(D_MODEL, FFN_NC, FFN_F), (1, 0, 2)).astype(BF16),
        'w_up': jnp.transpose(w_up[i][:, D_FF:].reshape(D_MODEL, FFN_NC, FFN_F), (1, 0, 2)).astype(BF16),
        'w_down': w_down[i].reshape(FFN_NC, FFN_F, D_MODEL).astype(BF16),
        'ffn_cw': jnp.transpose(ffn_cw.reshape(SUBLANES, FFN_NC, FFN_F), (1, 0, 2)),
    }


def kernel(x_prompt, x_sample, cache_k, cache_v, state_ssm, state_ssm_conv, state_ffn_conv, g_mix_pre, g_mix_post, w_in, ssm_conv_w, ssm_conv_b, dt_bias, a_log, d_skip, g_ssm_out, g_attn_out, w_out, g_ffn_pre, g_ffn_post, w_up, ffn_conv_w, ffn_conv_b, w_down):
    depth = w_in.shape[0]
    bp = x_prompt.shape[0]
    dtp = x_prompt.dtype
    zh = jnp.zeros((bp, N_SSM_HEADS, SSM_HEAD_DIM, SSM_STATE), dtp)
    zcs = jnp.zeros((bp, SSM_CONV - 1, CONV_DIM), dtp)
    zcf = jnp.zeros((bp, FFN_CONV - 1, D_FF), dtp)
    y_p, y_s = x_prompt, x_sample
    outs_p, outs_s = [], []
    for i in range(depth):
        p = _prep_params(i, g_mix_pre, g_mix_post, w_in, ssm_conv_w, ssm_conv_b, dt_bias, a_log, d_skip,
                         g_ssm_out, g_attn_out, w_out, g_ffn_pre, g_ffn_post, w_up, ffn_conv_w, ffn_conv_b,
                         w_down)
        rp = _layer(y_p, None, None, zh, zcs, zcf, p, True)
        b_s, past = cache_k.shape[1], cache_k.shape[2]
        rs = _layer(y_s, cache_k[i].reshape(b_s, past, D_ATT), cache_v[i].reshape(b_s, past, D_ATT),
                    state_ssm[i], state_ssm_conv[i], state_ffn_conv[i], p, False)
        y_p, y_s = rp[0], rs[0]
        outs_p.append(rp[1:])
        outs_s.append(rs[1:])
    stack = lambda outs, j: jnp.stack([o[j] for o in outs])
    return (y_p, y_s) + tuple(stack(outs_p, j) for j in range(5)) + tuple(stack(outs_s, j) for j in range(5))
```

```python
import functools
import math

import jax
import jax.numpy as jnp
from jax import lax
from jax.experimental import pallas as pl
from jax.experimental.pallas import tpu as pltpu

F32 = jnp.float32
BF16 = jnp.bfloat16

D_MODEL = 1024
D_ATT = 512
N_ATT_HEADS = 8
ATT_HEAD_DIM = 64
D_SSM = 512
N_SSM_HEADS = 8
SSM_HEAD_DIM = 64
SSM_STATE = 64
SSM_GROUPS = 2
SSM_CONV = 4
CONV_DIM = D_SSM + 2 * SSM_GROUPS * SSM_STATE
D_FF = 2816
FFN_CONV = 3
NORM_EPS = 1e-6
D_MAIN_PROJ = 3 * D_ATT + D_SSM + CONV_DIM
ATT_SCALE = ATT_HEAD_DIM ** -0.5

LANES = 128
SUBLANES = 8
HEADS_PER_LANE_TILE = LANES // ATT_HEAD_DIM
N_PAIRS = N_SSM_HEADS // HEADS_PER_LANE_TILE

PROJ_TM = 512
ATT_TQ = 256
ATT_TK = 128
ATT_CACHE_CHUNK = 512
SSD_Q = 128
FFN_TM = 512
FFN_F = 256
FFN_NC = D_FF // FFN_F
VMEM_LIMIT = 56 * 1024 * 1024


def _rmsnorm(x, g):
    y = x * lax.rsqrt(jnp.mean(x * x, axis=-1, keepdims=True) + NORM_EPS)
    return y * g


def _softplus(x):
    return jnp.maximum(x, 0.0) + jnp.log1p(jnp.exp(-jnp.abs(x)))


def _silu(x):
    return x * jax.nn.sigmoid(x)


def _dot(a, b):
    return jnp.dot(a, b, preferred_element_type=F32)


def _dot_nt(a, b):
    return lax.dot_general(a, b, (((1,), (1,)), ((), ())), preferred_element_type=F32)


def _dot_tn(a, b):
    return lax.dot_general(a, b, (((0,), (0,)), ((), ())), preferred_element_type=F32)


def _split2(x):
    hi = x.astype(BF16)
    lo = (x - hi.astype(F32)).astype(BF16)
    return hi, lo


def _split3(x):
    hi = x.astype(BF16)
    r1 = x - hi.astype(F32)
    mid = r1.astype(BF16)
    lo = (r1 - mid.astype(F32)).astype(BF16)
    return hi, mid, lo


def _in_proj_kernel(x_ref, g_ref, w_ref, wdt_ref, dtb_ref,
                    qb_ref, kb_ref, vb_ref, k_ref, v_ref, z_ref, xbc_ref, dt_ref):
    h = _rmsnorm(x_ref[...], g_ref[...]).astype(BF16)

    def proj(lo, hi):
        return _dot(h, w_ref[:, lo:hi])

    qb_ref[...] = (proj(0, D_ATT) * ATT_SCALE).astype(BF16)
    k = proj(D_ATT, 2 * D_ATT)
    k_ref[...] = _lanes_to_heads(k)
    kb_ref[...] = k.astype(BF16)
    v = proj(2 * D_ATT, 3 * D_ATT)
    v_ref[...] = _lanes_to_heads(v)
    vb_ref[...] = v.astype(BF16)
    z_ref[...] = proj(3 * D_ATT, 3 * D_ATT + D_SSM)
    xbc_ref[...] = proj(3 * D_ATT + D_SSM, D_MAIN_PROJ)
    dt_ref[...] = _softplus(_dot(h, wdt_ref[...]) + dtb_ref[...])


def _in_proj(x2d, g, w_main, w_dt, dt_bias):
    t = x2d.shape[0]
    tm = min(PROJ_TM, t)
    assert t % tm == 0
    row = lambda n: pl.BlockSpec((tm, n), lambda i: (i, 0))
    full = lambda a: pl.BlockSpec(a.shape, lambda i: (0,) * a.ndim)
    out_shape = (
        jax.ShapeDtypeStruct((t, D_ATT), BF16), jax.ShapeDtypeStruct((t, D_ATT), BF16),
        jax.ShapeDtypeStruct((t, D_ATT), BF16),
        jax.ShapeDtypeStruct((t, N_ATT_HEADS, ATT_HEAD_DIM), F32),
        jax.ShapeDtypeStruct((t, N_ATT_HEADS, ATT_HEAD_DIM), F32),
        jax.ShapeDtypeStruct((t, D_SSM), F32), jax.ShapeDtypeStruct((t, CONV_DIM), F32),
        jax.ShapeDtypeStruct((t, LANES), F32),
    )
    heads = pl.BlockSpec((tm, N_ATT_HEADS, ATT_HEAD_DIM), lambda i: (i, 0, 0))
    return pl.pallas_call(
        _in_proj_kernel,
        grid=(t // tm,),
        in_specs=[row(D_MODEL), full(g), full(w_main), full(w_dt), full(dt_bias)],
        out_specs=(row(D_ATT), row(D_ATT), row(D_ATT), heads, heads,
                   row(D_SSM), row(CONV_DIM), row(LANES)),
        out_shape=out_shape,
        compiler_params=pltpu.CompilerParams(
            dimension_semantics=("arbitrary",), vmem_limit_bytes=VMEM_LIMIT),
        name="in_proj",
    )(x2d, g, w_main, w_dt, dt_bias)


def _sb_weights(s, c, u2, mask):
    lk = -(jnp.maximum(s, 0.0) + jnp.log(1.0 + jnp.exp(-jnp.abs(s))))
    if mask is not None:
        lk = jnp.where(mask, lk, 0.0)
    hi, lo = _split2(lk)
    r = _dot(jnp.concatenate([hi, lo], axis=1), u2) + c
    w = jnp.exp(s + r)
    if mask is not None:
        w = jnp.where(mask, w, 0.0)
    return w, c + jnp.sum(lk, axis=1, keepdims=True)


def _head_lane_masks(n_lanes, dtype):
    lane = lax.broadcasted_iota(jnp.int32, (1, n_lanes), 1)
    return [jnp.where((lane >= h * ATT_HEAD_DIM) & (lane < (h + 1) * ATT_HEAD_DIM), 1.0, 0.0).astype(dtype)
            for h in range(n_lanes // ATT_HEAD_DIM)]


def _attn_prompt_kernel(q_ref, k_ref, v_ref, u2x_ref, o_ref, s_buf, hl_buf, r_buf, *, tq, nq):
    tk = tq // 2
    hm = _head_lane_masks(LANES, BF16)
    n_items = nq * (nq + 1) // 2
    big = 1e30
    delta = (lax.broadcasted_iota(jnp.int32, (tq, tq), 1) - lax.broadcasted_iota(jnp.int32, (tq, tq), 0))

    s_buf[...] = jnp.full(s_buf.shape, big, F32)
    hl_buf[...] = jnp.zeros(hl_buf.shape, BF16)
    r_buf[...] = jnp.zeros(r_buf.shape, F32)

    def advance(i, j):
        last = j == 0
        return jnp.where(last, i + 1, i), jnp.where(last, i + 1, j - 1)

    def row_off(idx):
        return pl.multiple_of(jnp.minimum(idx, nq - 1) * tq, tq)

    def stage1(i, j, slot3, slot2):
        q = q_ref[0, pl.ds(row_off(i), tq), :]
        qn = -q
        qs = jnp.concatenate([qn * hm[0], qn * hm[1]], axis=0)
        kb = k_ref[0, pl.ds(row_off(j), tq), :]
        t = _dot_nt(qs, kb)
        mask = delta < jnp.where(i == j, 0, 2 * tq)
        for h in range(HEADS_PER_LANE_TILE):
            th = jnp.where(mask, t[h * tq:(h + 1) * tq], big)
            s_buf[slot3, h * tq:(h + 1) * tq, :] = th
            lk = jnp.minimum(th, 0.0) - jnp.log(1.0 + jnp.exp(-jnp.abs(th)))
            hi, lo = _split2(lk)
            for half in range(2):
                r0 = half * 2 * tq + h * tq
                hl_buf[slot2, r0:r0 + tq, 0:tk] = hi[:, half * tk:(half + 1) * tk]
                hl_buf[slot2, r0:r0 + tq, tk:tq] = lo[:, half * tk:(half + 1) * tk]

    def stage2(slot2):
        r_buf[slot2] = _dot(hl_buf[slot2], u2x_ref[...])

    def stage3(i, j, slot3, slot2, c, acc):
        first = i == j
        c = jnp.where(first, 0.0, c)
        acc = jnp.where(first, 0.0, acc)
        ws = []
        for half in (1, 0):
            r0 = half * 2 * tq
            r = r_buf[slot2, r0:r0 + 2 * tq, 0:tk] + c
            th = s_buf[slot3, :, half * tk:(half + 1) * tk]
            w = jnp.exp(r - th).astype(BF16)
            c = c + r_buf[slot2, r0:r0 + 2 * tq, tk:tq]
            ws += [w[0:tq], w[tq:2 * tq]]
        vsb = v_ref[0, pl.ds(row_off(j), tq), :]
        vs = []
        for half in (1, 0):
            vb = vsb[half * tk:(half + 1) * tk]
            vs += [vb * hm[0], vb * hm[1]]
        acc = acc + _dot(jnp.concatenate(ws, axis=1), jnp.concatenate(vs, axis=0))
        o_ref[0, pl.ds(row_off(i), tq), :] = acc
        return c, acc

    def body(n, carry):
        i1, j1, i2, j2, i3, j3, slot3, c, acc = carry
        slot2 = n & 1
        slot3_s3 = jnp.where(slot3 == 2, 0, slot3 + 1)
        c, acc = stage3(i3, j3, slot3_s3, slot2, c, acc)
        stage2(1 - slot2)
        stage1(i1, j1, slot3, slot2)
        ni, nj = advance(i1, j1)
        return ni, nj, i1, j1, i2, j2, slot3_s3, c, acc

    z = jnp.int32(0)
    init = (z, z, z, z, z, z, z, jnp.zeros((2 * tq, tk), F32), jnp.zeros((tq, LANES), F32))
    lax.fori_loop(0, n_items + 2, body, init)


def _u2_matrix(tk):
    j = jnp.arange(2 * tk)[:, None] % tk
    s = jnp.arange(tk)[None, :]
    return (j >= s).astype(BF16)


def _attn_prompt(qb, kb, vb):
    b, l, _ = qb.shape
    tq = min(ATT_TQ, l)
    assert l % tq == 0 and tq == 2 * ATT_TK
    u2x = jnp.concatenate([_u2_matrix(ATT_TK), jnp.ones((tq, ATT_TK), BF16)], axis=1)
    kernel = functools.partial(_attn_prompt_kernel, tq=tq, nq=l // tq)
    seq = pl.BlockSpec((1, l, LANES), lambda bi, hp: (bi, 0, hp))
    return pl.pallas_call(
        kernel,
        grid=(b, D_ATT // LANES),
        in_specs=[seq, seq, seq, pl.BlockSpec(u2x.shape, lambda bi, hp: (0, 0))],
        out_specs=seq,
        out_shape=jax.ShapeDtypeStruct((b, l, D_ATT), F32),
        scratch_shapes=[pltpu.VMEM((3, 2 * tq, tq), F32), pltpu.VMEM((2, 4 * tq, tq), BF16),
                        pltpu.VMEM((2, 4 * tq, tq), F32)],
        compiler_params=pltpu.CompilerParams(
            dimension_semantics=("arbitrary", "arbitrary"), vmem_limit_bytes=VMEM_LIMIT),
        name="attn_prompt",
    )(qb, kb, vb, u2x)


def _heads_to_lanes(x):
    t = x.shape[0]
    y = jnp.swapaxes(x.reshape(t // SUBLANES, SUBLANES, N_ATT_HEADS, ATT_HEAD_DIM), 1, 2)
    return jnp.concatenate([y[:, v].reshape(t, ATT_HEAD_DIM) for v in range(N_ATT_HEADS)], axis=1)


def _lanes_to_heads(x):
    t = x.shape[0]
    parts = []
    for g in range(D_ATT // LANES):
        a = x[:, g * LANES:(g + 1) * LANES]
        b = pltpu.roll(a, ATT_HEAD_DIM, axis=1)
        parts += [a.reshape(t // SUBLANES, SUBLANES, LANES), b.reshape(t // SUBLANES, SUBLANES, LANES)]
    y = jnp.swapaxes(jnp.stack(parts, axis=1), 1, 2).reshape(t, N_ATT_HEADS, LANES)
    return y[:, :, 0:ATT_HEAD_DIM]


def _attn_sample_kernel(q_ref, kn_ref, vn_ref, ck_ref, cv_ref, u2_ref, o_ref, c_scr, acc_scr, *, l, tk, nch):
    ch = pl.program_id(1)
    m = N_ATT_HEADS * l
    hm = _head_lane_masks(D_ATT, BF16)
    q = q_ref[0]
    qs = jnp.concatenate([q * mk for mk in hm], axis=0)
    u2 = u2_ref[...]

    def block(kb, vb, carry, mask):
        c, acc = carry
        w, c = _sb_weights(_dot_nt(qs, kb), c, u2, mask)
        return c, acc + _dot(w.astype(BF16), vb)

    @pl.when(ch == 0)
    def _():
        pad = jnp.zeros((tk - l, D_ATT), BF16)
        kn = jnp.concatenate([kn_ref[0], pad], axis=0)
        vn = jnp.concatenate([vn_ref[0], pad], axis=0)
        row = jnp.concatenate([lax.broadcasted_iota(jnp.int32, (l, tk), 0)] * N_ATT_HEADS, axis=0)
        col = lax.broadcasted_iota(jnp.int32, (m, tk), 1)
        c, acc = block(kn, vn, (jnp.zeros((m, 1), F32), jnp.zeros((m, D_ATT), F32)), col < row)
        c_scr[...] = c
        acc_scr[...] = acc

    kc = _heads_to_lanes(ck_ref[0, 0]).astype(BF16)
    vc = _heads_to_lanes(cv_ref[0, 0]).astype(BF16)
    carry = (c_scr[...], acc_scr[...])
    for t in reversed(range(kc.shape[0] // tk)):
        carry = block(kc[t * tk:(t + 1) * tk], vc[t * tk:(t + 1) * tk], carry, None)
    c_scr[...] = carry[0]
    acc_scr[...] = carry[1]

    @pl.when(ch == nch - 1)
    def _():
        acc = acc_scr[...]
        hmf = _head_lane_masks(D_ATT, F32)
        out = acc[0:l] * hmf[0]
        for h in range(1, N_ATT_HEADS):
            out = out + acc[h * l:(h + 1) * l] * hmf[h]
        o_ref[0] = out


def _attn_sample(qb, kb_new, vb_new, cache_k, cache_v, layer):
    b, l, _ = qb.shape
    past = cache_k.shape[2]
    tk = ATT_TK
    chunk = min(ATT_CACHE_CHUNK, past)
    assert past % chunk == 0 and chunk % tk == 0 and l <= tk and l % 16 == 0
    nch = past // chunk
    u2 = _u2_matrix(tk)
    kernel = functools.partial(_attn_sample_kernel, l=l, tk=tk, nch=nch)
    new = pl.BlockSpec((1, l, D_ATT), lambda bi, ch: (bi, 0, 0))
    old = pl.BlockSpec((1, 1, chunk, N_ATT_HEADS, ATT_HEAD_DIM), lambda bi, ch: (layer, bi, nch - 1 - ch, 0, 0))
    return pl.pallas_call(
        kernel,
        grid=(b, nch),
        in_specs=[new, new, new, old, old, pl.BlockSpec(u2.shape, lambda bi, ch: (0, 0))],
        out_specs=new,
        out_shape=jax.ShapeDtypeStruct((b, l, D_ATT), F32),
        scratch_shapes=[pltpu.VMEM((N_ATT_HEADS * l, 1), F32), pltpu.VMEM((N_ATT_HEADS * l, D_ATT), F32)],
        compiler_params=pltpu.CompilerParams(
            dimension_semantics=("arbitrary", "arbitrary"), vmem_limit_bytes=VMEM_LIMIT),
        name="attn_sample",
    )(qb, kb_new, vb_new, cache_k, cache_v, u2)


def _ssd_kernel(xbc_ref, z_ref, dt_ref, conv0_ref, s0_ref, cw_ref, cb_ref, alog_ref, dskip_ref, g_ref,
                tri_ref, y_ref, sfin_ref, cbuf, state, *, q):
    c = pl.program_id(1)

    @pl.when(c == 0)
    def _():
        cbuf[0:SUBLANES, :] = conv0_ref[0]
        state[...] = s0_ref[0]

    cbuf[SUBLANES:SUBLANES + q, :] = xbc_ref[0]
    xc = cb_ref[...]
    for i in range(SSM_CONV):
        xc = xc + cbuf[pl.ds(SUBLANES - (SSM_CONV - 1) + i, q), :] * cw_ref[i:i + 1, :]
    xc = _silu(xc)
    cbuf[0:SUBLANES, :] = cbuf[q:q + SUBLANES, :]

    lane = lax.broadcasted_iota(jnp.int32, (1, LANES), 1)
    lo_half = lane < SSM_STATE
    bmat = xc[:, D_SSM:D_SSM + LANES]
    cmat = xc[:, D_SSM + LANES:D_SSM + 2 * LANES]
    b_sw = pltpu.roll(bmat, SSM_STATE, axis=1)
    c_sw = pltpu.roll(cmat, SSM_STATE, axis=1)
    bdup = [jnp.where(lo_half, bmat, b_sw), jnp.where(lo_half, b_sw, bmat)]
    cdup = [jnp.where(lo_half, cmat, c_sw), jnp.where(lo_half, c_sw, cmat)]
    gmask = [jnp.where(lo_half, 1.0, 0.0), jnp.where(lo_half, 0.0, 1.0)]
    bmat_b = bmat.astype(BF16)
    cb_g = [_dot_nt((cmat * gmask[g]).astype(BF16), bmat_b) for g in range(SSM_GROUPS)]

    dt = dt_ref[0]
    da = dt * (-jnp.exp(alog_ref[...]))
    tri = tri_ref[...]
    hi, mid, lo = _split3(da)
    acum = _dot(tri, hi) + _dot(tri, mid) + _dot(tri, lo)
    acum_t = acum.T
    dt_t = dt.T
    a_end = acum[q - 1:q, :]
    trow = lax.broadcasted_iota(jnp.int32, (q, q), 0)
    tcol = lax.broadcasted_iota(jnp.int32, (q, q), 1)
    causal = tcol <= trow
    bd_r = lax.broadcasted_iota(jnp.int32, (LANES, LANES), 0) < SSM_STATE
    bd_c = lax.broadcasted_iota(jnp.int32, (LANES, LANES), 1) < SSM_STATE
    block_diag = bd_r == bd_c
    hm_b = [jnp.where(lo_half, 1.0, 0.0).astype(BF16), jnp.where(lo_half, 0.0, 1.0).astype(BF16)]

    ys = []
    ssq = jnp.zeros((q, 1), F32)
    for pr in range(N_PAIRS):
        g = pr // (N_PAIRS // SSM_GROUPS)
        x_pair = xc[:, pr * LANES:(pr + 1) * LANES]
        x_b = x_pair.astype(BF16)
        ms = []
        for hh in range(HEADS_PER_LANE_TILE):
            h = pr * HEADS_PER_LANE_TILE + hh
            seg = acum[:, h:h + 1] - acum_t[h:h + 1, :]
            lmat = jnp.where(causal, jnp.exp(seg), 0.0)
            ms.append((cb_g[g] * lmat * dt_t[h:h + 1, :]).astype(BF16))
        h0 = pr * HEADS_PER_LANE_TILE
        acol = jnp.where(lo_half, acum[:, h0:h0 + 1], acum[:, h0 + 1:h0 + 2])
        dcol = jnp.where(lo_half, dt[:, h0:h0 + 1], dt[:, h0 + 1:h0 + 2])
        aend = jnp.where(lo_half, a_end[:, h0:h0 + 1], a_end[:, h0 + 1:h0 + 2])
        s_pair = state[pr]
        y_diag = _dot(jnp.concatenate(ms, axis=1), jnp.concatenate([x_b * hm_b[0], x_b * hm_b[1]], axis=0))
        y_off = _dot((cdup[g] * jnp.exp(acol)).astype(BF16), s_pair.astype(BF16))
        new = _dot_tn((bdup[g] * (jnp.exp(aend - acol) * dcol)).astype(BF16), x_b)
        state[pr] = s_pair * jnp.exp(aend) + jnp.where(block_diag, new, 0.0)
        y = y_diag + y_off + dskip_ref[:, pr * LANES:(pr + 1) * LANES] * x_pair
        yz = y * _silu(z_ref[0, :, pr * LANES:(pr + 1) * LANES])
        ssq = ssq + jnp.sum(yz * yz, axis=1, keepdims=True)
        ys.append(yz)

    inv = lax.rsqrt(ssq * (1.0 / D_SSM) + NORM_EPS)
    for pr in range(N_PAIRS):
        y_ref[0, :, pr * LANES:(pr + 1) * LANES] = (
            ys[pr] * inv * g_ref[:, pr * LANES:(pr + 1) * LANES]).astype(BF16)
    sfin_ref[0] = state[...]


def _ssd(xbc, z, dt, conv0, s0, conv_w, conv_b, a_log_pad, dskip_lanes, g_ssm):
    b, l, _ = xbc.shape
    q = min(SSD_Q, l)
    assert l % q == 0 and q % SUBLANES == 0
    tri = (jnp.arange(q)[None, :] <= jnp.arange(q)[:, None]).astype(BF16)
    seq = lambda n: pl.BlockSpec((1, q, n), lambda bi, ci: (bi, ci, 0))
    per_b = lambda a: pl.BlockSpec((1,) + a.shape[1:], lambda bi, ci: (bi,) + (0,) * (a.ndim - 1))
    full = lambda a: pl.BlockSpec(a.shape, lambda bi, ci: (0,) * a.ndim)
    kernel = functools.partial(_ssd_kernel, q=q)
    return pl.pallas_call(
        kernel,
        grid=(b, l // q),
        in_specs=[seq(CONV_DIM), seq(D_SSM), seq(LANES), per_b(conv0), per_b(s0),
                  full(conv_w), full(conv_b), full(a_log_pad), full(dskip_lanes), full(g_ssm), full(tri)],
        out_specs=(seq(D_SSM), per_b(s0)),
        out_shape=(jax.ShapeDtypeStruct((b, l, D_SSM), BF16), jax.ShapeDtypeStruct(s0.shape, F32)),
        scratch_shapes=[pltpu.VMEM((SUBLANES + q, CONV_DIM), F32),
                        pltpu.VMEM((N_PAIRS, LANES, LANES), F32)],
        compiler_params=pltpu.CompilerParams(
            dimension_semantics=("arbitrary", "arbitrary"), vmem_limit_bytes=VMEM_LIMIT),
        name="ssd",
    )(xbc, z, dt, conv0, s0, conv_w, conv_b, a_log_pad, dskip_lanes, g_ssm, tri)


def _state_to_pairs(st):
    b = st.shape[0]
    s5 = st.reshape(b, N_PAIRS, HEADS_PER_LANE_TILE, SSM_HEAD_DIM, SSM_STATE)
    eye = jnp.eye(HEADS_PER_LANE_TILE, dtype=st.dtype)
    return jnp.einsum('bqipn,ij->bqinjp', s5, eye).reshape(b, N_PAIRS, LANES, LANES)


def _pairs_to_state(sp):
    b = sp.shape[0]
    s6 = sp.reshape(b, N_PAIRS, HEADS_PER_LANE_TILE, SSM_STATE, HEADS_PER_LANE_TILE, SSM_HEAD_DIM)
    diag = jnp.stack([s6[:, :, i, :, i, :] for i in range(HEADS_PER_LANE_TILE)], axis=2)
    return jnp.swapaxes(diag, -1, -2).reshape(b, N_SSM_HEADS, SSM_HEAD_DIM, SSM_STATE)


def _gelu_tanh(x):
    return 0.5 * x * (1.0 + jnp.tanh(math.sqrt(2.0 / math.pi) * (x + 0.044715 * (x * x * x))))


def _out_ffn_kernel(x_ref, attn_ref, ys_ref, fc0_ref, ga_ref, gpost_ref, gpre_ref, gfpost_ref,
                    woa_ref, wos_ref, wg_ref, wu_ref, wd_ref, cw_ref,
                    y_ref, fcn_ref, h2_buf, acc, carry, *, n_seq, lt):
    t = pl.program_id(1)

    @pl.when(t == 0)
    def _():
        carry[...] = fc0_ref[0]

    an = _rmsnorm(attn_ref[...], ga_ref[...]).astype(BF16)
    m = _dot(an, woa_ref[...]) + _dot(ys_ref[...], wos_ref[...])
    x1 = x_ref[...] + _rmsnorm(m, gpost_ref[...])
    h2_buf[...] = _rmsnorm(x1, gpre_ref[...]).astype(BF16)
    acc[...] = jnp.zeros_like(acc)
    row = lax.broadcasted_iota(jnp.int32, (lt, FFN_F), 0)

    def chunk(ci, _):
        h2 = h2_buf[...]
        gate = _dot(h2, wg_ref[ci])
        up = _dot(h2, wu_ref[ci])
        cw = cw_ref[ci]
        acts = []
        for s in range(n_seq):
            gs = gate[s * lt:(s + 1) * lt]
            prev = carry[ci, s * SUBLANES:(s + 1) * SUBLANES, :]
            p1 = prev[SUBLANES - 1:SUBLANES, :]
            p2 = prev[SUBLANES - 2:SUBLANES - 1, :]
            g1 = jnp.where(row == 0, p1, pltpu.roll(gs, 1, axis=0))
            g2 = jnp.where(row == 0, p2, jnp.where(row == 1, p1, pltpu.roll(gs, 2, axis=0)))
            gc = cw[3:4, :] + g2 * cw[0:1, :] + g1 * cw[1:2, :] + gs * cw[2:3, :]
            carry[ci, s * SUBLANES:(s + 1) * SUBLANES, :] = gs[lt - SUBLANES:lt]
            acts.append(_gelu_tanh(gc))
        act = acts[0] if n_seq == 1 else jnp.concatenate(acts, axis=0)
        acc[...] += _dot((act * up).astype(BF16), wd_ref[ci])
        return 0

    lax.fori_loop(0, FFN_NC, chunk, 0)
    y_ref[...] = x1 + _rmsnorm(acc[...], gfpost_ref[...])
    fcn_ref[0] = carry[...]


def _out_ffn(x2d, attn2d, ys2d, fc0, n_seq, lt, gains, weights):
    t = x2d.shape[0]
    tm = n_seq * lt
    n_groups = fc0.shape[0]
    tiles = t // (tm * n_groups)
    assert tiles * tm * n_groups == t
    row = lambda n: pl.BlockSpec((tm, n), lambda gi, ti: (gi * tiles + ti, 0))
    full = lambda a: pl.BlockSpec(a.shape, lambda gi, ti: (0,) * a.ndim, pipeline_mode=pl.Buffered(1))
    fc_spec = pl.BlockSpec((1,) + fc0.shape[1:], lambda gi, ti: (gi, 0, 0, 0))
    kernel = functools.partial(_out_ffn_kernel, n_seq=n_seq, lt=lt)
    return pl.pallas_call(
        kernel,
        grid=(n_groups, tiles),
        in_specs=[row(D_MODEL), row(D_ATT), row(D_SSM), fc_spec] + [full(a) for a in gains]
                 + [full(a) for a in weights],
        out_specs=(row(D_MODEL), fc_spec),
        out_shape=(jax.ShapeDtypeStruct((t, D_MODEL), F32), jax.ShapeDtypeStruct(fc0.shape, F32)),
        scratch_shapes=[pltpu.VMEM((tm, D_MODEL), BF16), pltpu.VMEM((tm, D_MODEL), F32),
                        pltpu.VMEM(fc0.shape[1:], F32)],
        compiler_params=pltpu.CompilerParams(
            dimension_semantics=("arbitrary", "arbitrary"), vmem_limit_bytes=VMEM_LIMIT),
        name="out_ffn",
    )(x2d, attn2d, ys2d, fc0, *gains, *weights)


def _ffn_state_to_chunks(st, n_seq):
    b = st.shape[0]
    s = st.reshape(b // n_seq, n_seq, FFN_CONV - 1, FFN_NC, FFN_F)
    s = jnp.pad(s, ((0, 0), (0, 0), (SUBLANES - (FFN_CONV - 1), 0), (0, 0), (0, 0)))
    return jnp.transpose(s, (0, 3, 1, 2, 4)).reshape(b // n_seq, FFN_NC, n_seq * SUBLANES, FFN_F)


def _chunks_to_ffn_state(ch, n_seq):
    g = ch.shape[0]
    s = ch.reshape(g, FFN_NC, n_seq, SUBLANES, FFN_F)[:, :, :, SUBLANES - (FFN_CONV - 1):, :]
    return jnp.transpose(s, (0, 2, 3, 1, 4)).reshape(g * n_seq, FFN_CONV - 1, D_FF)


def _layer(x, caches, ssm_h0, ssm_conv_prev, ffn_conv_prev, p, is_prompt):
    b, l, _ = x.shape
    t = b * l
    x2d = x.reshape(t, D_MODEL)
    qb, kb, vb, k, v, z, xbc, dt = _in_proj(x2d, p['g_mix_pre'], p['w_main'], p['w_dt'], p['dt_bias'])
    r3 = lambda a: a.reshape(b, l, a.shape[-1])

    if is_prompt:
        attn = _attn_prompt(r3(qb), r3(kb), r3(vb))
    else:
        attn = _attn_sample(r3(qb), r3(kb), r3(vb), *caches)

    conv0 = jnp.pad(ssm_conv_prev, ((0, 0), (SUBLANES - (SSM_CONV - 1), 0), (0, 0)))
    l_ssd = l if l % SSD_Q == 0 else SSD_Q * pl.cdiv(l, SSD_Q)
    pad_t = lambda a: jnp.pad(r3(a), ((0, 0), (0, l_ssd - l), (0, 0)))
    ys, s_fin = _ssd(pad_t(xbc), pad_t(z), pad_t(dt), conv0, _state_to_pairs(ssm_h0),
                     p['ssm_conv_w'], p['ssm_conv_b'], p['a_log'], p['d_skip'], p['g_ssm_out'])
    ys = ys[:, :l, :]
    ssm_conv_new = r3(xbc)[:, l - (SSM_CONV - 1):, :]

    if is_prompt:
        n_seq, lt = 1, min(FFN_TM, l)
    else:
        n_seq, lt = b, l
    fc0 = _ffn_state_to_chunks(ffn_conv_prev, n_seq)
    gains = (p['g_attn_out'], p['g_mix_post'], p['g_ffn_pre'], p['g_ffn_post'])
    weights = (p['w_out_a'], p['w_out_s'], p['w_gate'], p['w_up'], p['w_down'], p['ffn_cw'])
    y2d, fcn = _out_ffn(x2d, attn.reshape(t, D_ATT), ys.reshape(t, D_SSM), fc0, n_seq, lt, gains, weights)
    return (y2d.reshape(b, l, D_MODEL), k.reshape(b, l, N_ATT_HEADS, ATT_HEAD_DIM),
            v.reshape(b, l, N_ATT_HEADS, ATT_HEAD_DIM), _pairs_to_state(s_fin), ssm_conv_new,
            _chunks_to_ffn_state(fcn, n_seq))


def _prep_params(i, g_mix_pre, g_mix_post, w_in, ssm_conv_w, ssm_conv_b, dt_bias, a_log, d_skip,
                 g_ssm_out, g_attn_out, w_out, g_ffn_pre, g_ffn_post, w_up, ffn_conv_w, ffn_conv_b, w_down):
    row = lambda a: a[i].reshape(1, -1).astype(F32)
    pad_lanes = lambda a: jnp.pad(a, ((0, 0), (0, LANES - a.shape[1])))
    wi = w_in[i]
    ffn_cw = jnp.concatenate([ffn_conv_w[i], ffn_conv_b[i][None, :],
                              jnp.zeros((SUBLANES - FFN_CONV - 1, D_FF), F32)], axis=0)
    return {
        'g_mix_pre': row(g_mix_pre), 'g_mix_post': row(g_mix_post),
        'g_ffn_pre': row(g_ffn_pre), 'g_ffn_post': row(g_ffn_post),
        'g_attn_out': row(g_attn_out), 'g_ssm_out': row(g_ssm_out),
        'w_main': wi[:, :D_MAIN_PROJ].astype(BF16),
        'w_dt': pad_lanes(wi[:, D_MAIN_PROJ:]).astype(BF16),
        'dt_bias': pad_lanes(row(dt_bias)),
        'ssm_conv_w': ssm_conv_w[i].astype(F32), 'ssm_conv_b': row(ssm_conv_b),
        'a_log': pad_lanes(row(a_log)),
        'd_skip': jnp.repeat(d_skip[i].astype(F32), SSM_HEAD_DIM).reshape(1, D_SSM),
        'w_out_a': w_out[i][:D_ATT].astype(BF16), 'w_out_s': w_out[i][D_ATT:].astype(BF16),
        'w_gate': jnp.transpose(w_up[i][:, :D_FF].reshape(D_MODEL, FFN_NC, FFN_F), (1, 0, 2)).astype(BF16),
        'w_up': jnp.transpose(w_up[i][:, D_FF:].reshape(D_MODEL, FFN_NC, FFN_F), (1, 0, 2)).astype(BF16),
        'w_down': w_down[i].reshape(FFN_NC, FFN_F, D_MODEL).astype(BF16),
        'ffn_cw': jnp.transpose(ffn_cw.reshape(SUBLANES, FFN_NC, FFN_F), (1, 0, 2)),
    }


def kernel(x_prompt, x_sample, cache_k, cache_v, state_ssm, state_ssm_conv, state_ffn_conv, g_mix_pre, g_mix_post, w_in, ssm_conv_w, ssm_conv_b, dt_bias, a_log, d_skip, g_ssm_out, g_attn_out, w_out, g_ffn_pre, g_ffn_post, w_up, ffn_conv_w, ffn_conv_b, w_down):
    depth = w_in.shape[0]
    bp = x_prompt.shape[0]
    dtp = x_prompt.dtype
    zh = jnp.zeros((bp, N_SSM_HEADS, SSM_HEAD_DIM, SSM_STATE), dtp)
    zcs = jnp.zeros((bp, SSM_CONV - 1, CONV_DIM), dtp)
    zcf = jnp.zeros((bp, FFN_CONV - 1, D_FF), dtp)
    y_p, y_s = x_prompt, x_sample
    outs_p, outs_s = [], []
    for i in range(depth):
        p = _prep_params(i, g_mix_pre, g_mix_post, w_in, ssm_conv_w, ssm_conv_b, dt_bias, a_log, d_skip,
                         g_ssm_out, g_attn_out, w_out, g_ffn_pre, g_ffn_post, w_up, ffn_conv_w, ffn_conv_b,
                         w_down)
        rp = _layer(y_p, None, zh, zcs, zcf, p, True)
        rs = _layer(y_s, (cache_k, cache_v, i), state_ssm[i], state_ssm_conv[i], state_ffn_conv[i], p, False)
        y_p, y_s = rp[0], rs[0]
        outs_p.append(rp[1:])
        outs_s.append(rs[1:])
    stack = lambda outs, j: jnp.stack([o[j] for o in outs])
    return (y_p, y_s) + tuple(stack(outs_p, j) for j in range(5)) + tuple(stack(outs_s, j) for j in range(5))
```

```python
import functools
import math

import jax
import jax.numpy as jnp
from jax import lax
from jax.experimental import pallas as pl
from jax.experimental.pallas import tpu as pltpu

F32 = jnp.float32
BF16 = jnp.bfloat16

D_MODEL = 1024
D_ATT = 512
N_ATT_HEADS = 8
ATT_HEAD_DIM = 64
D_SSM = 512
N_SSM_HEADS = 8
SSM_HEAD_DIM = 64
SSM_STATE = 64
SSM_GROUPS = 2
SSM_CONV = 4
CONV_DIM = D_SSM + 2 * SSM_GROUPS * SSM_STATE
D_FF = 2816
FFN_CONV = 3
NORM_EPS = 1e-6
D_MAIN_PROJ = 3 * D_ATT + D_SSM + CONV_DIM
ATT_SCALE = ATT_HEAD_DIM ** -0.5

LANES = 128
SUBLANES = 8
HEADS_PER_LANE_TILE = LANES // ATT_HEAD_DIM
N_PAIRS = N_SSM_HEADS // HEADS_PER_LANE_TILE

PROJ_TM = 512
ATT_TQ = 256
ATT_TK = 128
ATT_CACHE_CHUNK = 512
SSD_Q = 128
FFN_TM = 512
FFN_F = 256
FFN_NC = D_FF // FFN_F
VMEM_LIMIT = 56 * 1024 * 1024


def _rmsnorm(x, g):
    y = x * lax.rsqrt(jnp.mean(x * x, axis=-1, keepdims=True) + NORM_EPS)
    return y * g


def _softplus(x):
    return jnp.maximum(x, 0.0) + jnp.log1p(jnp.exp(-jnp.abs(x)))


def _silu(x):
    return x * jax.nn.sigmoid(x)


def _dot(a, b):
    return jnp.dot(a, b, preferred_element_type=F32)


def _dot_nt(a, b):
    return lax.dot_general(a, b, (((1,), (1,)), ((), ())), preferred_element_type=F32)


def _dot_tn(a, b):
    return lax.dot_general(a, b, (((0,), (0,)), ((), ())), preferred_element_type=F32)


def _split2(x):
    hi = x.astype(BF16)
    lo = (x - hi.astype(F32)).astype(BF16)
    return hi, lo


def _split3(x):
    hi = x.astype(BF16)
    r1 = x - hi.astype(F32)
    mid = r1.astype(BF16)
    lo = (r1 - mid.astype(F32)).astype(BF16)
    return hi, mid, lo


def _in_proj_kernel(x_ref, g_ref, w_ref, wkt_ref, wdt_ref, dtb_ref,
                    qb_ref, kb_ref, vb_ref, k_ref, v_ref, z_ref, xbc_ref, dt_ref, *, kv_transposed):
    h = _rmsnorm(x_ref[0], g_ref[...]).astype(BF16)

    def proj(lo, hi):
        return _dot(h, w_ref[:, lo:hi])

    qb_ref[0] = (proj(0, D_ATT) * ATT_SCALE).astype(BF16)
    v = proj(2 * D_ATT, 3 * D_ATT)
    vb_ref[0] = v.astype(BF16)
    if kv_transposed:
        tm = v.shape[0]
        kt = _dot_nt(wkt_ref[...], h)
        kb_ref[0] = kt.astype(BF16)
        k_ref[0] = kt.reshape(N_ATT_HEADS, ATT_HEAD_DIM, tm)
        v_ref[0] = v.T.reshape(N_ATT_HEADS, ATT_HEAD_DIM, tm)
    else:
        k = proj(D_ATT, 2 * D_ATT)
        kb_ref[0] = k.astype(BF16)
        k_ref[0] = _lanes_to_heads(k)
        v_ref[0] = _lanes_to_heads(v)
    z_ref[0] = proj(3 * D_ATT, 3 * D_ATT + D_SSM)
    xbc_ref[0] = proj(3 * D_ATT + D_SSM, D_MAIN_PROJ)
    dt_ref[0] = _softplus(_dot(h, wdt_ref[...]) + dtb_ref[...])


def _in_proj(x, g, w_main, w_kt, w_dt, dt_bias, kv_transposed):
    b, l, _ = x.shape
    tm = min(PROJ_TM, l)
    assert l % tm == 0
    row = lambda n: pl.BlockSpec((1, tm, n), lambda bi, i: (bi, i, 0))
    full = lambda a: pl.BlockSpec(a.shape, lambda bi, i: (0,) * a.ndim)
    sds = jax.ShapeDtypeStruct
    if kv_transposed:
        kb_shape, kb_spec = sds((b, D_ATT, l), BF16), pl.BlockSpec((1, D_ATT, tm), lambda bi, i: (bi, 0, i))
        kv_shape = sds((b, N_ATT_HEADS, ATT_HEAD_DIM, l), F32)
        kv_spec = pl.BlockSpec((1, N_ATT_HEADS, ATT_HEAD_DIM, tm), lambda bi, i: (bi, 0, 0, i))
    else:
        kb_shape, kb_spec = sds((b, l, D_ATT), BF16), row(D_ATT)
        kv_shape = sds((b, l, N_ATT_HEADS, ATT_HEAD_DIM), F32)
        kv_spec = pl.BlockSpec((1, tm, N_ATT_HEADS, ATT_HEAD_DIM), lambda bi, i: (bi, i, 0, 0))
    out_shape = (sds((b, l, D_ATT), BF16), kb_shape, sds((b, l, D_ATT), BF16), kv_shape, kv_shape,
                 sds((b, l, D_SSM), F32), sds((b, l, CONV_DIM), F32), sds((b, l, LANES), F32))
    return pl.pallas_call(
        functools.partial(_in_proj_kernel, kv_transposed=kv_transposed),
        grid=(b, l // tm),
        in_specs=[row(D_MODEL), full(g), full(w_main), full(w_kt), full(w_dt), full(dt_bias)],
        out_specs=(row(D_ATT), kb_spec, row(D_ATT), kv_spec, kv_spec,
                   row(D_SSM), row(CONV_DIM), row(LANES)),
        out_shape=out_shape,
        compiler_params=pltpu.CompilerParams(
            dimension_semantics=("arbitrary", "arbitrary"), vmem_limit_bytes=VMEM_LIMIT),
        name="in_proj",
    )(x, g, w_main, w_kt, w_dt, dt_bias)


def _sb_weights(s, c, u2, mask):
    lk = -(jnp.maximum(s, 0.0) + jnp.log(1.0 + jnp.exp(-jnp.abs(s))))
    if mask is not None:
        lk = jnp.where(mask, lk, 0.0)
    hi, lo = _split2(lk)
    r = _dot(jnp.concatenate([hi, lo], axis=1), u2) + c
    w = jnp.exp(s + r)
    if mask is not None:
        w = jnp.where(mask, w, 0.0)
    return w, c + jnp.sum(lk, axis=1, keepdims=True)


def _head_lane_masks(n_lanes, dtype):
    lane = lax.broadcasted_iota(jnp.int32, (1, n_lanes), 1)
    return [jnp.where((lane >= h * ATT_HEAD_DIM) & (lane < (h + 1) * ATT_HEAD_DIM), 1.0, 0.0).astype(dtype)
            for h in range(n_lanes // ATT_HEAD_DIM)]


def _attn_prompt_kernel(q_ref, kt_ref, v_ref, u2x_ref, o_ref, s_buf, hl_buf, r_buf, *, tq, nq):
    tk = tq // 2
    hm = _head_lane_masks(LANES, BF16)
    n_items = nq * (nq + 1) // 2
    big = 1e30
    delta = (lax.broadcasted_iota(jnp.int32, (tq, tq), 1) - lax.broadcasted_iota(jnp.int32, (tq, tq), 0))

    s_buf[...] = jnp.full(s_buf.shape, big, F32)
    hl_buf[...] = jnp.zeros(hl_buf.shape, BF16)
    r_buf[...] = jnp.zeros(r_buf.shape, F32)

    def advance(i, j):
        last = j == 0
        return jnp.where(last, i + 1, i), jnp.where(last, i + 1, j - 1)

    def row_off(idx):
        return pl.multiple_of(jnp.minimum(idx, nq - 1) * tq, tq)

    def stage1(i, j, slot3, slot2):
        q = q_ref[0, pl.ds(row_off(i), tq), :]
        qn = -q
        qs = jnp.concatenate([qn * hm[0], qn * hm[1]], axis=0)
        kb = kt_ref[0, :, pl.ds(row_off(j), tq)]
        t = _dot(qs, kb)
        mask = delta < jnp.where(i == j, 0, 2 * tq)
        for h in range(HEADS_PER_LANE_TILE):
            th = jnp.where(mask, t[h * tq:(h + 1) * tq], big)
            s_buf[slot3, h * tq:(h + 1) * tq, :] = th
            lk = jnp.minimum(th, 0.0) - jnp.log(1.0 + jnp.exp(-jnp.abs(th)))
            hi, lo = _split2(lk)
            for half in range(2):
                r0 = half * 2 * tq + h * tq
                hl_buf[slot2, r0:r0 + tq, 0:tk] = hi[:, half * tk:(half + 1) * tk]
                hl_buf[slot2, r0:r0 + tq, tk:tq] = lo[:, half * tk:(half + 1) * tk]

    def stage2(slot2):
        r_buf[slot2] = _dot(hl_buf[slot2], u2x_ref[...])

    def stage3(i, j, slot3, slot2, c, acc):
        first = i == j
        c = jnp.where(first, 0.0, c)
        acc = jnp.where(first, 0.0, acc)
        ws = []
        for half in (1, 0):
            r0 = half * 2 * tq
            r = r_buf[slot2, r0:r0 + 2 * tq, 0:tk] + c
            th = s_buf[slot3, :, half * tk:(half + 1) * tk]
            w = jnp.exp(r - th).astype(BF16)
            c = c + r_buf[slot2, r0:r0 + 2 * tq, tk:tq]
            ws += [w[0:tq], w[tq:2 * tq]]
        vsb = v_ref[0, pl.ds(row_off(j), tq), :]
        vs = []
        for half in (1, 0):
            vb = vsb[half * tk:(half + 1) * tk]
            vs += [vb * hm[0], vb * hm[1]]
        acc = acc + _dot(jnp.concatenate(ws, axis=1), jnp.concatenate(vs, axis=0))
        o_ref[0, pl.ds(row_off(i), tq), :] = acc
        return c, acc

    def body(n, carry):
        i1, j1, i2, j2, i3, j3, slot3, c, acc = carry
        slot2 = n & 1
        slot3_s3 = jnp.where(slot3 == 2, 0, slot3 + 1)
        c, acc = stage3(i3, j3, slot3_s3, slot2, c, acc)
        stage2(1 - slot2)
        stage1(i1, j1, slot3, slot2)
        ni, nj = advance(i1, j1)
        return ni, nj, i1, j1, i2, j2, slot3_s3, c, acc

    z = jnp.int32(0)
    init = (z, z, z, z, z, z, z, jnp.zeros((2 * tq, tk), F32), jnp.zeros((tq, LANES), F32))
    lax.fori_loop(0, n_items + 2, body, init)


def _u2_matrix(tk):
    j = jnp.arange(2 * tk)[:, None] % tk
    s = jnp.arange(tk)[None, :]
    return (j >= s).astype(BF16)


def _attn_prompt(qb, kb, vb):
    b, l, _ = qb.shape
    tq = min(ATT_TQ, l)
    assert l % tq == 0 and tq == 2 * ATT_TK
    u2x = jnp.concatenate([_u2_matrix(ATT_TK), jnp.ones((tq, ATT_TK), BF16)], axis=1)
    kernel = functools.partial(_attn_prompt_kernel, tq=tq, nq=l // tq)
    seq = pl.BlockSpec((1, l, LANES), lambda bi, hp: (bi, 0, hp))
    return pl.pallas_call(
        kernel,
        grid=(b, D_ATT // LANES),
        in_specs=[seq, pl.BlockSpec((1, LANES, l), lambda bi, hp: (bi, hp, 0)), seq,
                  pl.BlockSpec(u2x.shape, lambda bi, hp: (0, 0))],
        out_specs=seq,
        out_shape=jax.ShapeDtypeStruct((b, l, D_ATT), F32),
        scratch_shapes=[pltpu.VMEM((3, 2 * tq, tq), F32), pltpu.VMEM((2, 4 * tq, tq), BF16),
                        pltpu.VMEM((2, 4 * tq, tq), F32)],
        compiler_params=pltpu.CompilerParams(
            dimension_semantics=("arbitrary", "arbitrary"), vmem_limit_bytes=VMEM_LIMIT),
        name="attn_prompt",
    )(qb, kb, vb, u2x)


def _heads_to_lanes(x):
    t = x.shape[0]
    y = jnp.swapaxes(x.reshape(t // SUBLANES, SUBLANES, N_ATT_HEADS, ATT_HEAD_DIM), 1, 2)
    return jnp.concatenate([y[:, v].reshape(t, ATT_HEAD_DIM) for v in range(N_ATT_HEADS)], axis=1)


def _lanes_to_heads(x):
    t = x.shape[0]
    parts = []
    for g in range(D_ATT // LANES):
        a = x[:, g * LANES:(g + 1) * LANES]
        b = pltpu.roll(a, ATT_HEAD_DIM, axis=1)
        parts += [a.reshape(t // SUBLANES, SUBLANES, LANES), b.reshape(t // SUBLANES, SUBLANES, LANES)]
    y = jnp.swapaxes(jnp.stack(parts, axis=1), 1, 2).reshape(t, N_ATT_HEADS, LANES)
    return y[:, :, 0:ATT_HEAD_DIM]


def _attn_sample_kernel(q_ref, kn_ref, vn_ref, ckt_ref, cvt_ref, u2_ref, o_ref, c_scr, acc_scr, *, l, tk, nch):
    ch = pl.program_id(1)
    m = N_ATT_HEADS * l
    hm = _head_lane_masks(D_ATT, BF16)
    q = q_ref[0]
    qs = jnp.concatenate([q * mk for mk in hm], axis=0)
    u2 = u2_ref[...]

    @pl.when(ch == 0)
    def _():
        pad = jnp.zeros((tk - l, D_ATT), BF16)
        kn = jnp.concatenate([kn_ref[0], pad], axis=0)
        vn = jnp.concatenate([vn_ref[0], pad], axis=0)
        row = jnp.concatenate([lax.broadcasted_iota(jnp.int32, (l, tk), 0)] * N_ATT_HEADS, axis=0)
        col = lax.broadcasted_iota(jnp.int32, (m, tk), 1)
        w, c = _sb_weights(_dot_nt(qs, kn), jnp.zeros((m, 1), F32), u2, col < row)
        c_scr[...] = c
        acc_scr[...] = _dot(w.astype(BF16), vn)

    chunk = ckt_ref.shape[-1]
    kt = ckt_ref[0, 0].reshape(D_ATT, chunk).astype(BF16)
    vt = cvt_ref[0, 0].reshape(D_ATT, chunk).astype(BF16)
    c, acc = c_scr[...], acc_scr[...]
    for t in reversed(range(chunk // tk)):
        w, c = _sb_weights(_dot(qs, kt[:, t * tk:(t + 1) * tk]), c, u2, None)
        acc = acc + _dot_nt(w.astype(BF16), vt[:, t * tk:(t + 1) * tk])
    c_scr[...] = c
    acc_scr[...] = acc

    @pl.when(ch == nch - 1)
    def _():
        acc = acc_scr[...]
        hmf = _head_lane_masks(D_ATT, F32)
        out = acc[0:l] * hmf[0]
        for h in range(1, N_ATT_HEADS):
            out = out + acc[h * l:(h + 1) * l] * hmf[h]
        o_ref[0] = out


def _attn_sample(qb, kb_new, vb_new, cache_k, cache_v, layer):
    b, l, _ = qb.shape
    past = cache_k.shape[2]
    cache_k = jnp.transpose(cache_k, (0, 1, 3, 4, 2))
    cache_v = jnp.transpose(cache_v, (0, 1, 3, 4, 2))
    tk = ATT_TK
    chunk = min(ATT_CACHE_CHUNK, past)
    assert past % chunk == 0 and chunk % tk == 0 and l <= tk and l % 16 == 0
    nch = past // chunk
    u2 = _u2_matrix(tk)
    kernel = functools.partial(_attn_sample_kernel, l=l, tk=tk, nch=nch)
    new = pl.BlockSpec((1, l, D_ATT), lambda bi, ch: (bi, 0, 0))
    old = pl.BlockSpec((1, 1, N_ATT_HEADS, ATT_HEAD_DIM, chunk), lambda bi, ch: (layer, bi, 0, 0, nch - 1 - ch))
    return pl.pallas_call(
        kernel,
        grid=(b, nch),
        in_specs=[new, new, new, old, old, pl.BlockSpec(u2.shape, lambda bi, ch: (0, 0))],
        out_specs=new,
        out_shape=jax.ShapeDtypeStruct((b, l, D_ATT), F32),
        scratch_shapes=[pltpu.VMEM((N_ATT_HEADS * l, 1), F32), pltpu.VMEM((N_ATT_HEADS * l, D_ATT), F32)],
        compiler_params=pltpu.CompilerParams(
            dimension_semantics=("arbitrary", "arbitrary"), vmem_limit_bytes=VMEM_LIMIT),
        name="attn_sample",
    )(qb, kb_new, vb_new, cache_k, cache_v, u2)


def _ssd_kernel(xbc_ref, z_ref, dt_ref, conv0_ref, s0_ref, cw_ref, cb_ref, alog_ref, dskip_ref, g_ref,
                tri_ref, y_ref, sfin_ref, cbuf, state, *, q):
    c = pl.program_id(1)

    @pl.when(c == 0)
    def _():
        cbuf[0:SUBLANES, :] = conv0_ref[0]
        state[...] = s0_ref[0]

    cbuf[SUBLANES:SUBLANES + q, :] = xbc_ref[0]
    xc = cb_ref[...]
    for i in range(SSM_CONV):
        xc = xc + cbuf[pl.ds(SUBLANES - (SSM_CONV - 1) + i, q), :] * cw_ref[i:i + 1, :]
    xc = _silu(xc)
    cbuf[0:SUBLANES, :] = cbuf[q:q + SUBLANES, :]

    lane = lax.broadcasted_iota(jnp.int32, (1, LANES), 1)
    lo_half = lane < SSM_STATE
    bmat = xc[:, D_SSM:D_SSM + LANES]
    cmat = xc[:, D_SSM + LANES:D_SSM + 2 * LANES]
    b_sw = pltpu.roll(bmat, SSM_STATE, axis=1)
    c_sw = pltpu.roll(cmat, SSM_STATE, axis=1)
    bdup = [jnp.where(lo_half, bmat, b_sw), jnp.where(lo_half, b_sw, bmat)]
    cdup = [jnp.where(lo_half, cmat, c_sw), jnp.where(lo_half, c_sw, cmat)]
    gmask = [jnp.where(lo_half, 1.0, 0.0), jnp.where(lo_half, 0.0, 1.0)]
    bmat_b = bmat.astype(BF16)
    cb_g = [_dot_nt((cmat * gmask[g]).astype(BF16), bmat_b) for g in range(SSM_GROUPS)]

    dt = dt_ref[0]
    da = dt * (-jnp.exp(alog_ref[...]))
    tri = tri_ref[...]
    hi, mid, lo = _split3(da)
    acum = _dot(tri, hi) + _dot(tri, mid) + _dot(tri, lo)
    acum_t = acum.T
    dt_t = dt.T
    a_end = acum[q - 1:q, :]
    trow = lax.broadcasted_iota(jnp.int32, (q, q), 0)
    tcol = lax.broadcasted_iota(jnp.int32, (q, q), 1)
    causal = tcol <= trow
    bd_r = lax.broadcasted_iota(jnp.int32, (LANES, LANES), 0) < SSM_STATE
    bd_c = lax.broadcasted_iota(jnp.int32, (LANES, LANES), 1) < SSM_STATE
    block_diag = bd_r == bd_c
    hm_b = [jnp.where(lo_half, 1.0, 0.0).astype(BF16), jnp.where(lo_half, 0.0, 1.0).astype(BF16)]

    ys = []
    ssq = jnp.zeros((q, 1), F32)
    for pr in range(N_PAIRS):
        g = pr // (N_PAIRS // SSM_GROUPS)
        x_pair = xc[:, pr * LANES:(pr + 1) * LANES]
        x_b = x_pair.astype(BF16)
        ms = []
        for hh in range(HEADS_PER_LANE_TILE):
            h = pr * HEADS_PER_LANE_TILE + hh
            seg = acum[:, h:h + 1] - acum_t[h:h + 1, :]
            lmat = jnp.where(causal, jnp.exp(seg), 0.0)
            ms.append((cb_g[g] * lmat * dt_t[h:h + 1, :]).astype(BF16))
        h0 = pr * HEADS_PER_LANE_TILE
        acol = jnp.where(lo_half, acum[:, h0:h0 + 1], acum[:, h0 + 1:h0 + 2])
        dcol = jnp.where(lo_half, dt[:, h0:h0 + 1], dt[:, h0 + 1:h0 + 2])
        aend = jnp.where(lo_half, a_end[:, h0:h0 + 1], a_end[:, h0 + 1:h0 + 2])
        s_pair = state[pr]
        y_diag = _dot(jnp.concatenate(ms, axis=1), jnp.concatenate([x_b * hm_b[0], x_b * hm_b[1]], axis=0))
        y_off = _dot((cdup[g] * jnp.exp(acol)).astype(BF16), s_pair.astype(BF16))
        new = _dot_tn((bdup[g] * (jnp.exp(aend - acol) * dcol)).astype(BF16), x_b)
        state[pr] = s_pair * jnp.exp(aend) + jnp.where(block_diag, new, 0.0)
        y = y_diag + y_off + dskip_ref[:, pr * LANES:(pr + 1) * LANES] * x_pair
        yz = y * _silu(z_ref[0, :, pr * LANES:(pr + 1) * LANES])
        ssq = ssq + jnp.sum(yz * yz, axis=1, keepdims=True)
        ys.append(yz)

    inv = lax.rsqrt(ssq * (1.0 / D_SSM) + NORM_EPS)
    for pr in range(N_PAIRS):
        y_ref[0, :, pr * LANES:(pr + 1) * LANES] = (
            ys[pr] * inv * g_ref[:, pr * LANES:(pr + 1) * LANES]).astype(BF16)
    sfin_ref[0] = state[...]


def _ssd(xbc, z, dt, conv0, s0, conv_w, conv_b, a_log_pad, dskip_lanes, g_ssm):
    b, l, _ = xbc.shape
    q = min(SSD_Q, l)
    assert l % q == 0 and q % SUBLANES == 0
    tri = (jnp.arange(q)[None, :] <= jnp.arange(q)[:, None]).astype(BF16)
    seq = lambda n: pl.BlockSpec((1, q, n), lambda bi, ci: (bi, ci, 0))
    per_b = lambda a: pl.BlockSpec((1,) + a.shape[1:], lambda bi, ci: (bi,) + (0,) * (a.ndim - 1))
    full = lambda a: pl.BlockSpec(a.shape, lambda bi, ci: (0,) * a.ndim)
    kernel = functools.partial(_ssd_kernel, q=q)
    return pl.pallas_call(
        kernel,
        grid=(b, l // q),
        in_specs=[seq(CONV_DIM), seq(D_SSM), seq(LANES), per_b(conv0), per_b(s0),
                  full(conv_w), full(conv_b), full(a_log_pad), full(dskip_lanes), full(g_ssm), full(tri)],
        out_specs=(seq(D_SSM), per_b(s0)),
        out_shape=(jax.ShapeDtypeStruct((b, l, D_SSM), BF16), jax.ShapeDtypeStruct(s0.shape, F32)),
        scratch_shapes=[pltpu.VMEM((SUBLANES + q, CONV_DIM), F32),
                        pltpu.VMEM((N_PAIRS, LANES, LANES), F32)],
        compiler_params=pltpu.CompilerParams(
            dimension_semantics=("arbitrary", "arbitrary"), vmem_limit_bytes=VMEM_LIMIT),
        name="ssd",
    )(xbc, z, dt, conv0, s0, conv_w, conv_b, a_log_pad, dskip_lanes, g_ssm, tri)


def _state_to_pairs(st):
    b = st.shape[0]
    s5 = st.reshape(b, N_PAIRS, HEADS_PER_LANE_TILE, SSM_HEAD_DIM, SSM_STATE)
    eye = jnp.eye(HEADS_PER_LANE_TILE, dtype=st.dtype)
    return jnp.einsum('bqipn,ij->bqinjp', s5, eye).reshape(b, N_PAIRS, LANES, LANES)


def _pairs_to_state(sp):
    b = sp.shape[0]
    s6 = sp.reshape(b, N_PAIRS, HEADS_PER_LANE_TILE, SSM_STATE, HEADS_PER_LANE_TILE, SSM_HEAD_DIM)
    diag = jnp.stack([s6[:, :, i, :, i, :] for i in range(HEADS_PER_LANE_TILE)], axis=2)
    return jnp.swapaxes(diag, -1, -2).reshape(b, N_SSM_HEADS, SSM_HEAD_DIM, SSM_STATE)


def _gelu_tanh(x):
    return 0.5 * x * (1.0 + jnp.tanh(math.sqrt(2.0 / math.pi) * (x + 0.044715 * (x * x * x))))


def _out_ffn_kernel(x_ref, attn_ref, ys_ref, fc0_ref, ga_ref, gpost_ref, gpre_ref, gfpost_ref,
                    woa_ref, wos_ref, wg_ref, wu_ref, wd_ref, cw_ref,
                    y_ref, fcn_ref, h2_buf, acc, carry, *, n_seq, lt):
    t = pl.program_id(1)

    @pl.when(t == 0)
    def _():
        carry[...] = fc0_ref[0]

    an = _rmsnorm(attn_ref[...], ga_ref[...]).astype(BF16)
    m = _dot(an, woa_ref[...]) + _dot(ys_ref[...], wos_ref[...])
    x1 = x_ref[...] + _rmsnorm(m, gpost_ref[...])
    h2_buf[...] = _rmsnorm(x1, gpre_ref[...]).astype(BF16)
    acc[...] = jnp.zeros_like(acc)
    row = lax.broadcasted_iota(jnp.int32, (lt, FFN_F), 0)

    def chunk(ci, _):
        h2 = h2_buf[...]
        gate = _dot(h2, wg_ref[ci])
        up = _dot(h2, wu_ref[ci])
        cw = cw_ref[ci]
        acts = []
        for s in range(n_seq):
            gs = gate[s * lt:(s + 1) * lt]
            prev = carry[ci, s * SUBLANES:(s + 1) * SUBLANES, :]
            p1 = prev[SUBLANES - 1:SUBLANES, :]
            p2 = prev[SUBLANES - 2:SUBLANES - 1, :]
            g1 = jnp.where(row == 0, p1, pltpu.roll(gs, 1, axis=0))
            g2 = jnp.where(row == 0, p2, jnp.where(row == 1, p1, pltpu.roll(gs, 2, axis=0)))
            gc = cw[3:4, :] + g2 * cw[0:1, :] + g1 * cw[1:2, :] + gs * cw[2:3, :]
            carry[ci, s * SUBLANES:(s + 1) * SUBLANES, :] = gs[lt - SUBLANES:lt]
            acts.append(_gelu_tanh(gc))
        act = acts[0] if n_seq == 1 else jnp.concatenate(acts, axis=0)
        acc[...] += _dot((act * up).astype(BF16), wd_ref[ci])
        return 0

    lax.fori_loop(0, FFN_NC, chunk, 0)
    y_ref[...] = x1 + _rmsnorm(acc[...], gfpost_ref[...])
    fcn_ref[0] = carry[...]


def _out_ffn(x2d, attn2d, ys2d, fc0, n_seq, lt, gains, weights):
    t = x2d.shape[0]
    tm = n_seq * lt
    n_groups = fc0.shape[0]
    tiles = t // (tm * n_groups)
    assert tiles * tm * n_groups == t
    row = lambda n: pl.BlockSpec((tm, n), lambda gi, ti: (gi * tiles + ti, 0))
    full = lambda a: pl.BlockSpec(a.shape, lambda gi, ti: (0,) * a.ndim, pipeline_mode=pl.Buffered(1))
    fc_spec = pl.BlockSpec((1,) + fc0.shape[1:], lambda gi, ti: (gi, 0, 0, 0))
    kernel = functools.partial(_out_ffn_kernel, n_seq=n_seq, lt=lt)
    return pl.pallas_call(
        kernel,
        grid=(n_groups, tiles),
        in_specs=[row(D_MODEL), row(D_ATT), row(D_SSM), fc_spec] + [full(a) for a in gains]
                 + [full(a) for a in weights],
        out_specs=(row(D_MODEL), fc_spec),
        out_shape=(jax.ShapeDtypeStruct((t, D_MODEL), F32), jax.ShapeDtypeStruct(fc0.shape, F32)),
        scratch_shapes=[pltpu.VMEM((tm, D_MODEL), BF16), pltpu.VMEM((tm, D_MODEL), F32),
                        pltpu.VMEM(fc0.shape[1:], F32)],
        compiler_params=pltpu.CompilerParams(
            dimension_semantics=("arbitrary", "arbitrary"), vmem_limit_bytes=VMEM_LIMIT),
        name="out_ffn",
    )(x2d, attn2d, ys2d, fc0, *gains, *weights)


def _ffn_state_to_chunks(st, n_seq):
    b = st.shape[0]
    s = st.reshape(b // n_seq, n_seq, FFN_CONV - 1, FFN_NC, FFN_F)
    s = jnp.pad(s, ((0, 0), (0, 0), (SUBLANES - (FFN_CONV - 1), 0), (0, 0), (0, 0)))
    return jnp.transpose(s, (0, 3, 1, 2, 4)).reshape(b // n_seq, FFN_NC, n_seq * SUBLANES, FFN_F)


def _chunks_to_ffn_state(ch, n_seq):
    g = ch.shape[0]
    s = ch.reshape(g, FFN_NC, n_seq, SUBLANES, FFN_F)[:, :, :, SUBLANES - (FFN_CONV - 1):, :]
    return jnp.transpose(s, (0, 2, 3, 1, 4)).reshape(g * n_seq, FFN_CONV - 1, D_FF)


def _layer(x, caches, ssm_h0, ssm_conv_prev, ffn_conv_prev, p, is_prompt):
    b, l, _ = x.shape
    t = b * l
    x2d = x.reshape(t, D_MODEL)
    r3 = lambda a: a.reshape(b, l, a.shape[-1])
    proj_w = (p['g_mix_pre'], p['w_main'], p['w_kt'], p['w_dt'], p['dt_bias'])
    if is_prompt:
        qb, kb, vb, kt, vt, z, xbc, dt = _in_proj(x, *proj_w, kv_transposed=True)
        attn = _attn_prompt(qb, kb, vb)
        k, v = jnp.transpose(kt, (0, 3, 1, 2)), jnp.transpose(vt, (0, 3, 1, 2))
    else:
        qb, kb, vb, k, v, z, xbc, dt = _in_proj(x2d[None], *proj_w, kv_transposed=False)
        attn = _attn_sample(r3(qb), r3(kb), r3(vb), *caches)

    conv0 = jnp.pad(ssm_conv_prev, ((0, 0), (SUBLANES - (SSM_CONV - 1), 0), (0, 0)))
    l_ssd = l if l % SSD_Q == 0 else SSD_Q * pl.cdiv(l, SSD_Q)
    pad_t = lambda a: jnp.pad(r3(a), ((0, 0), (0, l_ssd - l), (0, 0)))
    ys, s_fin = _ssd(pad_t(xbc), pad_t(z), pad_t(dt), conv0, _state_to_pairs(ssm_h0),
                     p['ssm_conv_w'], p['ssm_conv_b'], p['a_log'], p['d_skip'], p['g_ssm_out'])
    ys = ys[:, :l, :]
    ssm_conv_new = r3(xbc)[:, l - (SSM_CONV - 1):, :]

    if is_prompt:
        n_seq, lt = 1, min(FFN_TM, l)
    else:
        n_seq, lt = b, l
    fc0 = _ffn_state_to_chunks(ffn_conv_prev, n_seq)
    gains = (p['g_attn_out'], p['g_mix_post'], p['g_ffn_pre'], p['g_ffn_post'])
    weights = (p['w_out_a'], p['w_out_s'], p['w_gate'], p['w_up'], p['w_down'], p['ffn_cw'])
    y2d, fcn = _out_ffn(x2d, attn.reshape(t, D_ATT), ys.reshape(t, D_SSM), fc0, n_seq, lt, gains, weights)
    return (y2d.reshape(b, l, D_MODEL), k.reshape(b, l, N_ATT_HEADS, ATT_HEAD_DIM),
            v.reshape(b, l, N_ATT_HEADS, ATT_HEAD_DIM), _pairs_to_state(s_fin), ssm_conv_new,
            _chunks_to_ffn_state(fcn, n_seq))


def _prep_params(i, g_mix_pre, g_mix_post, w_in, ssm_conv_w, ssm_conv_b, dt_bias, a_log, d_skip,
                 g_ssm_out, g_attn_out, w_out, g_ffn_pre, g_ffn_post, w_up, ffn_conv_w, ffn_conv_b, w_down):
    row = lambda a: a[i].reshape(1, -1).astype(F32)
    pad_lanes = lambda a: jnp.pad(a, ((0, 0), (0, LANES - a.shape[1])))
    wi = w_in[i]
    ffn_cw = jnp.concatenate([ffn_conv_w[i], ffn_conv_b[i][None, :],
                              jnp.zeros((SUBLANES - FFN_CONV - 1, D_FF), F32)], axis=0)
    return {
        'g_mix_pre': row(g_mix_pre), 'g_mix_post': row(g_mix_post),
        'g_ffn_pre': row(g_ffn_pre), 'g_ffn_post': row(g_ffn_post),
        'g_attn_out': row(g_attn_out), 'g_ssm_out': row(g_ssm_out),
        'w_main': wi[:, :D_MAIN_PROJ].astype(BF16),
        'w_kt': wi[:, D_ATT:2 * D_ATT].T.astype(BF16),
        'w_dt': pad_lanes(wi[:, D_MAIN_PROJ:]).astype(BF16),
        'dt_bias': pad_lanes(row(dt_bias)),
        'ssm_conv_w': ssm_conv_w[i].astype(F32), 'ssm_conv_b': row(ssm_conv_b),
        'a_log': pad_lanes(row(a_log)),
        'd_skip': jnp.repeat(d_skip[i].astype(F32), SSM_HEAD_DIM).reshape(1, D_SSM),
        'w_out_a': w_out[i][:D_ATT].astype(BF16), 'w_out_s': w_out[i][D_ATT:].astype(BF16),
        'w_gate': jnp.transpose(w_up[i][:, :D_FF].reshape(D_MODEL, FFN_NC, FFN_F), (1, 0, 2)).astype(BF16),
        'w_up': jnp.transpose(w_up[i][:, D_FF:].reshape(D_MODEL, FFN_NC, FFN_F), (1, 0, 2)).astype(BF16),
        'w_down': w_down[i].reshape(FFN_NC, FFN_F, D_MODEL).astype(BF16),
        'ffn_cw': jnp.transpose(ffn_cw.reshape(SUBLANES, FFN_NC, FFN_F), (1, 0, 2)),
    }


def kernel(x_prompt, x_sample, cache_k, cache_v, state_ssm, state_ssm_conv, state_ffn_conv, g_mix_pre, g_mix_post, w_in, ssm_conv_w, ssm_conv_b, dt_bias, a_log, d_skip, g_ssm_out, g_attn_out, w_out, g_ffn_pre, g_ffn_post, w_up, ffn_conv_w, ffn_conv_b, w_down):
    depth = w_in.shape[0]
    bp = x_prompt.shape[0]
    dtp = x_prompt.dtype
    zh = jnp.zeros((bp, N_SSM_HEADS, SSM_HEAD_DIM, SSM_STATE), dtp)
    zcs = jnp.zeros((bp, SSM_CONV - 1, CONV_DIM), dtp)
    zcf = jnp.zeros((bp, FFN_CONV - 1, D_FF), dtp)
    y_p, y_s = x_prompt, x_sample
    outs_p, outs_s = [], []
    for i in range(depth):
        p = _prep_params(i, g_mix_pre, g_mix_post, w_in, ssm_conv_w, ssm_conv_b, dt_bias, a_log, d_skip,
                         g_ssm_out, g_attn_out, w_out, g_ffn_pre, g_ffn_post, w_up, ffn_conv_w, ffn_conv_b,
                         w_down)
        rp = _layer(y_p, None, zh, zcs, zcf, p, True)
        rs = _layer(y_s, (cache_k, cache_v, i), state_ssm[i], state_ssm_conv[i], state_ffn_conv[i], p, False)
        y_p, y_s = rp[0], rs[0]
        outs_p.append(rp[1:])
        outs_s.append(rs[1:])
    stack = lambda outs, j: jnp.stack([o[j] for o in outs])
    return (y_p, y_s) + tuple(stack(outs_p, j) for j in range(5)) + tuple(stack(outs_s, j) for j in range(5))
```

```python
import functools
import math

import jax
import jax.numpy as jnp
from jax import lax
from jax.experimental import pallas as pl
from jax.experimental.pallas import tpu as pltpu

F32 = jnp.float32
BF16 = jnp.bfloat16

D_MODEL = 1024
D_ATT = 512
N_ATT_HEADS = 8
ATT_HEAD_DIM = 64
D_SSM = 512
N_SSM_HEADS = 8
SSM_HEAD_DIM = 64
SSM_STATE = 64
SSM_GROUPS = 2
SSM_CONV = 4
CONV_DIM = D_SSM + 2 * SSM_GROUPS * SSM_STATE
D_FF = 2816
FFN_CONV = 3
NORM_EPS = 1e-6
D_MAIN_PROJ = 3 * D_ATT + D_SSM + CONV_DIM
ATT_SCALE = ATT_HEAD_DIM ** -0.5

LANES = 128
SUBLANES = 8
HEADS_PER_LANE_TILE = LANES // ATT_HEAD_DIM
N_PAIRS = N_SSM_HEADS // HEADS_PER_LANE_TILE

PROJ_TM = 512
ATT_TQ = 256
ATT_TK = 128
ATT_CACHE_CHUNK = 512
SSD_Q = 128
FFN_TM = 512
FFN_F = 256
FFN_NC = D_FF // FFN_F
VMEM_LIMIT = 56 * 1024 * 1024


def _rmsnorm(x, g):
    y = x * lax.rsqrt(jnp.mean(x * x, axis=-1, keepdims=True) + NORM_EPS)
    return y * g


def _softplus(x):
    return jnp.maximum(x, 0.0) + jnp.log1p(jnp.exp(-jnp.abs(x)))


def _silu(x):
    return x * jax.nn.sigmoid(x)


def _dot(a, b):
    return jnp.dot(a, b, preferred_element_type=F32)


def _dot_nt(a, b):
    return lax.dot_general(a, b, (((1,), (1,)), ((), ())), preferred_element_type=F32)


def _dot_tn(a, b):
    return lax.dot_general(a, b, (((0,), (0,)), ((), ())), preferred_element_type=F32)


def _split2(x):
    hi = x.astype(BF16)
    lo = (x - hi.astype(F32)).astype(BF16)
    return hi, lo


def _split3(x):
    hi = x.astype(BF16)
    r1 = x - hi.astype(F32)
    mid = r1.astype(BF16)
    lo = (r1 - mid.astype(F32)).astype(BF16)
    return hi, mid, lo


def _in_proj_kernel(x_ref, g_ref, w_ref, wkt_ref, wdt_ref, dtb_ref,
                    qb_ref, kb_ref, vb_ref, k_ref, v_ref, z_ref, xbc_ref, dt_ref, *, kv_transposed):
    h = _rmsnorm(x_ref[0], g_ref[...]).astype(BF16)

    def proj(lo, hi):
        return _dot(h, w_ref[:, lo:hi])

    qb_ref[0] = (proj(0, D_ATT) * ATT_SCALE).astype(BF16)
    v = proj(2 * D_ATT, 3 * D_ATT)
    vb_ref[0] = v.astype(BF16)
    if kv_transposed:
        tm = v.shape[0]
        kt = _dot_nt(wkt_ref[...], h)
        kb_ref[0] = kt.astype(BF16)
        k_ref[0] = kt.reshape(N_ATT_HEADS, ATT_HEAD_DIM, tm)
        v_ref[0] = v.T.reshape(N_ATT_HEADS, ATT_HEAD_DIM, tm)
    else:
        k = proj(D_ATT, 2 * D_ATT)
        kb_ref[0] = k.astype(BF16)
        k_ref[0] = _lanes_to_heads(k)
        v_ref[0] = _lanes_to_heads(v)
    z_ref[0] = proj(3 * D_ATT, 3 * D_ATT + D_SSM)
    xbc_ref[0] = proj(3 * D_ATT + D_SSM, D_MAIN_PROJ)
    dt_ref[0] = _softplus(_dot(h, wdt_ref[...]) + dtb_ref[...])


def _in_proj(x, g, w_main, w_kt, w_dt, dt_bias, kv_transposed):
    b, l, _ = x.shape
    tm = min(PROJ_TM, l)
    assert l % tm == 0
    row = lambda n: pl.BlockSpec((1, tm, n), lambda bi, i: (bi, i, 0))
    full = lambda a: pl.BlockSpec(a.shape, lambda bi, i: (0,) * a.ndim)
    sds = jax.ShapeDtypeStruct
    if kv_transposed:
        kb_shape, kb_spec = sds((b, D_ATT, l), BF16), pl.BlockSpec((1, D_ATT, tm), lambda bi, i: (bi, 0, i))
        kv_shape = sds((b, N_ATT_HEADS, ATT_HEAD_DIM, l), F32)
        kv_spec = pl.BlockSpec((1, N_ATT_HEADS, ATT_HEAD_DIM, tm), lambda bi, i: (bi, 0, 0, i))
    else:
        kb_shape, kb_spec = sds((b, l, D_ATT), BF16), row(D_ATT)
        kv_shape = sds((b, l, N_ATT_HEADS, ATT_HEAD_DIM), F32)
        kv_spec = pl.BlockSpec((1, tm, N_ATT_HEADS, ATT_HEAD_DIM), lambda bi, i: (bi, i, 0, 0))
    out_shape = (sds((b, l, D_ATT), BF16), kb_shape, sds((b, l, D_ATT), BF16), kv_shape, kv_shape,
                 sds((b, l, D_SSM), F32), sds((b, l, CONV_DIM), F32), sds((b, l, LANES), F32))
    return pl.pallas_call(
        functools.partial(_in_proj_kernel, kv_transposed=kv_transposed),
        grid=(b, l // tm),
        in_specs=[row(D_MODEL), full(g), full(w_main), full(w_kt), full(w_dt), full(dt_bias)],
        out_specs=(row(D_ATT), kb_spec, row(D_ATT), kv_spec, kv_spec,
                   row(D_SSM), row(CONV_DIM), row(LANES)),
        out_shape=out_shape,
        compiler_params=pltpu.CompilerParams(
            dimension_semantics=("arbitrary", "arbitrary"), vmem_limit_bytes=VMEM_LIMIT),
        name="in_proj",
    )(x, g, w_main, w_kt, w_dt, dt_bias)


def _sb_weights(s, c, u2, mask):
    lk = -(jnp.maximum(s, 0.0) + jnp.log(1.0 + jnp.exp(-jnp.abs(s))))
    if mask is not None:
        lk = jnp.where(mask, lk, 0.0)
    hi, lo = _split2(lk)
    r = _dot(jnp.concatenate([hi, lo], axis=1), u2) + c
    w = jnp.exp(s + r)
    if mask is not None:
        w = jnp.where(mask, w, 0.0)
    return w, c + jnp.sum(lk, axis=1, keepdims=True)


def _head_lane_masks(n_lanes, dtype):
    lane = lax.broadcasted_iota(jnp.int32, (1, n_lanes), 1)
    return [jnp.where((lane >= h * ATT_HEAD_DIM) & (lane < (h + 1) * ATT_HEAD_DIM), 1.0, 0.0).astype(dtype)
            for h in range(n_lanes // ATT_HEAD_DIM)]


def _attn_prompt_kernel(q_ref, kt_ref, v_ref, u2x_ref, o_ref, s_buf, hl_buf, r_buf, *, tq, nq):
    tk = tq // 2
    hm = _head_lane_masks(LANES, BF16)
    n_items = nq * (nq + 1) // 2
    big = 1e30
    delta = (lax.broadcasted_iota(jnp.int32, (tq, tq), 1) - lax.broadcasted_iota(jnp.int32, (tq, tq), 0))

    s_buf[...] = jnp.full(s_buf.shape, big, F32)
    hl_buf[...] = jnp.zeros(hl_buf.shape, BF16)
    r_buf[...] = jnp.zeros(r_buf.shape, F32)

    def advance(i, j):
        last = j == 0
        return jnp.where(last, i + 1, i), jnp.where(last, i + 1, j - 1)

    def row_off(idx):
        return pl.multiple_of(jnp.minimum(idx, nq - 1) * tq, tq)

    def stage1(i, j, slot3, slot2):
        q = q_ref[0, pl.ds(row_off(i), tq), :]
        qn = -q
        qs = jnp.concatenate([qn * hm[0], qn * hm[1]], axis=0)
        kb = kt_ref[0, :, pl.ds(row_off(j), tq)]
        t = _dot(qs, kb)
        mask = delta < jnp.where(i == j, 0, 2 * tq)
        for h in range(HEADS_PER_LANE_TILE):
            th = jnp.where(mask, t[h * tq:(h + 1) * tq], big)
            s_buf[slot3, h * tq:(h + 1) * tq, :] = th
            lk = jnp.minimum(th, 0.0) - jnp.log(1.0 + jnp.exp(-jnp.abs(th)))
            hi, lo = _split2(lk)
            for half in range(2):
                r0 = half * 2 * tq + h * tq
                hl_buf[slot2, r0:r0 + tq, 0:tk] = hi[:, half * tk:(half + 1) * tk]
                hl_buf[slot2, r0:r0 + tq, tk:tq] = lo[:, half * tk:(half + 1) * tk]

    def stage2(slot2):
        r_buf[slot2] = _dot(hl_buf[slot2], u2x_ref[...])

    def stage3(i, j, slot3, slot2, c, acc):
        first = i == j
        c = jnp.where(first, 0.0, c)
        acc = jnp.where(first, 0.0, acc)
        ws = []
        for half in (1, 0):
            r0 = half * 2 * tq
            r = r_buf[slot2, r0:r0 + 2 * tq, 0:tk] + c
            th = s_buf[slot3, :, half * tk:(half + 1) * tk]
            w = jnp.exp(r - th).astype(BF16)
            c = c + r_buf[slot2, r0:r0 + 2 * tq, tk:tq]
            ws += [w[0:tq], w[tq:2 * tq]]
        vsb = v_ref[0, pl.ds(row_off(j), tq), :]
        vs = []
        for half in (1, 0):
            vb = vsb[half * tk:(half + 1) * tk]
            vs += [vb * hm[0], vb * hm[1]]
        acc = acc + _dot(jnp.concatenate(ws, axis=1), jnp.concatenate(vs, axis=0))
        o_ref[0, pl.ds(row_off(i), tq), :] = acc
        return c, acc

    def body(n, carry):
        i1, j1, i2, j2, i3, j3, slot3, c, acc = carry
        slot2 = n & 1
        slot3_s3 = jnp.where(slot3 == 2, 0, slot3 + 1)
        c, acc = stage3(i3, j3, slot3_s3, slot2, c, acc)
        stage2(1 - slot2)
        stage1(i1, j1, slot3, slot2)
        ni, nj = advance(i1, j1)
        return ni, nj, i1, j1, i2, j2, slot3_s3, c, acc

    z = jnp.int32(0)
    init = (z, z, z, z, z, z, z, jnp.zeros((2 * tq, tk), F32), jnp.zeros((tq, LANES), F32))
    lax.fori_loop(0, n_items + 2, body, init)


def _u2_matrix(tk):
    j = jnp.arange(2 * tk)[:, None] % tk
    s = jnp.arange(tk)[None, :]
    return (j >= s).astype(BF16)


def _attn_prompt(qb, kb, vb):
    b, l, _ = qb.shape
    tq = min(ATT_TQ, l)
    assert l % tq == 0 and tq == 2 * ATT_TK
    u2x = jnp.concatenate([_u2_matrix(ATT_TK), jnp.ones((tq, ATT_TK), BF16)], axis=1)
    kernel = functools.partial(_attn_prompt_kernel, tq=tq, nq=l // tq)
    seq = pl.BlockSpec((1, l, LANES), lambda bi, hp: (bi, 0, hp))
    return pl.pallas_call(
        kernel,
        grid=(b, D_ATT // LANES),
        in_specs=[seq, pl.BlockSpec((1, LANES, l), lambda bi, hp: (bi, hp, 0)), seq,
                  pl.BlockSpec(u2x.shape, lambda bi, hp: (0, 0))],
        out_specs=seq,
        out_shape=jax.ShapeDtypeStruct((b, l, D_ATT), F32),
        scratch_shapes=[pltpu.VMEM((3, 2 * tq, tq), F32), pltpu.VMEM((2, 4 * tq, tq), BF16),
                        pltpu.VMEM((2, 4 * tq, tq), F32)],
        compiler_params=pltpu.CompilerParams(
            dimension_semantics=("arbitrary", "arbitrary"), vmem_limit_bytes=VMEM_LIMIT),
        name="attn_prompt",
    )(qb, kb, vb, u2x)


def _heads_to_lanes(x):
    t = x.shape[0]
    y = jnp.swapaxes(x.reshape(t // SUBLANES, SUBLANES, N_ATT_HEADS, ATT_HEAD_DIM), 1, 2)
    return jnp.concatenate([y[:, v].reshape(t, ATT_HEAD_DIM) for v in range(N_ATT_HEADS)], axis=1)


def _lanes_to_heads(x):
    t = x.shape[0]
    parts = []
    for g in range(D_ATT // LANES):
        a = x[:, g * LANES:(g + 1) * LANES]
        b = pltpu.roll(a, ATT_HEAD_DIM, axis=1)
        parts += [a.reshape(t // SUBLANES, SUBLANES, LANES), b.reshape(t // SUBLANES, SUBLANES, LANES)]
    y = jnp.swapaxes(jnp.stack(parts, axis=1), 1, 2).reshape(t, N_ATT_HEADS, LANES)
    return y[:, :, 0:ATT_HEAD_DIM]


def _attn_sample_kernel(q_ref, kn_ref, vn_ref, ckt_ref, cvt_ref, u2_ref, o_ref, c_scr, acc_scr, *, l, tk, nch):
    ch = pl.program_id(1)
    m = N_ATT_HEADS * l
    hm = _head_lane_masks(D_ATT, BF16)
    q = q_ref[0]
    qs = jnp.concatenate([q * mk for mk in hm], axis=0)
    u2 = u2_ref[...]

    @pl.when(ch == 0)
    def _():
        pad = jnp.zeros((tk - l, D_ATT), BF16)
        kn = jnp.concatenate([kn_ref[0], pad], axis=0)
        vn = jnp.concatenate([vn_ref[0], pad], axis=0)
        row = jnp.concatenate([lax.broadcasted_iota(jnp.int32, (l, tk), 0)] * N_ATT_HEADS, axis=0)
        col = lax.broadcasted_iota(jnp.int32, (m, tk), 1)
        w, c = _sb_weights(_dot_nt(qs, kn), jnp.zeros((m, 1), F32), u2, col < row)
        c_scr[...] = c
        acc_scr[...] = _dot(w.astype(BF16), vn)

    chunk = ckt_ref.shape[-1]
    kt = ckt_ref[0, 0].reshape(D_ATT, chunk).astype(BF16)
    vt = cvt_ref[0, 0].reshape(D_ATT, chunk).astype(BF16)
    c, acc = c_scr[...], acc_scr[...]
    for t in reversed(range(chunk // tk)):
        w, c = _sb_weights(_dot(qs, kt[:, t * tk:(t + 1) * tk]), c, u2, None)
        acc = acc + _dot_nt(w.astype(BF16), vt[:, t * tk:(t + 1) * tk])
    c_scr[...] = c
    acc_scr[...] = acc

    @pl.when(ch == nch - 1)
    def _():
        acc = acc_scr[...]
        hmf = _head_lane_masks(D_ATT, F32)
        out = acc[0:l] * hmf[0]
        for h in range(1, N_ATT_HEADS):
            out = out + acc[h * l:(h + 1) * l] * hmf[h]
        o_ref[0] = out


def _attn_sample(qb, kb_new, vb_new, cache_k, cache_v, layer):
    b, l, _ = qb.shape
    past = cache_k.shape[2]
    cache_k = jnp.transpose(cache_k, (0, 1, 3, 4, 2))
    cache_v = jnp.transpose(cache_v, (0, 1, 3, 4, 2))
    tk = ATT_TK
    chunk = min(ATT_CACHE_CHUNK, past)
    assert past % chunk == 0 and chunk % tk == 0 and l <= tk and l % 16 == 0
    nch = past // chunk
    u2 = _u2_matrix(tk)
    kernel = functools.partial(_attn_sample_kernel, l=l, tk=tk, nch=nch)
    new = pl.BlockSpec((1, l, D_ATT), lambda bi, ch: (bi, 0, 0))
    old = pl.BlockSpec((1, 1, N_ATT_HEADS, ATT_HEAD_DIM, chunk), lambda bi, ch: (layer, bi, 0, 0, nch - 1 - ch))
    return pl.pallas_call(
        kernel,
        grid=(b, nch),
        in_specs=[new, new, new, old, old, pl.BlockSpec(u2.shape, lambda bi, ch: (0, 0))],
        out_specs=new,
        out_shape=jax.ShapeDtypeStruct((b, l, D_ATT), F32),
        scratch_shapes=[pltpu.VMEM((N_ATT_HEADS * l, 1), F32), pltpu.VMEM((N_ATT_HEADS * l, D_ATT), F32)],
        compiler_params=pltpu.CompilerParams(
            dimension_semantics=("arbitrary", "arbitrary"), vmem_limit_bytes=VMEM_LIMIT),
        name="attn_sample",
    )(qb, kb_new, vb_new, cache_k, cache_v, u2)


def _ssd_kernel(xbc_ref, z_ref, dt_ref, conv0_ref, s0_ref, cw_ref, cb_ref, alog_ref, dskip_ref, g_ref,
                tri_ref, y_ref, sfin_ref, cbuf, state, *, q):
    c = pl.program_id(1)

    @pl.when(c == 0)
    def _():
        cbuf[0:SUBLANES, :] = conv0_ref[0]
        state[...] = s0_ref[0]

    cbuf[SUBLANES:SUBLANES + q, :] = xbc_ref[0]
    xc = cb_ref[...]
    for i in range(SSM_CONV):
        xc = xc + cbuf[pl.ds(SUBLANES - (SSM_CONV - 1) + i, q), :] * cw_ref[i:i + 1, :]
    xc = _silu(xc)
    cbuf[0:SUBLANES, :] = cbuf[q:q + SUBLANES, :]

    lane = lax.broadcasted_iota(jnp.int32, (1, LANES), 1)
    lo_half = lane < SSM_STATE
    bmat = xc[:, D_SSM:D_SSM + LANES]
    cmat = xc[:, D_SSM + LANES:D_SSM + 2 * LANES]
    b_sw = pltpu.roll(bmat, SSM_STATE, axis=1)
    c_sw = pltpu.roll(cmat, SSM_STATE, axis=1)
    bdup = [jnp.where(lo_half, bmat, b_sw), jnp.where(lo_half, b_sw, bmat)]
    cdup = [jnp.where(lo_half, cmat, c_sw), jnp.where(lo_half, c_sw, cmat)]
    gmask = [jnp.where(lo_half, 1.0, 0.0), jnp.where(lo_half, 0.0, 1.0)]
    bmat_b = bmat.astype(BF16)
    cb_g = [_dot_nt((cmat * gmask[g]).astype(BF16), bmat_b) for g in range(SSM_GROUPS)]

    dt = dt_ref[0]
    da = dt * (-jnp.exp(alog_ref[...]))
    tri = tri_ref[...]
    hi, mid, lo = _split3(da)
    acum = _dot(tri, hi) + _dot(tri, mid) + _dot(tri, lo)
    acum_t = acum.T
    dt_t = dt.T
    a_end = acum[q - 1:q, :]
    trow = lax.broadcasted_iota(jnp.int32, (q, q), 0)
    tcol = lax.broadcasted_iota(jnp.int32, (q, q), 1)
    causal = tcol <= trow
    bd_r = lax.broadcasted_iota(jnp.int32, (LANES, LANES), 0) < SSM_STATE
    bd_c = lax.broadcasted_iota(jnp.int32, (LANES, LANES), 1) < SSM_STATE
    block_diag = bd_r == bd_c
    hm_b = [jnp.where(lo_half, 1.0, 0.0).astype(BF16), jnp.where(lo_half, 0.0, 1.0).astype(BF16)]

    ys = []
    ssq = jnp.zeros((q, 1), F32)
    for pr in range(N_PAIRS):
        g = pr // (N_PAIRS // SSM_GROUPS)
        x_pair = xc[:, pr * LANES:(pr + 1) * LANES]
        x_b = x_pair.astype(BF16)
        ms = []
        for hh in range(HEADS_PER_LANE_TILE):
            h = pr * HEADS_PER_LANE_TILE + hh
            seg = acum[:, h:h + 1] - acum_t[h:h + 1, :]
            lmat = jnp.where(causal, jnp.exp(seg), 0.0)
            ms.append((cb_g[g] * lmat * dt_t[h:h + 1, :]).astype(BF16))
        h0 = pr * HEADS_PER_LANE_TILE
        acol = jnp.where(lo_half, acum[:, h0:h0 + 1], acum[:, h0 + 1:h0 + 2])
        dcol = jnp.where(lo_half, dt[:, h0:h0 + 1], dt[:, h0 + 1:h0 + 2])
        aend = jnp.where(lo_half, a_end[:, h0:h0 + 1], a_end[:, h0 + 1:h0 + 2])
        s_pair = state[pr]
        y_diag = _dot(jnp.concatenate(ms, axis=1), jnp.concatenate([x_b * hm_b[0], x_b * hm_b[1]], axis=0))
        y_off = _dot((cdup[g] * jnp.exp(acol)).astype(BF16), s_pair.astype(BF16))
        new = _dot_tn((bdup[g] * (jnp.exp(aend - acol) * dcol)).astype(BF16), x_b)
        state[pr] = s_pair * jnp.exp(aend) + jnp.where(block_diag, new, 0.0)
        y = y_diag + y_off + dskip_ref[:, pr * LANES:(pr + 1) * LANES] * x_pair
        yz = y * _silu(z_ref[0, :, pr * LANES:(pr + 1) * LANES])
        ssq = ssq + jnp.sum(yz * yz, axis=1, keepdims=True)
        ys.append(yz)

    inv = lax.rsqrt(ssq * (1.0 / D_SSM) + NORM_EPS)
    for pr in range(N_PAIRS):
        y_ref[0, :, pr * LANES:(pr + 1) * LANES] = (
            ys[pr] * inv * g_ref[:, pr * LANES:(pr + 1) * LANES]).astype(BF16)
    sfin_ref[0] = state[...]


def _ssd(xbc, z, dt, conv0, s0, conv_w, conv_b, a_log_pad, dskip_lanes, g_ssm):
    b, l, _ = xbc.shape
    q = min(SSD_Q, l)
    assert l % q == 0 and q % SUBLANES == 0
    tri = (jnp.arange(q)[None, :] <= jnp.arange(q)[:, None]).astype(BF16)
    seq = lambda n: pl.BlockSpec((1, q, n), lambda bi, ci: (bi, ci, 0))
    per_b = lambda a: pl.BlockSpec((1,) + a.shape[1:], lambda bi, ci: (bi,) + (0,) * (a.ndim - 1))
    full = lambda a: pl.BlockSpec(a.shape, lambda bi, ci: (0,) * a.ndim)
    kernel = functools.partial(_ssd_kernel, q=q)
    return pl.pallas_call(
        kernel,
        grid=(b, l // q),
        in_specs=[seq(CONV_DIM), seq(D_SSM), seq(LANES), per_b(conv0), per_b(s0),
                  full(conv_w), full(conv_b), full(a_log_pad), full(dskip_lanes), full(g_ssm), full(tri)],
        out_specs=(seq(D_SSM), per_b(s0)),
        out_shape=(jax.ShapeDtypeStruct((b, l, D_SSM), BF16), jax.ShapeDtypeStruct(s0.shape, F32)),
        scratch_shapes=[pltpu.VMEM((SUBLANES + q, CONV_DIM), F32),
                        pltpu.VMEM((N_PAIRS, LANES, LANES), F32)],
        compiler_params=pltpu.CompilerParams(
            dimension_semantics=("arbitrary", "arbitrary"), vmem_limit_bytes=VMEM_LIMIT),
        name="ssd",
    )(xbc, z, dt, conv0, s0, conv_w, conv_b, a_log_pad, dskip_lanes, g_ssm, tri)


def _state_to_pairs(st):
    b = st.shape[0]
    s5 = st.reshape(b, N_PAIRS, HEADS_PER_LANE_TILE, SSM_HEAD_DIM, SSM_STATE)
    eye = jnp.eye(HEADS_PER_LANE_TILE, dtype=st.dtype)
    return jnp.einsum('bqipn,ij->bqinjp', s5, eye).reshape(b, N_PAIRS, LANES, LANES)


def _pairs_to_state(sp):
    b = sp.shape[0]
    s6 = sp.reshape(b, N_PAIRS, HEADS_PER_LANE_TILE, SSM_STATE, HEADS_PER_LANE_TILE, SSM_HEAD_DIM)
    diag = jnp.stack([s6[:, :, i, :, i, :] for i in range(HEADS_PER_LANE_TILE)], axis=2)
    return jnp.swapaxes(diag, -1, -2).reshape(b, N_SSM_HEADS, SSM_HEAD_DIM, SSM_STATE)


def _gelu_tanh(x):
    return 0.5 * x * (1.0 + jnp.tanh(math.sqrt(2.0 / math.pi) * (x + 0.044715 * (x * x * x))))


def _out_ffn_kernel(x_ref, attn_ref, ys_ref, fc0_ref, ga_ref, gpost_ref, gpre_ref, gfpost_ref,
                    woa_ref, wos_ref, wgu_ref, wd_ref, cw_ref,
                    y_ref, fcn_ref, h2_buf, acc, carry, g_buf, u_buf, *, n_seq, lt):
    t = pl.program_id(1)

    @pl.when(t == 0)
    def _():
        carry[...] = fc0_ref[0]

    an = _rmsnorm(attn_ref[...], ga_ref[...]).astype(BF16)
    m = _dot(an, woa_ref[...]) + _dot(ys_ref[...], wos_ref[...])
    x1 = x_ref[...] + _rmsnorm(m, gpost_ref[...])
    y_ref[...] = x1
    h2_buf[...] = _rmsnorm(x1, gpre_ref[...]).astype(BF16)
    row = lax.broadcasted_iota(jnp.int32, (lt, FFN_F), 0)

    def cols(ci, base=0):
        start = base + ci * FFN_F
        return pl.ds(start if isinstance(ci, int) else pl.multiple_of(start, FFN_F), FFN_F)

    def stage1(ci, slot):
        h2 = h2_buf[...]
        g_buf[slot] = _dot(h2, wgu_ref[:, cols(ci)])
        u_buf[slot] = _dot(h2, wgu_ref[:, cols(ci, D_FF)])

    def stage2(ci, slot):
        gate = g_buf[slot]
        cw = cw_ref[:, cols(ci)]
        acts = []
        for s in range(n_seq):
            gs = gate[s * lt:(s + 1) * lt]
            prev = carry[ci, s * SUBLANES:(s + 1) * SUBLANES, :]
            p1 = prev[SUBLANES - 1:SUBLANES, :]
            p2 = prev[SUBLANES - 2:SUBLANES - 1, :]
            g1 = jnp.where(row == 0, p1, pltpu.roll(gs, 1, axis=0))
            g2 = jnp.where(row == 0, p2, jnp.where(row == 1, p1, pltpu.roll(gs, 2, axis=0)))
            gc = cw[3:4, :] + g2 * cw[0:1, :] + g1 * cw[1:2, :] + gs * cw[2:3, :]
            carry[ci, s * SUBLANES:(s + 1) * SUBLANES, :] = gs[lt - SUBLANES:lt]
            acts.append(_gelu_tanh(gc))
        act = acts[0] if n_seq == 1 else jnp.concatenate(acts, axis=0)
        acc[...] += _dot((act * u_buf[slot]).astype(BF16), wd_ref[cols(ci), :])

    acc[...] = jnp.zeros_like(acc)
    stage1(0, 0)

    def body(p, _):
        c = 2 * p + 1
        stage1(c, 1)
        stage2(c - 1, 0)
        stage1(c + 1, 0)
        stage2(c, 1)
        return 0

    assert FFN_NC % 2 == 1
    lax.fori_loop(0, FFN_NC // 2, body, 0)
    stage2(FFN_NC - 1, 0)
    y_ref[...] = y_ref[...] + _rmsnorm(acc[...], gfpost_ref[...])
    fcn_ref[0] = carry[...]


def _out_ffn(x2d, attn2d, ys2d, fc0, n_seq, lt, gains, weights):
    t = x2d.shape[0]
    tm = n_seq * lt
    n_groups = fc0.shape[0]
    tiles = t // (tm * n_groups)
    assert tiles * tm * n_groups == t
    row = lambda n: pl.BlockSpec((tm, n), lambda gi, ti: (gi * tiles + ti, 0))
    full = lambda a: pl.BlockSpec(a.shape, lambda gi, ti: (0,) * a.ndim, pipeline_mode=pl.Buffered(1))
    fc_spec = pl.BlockSpec((1,) + fc0.shape[1:], lambda gi, ti: (gi, 0, 0, 0))
    kernel = functools.partial(_out_ffn_kernel, n_seq=n_seq, lt=lt)
    return pl.pallas_call(
        kernel,
        grid=(n_groups, tiles),
        in_specs=[row(D_MODEL), row(D_ATT), row(D_SSM), fc_spec] + [full(a) for a in gains]
                 + [full(a) for a in weights],
        out_specs=(row(D_MODEL), fc_spec),
        out_shape=(jax.ShapeDtypeStruct((t, D_MODEL), F32), jax.ShapeDtypeStruct(fc0.shape, F32)),
        scratch_shapes=[pltpu.VMEM((tm, D_MODEL), BF16), pltpu.VMEM((tm, D_MODEL), F32),
                        pltpu.VMEM(fc0.shape[1:], F32), pltpu.VMEM((2, tm, FFN_F), F32),
                        pltpu.VMEM((2, tm, FFN_F), F32)],
        compiler_params=pltpu.CompilerParams(
            dimension_semantics=("arbitrary", "arbitrary"), vmem_limit_bytes=VMEM_LIMIT),
        name="out_ffn",
    )(x2d, attn2d, ys2d, fc0, *gains, *weights)


def _ffn_state_to_chunks(st, n_seq):
    b = st.shape[0]
    s = st.reshape(b // n_seq, n_seq, FFN_CONV - 1, FFN_NC, FFN_F)
    s = jnp.pad(s, ((0, 0), (0, 0), (SUBLANES - (FFN_CONV - 1), 0), (0, 0), (0, 0)))
    return jnp.transpose(s, (0, 3, 1, 2, 4)).reshape(b // n_seq, FFN_NC, n_seq * SUBLANES, FFN_F)


def _chunks_to_ffn_state(ch, n_seq):
    g = ch.shape[0]
    s = ch.reshape(g, FFN_NC, n_seq, SUBLANES, FFN_F)[:, :, :, SUBLANES - (FFN_CONV - 1):, :]
    return jnp.transpose(s, (0, 2, 3, 1, 4)).reshape(g * n_seq, FFN_CONV - 1, D_FF)


def _layer(x, caches, ssm_h0, ssm_conv_prev, ffn_conv_prev, p, is_prompt):
    b, l, _ = x.shape
    t = b * l
    x2d = x.reshape(t, D_MODEL)
    r3 = lambda a: a.reshape(b, l, a.shape[-1])
    proj_w = (p['g_mix_pre'], p['w_main'], p['w_kt'], p['w_dt'], p['dt_bias'])
    if is_prompt:
        qb, kb, vb, kt, vt, z, xbc, dt = _in_proj(x, *proj_w, kv_transposed=True)
        attn = _attn_prompt(qb, kb, vb)
        k, v = jnp.transpose(kt, (0, 3, 1, 2)), jnp.transpose(vt, (0, 3, 1, 2))
    else:
        qb, kb, vb, k, v, z, xbc, dt = _in_proj(x2d[None], *proj_w, kv_transposed=False)
        attn = _attn_sample(r3(qb), r3(kb), r3(vb), *caches)

    conv0 = jnp.pad(ssm_conv_prev, ((0, 0), (SUBLANES - (SSM_CONV - 1), 0), (0, 0)))
    l_ssd = l if l % SSD_Q == 0 else SSD_Q * pl.cdiv(l, SSD_Q)
    pad_t = lambda a: jnp.pad(r3(a), ((0, 0), (0, l_ssd - l), (0, 0)))
    ys, s_fin = _ssd(pad_t(xbc), pad_t(z), pad_t(dt), conv0, _state_to_pairs(ssm_h0),
                     p['ssm_conv_w'], p['ssm_conv_b'], p['a_log'], p['d_skip'], p['g_ssm_out'])
    ys = ys[:, :l, :]
    ssm_conv_new = r3(xbc)[:, l - (SSM_CONV - 1):, :]

    if is_prompt:
        n_seq, lt = 1, min(FFN_TM, l)
    else:
        n_seq, lt = b, l
    fc0 = _ffn_state_to_chunks(ffn_conv_prev, n_seq)
    gains = (p['g_attn_out'], p['g_mix_post'], p['g_ffn_pre'], p['g_ffn_post'])
    weights = (p['w_out_a'], p['w_out_s'], p['w_gu'], p['w_down'], p['ffn_cw'])
    y2d, fcn = _out_ffn(x2d, attn.reshape(t, D_ATT), ys.reshape(t, D_SSM), fc0, n_seq, lt, gains, weights)
    return (y2d.reshape(b, l, D_MODEL), k.reshape(b, l, N_ATT_HEADS, ATT_HEAD_DIM),
            v.reshape(b, l, N_ATT_HEADS, ATT_HEAD_DIM), _pairs_to_state(s_fin), ssm_conv_new,
            _chunks_to_ffn_state(fcn, n_seq))


def _prep_params(i, g_mix_pre, g_mix_post, w_in, ssm_conv_w, ssm_conv_b, dt_bias, a_log, d_skip,
                 g_ssm_out, g_attn_out, w_out, g_ffn_pre, g_ffn_post, w_up, ffn_conv_w, ffn_conv_b, w_down):
    row = lambda a: a[i].reshape(1, -1).astype(F32)
    pad_lanes = lambda a: jnp.pad(a, ((0, 0), (0, LANES - a.shape[1])))
    wi = w_in[i]
    ffn_cw = jnp.concatenate([ffn_conv_w[i], ffn_conv_b[i][None, :],
                              jnp.zeros((SUBLANES - FFN_CONV - 1, D_FF), F32)], axis=0)
    return {
        'g_mix_pre': row(g_mix_pre), 'g_mix_post': row(g_mix_post),
        'g_ffn_pre': row(g_ffn_pre), 'g_ffn_post': row(g_ffn_post),
        'g_attn_out': row(g_attn_out), 'g_ssm_out': row(g_ssm_out),
        'w_main': wi[:, :D_MAIN_PROJ].astype(BF16),
        'w_kt': wi[:, D_ATT:2 * D_ATT].T.astype(BF16),
        'w_dt': pad_lanes(wi[:, D_MAIN_PROJ:]).astype(BF16),
        'dt_bias': pad_lanes(row(dt_bias)),
        'ssm_conv_w': ssm_conv_w[i].astype(F32), 'ssm_conv_b': row(ssm_conv_b),
        'a_log': pad_lanes(row(a_log)),
        'd_skip': jnp.repeat(d_skip[i].astype(F32), SSM_HEAD_DIM).reshape(1, D_SSM),
        'w_out_a': w_out[i][:D_ATT].astype(BF16), 'w_out_s': w_out[i][D_ATT:].astype(BF16),
        'w_gu': w_up[i].astype(BF16),
        'w_down': w_down[i].astype(BF16),
        'ffn_cw': ffn_cw,
    }


def kernel(x_prompt, x_sample, cache_k, cache_v, state_ssm, state_ssm_conv, state_ffn_conv, g_mix_pre, g_mix_post, w_in, ssm_conv_w, ssm_conv_b, dt_bias, a_log, d_skip, g_ssm_out, g_attn_out, w_out, g_ffn_pre, g_ffn_post, w_up, ffn_conv_w, ffn_conv_b, w_down):
    depth = w_in.shape[0]
    bp = x_prompt.shape[0]
    dtp = x_prompt.dtype
    zh = jnp.zeros((bp, N_SSM_HEADS, SSM_HEAD_DIM, SSM_STATE), dtp)
    zcs = jnp.zeros((bp, SSM_CONV - 1, CONV_DIM), dtp)
    zcf = jnp.zeros((bp, FFN_CONV - 1, D_FF), dtp)
    y_p, y_s = x_prompt, x_sample
    outs_p, outs_s = [], []
    for i in range(depth):
        p = _prep_params(i, g_mix_pre, g_mix_post, w_in, ssm_conv_w, ssm_conv_b, dt_bias, a_log, d_skip,
                         g_ssm_out, g_attn_out, w_out, g_ffn_pre, g_ffn_post, w_up, ffn_conv_w, ffn_conv_b,
                         w_down)
        rp = _layer(y_p, None, zh, zcs, zcf, p, True)
        rs = _layer(y_s, (cache_k, cache_v, i), state_ssm[i], state_ssm_conv[i], state_ffn_conv[i], p, False)
        y_p, y_s = rp[0], rs[0]
        outs_p.append(rp[1:])
        outs_s.append(rs[1:])
    stack = lambda outs, j: jnp.stack([o[j] for o in outs])
    return (y_p, y_s) + tuple(stack(outs_p, j) for j in range(5)) + tuple(stack(outs_s, j) for j in range(5))
```

```python
import functools
import math

import jax
import jax.numpy as jnp
from jax import lax
from jax.experimental import pallas as pl
from jax.experimental.pallas import tpu as pltpu

F32 = jnp.float32
BF16 = jnp.bfloat16

D_MODEL = 1024
D_ATT = 512
N_ATT_HEADS = 8
ATT_HEAD_DIM = 64
D_SSM = 512
N_SSM_HEADS = 8
SSM_HEAD_DIM = 64
SSM_STATE = 64
SSM_GROUPS = 2
SSM_CONV = 4
CONV_DIM = D_SSM + 2 * SSM_GROUPS * SSM_STATE
D_FF = 2816
FFN_CONV = 3
NORM_EPS = 1e-6
D_MAIN_PROJ = 3 * D_ATT + D_SSM + CONV_DIM
ATT_SCALE = ATT_HEAD_DIM ** -0.5
LOG2E = math.log2(math.e)

LANES = 128
SUBLANES = 8
HEADS_PER_LANE_TILE = LANES // ATT_HEAD_DIM
N_PAIRS = N_SSM_HEADS // HEADS_PER_LANE_TILE

PROJ_TM = 512
ATT_TQ = 256
ATT_TK = 128
ATT_CACHE_CHUNK = 512
SSD_Q = 128
FFN_TM = 512
FFN_F = 256
FFN_NC = D_FF // FFN_F
VMEM_LIMIT = 56 * 1024 * 1024


def _rmsnorm(x, g):
    y = x * lax.rsqrt(jnp.mean(x * x, axis=-1, keepdims=True) + NORM_EPS)
    return y * g


def _softplus(x):
    return jnp.maximum(x, 0.0) + jnp.log1p(jnp.exp(-jnp.abs(x)))


def _silu(x):
    return x * jax.nn.sigmoid(x)


def _dot(a, b):
    return jnp.dot(a, b, preferred_element_type=F32)


def _dot_nt(a, b):
    return lax.dot_general(a, b, (((1,), (1,)), ((), ())), preferred_element_type=F32)


def _dot_tn(a, b):
    return lax.dot_general(a, b, (((0,), (0,)), ((), ())), preferred_element_type=F32)


def _split2(x):
    hi = x.astype(BF16)
    lo = (x - hi.astype(F32)).astype(BF16)
    return hi, lo


def _split3(x):
    hi = x.astype(BF16)
    r1 = x - hi.astype(F32)
    mid = r1.astype(BF16)
    lo = (r1 - mid.astype(F32)).astype(BF16)
    return hi, mid, lo


def _in_proj_kernel(x_ref, g_ref, w_ref, wkt_ref, wdt_ref, dtb_ref,
                    qb_ref, kb_ref, vb_ref, k_ref, v_ref, z_ref, xbc_ref, dt_ref, *, kv_transposed):
    h = _rmsnorm(x_ref[0], g_ref[...]).astype(BF16)

    def proj(lo, hi):
        return _dot(h, w_ref[:, lo:hi])

    qb_ref[0] = (proj(0, D_ATT) * ATT_SCALE).astype(BF16)
    v = proj(2 * D_ATT, 3 * D_ATT)
    vb_ref[0] = v.astype(BF16)
    if kv_transposed:
        tm = v.shape[0]
        kt = _dot_nt(wkt_ref[...], h)
        kb_ref[0] = kt.astype(BF16)
        k_ref[0] = kt.reshape(N_ATT_HEADS, ATT_HEAD_DIM, tm)
        v_ref[0] = v.T.reshape(N_ATT_HEADS, ATT_HEAD_DIM, tm)
    else:
        k = proj(D_ATT, 2 * D_ATT)
        kb_ref[0] = k.astype(BF16)
        k_ref[0] = _lanes_to_heads(k)
        v_ref[0] = _lanes_to_heads(v)
    z_ref[0] = proj(3 * D_ATT, 3 * D_ATT + D_SSM)
    xbc_ref[0] = proj(3 * D_ATT + D_SSM, D_MAIN_PROJ)
    dt_ref[0] = _softplus(_dot(h, wdt_ref[...]) + dtb_ref[...])


def _in_proj(x, g, w_main, w_kt, w_dt, dt_bias, kv_transposed):
    b, l, _ = x.shape
    tm = min(PROJ_TM, l)
    assert l % tm == 0
    row = lambda n: pl.BlockSpec((1, tm, n), lambda bi, i: (bi, i, 0))
    full = lambda a: pl.BlockSpec(a.shape, lambda bi, i: (0,) * a.ndim)
    sds = jax.ShapeDtypeStruct
    if kv_transposed:
        kb_shape, kb_spec = sds((b, D_ATT, l), BF16), pl.BlockSpec((1, D_ATT, tm), lambda bi, i: (bi, 0, i))
        kv_shape = sds((b, N_ATT_HEADS, ATT_HEAD_DIM, l), F32)
        kv_spec = pl.BlockSpec((1, N_ATT_HEADS, ATT_HEAD_DIM, tm), lambda bi, i: (bi, 0, 0, i))
    else:
        kb_shape, kb_spec = sds((b, l, D_ATT), BF16), row(D_ATT)
        kv_shape = sds((b, l, N_ATT_HEADS, ATT_HEAD_DIM), F32)
        kv_spec = pl.BlockSpec((1, tm, N_ATT_HEADS, ATT_HEAD_DIM), lambda bi, i: (bi, i, 0, 0))
    out_shape = (sds((b, l, D_ATT), BF16), kb_shape, sds((b, l, D_ATT), BF16), kv_shape, kv_shape,
                 sds((b, l, D_SSM), F32), sds((b, l, CONV_DIM), F32), sds((b, l, LANES), F32))
    return pl.pallas_call(
        functools.partial(_in_proj_kernel, kv_transposed=kv_transposed),
        grid=(b, l // tm),
        in_specs=[row(D_MODEL), full(g), full(w_main), full(w_kt), full(w_dt), full(dt_bias)],
        out_specs=(row(D_ATT), kb_spec, row(D_ATT), kv_spec, kv_spec,
                   row(D_SSM), row(CONV_DIM), row(LANES)),
        out_shape=out_shape,
        compiler_params=pltpu.CompilerParams(
            dimension_semantics=("arbitrary", "arbitrary"), vmem_limit_bytes=VMEM_LIMIT),
        name="in_proj",
    )(x, g, w_main, w_kt, w_dt, dt_bias)


def _sb_weights(s, c, u2, mask):
    lk = -(jnp.maximum(s, 0.0) + jnp.log(1.0 + jnp.exp(-jnp.abs(s))))
    if mask is not None:
        lk = jnp.where(mask, lk, 0.0)
    hi, lo = _split2(lk)
    r = _dot(jnp.concatenate([hi, lo], axis=1), u2) + c
    w = jnp.exp(s + r)
    if mask is not None:
        w = jnp.where(mask, w, 0.0)
    return w, c + jnp.sum(lk, axis=1, keepdims=True)


def _head_lane_masks(n_lanes, dtype):
    lane = lax.broadcasted_iota(jnp.int32, (1, n_lanes), 1)
    return [jnp.where((lane >= h * ATT_HEAD_DIM) & (lane < (h + 1) * ATT_HEAD_DIM), 1.0, 0.0).astype(dtype)
            for h in range(n_lanes // ATT_HEAD_DIM)]


def _attn_prompt_kernel(q_ref, kt_ref, v_ref, u2x_ref, o_ref, s_buf, hl_buf, mask_buf, qs_buf, vcat_buf,
                        *, tq, nq):
    tk = tq // 2
    hm = _head_lane_masks(LANES, BF16)
    n_items = nq * (nq + 1) // 2
    big = 1e30
    causal = lax.broadcasted_iota(jnp.int32, (tq, tq), 1) < lax.broadcasted_iota(jnp.int32, (tq, tq), 0)
    mask_buf[0] = jnp.full((tq, tq), -jnp.inf, F32)
    mask_buf[1] = jnp.where(causal, -jnp.inf, big)
    for i in range(nq):
        qn = -q_ref[0, i * tq:(i + 1) * tq, :]
        vsb = v_ref[0, i * tq:(i + 1) * tq, :]
        for h in range(HEADS_PER_LANE_TILE):
            qs_buf[i, h * tq:(h + 1) * tq, :] = qn * hm[h]
            for n, half in enumerate((1, 0)):
                r0 = (n * HEADS_PER_LANE_TILE + h) * tk
                vcat_buf[i, r0:r0 + tk, :] = vsb[half * tk:(half + 1) * tk] * hm[h]

    def advance(i, j):
        last = j == 0
        return jnp.where(last, i + 1, i), jnp.where(last, i + 1, j - 1)

    def row_off(idx):
        return pl.multiple_of(jnp.minimum(idx, nq - 1) * tq, tq)

    def stage_a(i, j, slot):
        kb = kt_ref[0, :, pl.ds(row_off(j), tq)]
        t = _dot(qs_buf[jnp.minimum(i, nq - 1)], kb)
        floor = mask_buf[(i == j).astype(jnp.int32)]
        for h in range(HEADS_PER_LANE_TILE):
            th = jnp.maximum(t[h * tq:(h + 1) * tq], floor)
            s_buf[slot, h * tq:(h + 1) * tq, :] = th
            e = jnp.exp2(jnp.abs(th) * (-LOG2E))
            lk = jnp.minimum(th, 0.0) - jnp.log(1.0 + e)
            hi, lo = _split2(lk)
            for half in range(2):
                r0 = half * 2 * tq + h * tq
                hl_buf[slot, r0:r0 + tq, 0:tk] = hi[:, half * tk:(half + 1) * tk]
                hl_buf[slot, r0:r0 + tq, tk:tq] = lo[:, half * tk:(half + 1) * tk]

    def stage_b(i, j, slot, c, acc):
        rr = _dot(hl_buf[slot], u2x_ref[...])
        first = i == j
        c = jnp.where(first, 0.0, c)
        acc = jnp.where(first, 0.0, acc)
        ws = []
        for half in (1, 0):
            r0 = half * 2 * tq
            r = rr[r0:r0 + 2 * tq, 0:tk] + c
            th = s_buf[slot, :, half * tk:(half + 1) * tk]
            w = jnp.exp(r - th).astype(BF16)
            c = c + rr[r0:r0 + 2 * tq, tk:tq]
            ws += [w[0:tq], w[tq:2 * tq]]
        acc = acc + _dot(jnp.concatenate(ws, axis=1), vcat_buf[jnp.minimum(j, nq - 1)])
        o_ref[0, pl.ds(row_off(i), tq), :] = acc
        return c, acc

    def two_items(carry):
        ib, jb, ia, ja, c, acc = carry
        stage_a(ia, ja, 1)
        c, acc = stage_b(ib, jb, 0, c, acc)
        i2, j2 = advance(ia, ja)
        stage_a(i2, j2, 0)
        c, acc = stage_b(ia, ja, 1, c, acc)
        i3, j3 = advance(i2, j2)
        return i2, j2, i3, j3, c, acc

    pairs_per_trip = 2 if n_items % 4 == 0 else 1

    def body(p, carry):
        for _ in range(pairs_per_trip):
            carry = two_items(carry)
        return carry

    z = jnp.int32(0)
    stage_a(z, z, 0)
    init = (z, z, z + 1, z + 1, jnp.zeros((2 * tq, tk), F32), jnp.zeros((tq, LANES), F32))
    ib, jb, _, _, c, acc = lax.fori_loop(0, n_items // (2 * pairs_per_trip), body, init)
    if n_items % 2:
        stage_b(ib, jb, 0, c, acc)


def _u2_matrix(tk):
    j = jnp.arange(2 * tk)[:, None] % tk
    s = jnp.arange(tk)[None, :]
    return (j >= s).astype(BF16)


def _attn_prompt(qb, kb, vb):
    b, l, _ = qb.shape
    tq = min(ATT_TQ, l)
    assert l % tq == 0 and tq == 2 * ATT_TK
    u2x = jnp.concatenate([_u2_matrix(ATT_TK), jnp.ones((tq, ATT_TK), BF16)], axis=1)
    kernel = functools.partial(_attn_prompt_kernel, tq=tq, nq=l // tq)
    seq = pl.BlockSpec((1, l, LANES), lambda bi, hp: (bi, 0, hp))
    return pl.pallas_call(
        kernel,
        grid=(b, D_ATT // LANES),
        in_specs=[seq, pl.BlockSpec((1, LANES, l), lambda bi, hp: (bi, hp, 0)), seq,
                  pl.BlockSpec(u2x.shape, lambda bi, hp: (0, 0))],
        out_specs=seq,
        out_shape=jax.ShapeDtypeStruct((b, l, D_ATT), F32),
        scratch_shapes=[pltpu.VMEM((2, 2 * tq, tq), F32), pltpu.VMEM((2, 4 * tq, tq), BF16),
                        pltpu.VMEM((2, tq, tq), F32), pltpu.VMEM((l // tq, 2 * tq, LANES), BF16),
                        pltpu.VMEM((l // tq, 2 * tq, LANES), BF16)],
        compiler_params=pltpu.CompilerParams(
            dimension_semantics=("arbitrary", "arbitrary"), vmem_limit_bytes=VMEM_LIMIT),
        name="attn_prompt",
    )(qb, kb, vb, u2x)


def _heads_to_lanes(x):
    t = x.shape[0]
    y = jnp.swapaxes(x.reshape(t // SUBLANES, SUBLANES, N_ATT_HEADS, ATT_HEAD_DIM), 1, 2)
    return jnp.concatenate([y[:, v].reshape(t, ATT_HEAD_DIM) for v in range(N_ATT_HEADS)], axis=1)


def _lanes_to_heads(x):
    t = x.shape[0]
    parts = []
    for g in range(D_ATT // LANES):
        a = x[:, g * LANES:(g + 1) * LANES]
        b = pltpu.roll(a, ATT_HEAD_DIM, axis=1)
        parts += [a.reshape(t // SUBLANES, SUBLANES, LANES), b.reshape(t // SUBLANES, SUBLANES, LANES)]
    y = jnp.swapaxes(jnp.stack(parts, axis=1), 1, 2).reshape(t, N_ATT_HEADS, LANES)
    return y[:, :, 0:ATT_HEAD_DIM]


def _attn_sample_kernel(q_ref, kn_ref, vn_ref, ckt_ref, cvt_ref, u2_ref, o_ref, c_scr, acc_scr, *, l, tk, nch):
    ch = pl.program_id(1)
    m = N_ATT_HEADS * l
    hm = _head_lane_masks(D_ATT, BF16)
    q = q_ref[0]
    qs = jnp.concatenate([q * mk for mk in hm], axis=0)
    u2 = u2_ref[...]

    @pl.when(ch == 0)
    def _():
        pad = jnp.zeros((tk - l, D_ATT), BF16)
        kn = jnp.concatenate([kn_ref[0], pad], axis=0)
        vn = jnp.concatenate([vn_ref[0], pad], axis=0)
        row = jnp.concatenate([lax.broadcasted_iota(jnp.int32, (l, tk), 0)] * N_ATT_HEADS, axis=0)
        col = lax.broadcasted_iota(jnp.int32, (m, tk), 1)
        w, c = _sb_weights(_dot_nt(qs, kn), jnp.zeros((m, 1), F32), u2, col < row)
        c_scr[...] = c
        acc_scr[...] = _dot(w.astype(BF16), vn)

    chunk = ckt_ref.shape[-1]
    kt = ckt_ref[0, 0].reshape(D_ATT, chunk).astype(BF16)
    vt = cvt_ref[0, 0].reshape(D_ATT, chunk).astype(BF16)
    c, acc = c_scr[...], acc_scr[...]
    for t in reversed(range(chunk // tk)):
        w, c = _sb_weights(_dot(qs, kt[:, t * tk:(t + 1) * tk]), c, u2, None)
        acc = acc + _dot_nt(w.astype(BF16), vt[:, t * tk:(t + 1) * tk])
    c_scr[...] = c
    acc_scr[...] = acc

    @pl.when(ch == nch - 1)
    def _():
        acc = acc_scr[...]
        hmf = _head_lane_masks(D_ATT, F32)
        out = acc[0:l] * hmf[0]
        for h in range(1, N_ATT_HEADS):
            out = out + acc[h * l:(h + 1) * l] * hmf[h]
        o_ref[0] = out


def _attn_sample(qb, kb_new, vb_new, cache_k, cache_v, layer):
    b, l, _ = qb.shape
    past = cache_k.shape[2]
    cache_k = jnp.transpose(cache_k, (0, 1, 3, 4, 2))
    cache_v = jnp.transpose(cache_v, (0, 1, 3, 4, 2))
    tk = ATT_TK
    chunk = min(ATT_CACHE_CHUNK, past)
    assert past % chunk == 0 and chunk % tk == 0 and l <= tk and l % 16 == 0
    nch = past // chunk
    u2 = _u2_matrix(tk)
    kernel = functools.partial(_attn_sample_kernel, l=l, tk=tk, nch=nch)
    new = pl.BlockSpec((1, l, D_ATT), lambda bi, ch: (bi, 0, 0))
    old = pl.BlockSpec((1, 1, N_ATT_HEADS, ATT_HEAD_DIM, chunk), lambda bi, ch: (layer, bi, 0, 0, nch - 1 - ch))
    return pl.pallas_call(
        kernel,
        grid=(b, nch),
        in_specs=[new, new, new, old, old, pl.BlockSpec(u2.shape, lambda bi, ch: (0, 0))],
        out_specs=new,
        out_shape=jax.ShapeDtypeStruct((b, l, D_ATT), F32),
        scratch_shapes=[pltpu.VMEM((N_ATT_HEADS * l, 1), F32), pltpu.VMEM((N_ATT_HEADS * l, D_ATT), F32)],
        compiler_params=pltpu.CompilerParams(
            dimension_semantics=("arbitrary", "arbitrary"), vmem_limit_bytes=VMEM_LIMIT),
        name="attn_sample",
    )(qb, kb_new, vb_new, cache_k, cache_v, u2)


def _ssd_kernel(xbc_ref, z_ref, dt_ref, conv0_ref, s0_ref, cw_ref, cb_ref, alog_ref, dskip_ref, g_ref,
                tri_ref, y_ref, sfin_ref, cbuf, state, *, q):
    c = pl.program_id(1)

    @pl.when(c == 0)
    def _():
        cbuf[0:SUBLANES, :] = conv0_ref[0]
        state[...] = s0_ref[0]

    cbuf[SUBLANES:SUBLANES + q, :] = xbc_ref[0]
    xc = cb_ref[...]
    for i in range(SSM_CONV):
        xc = xc + cbuf[pl.ds(SUBLANES - (SSM_CONV - 1) + i, q), :] * cw_ref[i:i + 1, :]
    xc = _silu(xc)
    cbuf[0:SUBLANES, :] = cbuf[q:q + SUBLANES, :]

    lane = lax.broadcasted_iota(jnp.int32, (1, LANES), 1)
    lo_half = lane < SSM_STATE
    bmat = xc[:, D_SSM:D_SSM + LANES]
    cmat = xc[:, D_SSM + LANES:D_SSM + 2 * LANES]
    b_sw = pltpu.roll(bmat, SSM_STATE, axis=1)
    c_sw = pltpu.roll(cmat, SSM_STATE, axis=1)
    bdup = [jnp.where(lo_half, bmat, b_sw), jnp.where(lo_half, b_sw, bmat)]
    cdup = [jnp.where(lo_half, cmat, c_sw), jnp.where(lo_half, c_sw, cmat)]
    gmask = [jnp.where(lo_half, 1.0, 0.0), jnp.where(lo_half, 0.0, 1.0)]
    bmat_b = bmat.astype(BF16)
    cb_g = [_dot_nt((cmat * gmask[g]).astype(BF16), bmat_b) for g in range(SSM_GROUPS)]

    dt = dt_ref[0]
    da = dt * (-jnp.exp(alog_ref[...]))
    tri = tri_ref[...]
    hi, mid, lo = _split3(da)
    acum = _dot(tri, hi) + _dot(tri, mid) + _dot(tri, lo)
    acum_t = acum.T
    dt_t = dt.T
    a_end = acum[q - 1:q, :]
    trow = lax.broadcasted_iota(jnp.int32, (q, q), 0)
    tcol = lax.broadcasted_iota(jnp.int32, (q, q), 1)
    causal = tcol <= trow
    bd_r = lax.broadcasted_iota(jnp.int32, (LANES, LANES), 0) < SSM_STATE
    bd_c = lax.broadcasted_iota(jnp.int32, (LANES, LANES), 1) < SSM_STATE
    block_diag = bd_r == bd_c
    hm_b = [jnp.where(lo_half, 1.0, 0.0).astype(BF16), jnp.where(lo_half, 0.0, 1.0).astype(BF16)]

    ys = []
    ssq = jnp.zeros((q, 1), F32)
    for pr in range(N_PAIRS):
        g = pr // (N_PAIRS // SSM_GROUPS)
        x_pair = xc[:, pr * LANES:(pr + 1) * LANES]
        x_b = x_pair.astype(BF16)
        ms = []
        for hh in range(HEADS_PER_LANE_TILE):
            h = pr * HEADS_PER_LANE_TILE + hh
            seg = acum[:, h:h + 1] - acum_t[h:h + 1, :]
            lmat = jnp.where(causal, jnp.exp(seg), 0.0)
            ms.append((cb_g[g] * lmat * dt_t[h:h + 1, :]).astype(BF16))
        h0 = pr * HEADS_PER_LANE_TILE
        acol = jnp.where(lo_half, acum[:, h0:h0 + 1], acum[:, h0 + 1:h0 + 2])
        dcol = jnp.where(lo_half, dt[:, h0:h0 + 1], dt[:, h0 + 1:h0 + 2])
        aend = jnp.where(lo_half, a_end[:, h0:h0 + 1], a_end[:, h0 + 1:h0 + 2])
        s_pair = state[pr]
        y_diag = _dot(jnp.concatenate(ms, axis=1), jnp.concatenate([x_b * hm_b[0], x_b * hm_b[1]], axis=0))
        y_off = _dot((cdup[g] * jnp.exp(acol)).astype(BF16), s_pair.astype(BF16))
        new = _dot_tn((bdup[g] * (jnp.exp(aend - acol) * dcol)).astype(BF16), x_b)
        state[pr] = s_pair * jnp.exp(aend) + jnp.where(block_diag, new, 0.0)
        y = y_diag + y_off + dskip_ref[:, pr * LANES:(pr + 1) * LANES] * x_pair
        yz = y * _silu(z_ref[0, :, pr * LANES:(pr + 1) * LANES])
        ssq = ssq + jnp.sum(yz * yz, axis=1, keepdims=True)
        ys.append(yz)

    inv = lax.rsqrt(ssq * (1.0 / D_SSM) + NORM_EPS)
    for pr in range(N_PAIRS):
        y_ref[0, :, pr * LANES:(pr + 1) * LANES] = (
            ys[pr] * inv * g_ref[:, pr * LANES:(pr + 1) * LANES]).astype(BF16)
    sfin_ref[0] = state[...]


def _ssd(xbc, z, dt, conv0, s0, conv_w, conv_b, a_log_pad, dskip_lanes, g_ssm):
    b, l, _ = xbc.shape
    q = min(SSD_Q, l)
    assert l % q == 0 and q % SUBLANES == 0
    tri = (jnp.arange(q)[None, :] <= jnp.arange(q)[:, None]).astype(BF16)
    seq = lambda n: pl.BlockSpec((1, q, n), lambda bi, ci: (bi, ci, 0))
    per_b = lambda a: pl.BlockSpec((1,) + a.shape[1:], lambda bi, ci: (bi,) + (0,) * (a.ndim - 1))
    full = lambda a: pl.BlockSpec(a.shape, lambda bi, ci: (0,) * a.ndim)
    kernel = functools.partial(_ssd_kernel, q=q)
    return pl.pallas_call(
        kernel,
        grid=(b, l // q),
        in_specs=[seq(CONV_DIM), seq(D_SSM), seq(LANES), per_b(conv0), per_b(s0),
                  full(conv_w), full(conv_b), full(a_log_pad), full(dskip_lanes), full(g_ssm), full(tri)],
        out_specs=(seq(D_SSM), per_b(s0)),
        out_shape=(jax.ShapeDtypeStruct((b, l, D_SSM), BF16), jax.ShapeDtypeStruct(s0.shape, F32)),
        scratch_shapes=[pltpu.VMEM((SUBLANES + q, CONV_DIM), F32),
                        pltpu.VMEM((N_PAIRS, LANES, LANES), F32)],
        compiler_params=pltpu.CompilerParams(
            dimension_semantics=("arbitrary", "arbitrary"), vmem_limit_bytes=VMEM_LIMIT),
        name="ssd",
    )(xbc, z, dt, conv0, s0, conv_w, conv_b, a_log_pad, dskip_lanes, g_ssm, tri)


def _state_to_pairs(st):
    b = st.shape[0]
    s5 = st.reshape(b, N_PAIRS, HEADS_PER_LANE_TILE, SSM_HEAD_DIM, SSM_STATE)
    eye = jnp.eye(HEADS_PER_LANE_TILE, dtype=st.dtype)
    return jnp.einsum('bqipn,ij->bqinjp', s5, eye).reshape(b, N_PAIRS, LANES, LANES)


def _pairs_to_state(sp):
    b = sp.shape[0]
    s6 = sp.reshape(b, N_PAIRS, HEADS_PER_LANE_TILE, SSM_STATE, HEADS_PER_LANE_TILE, SSM_HEAD_DIM)
    diag = jnp.stack([s6[:, :, i, :, i, :] for i in range(HEADS_PER_LANE_TILE)], axis=2)
    return jnp.swapaxes(diag, -1, -2).reshape(b, N_SSM_HEADS, SSM_HEAD_DIM, SSM_STATE)


def _gelu_tanh(x):
    return 0.5 * x * (1.0 + jnp.tanh(math.sqrt(2.0 / math.pi) * (x + 0.044715 * (x * x * x))))


def _out_ffn_kernel(x_ref, attn_ref, ys_ref, fc0_ref, ga_ref, gpost_ref, gpre_ref, gfpost_ref,
                    woa_ref, wos_ref, wgu_ref, wd_ref, cw_ref,
                    y_ref, fcn_ref, h2_buf, acc, carry, g_buf, u_buf, *, n_seq, lt):
    t = pl.program_id(1)

    @pl.when(t == 0)
    def _():
        carry[...] = fc0_ref[0]

    an = _rmsnorm(attn_ref[...], ga_ref[...]).astype(BF16)
    m = _dot(an, woa_ref[...]) + _dot(ys_ref[...], wos_ref[...])
    x1 = x_ref[...] + _rmsnorm(m, gpost_ref[...])
    y_ref[...] = x1
    h2_buf[...] = _rmsnorm(x1, gpre_ref[...]).astype(BF16)
    row = lax.broadcasted_iota(jnp.int32, (lt, FFN_F), 0)

    def cols(ci, base=0):
        start = base + ci * FFN_F
        return pl.ds(start if isinstance(ci, int) else pl.multiple_of(start, FFN_F), FFN_F)

    def stage1(ci, slot):
        h2 = h2_buf[...]
        g_buf[slot] = _dot(h2, wgu_ref[:, cols(ci)])
        u_buf[slot] = _dot(h2, wgu_ref[:, cols(ci, D_FF)])

    def stage2(ci, slot):
        gate = g_buf[slot]
        cw = cw_ref[:, cols(ci)]
        acts = []
        for s in range(n_seq):
            gs = gate[s * lt:(s + 1) * lt]
            prev = carry[ci, s * SUBLANES:(s + 1) * SUBLANES, :]
            p1 = prev[SUBLANES - 1:SUBLANES, :]
            p2 = prev[SUBLANES - 2:SUBLANES - 1, :]
            g1 = jnp.where(row == 0, p1, pltpu.roll(gs, 1, axis=0))
            g2 = jnp.where(row == 0, p2, jnp.where(row == 1, p1, pltpu.roll(gs, 2, axis=0)))
            gc = cw[3:4, :] + g2 * cw[0:1, :] + g1 * cw[1:2, :] + gs * cw[2:3, :]
            carry[ci, s * SUBLANES:(s + 1) * SUBLANES, :] = gs[lt - SUBLANES:lt]
            acts.append(_gelu_tanh(gc))
        act = acts[0] if n_seq == 1 else jnp.concatenate(acts, axis=0)
        acc[...] += _dot((act * u_buf[slot]).astype(BF16), wd_ref[cols(ci), :])

    acc[...] = jnp.zeros_like(acc)
    stage1(0, 0)

    def body(p, _):
        c = 2 * p + 1
        stage1(c, 1)
        stage2(c - 1, 0)
        stage1(c + 1, 0)
        stage2(c, 1)
        return 0

    assert FFN_NC % 2 == 1
    lax.fori_loop(0, FFN_NC // 2, body, 0)
    stage2(FFN_NC - 1, 0)
    y_ref[...] = y_ref[...] + _rmsnorm(acc[...], gfpost_ref[...])
    fcn_ref[0] = carry[...]


def _out_ffn(x2d, attn2d, ys2d, fc0, n_seq, lt, gains, weights):
    t = x2d.shape[0]
    tm = n_seq * lt
    n_groups = fc0.shape[0]
    tiles = t // (tm * n_groups)
    assert tiles * tm * n_groups == t
    row = lambda n: pl.BlockSpec((tm, n), lambda gi, ti: (gi * tiles + ti, 0))
    full = lambda a: pl.BlockSpec(a.shape, lambda gi, ti: (0,) * a.ndim, pipeline_mode=pl.Buffered(1))
    fc_spec = pl.BlockSpec((1,) + fc0.shape[1:], lambda gi, ti: (gi, 0, 0, 0))
    kernel = functools.partial(_out_ffn_kernel, n_seq=n_seq, lt=lt)
    return pl.pallas_call(
        kernel,
        grid=(n_groups, tiles),
        in_specs=[row(D_MODEL), row(D_ATT), row(D_SSM), fc_spec] + [full(a) for a in gains]
                 + [full(a) for a in weights],
        out_specs=(row(D_MODEL), fc_spec),
        out_shape=(jax.ShapeDtypeStruct((t, D_MODEL), F32), jax.ShapeDtypeStruct(fc0.shape, F32)),
        scratch_shapes=[pltpu.VMEM((tm, D_MODEL), BF16), pltpu.VMEM((tm, D_MODEL), F32),
                        pltpu.VMEM(fc0.shape[1:], F32), pltpu.VMEM((2, tm, FFN_F), F32),
                        pltpu.VMEM((2, tm, FFN_F), F32)],
        compiler_params=pltpu.CompilerParams(
            dimension_semantics=("arbitrary", "arbitrary"), vmem_limit_bytes=VMEM_LIMIT),
        name="out_ffn",
    )(x2d, attn2d, ys2d, fc0, *gains, *weights)


def _ffn_state_to_chunks(st, n_seq):
    b = st.shape[0]
    s = st.reshape(b // n_seq, n_seq, FFN_CONV - 1, FFN_NC, FFN_F)
    s = jnp.pad(s, ((0, 0), (0, 0), (SUBLANES - (FFN_CONV - 1), 0), (0, 0), (0, 0)))
    return jnp.transpose(s, (0, 3, 1, 2, 4)).reshape(b // n_seq, FFN_NC, n_seq * SUBLANES, FFN_F)


def _chunks_to_ffn_state(ch, n_seq):
    g = ch.shape[0]
    s = ch.reshape(g, FFN_NC, n_seq, SUBLANES, FFN_F)[:, :, :, SUBLANES - (FFN_CONV - 1):, :]
    return jnp.transpose(s, (0, 2, 3, 1, 4)).reshape(g * n_seq, FFN_CONV - 1, D_FF)


def _layer(x, caches, ssm_h0, ssm_conv_prev, ffn_conv_prev, p, is_prompt):
    b, l, _ = x.shape
    t = b * l
    x2d = x.reshape(t, D_MODEL)
    r3 = lambda a: a.reshape(b, l, a.shape[-1])
    proj_w = (p['g_mix_pre'], p['w_main'], p['w_kt'], p['w_dt'], p['dt_bias'])
    if is_prompt:
        qb, kb, vb, kt, vt, z, xbc, dt = _in_proj(x, *proj_w, kv_transposed=True)
        attn = _attn_prompt(qb, kb, vb)
        k, v = jnp.transpose(kt, (0, 3, 1, 2)), jnp.transpose(vt, (0, 3, 1, 2))
    else:
        qb, kb, vb, k, v, z, xbc, dt = _in_proj(x2d[None], *proj_w, kv_transposed=False)
        attn = _attn_sample(r3(qb), r3(kb), r3(vb), *caches)

    conv0 = jnp.pad(ssm_conv_prev, ((0, 0), (SUBLANES - (SSM_CONV - 1), 0), (0, 0)))
    l_ssd = l if l % SSD_Q == 0 else SSD_Q * pl.cdiv(l, SSD_Q)
    pad_t = lambda a: jnp.pad(r3(a), ((0, 0), (0, l_ssd - l), (0, 0)))
    ys, s_fin = _ssd(pad_t(xbc), pad_t(z), pad_t(dt), conv0, _state_to_pairs(ssm_h0),
                     p['ssm_conv_w'], p['ssm_conv_b'], p['a_log'], p['d_skip'], p['g_ssm_out'])
    ys = ys[:, :l, :]
    ssm_conv_new = r3(xbc)[:, l - (SSM_CONV - 1):, :]

    if is_prompt:
        n_seq, lt = 1, min(FFN_TM, l)
    else:
        n_seq, lt = b, l
    fc0 = _ffn_state_to_chunks(ffn_conv_prev, n_seq)
    gains = (p['g_attn_out'], p['g_mix_post'], p['g_ffn_pre'], p['g_ffn_post'])
    weights = (p['w_out_a'], p['w_out_s'], p['w_gu'], p['w_down'], p['ffn_cw'])
    y2d, fcn = _out_ffn(x2d, attn.reshape(t, D_ATT), ys.reshape(t, D_SSM), fc0, n_seq, lt, gains, weights)
    return (y2d.reshape(b, l, D_MODEL), k.reshape(b, l, N_ATT_HEADS, ATT_HEAD_DIM),
            v.reshape(b, l, N_ATT_HEADS, ATT_HEAD_DIM), _pairs_to_state(s_fin), ssm_conv_new,
            _chunks_to_ffn_state(fcn, n_seq))


def _prep_params(i, g_mix_pre, g_mix_post, w_in, ssm_conv_w, ssm_conv_b, dt_bias, a_log, d_skip,
                 g_ssm_out, g_attn_out, w_out, g_ffn_pre, g_ffn_post, w_up, ffn_conv_w, ffn_conv_b, w_down):
    row = lambda a: a[i].reshape(1, -1).astype(F32)
    pad_lanes = lambda a: jnp.pad(a, ((0, 0), (0, LANES - a.shape[1])))
    wi = w_in[i]
    ffn_cw = jnp.concatenate([ffn_conv_w[i], ffn_conv_b[i][None, :],
                              jnp.zeros((SUBLANES - FFN_CONV - 1, D_FF), F32)], axis=0)
    return {
        'g_mix_pre': row(g_mix_pre), 'g_mix_post': row(g_mix_post),
        'g_ffn_pre': row(g_ffn_pre), 'g_ffn_post': row(g_ffn_post),
        'g_attn_out': row(g_attn_out), 'g_ssm_out': row(g_ssm_out),
        'w_main': wi[:, :D_MAIN_PROJ].astype(BF16),
        'w_kt': wi[:, D_ATT:2 * D_ATT].T.astype(BF16),
        'w_dt': pad_lanes(wi[:, D_MAIN_PROJ:]).astype(BF16),
        'dt_bias': pad_lanes(row(dt_bias)),
        'ssm_conv_w': ssm_conv_w[i].astype(F32), 'ssm_conv_b': row(ssm_conv_b),
        'a_log': pad_lanes(row(a_log)),
        'd_skip': jnp.repeat(d_skip[i].astype(F32), SSM_HEAD_DIM).reshape(1, D_SSM),
        'w_out_a': w_out[i][:D_ATT].astype(BF16), 'w_out_s': w_out[i][D_ATT:].astype(BF16),
        'w_gu': w_up[i].astype(BF16),
        'w_down': w_down[i].astype(BF16),
        'ffn_cw': ffn_cw,
    }


def kernel(x_prompt, x_sample, cache_k, cache_v, state_ssm, state_ssm_conv, state_ffn_conv, g_mix_pre, g_mix_post, w_in, ssm_conv_w, ssm_conv_b, dt_bias, a_log, d_skip, g_ssm_out, g_attn_out, w_out, g_ffn_pre, g_ffn_post, w_up, ffn_conv_w, ffn_conv_b, w_down):
    depth = w_in.shape[0]
    bp = x_prompt.shape[0]
    dtp = x_prompt.dtype
    zh = jnp.zeros((bp, N_SSM_HEADS, SSM_HEAD_DIM, SSM_STATE), dtp)
    zcs = jnp.zeros((bp, SSM_CONV - 1, CONV_DIM), dtp)
    zcf = jnp.zeros((bp, FFN_CONV - 1, D_FF), dtp)
    y_p, y_s = x_prompt, x_sample
    outs_p, outs_s = [], []
    for i in range(depth):
        p = _prep_params(i, g_mix_pre, g_mix_post, w_in, ssm_conv_w, ssm_conv_b, dt_bias, a_log, d_skip,
                         g_ssm_out, g_attn_out, w_out, g_ffn_pre, g_ffn_post, w_up, ffn_conv_w, ffn_conv_b,
                         w_down)
        rp = _layer(y_p, None, zh, zcs, zcf, p, True)
        rs = _layer(y_s, (cache_k, cache_v, i), state_ssm[i], state_ssm_conv[i], state_ffn_conv[i], p, False)
        y_p, y_s = rp[0], rs[0]
        outs_p.append(rp[1:])
        outs_s.append(rs[1:])
    stack = lambda outs, j: jnp.stack([o[j] for o in outs])
    return (y_p, y_s) + tuple(stack(outs_p, j) for j in range(5)) + tuple(stack(outs_s, j) for j in range(5))
```

```python
import functools
import math

import jax
import jax.numpy as jnp
from jax import lax
from jax.experimental import pallas as pl
from jax.experimental.pallas import tpu as pltpu

F32 = jnp.float32
BF16 = jnp.bfloat16

D_MODEL = 1024
D_ATT = 512
N_ATT_HEADS = 8
ATT_HEAD_DIM = 64
D_SSM = 512
N_SSM_HEADS = 8
SSM_HEAD_DIM = 64
SSM_STATE = 64
SSM_GROUPS = 2
SSM_CONV = 4
CONV_DIM = D_SSM + 2 * SSM_GROUPS * SSM_STATE
D_FF = 2816
FFN_CONV = 3
NORM_EPS = 1e-6
D_MAIN_PROJ = 3 * D_ATT + D_SSM + CONV_DIM
ATT_SCALE = ATT_HEAD_DIM ** -0.5
LOG2E = math.log2(math.e)

LANES = 128
SUBLANES = 8
HEADS_PER_LANE_TILE = LANES // ATT_HEAD_DIM
N_PAIRS = N_SSM_HEADS // HEADS_PER_LANE_TILE

PROJ_TM = 512
ATT_TQ = 256
ATT_TK = 128
ATT_CACHE_CHUNK = 512
SSD_Q = 128
FFN_TM = 512
FFN_F = 256
FFN_ROWS = 128
FFN_NC = D_FF // FFN_F
VMEM_LIMIT = 56 * 1024 * 1024


def _rmsnorm(x, g):
    y = x * lax.rsqrt(jnp.mean(x * x, axis=-1, keepdims=True) + NORM_EPS)
    return y * g


def _softplus(x):
    return jnp.maximum(x, 0.0) + jnp.log1p(jnp.exp(-jnp.abs(x)))


def _silu(x):
    return x * (1.0 / (1.0 + jnp.exp(-x)))


def _dot(a, b):
    return jnp.dot(a, b, preferred_element_type=F32)


def _dot_nt(a, b):
    return lax.dot_general(a, b, (((1,), (1,)), ((), ())), preferred_element_type=F32)


def _dot_tn(a, b):
    return lax.dot_general(a, b, (((0,), (0,)), ((), ())), preferred_element_type=F32)


def _split2(x):
    hi = x.astype(BF16)
    lo = (x - hi.astype(F32)).astype(BF16)
    return hi, lo


def _split3(x):
    hi = x.astype(BF16)
    r1 = x - hi.astype(F32)
    mid = r1.astype(BF16)
    lo = (r1 - mid.astype(F32)).astype(BF16)
    return hi, mid, lo


def _in_proj_kernel(x_ref, g_ref, w_ref, wkt_ref, wdt_ref, dtb_ref,
                    qb_ref, kb_ref, vb_ref, k_ref, v_ref, z_ref, xbc_ref, dt_ref, *, kv_transposed):
    h = _rmsnorm(x_ref[0], g_ref[...]).astype(BF16)

    def proj(lo, hi):
        return _dot(h, w_ref[:, lo:hi])

    qb_ref[0] = (proj(0, D_ATT) * ATT_SCALE).astype(BF16)
    v = proj(2 * D_ATT, 3 * D_ATT)
    vb_ref[0] = v.astype(BF16)
    if kv_transposed:
        tm = v.shape[0]
        kt = _dot_nt(wkt_ref[...], h)
        kb_ref[0] = kt.astype(BF16)
        k_ref[0] = kt.reshape(N_ATT_HEADS, ATT_HEAD_DIM, tm)
        v_ref[0] = v.T.reshape(N_ATT_HEADS, ATT_HEAD_DIM, tm)
    else:
        k = proj(D_ATT, 2 * D_ATT)
        kb_ref[0] = k.astype(BF16)
        k_ref[0] = _lanes_to_heads(k)
        v_ref[0] = _lanes_to_heads(v)
    z_ref[0] = proj(3 * D_ATT, 3 * D_ATT + D_SSM)
    xbc_ref[0] = proj(3 * D_ATT + D_SSM, D_MAIN_PROJ)
    dt_ref[0] = _softplus(_dot(h, wdt_ref[...]) + dtb_ref[...])


def _in_proj(x, g, w_main, w_kt, w_dt, dt_bias, kv_transposed):
    b, l, _ = x.shape
    tm = min(PROJ_TM, l)
    assert l % tm == 0
    row = lambda n: pl.BlockSpec((1, tm, n), lambda bi, i: (bi, i, 0))
    full = lambda a: pl.BlockSpec(a.shape, lambda bi, i: (0,) * a.ndim)
    sds = jax.ShapeDtypeStruct
    if kv_transposed:
        kb_shape, kb_spec = sds((b, D_ATT, l), BF16), pl.BlockSpec((1, D_ATT, tm), lambda bi, i: (bi, 0, i))
        kv_shape = sds((b, N_ATT_HEADS, ATT_HEAD_DIM, l), F32)
        kv_spec = pl.BlockSpec((1, N_ATT_HEADS, ATT_HEAD_DIM, tm), lambda bi, i: (bi, 0, 0, i))
    else:
        kb_shape, kb_spec = sds((b, l, D_ATT), BF16), row(D_ATT)
        kv_shape = sds((b, l, N_ATT_HEADS, ATT_HEAD_DIM), F32)
        kv_spec = pl.BlockSpec((1, tm, N_ATT_HEADS, ATT_HEAD_DIM), lambda bi, i: (bi, i, 0, 0))
    out_shape = (sds((b, l, D_ATT), BF16), kb_shape, sds((b, l, D_ATT), BF16), kv_shape, kv_shape,
                 sds((b, l, D_SSM), F32), sds((b, l, CONV_DIM), F32), sds((b, l, LANES), F32))
    return pl.pallas_call(
        functools.partial(_in_proj_kernel, kv_transposed=kv_transposed),
        grid=(b, l // tm),
        in_specs=[row(D_MODEL), full(g), full(w_main), full(w_kt), full(w_dt), full(dt_bias)],
        out_specs=(row(D_ATT), kb_spec, row(D_ATT), kv_spec, kv_spec,
                   row(D_SSM), row(CONV_DIM), row(LANES)),
        out_shape=out_shape,
        compiler_params=pltpu.CompilerParams(
            dimension_semantics=("arbitrary", "arbitrary"), vmem_limit_bytes=VMEM_LIMIT),
        name="in_proj",
    )(x, g, w_main, w_kt, w_dt, dt_bias)


def _sb_weights(s, c, u2, mask):
    lk = -(jnp.maximum(s, 0.0) + jnp.log(1.0 + jnp.exp(-jnp.abs(s))))
    if mask is not None:
        lk = jnp.where(mask, lk, 0.0)
    hi, lo = _split2(lk)
    r = _dot(jnp.concatenate([hi, lo], axis=1), u2) + c
    w = jnp.exp(s + r)
    if mask is not None:
        w = jnp.where(mask, w, 0.0)
    return w, c + jnp.sum(lk, axis=1, keepdims=True)


def _head_lane_masks(n_lanes, dtype):
    lane = lax.broadcasted_iota(jnp.int32, (1, n_lanes), 1)
    return [jnp.where((lane >= h * ATT_HEAD_DIM) & (lane < (h + 1) * ATT_HEAD_DIM), 1.0, 0.0).astype(dtype)
            for h in range(n_lanes // ATT_HEAD_DIM)]


def _attn_prompt_kernel(q_ref, kt_ref, v_ref, u2x_ref, o_ref, s_buf, hl_buf, mask_buf, qs_buf, vcat_buf,
                        *, tq, nq):
    tk = tq // 2
    hm = _head_lane_masks(LANES, BF16)
    n_items = nq * (nq + 1) // 2
    big = 1e30
    causal = lax.broadcasted_iota(jnp.int32, (tq, tq), 1) < lax.broadcasted_iota(jnp.int32, (tq, tq), 0)
    mask_buf[0] = jnp.full((tq, tq), -jnp.inf, F32)
    mask_buf[1] = jnp.where(causal, -jnp.inf, big)
    for i in range(nq):
        qn = -q_ref[0, i * tq:(i + 1) * tq, :]
        vsb = v_ref[0, i * tq:(i + 1) * tq, :]
        for h in range(HEADS_PER_LANE_TILE):
            qs_buf[i, h * tq:(h + 1) * tq, :] = qn * hm[h]
            for n, half in enumerate((1, 0)):
                r0 = (n * HEADS_PER_LANE_TILE + h) * tk
                vcat_buf[i, r0:r0 + tk, :] = vsb[half * tk:(half + 1) * tk] * hm[h]

    def advance(i, j):
        last = j == 0
        return jnp.where(last, i + 1, i), jnp.where(last, i + 1, j - 1)

    def row_off(idx):
        return pl.multiple_of(jnp.minimum(idx, nq - 1) * tq, tq)

    def stage_a(i, j, slot):
        kb = kt_ref[0, :, pl.ds(row_off(j), tq)]
        t = _dot(qs_buf[jnp.minimum(i, nq - 1)], kb)
        floor = mask_buf[(i == j).astype(jnp.int32)]
        for h in range(HEADS_PER_LANE_TILE):
            th = jnp.maximum(t[h * tq:(h + 1) * tq], floor)
            s_buf[slot, h * tq:(h + 1) * tq, :] = th
            e = jnp.exp2(jnp.abs(th) * (-LOG2E))
            lk = jnp.minimum(th, 0.0) - jnp.log(1.0 + e)
            hi, lo = _split2(lk)
            for half in range(2):
                r0 = half * 2 * tq + h * tq
                hl_buf[slot, r0:r0 + tq, 0:tk] = hi[:, half * tk:(half + 1) * tk]
                hl_buf[slot, r0:r0 + tq, tk:tq] = lo[:, half * tk:(half + 1) * tk]

    def stage_b(i, j, slot, c, acc):
        rr = _dot(hl_buf[slot], u2x_ref[...])
        first = i == j
        c = jnp.where(first, 0.0, c)
        acc = jnp.where(first, 0.0, acc)
        ws = []
        for half in (1, 0):
            r0 = half * 2 * tq
            r = rr[r0:r0 + 2 * tq, 0:tk] + c
            th = s_buf[slot, :, half * tk:(half + 1) * tk]
            w = jnp.exp(r - th).astype(BF16)
            c = c + rr[r0:r0 + 2 * tq, tk:tq]
            ws += [w[0:tq], w[tq:2 * tq]]
        acc = acc + _dot(jnp.concatenate(ws, axis=1), vcat_buf[jnp.minimum(j, nq - 1)])
        o_ref[0, pl.ds(row_off(i), tq), :] = acc
        return c, acc

    def two_items(carry):
        ib, jb, ia, ja, c, acc = carry
        stage_a(ia, ja, 1)
        c, acc = stage_b(ib, jb, 0, c, acc)
        i2, j2 = advance(ia, ja)
        stage_a(i2, j2, 0)
        c, acc = stage_b(ia, ja, 1, c, acc)
        i3, j3 = advance(i2, j2)
        return i2, j2, i3, j3, c, acc

    pairs_per_trip = 2 if n_items % 4 == 0 else 1

    def body(p, carry):
        for _ in range(pairs_per_trip):
            carry = two_items(carry)
        return carry

    z = jnp.int32(0)
    stage_a(z, z, 0)
    init = (z, z, z + 1, z + 1, jnp.zeros((2 * tq, tk), F32), jnp.zeros((tq, LANES), F32))
    ib, jb, _, _, c, acc = lax.fori_loop(0, n_items // (2 * pairs_per_trip), body, init)
    if n_items % 2:
        stage_b(ib, jb, 0, c, acc)


def _u2_matrix(tk):
    j = jnp.arange(2 * tk)[:, None] % tk
    s = jnp.arange(tk)[None, :]
    return (j >= s).astype(BF16)


def _attn_prompt(qb, kb, vb):
    b, l, _ = qb.shape
    tq = min(ATT_TQ, l)
    assert l % tq == 0 and tq == 2 * ATT_TK
    u2x = jnp.concatenate([_u2_matrix(ATT_TK), jnp.ones((tq, ATT_TK), BF16)], axis=1)
    kernel = functools.partial(_attn_prompt_kernel, tq=tq, nq=l // tq)
    seq = pl.BlockSpec((1, l, LANES), lambda bi, hp: (bi, 0, hp))
    return pl.pallas_call(
        kernel,
        grid=(b, D_ATT // LANES),
        in_specs=[seq, pl.BlockSpec((1, LANES, l), lambda bi, hp: (bi, hp, 0)), seq,
                  pl.BlockSpec(u2x.shape, lambda bi, hp: (0, 0))],
        out_specs=seq,
        out_shape=jax.ShapeDtypeStruct((b, l, D_ATT), F32),
        scratch_shapes=[pltpu.VMEM((2, 2 * tq, tq), F32), pltpu.VMEM((2, 4 * tq, tq), BF16),
                        pltpu.VMEM((2, tq, tq), F32), pltpu.VMEM((l // tq, 2 * tq, LANES), BF16),
                        pltpu.VMEM((l // tq, 2 * tq, LANES), BF16)],
        compiler_params=pltpu.CompilerParams(
            dimension_semantics=("arbitrary", "arbitrary"), vmem_limit_bytes=VMEM_LIMIT),
        name="attn_prompt",
    )(qb, kb, vb, u2x)


def _heads_to_lanes(x):
    t = x.shape[0]
    y = jnp.swapaxes(x.reshape(t // SUBLANES, SUBLANES, N_ATT_HEADS, ATT_HEAD_DIM), 1, 2)
    return jnp.concatenate([y[:, v].reshape(t, ATT_HEAD_DIM) for v in range(N_ATT_HEADS)], axis=1)


def _lanes_to_heads(x):
    t = x.shape[0]
    parts = []
    for g in range(D_ATT // LANES):
        a = x[:, g * LANES:(g + 1) * LANES]
        b = pltpu.roll(a, ATT_HEAD_DIM, axis=1)
        parts += [a.reshape(t // SUBLANES, SUBLANES, LANES), b.reshape(t // SUBLANES, SUBLANES, LANES)]
    y = jnp.swapaxes(jnp.stack(parts, axis=1), 1, 2).reshape(t, N_ATT_HEADS, LANES)
    return y[:, :, 0:ATT_HEAD_DIM]


def _attn_sample_kernel(q_ref, kn_ref, vn_ref, ckt_ref, cvt_ref, u2_ref, o_ref, c_scr, acc_scr, *, l, tk, nch):
    ch = pl.program_id(1)
    m = N_ATT_HEADS * l
    hm = _head_lane_masks(D_ATT, BF16)
    q = q_ref[0]
    qs = jnp.concatenate([q * mk for mk in hm], axis=0)
    u2 = u2_ref[...]

    @pl.when(ch == 0)
    def _():
        pad = jnp.zeros((tk - l, D_ATT), BF16)
        kn = jnp.concatenate([kn_ref[0], pad], axis=0)
        vn = jnp.concatenate([vn_ref[0], pad], axis=0)
        row = jnp.concatenate([lax.broadcasted_iota(jnp.int32, (l, tk), 0)] * N_ATT_HEADS, axis=0)
        col = lax.broadcasted_iota(jnp.int32, (m, tk), 1)
        w, c = _sb_weights(_dot_nt(qs, kn), jnp.zeros((m, 1), F32), u2, col < row)
        c_scr[...] = c
        acc_scr[...] = _dot(w.astype(BF16), vn)

    chunk = ckt_ref.shape[-1]
    kt = ckt_ref[0, 0].reshape(D_ATT, chunk).astype(BF16)
    vt = cvt_ref[0, 0].reshape(D_ATT, chunk).astype(BF16)
    c, acc = c_scr[...], acc_scr[...]
    for t in reversed(range(chunk // tk)):
        w, c = _sb_weights(_dot(qs, kt[:, t * tk:(t + 1) * tk]), c, u2, None)
        acc = acc + _dot_nt(w.astype(BF16), vt[:, t * tk:(t + 1) * tk])
    c_scr[...] = c
    acc_scr[...] = acc

    @pl.when(ch == nch - 1)
    def _():
        acc = acc_scr[...]
        hmf = _head_lane_masks(D_ATT, F32)
        out = acc[0:l] * hmf[0]
        for h in range(1, N_ATT_HEADS):
            out = out + acc[h * l:(h + 1) * l] * hmf[h]
        o_ref[0] = out


def _attn_sample(qb, kb_new, vb_new, cache_k, cache_v, layer):
    b, l, _ = qb.shape
    past = cache_k.shape[2]
    cache_k = jnp.transpose(cache_k, (0, 1, 3, 4, 2))
    cache_v = jnp.transpose(cache_v, (0, 1, 3, 4, 2))
    tk = ATT_TK
    chunk = min(ATT_CACHE_CHUNK, past)
    assert past % chunk == 0 and chunk % tk == 0 and l <= tk and l % 16 == 0
    nch = past // chunk
    u2 = _u2_matrix(tk)
    kernel = functools.partial(_attn_sample_kernel, l=l, tk=tk, nch=nch)
    new = pl.BlockSpec((1, l, D_ATT), lambda bi, ch: (bi, 0, 0))
    old = pl.BlockSpec((1, 1, N_ATT_HEADS, ATT_HEAD_DIM, chunk), lambda bi, ch: (layer, bi, 0, 0, nch - 1 - ch))
    return pl.pallas_call(
        kernel,
        grid=(b, nch),
        in_specs=[new, new, new, old, old, pl.BlockSpec(u2.shape, lambda bi, ch: (0, 0))],
        out_specs=new,
        out_shape=jax.ShapeDtypeStruct((b, l, D_ATT), F32),
        scratch_shapes=[pltpu.VMEM((N_ATT_HEADS * l, 1), F32), pltpu.VMEM((N_ATT_HEADS * l, D_ATT), F32)],
        compiler_params=pltpu.CompilerParams(
            dimension_semantics=("arbitrary", "arbitrary"), vmem_limit_bytes=VMEM_LIMIT),
        name="attn_sample",
    )(qb, kb_new, vb_new, cache_k, cache_v, u2)


def _ssd_kernel(xbc_ref, z_ref, dt_ref, conv0_ref, s0_ref, cw_ref, cb_ref, alog_ref, dskip_ref, g_ref,
                tri_ref, y_ref, sfin_ref, cbuf, state, *, q):
    c = pl.program_id(1)

    @pl.when(c == 0)
    def _():
        cbuf[0:SUBLANES, :] = conv0_ref[0]
        zero = jnp.zeros((SSM_HEAD_DIM, SSM_STATE), F32)
        for pr in range(N_PAIRS):
            top = jnp.concatenate([s0_ref[0, 2 * pr], zero], axis=1)
            bot = jnp.concatenate([zero, s0_ref[0, 2 * pr + 1]], axis=1)
            state[pr] = jnp.concatenate([top, bot], axis=0)

    cbuf[SUBLANES:SUBLANES + q, :] = xbc_ref[0]
    xc = cb_ref[...]
    for i in range(SSM_CONV):
        xc = xc + cbuf[pl.ds(SUBLANES - (SSM_CONV - 1) + i, q), :] * cw_ref[i:i + 1, :]
    xc = _silu(xc)
    cbuf[0:SUBLANES, :] = cbuf[q:q + SUBLANES, :]

    lane = lax.broadcasted_iota(jnp.int32, (1, LANES), 1)
    lo_half = lane < SSM_STATE
    bmat = xc[:, D_SSM:D_SSM + LANES]
    cmat = xc[:, D_SSM + LANES:D_SSM + 2 * LANES]
    b_sw = pltpu.roll(bmat, SSM_STATE, axis=1)
    c_sw = pltpu.roll(cmat, SSM_STATE, axis=1)
    bdup = [jnp.where(lo_half, bmat, b_sw), jnp.where(lo_half, b_sw, bmat)]
    cdup = [jnp.where(lo_half, cmat, c_sw), jnp.where(lo_half, c_sw, cmat)]
    gmask = [jnp.where(lo_half, 1.0, 0.0), jnp.where(lo_half, 0.0, 1.0)]
    bmat_b = bmat.astype(BF16)
    cb_g = [_dot_nt((cmat * gmask[g]).astype(BF16), bmat_b) for g in range(SSM_GROUPS)]

    dt = dt_ref[0]
    da = dt * (-jnp.exp(alog_ref[...]))
    tri = tri_ref[...]
    hi, mid, lo = _split3(da)
    acum = _dot(tri, hi) + _dot(tri, mid) + _dot(tri, lo)
    acum_t = acum.T
    dt_t = dt.T
    a_end = acum[q - 1:q, :]
    trow = lax.broadcasted_iota(jnp.int32, (q, q), 0)
    tcol = lax.broadcasted_iota(jnp.int32, (q, q), 1)
    causal = tcol <= trow
    bd_r = lax.broadcasted_iota(jnp.int32, (LANES, LANES), 0) < SSM_STATE
    bd_c = lax.broadcasted_iota(jnp.int32, (LANES, LANES), 1) < SSM_STATE
    block_diag = bd_r == bd_c
    hm_b = [jnp.where(lo_half, 1.0, 0.0).astype(BF16), jnp.where(lo_half, 0.0, 1.0).astype(BF16)]

    ys = []
    ssq = jnp.zeros((q, 1), F32)
    for pr in range(N_PAIRS):
        g = pr // (N_PAIRS // SSM_GROUPS)
        x_pair = xc[:, pr * LANES:(pr + 1) * LANES]
        x_b = x_pair.astype(BF16)
        ms = []
        for hh in range(HEADS_PER_LANE_TILE):
            h = pr * HEADS_PER_LANE_TILE + hh
            seg = acum[:, h:h + 1] - acum_t[h:h + 1, :]
            lmat = jnp.where(causal, jnp.exp(seg), 0.0)
            ms.append((cb_g[g] * lmat * dt_t[h:h + 1, :]).astype(BF16))
        h0 = pr * HEADS_PER_LANE_TILE
        acol = jnp.where(lo_half, acum[:, h0:h0 + 1], acum[:, h0 + 1:h0 + 2])
        dcol = jnp.where(lo_half, dt[:, h0:h0 + 1], dt[:, h0 + 1:h0 + 2])
        aend = jnp.where(lo_half, a_end[:, h0:h0 + 1], a_end[:, h0 + 1:h0 + 2])
        s_pair = state[pr]
        y_diag = _dot(jnp.concatenate(ms, axis=1), jnp.concatenate([x_b * hm_b[0], x_b * hm_b[1]], axis=0))
        y_off = _dot_nt((cdup[g] * jnp.exp(acol)).astype(BF16), s_pair.astype(BF16))
        new = _dot_tn(x_b, (bdup[g] * (jnp.exp(aend - acol) * dcol)).astype(BF16))
        state[pr] = s_pair * jnp.exp(aend) + jnp.where(block_diag, new, 0.0)
        y = y_diag + y_off + dskip_ref[:, pr * LANES:(pr + 1) * LANES] * x_pair
        yz = y * _silu(z_ref[0, :, pr * LANES:(pr + 1) * LANES])
        ssq = ssq + jnp.sum(yz * yz, axis=1, keepdims=True)
        ys.append(yz)

    inv = lax.rsqrt(ssq * (1.0 / D_SSM) + NORM_EPS)
    for pr in range(N_PAIRS):
        y_ref[0, :, pr * LANES:(pr + 1) * LANES] = (
            ys[pr] * inv * g_ref[:, pr * LANES:(pr + 1) * LANES]).astype(BF16)

    @pl.when(c == pl.num_programs(1) - 1)
    def _():
        for pr in range(N_PAIRS):
            sfin_ref[0, 2 * pr] = state[pr, 0:SSM_HEAD_DIM, 0:SSM_STATE]
            sfin_ref[0, 2 * pr + 1] = state[pr, SSM_HEAD_DIM:LANES, SSM_STATE:LANES]


def _ssd(xbc, z, dt, conv0, s0, conv_w, conv_b, a_log_pad, dskip_lanes, g_ssm):
    b, l, _ = xbc.shape
    q = min(SSD_Q, l)
    assert l % q == 0 and q % SUBLANES == 0
    tri = (jnp.arange(q)[None, :] <= jnp.arange(q)[:, None]).astype(BF16)
    seq = lambda n: pl.BlockSpec((1, q, n), lambda bi, ci: (bi, ci, 0))
    per_b = lambda a: pl.BlockSpec((1,) + a.shape[1:], lambda bi, ci: (bi,) + (0,) * (a.ndim - 1))
    full = lambda a: pl.BlockSpec(a.shape, lambda bi, ci: (0,) * a.ndim)
    kernel = functools.partial(_ssd_kernel, q=q)
    return pl.pallas_call(
        kernel,
        grid=(b, l // q),
        in_specs=[seq(CONV_DIM), seq(D_SSM), seq(LANES), per_b(conv0), per_b(s0),
                  full(conv_w), full(conv_b), full(a_log_pad), full(dskip_lanes), full(g_ssm), full(tri)],
        out_specs=(seq(D_SSM), per_b(s0)),
        out_shape=(jax.ShapeDtypeStruct((b, l, D_SSM), BF16), jax.ShapeDtypeStruct(s0.shape, F32)),
        scratch_shapes=[pltpu.VMEM((SUBLANES + q, CONV_DIM), F32),
                        pltpu.VMEM((N_PAIRS, LANES, LANES), F32)],
        compiler_params=pltpu.CompilerParams(
            dimension_semantics=("arbitrary", "arbitrary"), vmem_limit_bytes=VMEM_LIMIT),
        name="ssd",
    )(xbc, z, dt, conv0, s0, conv_w, conv_b, a_log_pad, dskip_lanes, g_ssm, tri)


def _gelu_tanh(x):
    return 0.5 * x * (1.0 + jnp.tanh(math.sqrt(2.0 / math.pi) * (x + 0.044715 * (x * x * x))))


def _out_ffn_kernel(x_ref, attn_ref, ys_ref, fc0_ref, ga_ref, gpost_ref, gpre_ref, gfpost_ref,
                    woa_ref, wos_ref, wgu_ref, wd_ref, cw_ref,
                    y_ref, fcn_ref, h2_buf, acc, carry, g_buf, u_buf, *, n_seq, lt):
    t = pl.program_id(1)

    @pl.when(t == 0)
    def _():
        carry[...] = fc0_ref[0]

    tm = x_ref.shape[0]
    for r0 in range(0, tm, FFN_ROWS):
        rows = slice(r0, r0 + FFN_ROWS)
        an = _rmsnorm(attn_ref[rows, :], ga_ref[...]).astype(BF16)
        m = _dot(an, woa_ref[...]) + _dot(ys_ref[rows, :], wos_ref[...])
        x1 = x_ref[rows, :] + _rmsnorm(m, gpost_ref[...])
        y_ref[rows, :] = x1
        h2_buf[rows, :] = _rmsnorm(x1, gpre_ref[...]).astype(BF16)
    row = lax.broadcasted_iota(jnp.int32, (lt, FFN_F), 0)

    def cols(ci, base=0):
        start = base + ci * FFN_F
        return pl.ds(start if isinstance(ci, int) else pl.multiple_of(start, FFN_F), FFN_F)

    def stage1(ci, slot):
        h2 = h2_buf[...]
        g_buf[slot] = _dot(h2, wgu_ref[:, cols(ci)])
        u_buf[slot] = _dot(h2, wgu_ref[:, cols(ci, D_FF)])

    def stage2(ci, slot):
        gate = g_buf[slot]
        cw = cw_ref[:, cols(ci)]
        acts = []
        for s in range(n_seq):
            gs = gate[s * lt:(s + 1) * lt]
            prev = carry[ci, s * SUBLANES:(s + 1) * SUBLANES, :]
            p1 = prev[SUBLANES - 1:SUBLANES, :]
            p2 = prev[SUBLANES - 2:SUBLANES - 1, :]
            g1 = jnp.where(row == 0, p1, pltpu.roll(gs, 1, axis=0))
            g2 = jnp.where(row == 0, p2, jnp.where(row == 1, p1, pltpu.roll(gs, 2, axis=0)))
            gc = cw[3:4, :] + g2 * cw[0:1, :] + g1 * cw[1:2, :] + gs * cw[2:3, :]
            carry[ci, s * SUBLANES:(s + 1) * SUBLANES, :] = gs[lt - SUBLANES:lt]
            acts.append(_gelu_tanh(gc))
        act = acts[0] if n_seq == 1 else jnp.concatenate(acts, axis=0)
        acc[...] += _dot((act * u_buf[slot]).astype(BF16), wd_ref[cols(ci), :])

    acc[...] = jnp.zeros_like(acc)
    stage1(0, 0)

    def body(p, _):
        c = 2 * p + 1
        stage1(c, 1)
        stage2(c - 1, 0)
        stage1(c + 1, 0)
        stage2(c, 1)
        return 0

    assert FFN_NC % 2 == 1
    lax.fori_loop(0, FFN_NC // 2, body, 0)
    stage2(FFN_NC - 1, 0)
    for r0 in range(0, tm, FFN_ROWS):
        rows = slice(r0, r0 + FFN_ROWS)
        y_ref[rows, :] = y_ref[rows, :] + _rmsnorm(acc[rows, :], gfpost_ref[...])
    fcn_ref[0] = carry[...]


def _out_ffn(x2d, attn2d, ys2d, fc0, n_seq, lt, gains, weights):
    t = x2d.shape[0]
    tm = n_seq * lt
    n_groups = fc0.shape[0]
    tiles = t // (tm * n_groups)
    assert tiles * tm * n_groups == t
    row = lambda n: pl.BlockSpec((tm, n), lambda gi, ti: (gi * tiles + ti, 0))
    full = lambda a: pl.BlockSpec(a.shape, lambda gi, ti: (0,) * a.ndim, pipeline_mode=pl.Buffered(1))
    fc_spec = pl.BlockSpec((1,) + fc0.shape[1:], lambda gi, ti: (gi, 0, 0, 0))
    kernel = functools.partial(_out_ffn_kernel, n_seq=n_seq, lt=lt)
    return pl.pallas_call(
        kernel,
        grid=(n_groups, tiles),
        in_specs=[row(D_MODEL), row(D_ATT), row(D_SSM), fc_spec] + [full(a) for a in gains]
                 + [full(a) for a in weights],
        out_specs=(row(D_MODEL), fc_spec),
        out_shape=(jax.ShapeDtypeStruct((t, D_MODEL), F32), jax.ShapeDtypeStruct(fc0.shape, F32)),
        scratch_shapes=[pltpu.VMEM((tm, D_MODEL), BF16), pltpu.VMEM((tm, D_MODEL), F32),
                        pltpu.VMEM(fc0.shape[1:], F32), pltpu.VMEM((2, tm, FFN_F), F32),
                        pltpu.VMEM((2, tm, FFN_F), F32)],
        compiler_params=pltpu.CompilerParams(
            dimension_semantics=("arbitrary", "arbitrary"), vmem_limit_bytes=VMEM_LIMIT),
        name="out_ffn",
    )(x2d, attn2d, ys2d, fc0, *gains, *weights)


def _ffn_state_to_chunks(st, n_seq):
    b = st.shape[0]
    s = st.reshape(b // n_seq, n_seq, FFN_CONV - 1, FFN_NC, FFN_F)
    s = jnp.pad(s, ((0, 0), (0, 0), (SUBLANES - (FFN_CONV - 1), 0), (0, 0), (0, 0)))
    return jnp.transpose(s, (0, 3, 1, 2, 4)).reshape(b // n_seq, FFN_NC, n_seq * SUBLANES, FFN_F)


def _chunks_to_ffn_state(ch, n_seq):
    g = ch.shape[0]
    s = ch.reshape(g, FFN_NC, n_seq, SUBLANES, FFN_F)[:, :, :, SUBLANES - (FFN_CONV - 1):, :]
    return jnp.transpose(s, (0, 2, 3, 1, 4)).reshape(g * n_seq, FFN_CONV - 1, D_FF)


def _layer(x, caches, ssm_h0, ssm_conv_prev, ffn_conv_prev, p, is_prompt):
    b, l, _ = x.shape
    t = b * l
    x2d = x.reshape(t, D_MODEL)
    r3 = lambda a: a.reshape(b, l, a.shape[-1])
    proj_w = (p['g_mix_pre'], p['w_main'], p['w_kt'], p['w_dt'], p['dt_bias'])
    if is_prompt:
        qb, kb, vb, kt, vt, z, xbc, dt = _in_proj(x, *proj_w, kv_transposed=True)
        attn = _attn_prompt(qb, kb, vb)
        k, v = jnp.transpose(kt, (0, 3, 1, 2)), jnp.transpose(vt, (0, 3, 1, 2))
    else:
        qb, kb, vb, k, v, z, xbc, dt = _in_proj(x2d[None], *proj_w, kv_transposed=False)
        attn = _attn_sample(r3(qb), r3(kb), r3(vb), *caches)

    conv0 = jnp.pad(ssm_conv_prev, ((0, 0), (SUBLANES - (SSM_CONV - 1), 0), (0, 0)))
    l_ssd = l if l % SSD_Q == 0 else SSD_Q * pl.cdiv(l, SSD_Q)
    pad_t = lambda a: jnp.pad(r3(a), ((0, 0), (0, l_ssd - l), (0, 0)))
    ys, s_fin = _ssd(pad_t(xbc), pad_t(z), pad_t(dt), conv0, ssm_h0,
                     p['ssm_conv_w'], p['ssm_conv_b'], p['a_log'], p['d_skip'], p['g_ssm_out'])
    ys = ys[:, :l, :]
    ssm_conv_new = r3(xbc)[:, l - (SSM_CONV - 1):, :]

    if is_prompt:
        n_seq, lt = 1, min(FFN_TM, l)
    else:
        n_seq, lt = b, l
    fc0 = _ffn_state_to_chunks(ffn_conv_prev, n_seq)
    gains = (p['g_attn_out'], p['g_mix_post'], p['g_ffn_pre'], p['g_ffn_post'])
    weights = (p['w_out_a'], p['w_out_s'], p['w_gu'], p['w_down'], p['ffn_cw'])
    y2d, fcn = _out_ffn(x2d, attn.reshape(t, D_ATT), ys.reshape(t, D_SSM), fc0, n_seq, lt, gains, weights)
    return (y2d.reshape(b, l, D_MODEL), k.reshape(b, l, N_ATT_HEADS, ATT_HEAD_DIM),
            v.reshape(b, l, N_ATT_HEADS, ATT_HEAD_DIM), s_fin, ssm_conv_new,
            _chunks_to_ffn_state(fcn, n_seq))


def _prep_params(i, g_mix_pre, g_mix_post, w_in, ssm_conv_w, ssm_conv_b, dt_bias, a_log, d_skip,
                 g_ssm_out, g_attn_out, w_out, g_ffn_pre, g_ffn_post, w_up, ffn_conv_w, ffn_conv_b, w_down):
    row = lambda a: a[i].reshape(1, -1).astype(F32)
    pad_lanes = lambda a: jnp.pad(a, ((0, 0), (0, LANES - a.shape[1])))
    wi = w_in[i]
    ffn_cw = jnp.concatenate([ffn_conv_w[i], ffn_conv_b[i][None, :],
                              jnp.zeros((SUBLANES - FFN_CONV - 1, D_FF), F32)], axis=0)
    return {
        'g_mix_pre': row(g_mix_pre), 'g_mix_post': row(g_mix_post),
        'g_ffn_pre': row(g_ffn_pre), 'g_ffn_post': row(g_ffn_post),
        'g_attn_out': row(g_attn_out), 'g_ssm_out': row(g_ssm_out),
        'w_main': wi[:, :D_MAIN_PROJ].astype(BF16),
        'w_kt': wi[:, D_ATT:2 * D_ATT].T.astype(BF16),
        'w_dt': pad_lanes(wi[:, D_MAIN_PROJ:]).astype(BF16),
        'dt_bias': pad_lanes(row(dt_bias)),
        'ssm_conv_w': ssm_conv_w[i].astype(F32), 'ssm_conv_b': row(ssm_conv_b),
        'a_log': pad_lanes(row(a_log)),
        'd_skip': jnp.repeat(d_skip[i].astype(F32), SSM_HEAD_DIM).reshape(1, D_SSM),
        'w_out_a': w_out[i][:D_ATT].astype(BF16), 'w_out_s': w_out[i][D_ATT:].astype(BF16),
        'w_gu': w_up[i].astype(BF16),
        'w_down': w_down[i].astype(BF16),
        'ffn_cw': ffn_cw,
    }


def kernel(x_prompt, x_sample, cache_k, cache_v, state_ssm, state_ssm_conv, state_ffn_conv, g_mix_pre, g_mix_post, w_in, ssm_conv_w, ssm_conv_b, dt_bias, a_log, d_skip, g_ssm_out, g_attn_out, w_out, g_ffn_pre, g_ffn_post, w_up, ffn_conv_w, ffn_conv_b, w_down):
    depth = w_in.shape[0]
    bp = x_prompt.shape[0]
    dtp = x_prompt.dtype
    zh = jnp.zeros((bp, N_SSM_HEADS, SSM_HEAD_DIM, SSM_STATE), dtp)
    zcs = jnp.zeros((bp, SSM_CONV - 1, CONV_DIM), dtp)
    zcf = jnp.zeros((bp, FFN_CONV - 1, D_FF), dtp)
    y_p, y_s = x_prompt, x_sample
    outs_p, outs_s = [], []
    for i in range(depth):
        p = _prep_params(i, g_mix_pre, g_mix_post, w_in, ssm_conv_w, ssm_conv_b, dt_bias, a_log, d_skip,
                         g_ssm_out, g_attn_out, w_out, g_ffn_pre, g_ffn_post, w_up, ffn_conv_w, ffn_conv_b,
                         w_down)
        rp = _layer(y_p, None, zh, zcs, zcf, p, True)
        rs = _layer(y_s, (cache_k, cache_v, i), state_ssm[i], state_ssm_conv[i], state_ffn_conv[i], p, False)
        y_p, y_s = rp[0], rs[0]
        outs_p.append(rp[1:])
        outs_s.append(rs[1:])
    stack = lambda outs, j: jnp.stack([o[j] for o in outs])
    return (y_p, y_s) + tuple(stack(outs_p, j) for j in range(5)) + tuple(stack(outs_s, j) for j in range(5))
```

```python
import functools
import math

import jax
import jax.numpy as jnp
from jax import lax
from jax.experimental import pallas as pl
from jax.experimental.pallas import tpu as pltpu

F32 = jnp.float32
BF16 = jnp.bfloat16

D_MODEL = 1024
D_ATT = 512
N_ATT_HEADS = 8
ATT_HEAD_DIM = 64
D_SSM = 512
N_SSM_HEADS = 8
SSM_HEAD_DIM = 64
SSM_STATE = 64
SSM_GROUPS = 2
SSM_CONV = 4
CONV_DIM = D_SSM + 2 * SSM_GROUPS * SSM_STATE
D_FF = 2816
FFN_CONV = 3
NORM_EPS = 1e-6
D_MAIN_PROJ = 3 * D_ATT + D_SSM + CONV_DIM
ATT_SCALE = ATT_HEAD_DIM ** -0.5
LOG2E = math.log2(math.e)
SB_EXHAUSTED = -120.0

LANES = 128
SUBLANES = 8
HEADS_PER_LANE_TILE = LANES // ATT_HEAD_DIM
N_PAIRS = N_SSM_HEADS // HEADS_PER_LANE_TILE

PROJ_TM = 512
ATT_TQ = 256
ATT_TK = 128
ATT_CACHE_CHUNK = 512
SSD_Q = 128
FFN_TM = 512
FFN_F = 256
FFN_ROWS = 128
FFN_NC = D_FF // FFN_F
VMEM_LIMIT = 56 * 1024 * 1024


def _rmsnorm(x, g):
    y = x * lax.rsqrt(jnp.mean(x * x, axis=-1, keepdims=True) + NORM_EPS)
    return y * g


def _softplus(x):
    return jnp.maximum(x, 0.0) + jnp.log1p(jnp.exp(-jnp.abs(x)))


def _silu(x):
    return x * (1.0 / (1.0 + jnp.exp(-x)))


def _dot(a, b):
    return jnp.dot(a, b, preferred_element_type=F32)


def _dot_nt(a, b):
    return lax.dot_general(a, b, (((1,), (1,)), ((), ())), preferred_element_type=F32)


def _dot_tn(a, b):
    return lax.dot_general(a, b, (((0,), (0,)), ((), ())), preferred_element_type=F32)


def _split2(x):
    hi = x.astype(BF16)
    lo = (x - hi.astype(F32)).astype(BF16)
    return hi, lo


def _split3(x):
    hi = x.astype(BF16)
    r1 = x - hi.astype(F32)
    mid = r1.astype(BF16)
    lo = (r1 - mid.astype(F32)).astype(BF16)
    return hi, mid, lo


def _in_proj_kernel(x_ref, g_ref, w_ref, wkt_ref, wdt_ref, dtb_ref,
                    qb_ref, kb_ref, vb_ref, k_ref, v_ref, z_ref, xbc_ref, dt_ref, *, kv_transposed):
    h = _rmsnorm(x_ref[0], g_ref[...]).astype(BF16)

    def proj(lo, hi):
        return _dot(h, w_ref[:, lo:hi])

    qb_ref[0] = (proj(0, D_ATT) * ATT_SCALE).astype(BF16)
    v = proj(2 * D_ATT, 3 * D_ATT)
    vb_ref[0] = v.astype(BF16)
    if kv_transposed:
        tm = v.shape[0]
        kt = _dot_nt(wkt_ref[...], h)
        kb_ref[0] = kt.astype(BF16)
        k_ref[0] = kt.reshape(N_ATT_HEADS, ATT_HEAD_DIM, tm)
        v_ref[0] = v.T.reshape(N_ATT_HEADS, ATT_HEAD_DIM, tm)
    else:
        k = proj(D_ATT, 2 * D_ATT)
        kb_ref[0] = k.astype(BF16)
        k_ref[0] = _lanes_to_heads(k)
        v_ref[0] = _lanes_to_heads(v)
    z_ref[0] = proj(3 * D_ATT, 3 * D_ATT + D_SSM)
    xbc_ref[0] = proj(3 * D_ATT + D_SSM, D_MAIN_PROJ)
    dt_ref[0] = _softplus(_dot(h, wdt_ref[...]) + dtb_ref[...])


def _in_proj(x, g, w_main, w_kt, w_dt, dt_bias, kv_transposed):
    b, l, _ = x.shape
    tm = min(PROJ_TM, l)
    assert l % tm == 0
    row = lambda n: pl.BlockSpec((1, tm, n), lambda bi, i: (bi, i, 0))
    full = lambda a: pl.BlockSpec(a.shape, lambda bi, i: (0,) * a.ndim)
    sds = jax.ShapeDtypeStruct
    if kv_transposed:
        kb_shape, kb_spec = sds((b, D_ATT, l), BF16), pl.BlockSpec((1, D_ATT, tm), lambda bi, i: (bi, 0, i))
        kv_shape = sds((b, N_ATT_HEADS, ATT_HEAD_DIM, l), F32)
        kv_spec = pl.BlockSpec((1, N_ATT_HEADS, ATT_HEAD_DIM, tm), lambda bi, i: (bi, 0, 0, i))
    else:
        kb_shape, kb_spec = sds((b, l, D_ATT), BF16), row(D_ATT)
        kv_shape = sds((b, l, N_ATT_HEADS, ATT_HEAD_DIM), F32)
        kv_spec = pl.BlockSpec((1, tm, N_ATT_HEADS, ATT_HEAD_DIM), lambda bi, i: (bi, i, 0, 0))
    out_shape = (sds((b, l, D_ATT), BF16), kb_shape, sds((b, l, D_ATT), BF16), kv_shape, kv_shape,
                 sds((b, l, D_SSM), F32), sds((b, l, CONV_DIM), F32), sds((b, l, LANES), F32))
    return pl.pallas_call(
        functools.partial(_in_proj_kernel, kv_transposed=kv_transposed),
        grid=(b, l // tm),
        in_specs=[row(D_MODEL), full(g), full(w_main), full(w_kt), full(w_dt), full(dt_bias)],
        out_specs=(row(D_ATT), kb_spec, row(D_ATT), kv_spec, kv_spec,
                   row(D_SSM), row(CONV_DIM), row(LANES)),
        out_shape=out_shape,
        compiler_params=pltpu.CompilerParams(
            dimension_semantics=("arbitrary", "arbitrary"), vmem_limit_bytes=VMEM_LIMIT),
        name="in_proj",
    )(x, g, w_main, w_kt, w_dt, dt_bias)


def _sb_weights(s, c, u2, mask):
    lk = -(jnp.maximum(s, 0.0) + jnp.log(1.0 + jnp.exp(-jnp.abs(s))))
    if mask is not None:
        lk = jnp.where(mask, lk, 0.0)
    hi, lo = _split2(lk)
    r = _dot(jnp.concatenate([hi, lo], axis=1), u2) + c
    w = jnp.exp(s + r)
    if mask is not None:
        w = jnp.where(mask, w, 0.0)
    return w, c + jnp.sum(lk, axis=1, keepdims=True)


def _head_lane_masks(n_lanes, dtype):
    lane = lax.broadcasted_iota(jnp.int32, (1, n_lanes), 1)
    return [jnp.where((lane >= h * ATT_HEAD_DIM) & (lane < (h + 1) * ATT_HEAD_DIM), 1.0, 0.0).astype(dtype)
            for h in range(n_lanes // ATT_HEAD_DIM)]


def _attn_prompt_kernel(q_ref, kt_ref, v_ref, u2x_ref, o_ref, s_buf, hl_buf, mask_buf, qs_buf, vcat_buf,
                        out_buf, *, tq, nq):
    tk = tq // 2
    hm = _head_lane_masks(LANES, BF16)
    big = 1e30
    causal = lax.broadcasted_iota(jnp.int32, (tq, tq), 1) < lax.broadcasted_iota(jnp.int32, (tq, tq), 0)
    mask_buf[0] = jnp.full((tq, tq), -jnp.inf, F32)
    mask_buf[1] = jnp.where(causal, -jnp.inf, big)
    for i in range(nq):
        qn = -q_ref[0, i * tq:(i + 1) * tq, :]
        vsb = v_ref[0, i * tq:(i + 1) * tq, :]
        for h in range(HEADS_PER_LANE_TILE):
            qs_buf[i, h * tq:(h + 1) * tq, :] = qn * hm[h]
            for n, half in enumerate((1, 0)):
                r0 = (n * HEADS_PER_LANE_TILE + h) * tk
                vcat_buf[i, r0:r0 + tk, :] = vsb[half * tk:(half + 1) * tk] * hm[h]

    def advance(i, j, skip_rest):
        last = (j == 0) | skip_rest
        return jnp.where(last, i + 1, i), jnp.where(last, i + 1, j - 1)

    def exhausted(c):
        return jnp.max(c) < SB_EXHAUSTED

    def row_off(idx):
        return pl.multiple_of(jnp.minimum(idx, nq - 1) * tq, tq)

    def stage_a(i, j, slot):
        kb = kt_ref[0, :, pl.ds(row_off(j), tq)]
        t = _dot(qs_buf[jnp.minimum(i, nq - 1)], kb)
        floor = mask_buf[(i == j).astype(jnp.int32)]
        for h in range(HEADS_PER_LANE_TILE):
            th = jnp.maximum(t[h * tq:(h + 1) * tq], floor)
            s_buf[slot, h * tq:(h + 1) * tq, :] = th
            e = jnp.exp2(jnp.abs(th) * (-LOG2E))
            lk = jnp.minimum(th, 0.0) - jnp.log(1.0 + e)
            hi, lo = _split2(lk)
            for half in range(2):
                r0 = half * 2 * tq + h * tq
                hl_buf[slot, r0:r0 + tq, 0:tk] = hi[:, half * tk:(half + 1) * tk]
                hl_buf[slot, r0:r0 + tq, tk:tq] = lo[:, half * tk:(half + 1) * tk]

    def stage_b(i, j, slot, c, acc):
        rr = _dot(hl_buf[slot], u2x_ref[...])
        first = i == j
        c = jnp.where(first, 0.0, c)
        acc = jnp.where(first, 0.0, acc)
        ws = []
        for half in (1, 0):
            r0 = half * 2 * tq
            r = rr[r0:r0 + 2 * tq, 0:tk] + c
            th = s_buf[slot, :, half * tk:(half + 1) * tk]
            w = jnp.exp(r - th).astype(BF16)
            c = c + rr[r0:r0 + 2 * tq, tk:tq]
            ws += [w[0:tq], w[tq:2 * tq]]
        acc = acc + _dot(jnp.concatenate(ws, axis=1), vcat_buf[jnp.minimum(j, nq - 1)])
        out_buf[jnp.minimum(i, nq)] = acc
        return c, acc

    def two_items(carry):
        ib, jb, ia, ja, c, acc = carry
        stage_a(ia, ja, 1)
        c, acc = stage_b(ib, jb, 0, c, acc)
        i2, j2 = advance(ia, ja, exhausted(c) & (ib == ia))
        stage_a(i2, j2, 0)
        c, acc = stage_b(ia, ja, 1, c, acc)
        i3, j3 = advance(i2, j2, exhausted(c) & (ia == i2))
        return i2, j2, i3, j3, c, acc

    def body(carry):
        return two_items(two_items(carry))

    z = jnp.int32(0)
    stage_a(z, z, 0)
    init = (z, z, z + 1, z + 1, jnp.zeros((2 * tq, tk), F32), jnp.zeros((tq, LANES), F32))
    lax.while_loop(lambda carry: carry[0] < nq, body, init)
    for i in range(nq):
        o_ref[0, i * tq:(i + 1) * tq, :] = out_buf[i]


def _u2_matrix(tk):
    j = jnp.arange(2 * tk)[:, None] % tk
    s = jnp.arange(tk)[None, :]
    return (j >= s).astype(BF16)


def _attn_prompt(qb, kb, vb):
    b, l, _ = qb.shape
    tq = min(ATT_TQ, l)
    assert l % tq == 0 and tq == 2 * ATT_TK
    u2x = jnp.concatenate([_u2_matrix(ATT_TK), jnp.ones((tq, ATT_TK), BF16)], axis=1)
    kernel = functools.partial(_attn_prompt_kernel, tq=tq, nq=l // tq)
    seq = pl.BlockSpec((1, l, LANES), lambda bi, hp: (bi, 0, hp))
    return pl.pallas_call(
        kernel,
        grid=(b, D_ATT // LANES),
        in_specs=[seq, pl.BlockSpec((1, LANES, l), lambda bi, hp: (bi, hp, 0)), seq,
                  pl.BlockSpec(u2x.shape, lambda bi, hp: (0, 0))],
        out_specs=seq,
        out_shape=jax.ShapeDtypeStruct((b, l, D_ATT), F32),
        scratch_shapes=[pltpu.VMEM((2, 2 * tq, tq), F32), pltpu.VMEM((2, 4 * tq, tq), BF16),
                        pltpu.VMEM((2, tq, tq), F32), pltpu.VMEM((l // tq, 2 * tq, LANES), BF16),
                        pltpu.VMEM((l // tq, 2 * tq, LANES), BF16), pltpu.VMEM((l // tq + 1, tq, LANES), F32)],
        compiler_params=pltpu.CompilerParams(
            dimension_semantics=("arbitrary", "arbitrary"), vmem_limit_bytes=VMEM_LIMIT),
        name="attn_prompt",
    )(qb, kb, vb, u2x)


def _heads_to_lanes(x):
    t = x.shape[0]
    y = jnp.swapaxes(x.reshape(t // SUBLANES, SUBLANES, N_ATT_HEADS, ATT_HEAD_DIM), 1, 2)
    return jnp.concatenate([y[:, v].reshape(t, ATT_HEAD_DIM) for v in range(N_ATT_HEADS)], axis=1)


def _lanes_to_heads(x):
    t = x.shape[0]
    parts = []
    for g in range(D_ATT // LANES):
        a = x[:, g * LANES:(g + 1) * LANES]
        b = pltpu.roll(a, ATT_HEAD_DIM, axis=1)
        parts += [a.reshape(t // SUBLANES, SUBLANES, LANES), b.reshape(t // SUBLANES, SUBLANES, LANES)]
    y = jnp.swapaxes(jnp.stack(parts, axis=1), 1, 2).reshape(t, N_ATT_HEADS, LANES)
    return y[:, :, 0:ATT_HEAD_DIM]


def _attn_sample_kernel(q_ref, kn_ref, vn_ref, ckt_ref, cvt_ref, u2_ref, o_ref, c_scr, acc_scr, *, l, tk, nch):
    ch = pl.program_id(1)
    m = N_ATT_HEADS * l
    hm = _head_lane_masks(D_ATT, BF16)
    q = q_ref[0]
    qs = jnp.concatenate([q * mk for mk in hm], axis=0)
    u2 = u2_ref[...]

    @pl.when(ch == 0)
    def _():
        pad = jnp.zeros((tk - l, D_ATT), BF16)
        kn = jnp.concatenate([kn_ref[0], pad], axis=0)
        vn = jnp.concatenate([vn_ref[0], pad], axis=0)
        row = jnp.concatenate([lax.broadcasted_iota(jnp.int32, (l, tk), 0)] * N_ATT_HEADS, axis=0)
        col = lax.broadcasted_iota(jnp.int32, (m, tk), 1)
        w, c = _sb_weights(_dot_nt(qs, kn), jnp.zeros((m, 1), F32), u2, col < row)
        c_scr[...] = c
        acc_scr[...] = _dot(w.astype(BF16), vn)

    chunk = ckt_ref.shape[-1]
    kt = ckt_ref[0, 0].reshape(D_ATT, chunk).astype(BF16)
    vt = cvt_ref[0, 0].reshape(D_ATT, chunk).astype(BF16)
    c, acc = c_scr[...], acc_scr[...]
    for t in reversed(range(chunk // tk)):
        w, c = _sb_weights(_dot(qs, kt[:, t * tk:(t + 1) * tk]), c, u2, None)
        acc = acc + _dot_nt(w.astype(BF16), vt[:, t * tk:(t + 1) * tk])
    c_scr[...] = c
    acc_scr[...] = acc

    @pl.when(ch == nch - 1)
    def _():
        acc = acc_scr[...]
        hmf = _head_lane_masks(D_ATT, F32)
        out = acc[0:l] * hmf[0]
        for h in range(1, N_ATT_HEADS):
            out = out + acc[h * l:(h + 1) * l] * hmf[h]
        o_ref[0] = out


def _attn_sample(qb, kb_new, vb_new, cache_k, cache_v, layer):
    b, l, _ = qb.shape
    past = cache_k.shape[2]
    cache_k = jnp.transpose(cache_k, (0, 1, 3, 4, 2))
    cache_v = jnp.transpose(cache_v, (0, 1, 3, 4, 2))
    tk = ATT_TK
    chunk = min(ATT_CACHE_CHUNK, past)
    assert past % chunk == 0 and chunk % tk == 0 and l <= tk and l % 16 == 0
    nch = past // chunk
    u2 = _u2_matrix(tk)
    kernel = functools.partial(_attn_sample_kernel, l=l, tk=tk, nch=nch)
    new = pl.BlockSpec((1, l, D_ATT), lambda bi, ch: (bi, 0, 0))
    old = pl.BlockSpec((1, 1, N_ATT_HEADS, ATT_HEAD_DIM, chunk), lambda bi, ch: (layer, bi, 0, 0, nch - 1 - ch))
    return pl.pallas_call(
        kernel,
        grid=(b, nch),
        in_specs=[new, new, new, old, old, pl.BlockSpec(u2.shape, lambda bi, ch: (0, 0))],
        out_specs=new,
        out_shape=jax.ShapeDtypeStruct((b, l, D_ATT), F32),
        scratch_shapes=[pltpu.VMEM((N_ATT_HEADS * l, 1), F32), pltpu.VMEM((N_ATT_HEADS * l, D_ATT), F32)],
        compiler_params=pltpu.CompilerParams(
            dimension_semantics=("arbitrary", "arbitrary"), vmem_limit_bytes=VMEM_LIMIT),
        name="attn_sample",
    )(qb, kb_new, vb_new, cache_k, cache_v, u2)


def _ssd_kernel(xbc_ref, z_ref, dt_ref, conv0_ref, s0_ref, cw_ref, cb_ref, alog_ref, dskip_ref, g_ref,
                tri_ref, y_ref, sfin_ref, cbuf, state, *, q):
    c = pl.program_id(1)

    @pl.when(c == 0)
    def _():
        cbuf[0:SUBLANES, :] = conv0_ref[0]
        zero = jnp.zeros((SSM_HEAD_DIM, SSM_STATE), F32)
        for pr in range(N_PAIRS):
            top = jnp.concatenate([s0_ref[0, 2 * pr], zero], axis=1)
            bot = jnp.concatenate([zero, s0_ref[0, 2 * pr + 1]], axis=1)
            state[pr] = jnp.concatenate([top, bot], axis=0)

    cbuf[SUBLANES:SUBLANES + q, :] = xbc_ref[0]
    xc = cb_ref[...]
    for i in range(SSM_CONV):
        xc = xc + cbuf[pl.ds(SUBLANES - (SSM_CONV - 1) + i, q), :] * cw_ref[i:i + 1, :]
    xc = _silu(xc)
    cbuf[0:SUBLANES, :] = cbuf[q:q + SUBLANES, :]

    lane = lax.broadcasted_iota(jnp.int32, (1, LANES), 1)
    lo_half = lane < SSM_STATE
    bmat = xc[:, D_SSM:D_SSM + LANES]
    cmat = xc[:, D_SSM + LANES:D_SSM + 2 * LANES]
    b_sw = pltpu.roll(bmat, SSM_STATE, axis=1)
    c_sw = pltpu.roll(cmat, SSM_STATE, axis=1)
    bdup = [jnp.where(lo_half, bmat, b_sw), jnp.where(lo_half, b_sw, bmat)]
    cdup = [jnp.where(lo_half, cmat, c_sw), jnp.where(lo_half, c_sw, cmat)]
    gmask = [jnp.where(lo_half, 1.0, 0.0), jnp.where(lo_half, 0.0, 1.0)]
    bmat_b = bmat.astype(BF16)
    cb_g = [_dot_nt((cmat * gmask[g]).astype(BF16), bmat_b) for g in range(SSM_GROUPS)]

    dt = dt_ref[0]
    da = dt * (-jnp.exp(alog_ref[...]))
    tri = tri_ref[...]
    hi, mid, lo = _split3(da)
    acum = _dot(tri, hi) + _dot(tri, mid) + _dot(tri, lo)
    acum_t = acum.T
    dt_t = dt.T
    a_end = acum[q - 1:q, :]
    trow = lax.broadcasted_iota(jnp.int32, (q, q), 0)
    tcol = lax.broadcasted_iota(jnp.int32, (q, q), 1)
    causal = tcol <= trow
    bd_r = lax.broadcasted_iota(jnp.int32, (LANES, LANES), 0) < SSM_STATE
    bd_c = lax.broadcasted_iota(jnp.int32, (LANES, LANES), 1) < SSM_STATE
    block_diag = bd_r == bd_c
    hm_b = [jnp.where(lo_half, 1.0, 0.0).astype(BF16), jnp.where(lo_half, 0.0, 1.0).astype(BF16)]

    ys = []
    ssq = jnp.zeros((q, 1), F32)
    for pr in range(N_PAIRS):
        g = pr // (N_PAIRS // SSM_GROUPS)
        x_pair = xc[:, pr * LANES:(pr + 1) * LANES]
        x_b = x_pair.astype(BF16)
        ms = []
        for hh in range(HEADS_PER_LANE_TILE):
            h = pr * HEADS_PER_LANE_TILE + hh
            seg = acum[:, h:h + 1] - acum_t[h:h + 1, :]
            lmat = jnp.where(causal, jnp.exp(seg), 0.0)
            ms.append((cb_g[g] * lmat * dt_t[h:h + 1, :]).astype(BF16))
        h0 = pr * HEADS_PER_LANE_TILE
        acol = jnp.where(lo_half, acum[:, h0:h0 + 1], acum[:, h0 + 1:h0 + 2])
        dcol = jnp.where(lo_half, dt[:, h0:h0 + 1], dt[:, h0 + 1:h0 + 2])
        aend = jnp.where(lo_half, a_end[:, h0:h0 + 1], a_end[:, h0 + 1:h0 + 2])
        s_pair = state[pr]
        y_diag = _dot(jnp.concatenate(ms, axis=1), jnp.concatenate([x_b * hm_b[0], x_b * hm_b[1]], axis=0))
        y_off = _dot_nt((cdup[g] * jnp.exp(acol)).astype(BF16), s_pair.astype(BF16))
        new = _dot_tn(x_b, (bdup[g] * (jnp.exp(aend - acol) * dcol)).astype(BF16))
        state[pr] = s_pair * jnp.exp(aend) + jnp.where(block_diag, new, 0.0)
        y = y_diag + y_off + dskip_ref[:, pr * LANES:(pr + 1) * LANES] * x_pair
        yz = y * _silu(z_ref[0, :, pr * LANES:(pr + 1) * LANES])
        ssq = ssq + jnp.sum(yz * yz, axis=1, keepdims=True)
        ys.append(yz)

    inv = lax.rsqrt(ssq * (1.0 / D_SSM) + NORM_EPS)
    for pr in range(N_PAIRS):
        y_ref[0, :, pr * LANES:(pr + 1) * LANES] = (
            ys[pr] * inv * g_ref[:, pr * LANES:(pr + 1) * LANES]).astype(BF16)

    @pl.when(c == pl.num_programs(1) - 1)
    def _():
        for pr in range(N_PAIRS):
            sfin_ref[0, 2 * pr] = state[pr, 0:SSM_HEAD_DIM, 0:SSM_STATE]
            sfin_ref[0, 2 * pr + 1] = state[pr, SSM_HEAD_DIM:LANES, SSM_STATE:LANES]


def _ssd(xbc, z, dt, conv0, s0, conv_w, conv_b, a_log_pad, dskip_lanes, g_ssm):
    b, l, _ = xbc.shape
    q = min(SSD_Q, l)
    assert l % q == 0 and q % SUBLANES == 0
    tri = (jnp.arange(q)[None, :] <= jnp.arange(q)[:, None]).astype(BF16)
    seq = lambda n: pl.BlockSpec((1, q, n), lambda bi, ci: (bi, ci, 0))
    per_b = lambda a: pl.BlockSpec((1,) + a.shape[1:], lambda bi, ci: (bi,) + (0,) * (a.ndim - 1))
    full = lambda a: pl.BlockSpec(a.shape, lambda bi, ci: (0,) * a.ndim)
    kernel = functools.partial(_ssd_kernel, q=q)
    return pl.pallas_call(
        kernel,
        grid=(b, l // q),
        in_specs=[seq(CONV_DIM), seq(D_SSM), seq(LANES), per_b(conv0), per_b(s0),
                  full(conv_w), full(conv_b), full(a_log_pad), full(dskip_lanes), full(g_ssm), full(tri)],
        out_specs=(seq(D_SSM), per_b(s0)),
        out_shape=(jax.ShapeDtypeStruct((b, l, D_SSM), BF16), jax.ShapeDtypeStruct(s0.shape, F32)),
        scratch_shapes=[pltpu.VMEM((SUBLANES + q, CONV_DIM), F32),
                        pltpu.VMEM((N_PAIRS, LANES, LANES), F32)],
        compiler_params=pltpu.CompilerParams(
            dimension_semantics=("arbitrary", "arbitrary"), vmem_limit_bytes=VMEM_LIMIT),
        name="ssd",
    )(xbc, z, dt, conv0, s0, conv_w, conv_b, a_log_pad, dskip_lanes, g_ssm, tri)


def _gelu_tanh(x):
    return 0.5 * x * (1.0 + jnp.tanh(math.sqrt(2.0 / math.pi) * (x + 0.044715 * (x * x * x))))


def _out_ffn_kernel(x_ref, attn_ref, ys_ref, fc0_ref, ga_ref, gpost_ref, gpre_ref, gfpost_ref,
                    woa_ref, wos_ref, wgu_ref, wd_ref, cw_ref,
                    y_ref, fcn_ref, h2_buf, acc, carry, g_buf, u_buf, *, n_seq, lt):
    t = pl.program_id(1)

    @pl.when(t == 0)
    def _():
        carry[...] = fc0_ref[0]

    tm = x_ref.shape[0]
    for r0 in range(0, tm, FFN_ROWS):
        rows = slice(r0, r0 + FFN_ROWS)
        an = _rmsnorm(attn_ref[rows, :], ga_ref[...]).astype(BF16)
        m = _dot(an, woa_ref[...]) + _dot(ys_ref[rows, :], wos_ref[...])
        x1 = x_ref[rows, :] + _rmsnorm(m, gpost_ref[...])
        y_ref[rows, :] = x1
        h2_buf[rows, :] = _rmsnorm(x1, gpre_ref[...]).astype(BF16)
    row = lax.broadcasted_iota(jnp.int32, (lt, FFN_F), 0)

    def cols(ci, base=0):
        start = base + ci * FFN_F
        return pl.ds(start if isinstance(ci, int) else pl.multiple_of(start, FFN_F), FFN_F)

    def stage1(ci, slot):
        h2 = h2_buf[...]
        g_buf[slot] = _dot(h2, wgu_ref[:, cols(ci)])
        u_buf[slot] = _dot(h2, wgu_ref[:, cols(ci, D_FF)])

    def stage2(ci, slot):
        gate = g_buf[slot]
        cw = cw_ref[:, cols(ci)]
        acts = []
        for s in range(n_seq):
            gs = gate[s * lt:(s + 1) * lt]
            prev = carry[ci, s * SUBLANES:(s + 1) * SUBLANES, :]
            p1 = prev[SUBLANES - 1:SUBLANES, :]
            p2 = prev[SUBLANES - 2:SUBLANES - 1, :]
            g1 = jnp.where(row == 0, p1, pltpu.roll(gs, 1, axis=0))
            g2 = jnp.where(row == 0, p2, jnp.where(row == 1, p1, pltpu.roll(gs, 2, axis=0)))
            gc = cw[3:4, :] + g2 * cw[0:1, :] + g1 * cw[1:2, :] + gs * cw[2:3, :]
            carry[ci, s * SUBLANES:(s + 1) * SUBLANES, :] = gs[lt - SUBLANES:lt]
            acts.append(_gelu_tanh(gc))
        act = acts[0] if n_seq == 1 else jnp.concatenate(acts, axis=0)
        acc[...] += _dot((act * u_buf[slot]).astype(BF16), wd_ref[cols(ci), :])

    acc[...] = jnp.zeros_like(acc)
    stage1(0, 0)

    def body(p, _):
        c = 2 * p + 1
        stage1(c, 1)
        stage2(c - 1, 0)
        stage1(c + 1, 0)
        stage2(c, 1)
        return 0

    assert FFN_NC % 2 == 1
    lax.fori_loop(0, FFN_NC // 2, body, 0)
    stage2(FFN_NC - 1, 0)
    for r0 in range(0, tm, FFN_ROWS):
        rows = slice(r0, r0 + FFN_ROWS)
        y_ref[rows, :] = y_ref[rows, :] + _rmsnorm(acc[rows, :], gfpost_ref[...])
    fcn_ref[0] = carry[...]


def _out_ffn(x2d, attn2d, ys2d, fc0, n_seq, lt, gains, weights):
    t = x2d.shape[0]
    tm = n_seq * lt
    n_groups = fc0.shape[0]
    tiles = t // (tm * n_groups)
    assert tiles * tm * n_groups == t
    row = lambda n: pl.BlockSpec((tm, n), lambda gi, ti: (gi * tiles + ti, 0))
    full = lambda a: pl.BlockSpec(a.shape, lambda gi, ti: (0,) * a.ndim, pipeline_mode=pl.Buffered(1))
    fc_spec = pl.BlockSpec((1,) + fc0.shape[1:], lambda gi, ti: (gi, 0, 0, 0))
    kernel = functools.partial(_out_ffn_kernel, n_seq=n_seq, lt=lt)
    return pl.pallas_call(
        kernel,
        grid=(n_groups, tiles),
        in_specs=[row(D_MODEL), row(D_ATT), row(D_SSM), fc_spec] + [full(a) for a in gains]
                 + [full(a) for a in weights],
        out_specs=(row(D_MODEL), fc_spec),
        out_shape=(jax.ShapeDtypeStruct((t, D_MODEL), F32), jax.ShapeDtypeStruct(fc0.shape, F32)),
        scratch_shapes=[pltpu.VMEM((tm, D_MODEL), BF16), pltpu.VMEM((tm, D_MODEL), F32),
                        pltpu.VMEM(fc0.shape[1:], F32), pltpu.VMEM((2, tm, FFN_F), F32),
                        pltpu.VMEM((2, tm, FFN_F), F32)],
        compiler_params=pltpu.CompilerParams(
            dimension_semantics=("arbitrary", "arbitrary"), vmem_limit_bytes=VMEM_LIMIT),
        name="out_ffn",
    )(x2d, attn2d, ys2d, fc0, *gains, *weights)


def _ffn_state_to_chunks(st, n_seq):
    b = st.shape[0]
    s = st.reshape(b // n_seq, n_seq, FFN_CONV - 1, FFN_NC, FFN_F)
    s = jnp.pad(s, ((0, 0), (0, 0), (SUBLANES - (FFN_CONV - 1), 0), (0, 0), (0, 0)))
    return jnp.transpose(s, (0, 3, 1, 2, 4)).reshape(b // n_seq, FFN_NC, n_seq * SUBLANES, FFN_F)


def _chunks_to_ffn_state(ch, n_seq):
    g = ch.shape[0]
    s = ch.reshape(g, FFN_NC, n_seq, SUBLANES, FFN_F)[:, :, :, SUBLANES - (FFN_CONV - 1):, :]
    return jnp.transpose(s, (0, 2, 3, 1, 4)).reshape(g * n_seq, FFN_CONV - 1, D_FF)


def _layer(x, caches, ssm_h0, ssm_conv_prev, ffn_conv_prev, p, is_prompt):
    b, l, _ = x.shape
    t = b * l
    x2d = x.reshape(t, D_MODEL)
    r3 = lambda a: a.reshape(b, l, a.shape[-1])
    proj_w = (p['g_mix_pre'], p['w_main'], p['w_kt'], p['w_dt'], p['dt_bias'])
    if is_prompt:
        qb, kb, vb, kt, vt, z, xbc, dt = _in_proj(x, *proj_w, kv_transposed=True)
        attn = _attn_prompt(qb, kb, vb)
        k, v = jnp.transpose(kt, (0, 3, 1, 2)), jnp.transpose(vt, (0, 3, 1, 2))
    else:
        qb, kb, vb, k, v, z, xbc, dt = _in_proj(x2d[None], *proj_w, kv_transposed=False)
        attn = _attn_sample(r3(qb), r3(kb), r3(vb), *caches)

    conv0 = jnp.pad(ssm_conv_prev, ((0, 0), (SUBLANES - (SSM_CONV - 1), 0), (0, 0)))
    l_ssd = l if l % SSD_Q == 0 else SSD_Q * pl.cdiv(l, SSD_Q)
    pad_t = lambda a: jnp.pad(r3(a), ((0, 0), (0, l_ssd - l), (0, 0)))
    ys, s_fin = _ssd(pad_t(xbc), pad_t(z), pad_t(dt), conv0, ssm_h0,
                     p['ssm_conv_w'], p['ssm_conv_b'], p['a_log'], p['d_skip'], p['g_ssm_out'])
    ys = ys[:, :l, :]
    ssm_conv_new = r3(xbc)[:, l - (SSM_CONV - 1):, :]

    if is_prompt:
        n_seq, lt = 1, min(FFN_TM, l)
    else:
        n_seq, lt = b, l
    fc0 = _ffn_state_to_chunks(ffn_conv_prev, n_seq)
    gains = (p['g_attn_out'], p['g_mix_post'], p['g_ffn_pre'], p['g_ffn_post'])
    weights = (p['w_out_a'], p['w_out_s'], p['w_gu'], p['w_down'], p['ffn_cw'])
    y2d, fcn = _out_ffn(x2d, attn.reshape(t, D_ATT), ys.reshape(t, D_SSM), fc0, n_seq, lt, gains, weights)
    return (y2d.reshape(b, l, D_MODEL), k.reshape(b, l, N_ATT_HEADS, ATT_HEAD_DIM),
            v.reshape(b, l, N_ATT_HEADS, ATT_HEAD_DIM), s_fin, ssm_conv_new,
            _chunks_to_ffn_state(fcn, n_seq))


def _prep_params(i, g_mix_pre, g_mix_post, w_in, ssm_conv_w, ssm_conv_b, dt_bias, a_log, d_skip,
                 g_ssm_out, g_attn_out, w_out, g_ffn_pre, g_ffn_post, w_up, ffn_conv_w, ffn_conv_b, w_down):
    row = lambda a: a[i].reshape(1, -1).astype(F32)
    pad_lanes = lambda a: jnp.pad(a, ((0, 0), (0, LANES - a.shape[1])))
    wi = w_in[i]
    ffn_cw = jnp.concatenate([ffn_conv_w[i], ffn_conv_b[i][None, :],
                              jnp.zeros((SUBLANES - FFN_CONV - 1, D_FF), F32)], axis=0)
    return {
        'g_mix_pre': row(g_mix_pre), 'g_mix_post': row(g_mix_post),
        'g_ffn_pre': row(g_ffn_pre), 'g_ffn_post': row(g_ffn_post),
        'g_attn_out': row(g_attn_out), 'g_ssm_out': row(g_ssm_out),
        'w_main': wi[:, :D_MAIN_PROJ].astype(BF16),
        'w_kt': wi[:, D_ATT:2 * D_ATT].T.astype(BF16),
        'w_dt': pad_lanes(wi[:, D_MAIN_PROJ:]).astype(BF16),
        'dt_bias': pad_lanes(row(dt_bias)),
        'ssm_conv_w': ssm_conv_w[i].astype(F32), 'ssm_conv_b': row(ssm_conv_b),
        'a_log': pad_lanes(row(a_log)),
        'd_skip': jnp.repeat(d_skip[i].astype(F32), SSM_HEAD_DIM).reshape(1, D_SSM),
        'w_out_a': w_out[i][:D_ATT].astype(BF16), 'w_out_s': w_out[i][D_ATT:].astype(BF16),
        'w_gu': w_up[i].astype(BF16),
        'w_down': w_down[i].astype(BF16),
        'ffn_cw': ffn_cw,
    }


def kernel(x_prompt, x_sample, cache_k, cache_v, state_ssm, state_ssm_conv, state_ffn_conv, g_mix_pre, g_mix_post, w_in, ssm_conv_w, ssm_conv_b, dt_bias, a_log, d_skip, g_ssm_out, g_attn_out, w_out, g_ffn_pre, g_ffn_post, w_up, ffn_conv_w, ffn_conv_b, w_down):
    depth = w_in.shape[0]
    bp = x_prompt.shape[0]
    dtp = x_prompt.dtype
    zh = jnp.zeros((bp, N_SSM_HEADS, SSM_HEAD_DIM, SSM_STATE), dtp)
    zcs = jnp.zeros((bp, SSM_CONV - 1, CONV_DIM), dtp)
    zcf = jnp.zeros((bp, FFN_CONV - 1, D_FF), dtp)
    y_p, y_s = x_prompt, x_sample
    outs_p, outs_s = [], []
    for i in range(depth):
        p = _prep_params(i, g_mix_pre, g_mix_post, w_in, ssm_conv_w, ssm_conv_b, dt_bias, a_log, d_skip,
                         g_ssm_out, g_attn_out, w_out, g_ffn_pre, g_ffn_post, w_up, ffn_conv_w, ffn_conv_b,
                         w_down)
        rp = _layer(y_p, None, zh, zcs, zcf, p, True)
        rs = _layer(y_s, (cache_k, cache_v, i), state_ssm[i], state_ssm_conv[i], state_ffn_conv[i], p, False)
        y_p, y_s = rp[0], rs[0]
        outs_p.append(rp[1:])
        outs_s.append(rs[1:])
    stack = lambda outs, j: jnp.stack([o[j] for o in outs])
    return (y_p, y_s) + tuple(stack(outs_p, j) for j in range(5)) + tuple(stack(outs_s, j) for j in range(5))
```

```python
import functools
import math

import jax
import jax.numpy as jnp
from jax import lax
from jax.experimental import pallas as pl
from jax.experimental.pallas import tpu as pltpu

F32 = jnp.float32
BF16 = jnp.bfloat16

D_MODEL = 1024
D_ATT = 512
N_ATT_HEADS = 8
ATT_HEAD_DIM = 64
D_SSM = 512
N_SSM_HEADS = 8
SSM_HEAD_DIM = 64
SSM_STATE = 64
SSM_GROUPS = 2
SSM_CONV = 4
CONV_DIM = D_SSM + 2 * SSM_GROUPS * SSM_STATE
D_FF = 2816
FFN_CONV = 3
NORM_EPS = 1e-6
D_MAIN_PROJ = 3 * D_ATT + D_SSM + CONV_DIM
ATT_SCALE = ATT_HEAD_DIM ** -0.5
LOG2E = math.log2(math.e)
SB_EXHAUSTED = -120.0

LANES = 128
SUBLANES = 8
HEADS_PER_LANE_TILE = LANES // ATT_HEAD_DIM
N_PAIRS = N_SSM_HEADS // HEADS_PER_LANE_TILE

PROJ_TM = 512
ATT_TQ = 256
ATT_TK = 128
ATT_CACHE_CHUNK = 512
SSD_Q = 128
FFN_TM = 512
FFN_F = 256
FFN_ROWS = 128
FFN_NC = D_FF // FFN_F
VMEM_LIMIT = 56 * 1024 * 1024


def _rmsnorm(x, g):
    y = x * lax.rsqrt(jnp.mean(x * x, axis=-1, keepdims=True) + NORM_EPS)
    return y * g


def _softplus(x):
    return jnp.maximum(x, 0.0) + jnp.log1p(jnp.exp(-jnp.abs(x)))


def _silu(x):
    return x * (1.0 / (1.0 + jnp.exp(-x)))


def _dot(a, b):
    return jnp.dot(a, b, preferred_element_type=F32)


def _dot_nt(a, b):
    return lax.dot_general(a, b, (((1,), (1,)), ((), ())), preferred_element_type=F32)


def _dot_tn(a, b):
    return lax.dot_general(a, b, (((0,), (0,)), ((), ())), preferred_element_type=F32)


def _split2(x):
    hi = x.astype(BF16)
    lo = (x - hi.astype(F32)).astype(BF16)
    return hi, lo


def _split3(x):
    hi = x.astype(BF16)
    r1 = x - hi.astype(F32)
    mid = r1.astype(BF16)
    lo = (r1 - mid.astype(F32)).astype(BF16)
    return hi, mid, lo


def _in_proj_kernel(x_ref, g_ref, wt_ref, wdt_ref, dtb_ref,
                    qb_ref, kb_ref, vb_ref, k_ref, v_ref, z_ref, xbc_ref, dt_ref, *, kv_transposed):
    h = _rmsnorm(x_ref[0], g_ref[...]).astype(BF16)

    def proj(lo, hi):
        return _dot_nt(h, wt_ref[lo:hi, :])

    qb_ref[0] = (proj(0, D_ATT) * ATT_SCALE).astype(BF16)
    v = proj(2 * D_ATT, 3 * D_ATT)
    vb_ref[0] = v.astype(BF16)
    if kv_transposed:
        tm = v.shape[0]
        kt = _dot_nt(wt_ref[D_ATT:2 * D_ATT, :], h)
        kb_ref[0] = kt.astype(BF16)
        k_ref[0] = kt.reshape(N_ATT_HEADS, ATT_HEAD_DIM, tm)
        v_ref[0] = v.T.reshape(N_ATT_HEADS, ATT_HEAD_DIM, tm)
    else:
        k = proj(D_ATT, 2 * D_ATT)
        kb_ref[0] = k.astype(BF16)
        k_ref[0] = _lanes_to_heads(k)
        v_ref[0] = _lanes_to_heads(v)
    z_ref[0] = proj(3 * D_ATT, 3 * D_ATT + D_SSM)
    xbc_ref[0] = proj(3 * D_ATT + D_SSM, D_MAIN_PROJ)
    dt_ref[0] = _softplus(_dot_nt(h, wdt_ref[...]) + dtb_ref[...])


def _in_proj(x, g, w_t, w_dt, dt_bias, kv_transposed):
    b, l, _ = x.shape
    tm = min(PROJ_TM, l)
    assert l % tm == 0
    row = lambda n: pl.BlockSpec((1, tm, n), lambda bi, i: (bi, i, 0))
    full = lambda a: pl.BlockSpec(a.shape, lambda bi, i: (0,) * a.ndim)
    sds = jax.ShapeDtypeStruct
    if kv_transposed:
        kb_shape, kb_spec = sds((b, D_ATT, l), BF16), pl.BlockSpec((1, D_ATT, tm), lambda bi, i: (bi, 0, i))
        kv_shape = sds((b, N_ATT_HEADS, ATT_HEAD_DIM, l), F32)
        kv_spec = pl.BlockSpec((1, N_ATT_HEADS, ATT_HEAD_DIM, tm), lambda bi, i: (bi, 0, 0, i))
    else:
        kb_shape, kb_spec = sds((b, l, D_ATT), BF16), row(D_ATT)
        kv_shape = sds((b, l, N_ATT_HEADS, ATT_HEAD_DIM), F32)
        kv_spec = pl.BlockSpec((1, tm, N_ATT_HEADS, ATT_HEAD_DIM), lambda bi, i: (bi, i, 0, 0))
    out_shape = (sds((b, l, D_ATT), BF16), kb_shape, sds((b, l, D_ATT), BF16), kv_shape, kv_shape,
                 sds((b, l, D_SSM), F32), sds((b, l, CONV_DIM), F32), sds((b, l, LANES), F32))
    return pl.pallas_call(
        functools.partial(_in_proj_kernel, kv_transposed=kv_transposed),
        grid=(b, l // tm),
        in_specs=[row(D_MODEL), full(g), full(w_t), full(w_dt), full(dt_bias)],
        out_specs=(row(D_ATT), kb_spec, row(D_ATT), kv_spec, kv_spec,
                   row(D_SSM), row(CONV_DIM), row(LANES)),
        out_shape=out_shape,
        compiler_params=pltpu.CompilerParams(
            dimension_semantics=("arbitrary", "arbitrary"), vmem_limit_bytes=VMEM_LIMIT),
        name="in_proj",
    )(x, g, w_t, w_dt, dt_bias)


def _sb_weights(s, c, u2, mask):
    lk = -(jnp.maximum(s, 0.0) + jnp.log(1.0 + jnp.exp(-jnp.abs(s))))
    if mask is not None:
        lk = jnp.where(mask, lk, 0.0)
    hi, lo = _split2(lk)
    r = _dot(jnp.concatenate([hi, lo], axis=1), u2) + c
    w = jnp.exp(s + r)
    if mask is not None:
        w = jnp.where(mask, w, 0.0)
    return w, c + jnp.sum(lk, axis=1, keepdims=True)


def _head_lane_masks(n_lanes, dtype):
    lane = lax.broadcasted_iota(jnp.int32, (1, n_lanes), 1)
    return [jnp.where((lane >= h * ATT_HEAD_DIM) & (lane < (h + 1) * ATT_HEAD_DIM), 1.0, 0.0).astype(dtype)
            for h in range(n_lanes // ATT_HEAD_DIM)]


def _attn_prompt_kernel(q_ref, kt_ref, v_ref, u2x_ref, o_ref, s_buf, hl_buf, mask_buf, qs_buf, vcat_buf,
                        out_buf, *, tq, nq):
    tk = tq // 2
    hm = _head_lane_masks(LANES, BF16)
    big = 1e30
    causal = lax.broadcasted_iota(jnp.int32, (tq, tq), 1) < lax.broadcasted_iota(jnp.int32, (tq, tq), 0)
    mask_buf[0] = jnp.full((tq, tq), -jnp.inf, F32)
    mask_buf[1] = jnp.where(causal, -jnp.inf, big)
    for i in range(nq):
        qn = -q_ref[0, i * tq:(i + 1) * tq, :]
        vsb = v_ref[0, i * tq:(i + 1) * tq, :]
        for h in range(HEADS_PER_LANE_TILE):
            qs_buf[i, h * tq:(h + 1) * tq, :] = qn * hm[h]
            for n, half in enumerate((1, 0)):
                r0 = (n * HEADS_PER_LANE_TILE + h) * tk
                vcat_buf[i, r0:r0 + tk, :] = vsb[half * tk:(half + 1) * tk] * hm[h]

    def advance(i, j, skip_rest):
        last = (j == 0) | skip_rest
        return jnp.where(last, i + 1, i), jnp.where(last, i + 1, j - 1)

    def exhausted(c):
        return jnp.max(c) < SB_EXHAUSTED

    def row_off(idx):
        return pl.multiple_of(jnp.minimum(idx, nq - 1) * tq, tq)

    def stage_a(i, j, slot):
        kb = kt_ref[0, :, pl.ds(row_off(j), tq)]
        t = _dot(qs_buf[jnp.minimum(i, nq - 1)], kb)
        floor = mask_buf[(i == j).astype(jnp.int32)]
        for h in range(HEADS_PER_LANE_TILE):
            th = jnp.maximum(t[h * tq:(h + 1) * tq], floor)
            s_buf[slot, h * tq:(h + 1) * tq, :] = th
            e = jnp.exp2(jnp.abs(th) * (-LOG2E))
            lk = jnp.minimum(th, 0.0) - jnp.log(1.0 + e)
            hi, lo = _split2(lk)
            for half in range(2):
                r0 = half * 2 * tq + h * tq
                hl_buf[slot, r0:r0 + tq, 0:tk] = hi[:, half * tk:(half + 1) * tk]
                hl_buf[slot, r0:r0 + tq, tk:tq] = lo[:, half * tk:(half + 1) * tk]

    def stage_b(i, j, slot, c, acc):
        rr = _dot(hl_buf[slot], u2x_ref[...])
        first = i == j
        c = jnp.where(first, 0.0, c)
        acc = jnp.where(first, 0.0, acc)
        ws = []
        for half in (1, 0):
            r0 = half * 2 * tq
            r = rr[r0:r0 + 2 * tq, 0:tk] + c
            th = s_buf[slot, :, half * tk:(half + 1) * tk]
            w = jnp.exp(r - th).astype(BF16)
            c = c + rr[r0:r0 + 2 * tq, tk:tq]
            ws += [w[0:tq], w[tq:2 * tq]]
        acc = acc + _dot(jnp.concatenate(ws, axis=1), vcat_buf[jnp.minimum(j, nq - 1)])
        out_buf[jnp.minimum(i, nq)] = acc
        return c, acc

    def two_items(carry):
        ib, jb, ia, ja, c, acc = carry
        stage_a(ia, ja, 1)
        c, acc = stage_b(ib, jb, 0, c, acc)
        i2, j2 = advance(ia, ja, exhausted(c) & (ib == ia))
        stage_a(i2, j2, 0)
        c, acc = stage_b(ia, ja, 1, c, acc)
        i3, j3 = advance(i2, j2, exhausted(c) & (ia == i2))
        return i2, j2, i3, j3, c, acc

    def body(carry):
        return two_items(two_items(carry))

    z = jnp.int32(0)
    stage_a(z, z, 0)
    init = (z, z, z + 1, z + 1, jnp.zeros((2 * tq, tk), F32), jnp.zeros((tq, LANES), F32))
    lax.while_loop(lambda carry: carry[0] < nq, body, init)
    for i in range(nq):
        o_ref[0, i * tq:(i + 1) * tq, :] = out_buf[i]


def _u2_matrix(tk):
    j = jnp.arange(2 * tk)[:, None] % tk
    s = jnp.arange(tk)[None, :]
    return (j >= s).astype(BF16)


def _attn_prompt(qb, kb, vb):
    b, l, _ = qb.shape
    tq = min(ATT_TQ, l)
    assert l % tq == 0 and tq == 2 * ATT_TK
    u2x = jnp.concatenate([_u2_matrix(ATT_TK), jnp.ones((tq, ATT_TK), BF16)], axis=1)
    kernel = functools.partial(_attn_prompt_kernel, tq=tq, nq=l // tq)
    seq = pl.BlockSpec((1, l, LANES), lambda bi, hp: (bi, 0, hp))
    return pl.pallas_call(
        kernel,
        grid=(b, D_ATT // LANES),
        in_specs=[seq, pl.BlockSpec((1, LANES, l), lambda bi, hp: (bi, hp, 0)), seq,
                  pl.BlockSpec(u2x.shape, lambda bi, hp: (0, 0))],
        out_specs=seq,
        out_shape=jax.ShapeDtypeStruct((b, l, D_ATT), F32),
        scratch_shapes=[pltpu.VMEM((2, 2 * tq, tq), F32), pltpu.VMEM((2, 4 * tq, tq), BF16),
                        pltpu.VMEM((2, tq, tq), F32), pltpu.VMEM((l // tq, 2 * tq, LANES), BF16),
                        pltpu.VMEM((l // tq, 2 * tq, LANES), BF16), pltpu.VMEM((l // tq + 1, tq, LANES), F32)],
        compiler_params=pltpu.CompilerParams(
            dimension_semantics=("arbitrary", "arbitrary"), vmem_limit_bytes=VMEM_LIMIT),
        name="attn_prompt",
    )(qb, kb, vb, u2x)


def _heads_to_lanes(x):
    t = x.shape[0]
    y = jnp.swapaxes(x.reshape(t // SUBLANES, SUBLANES, N_ATT_HEADS, ATT_HEAD_DIM), 1, 2)
    return jnp.concatenate([y[:, v].reshape(t, ATT_HEAD_DIM) for v in range(N_ATT_HEADS)], axis=1)


def _lanes_to_heads(x):
    t = x.shape[0]
    parts = []
    for g in range(D_ATT // LANES):
        a = x[:, g * LANES:(g + 1) * LANES]
        b = pltpu.roll(a, ATT_HEAD_DIM, axis=1)
        parts += [a.reshape(t // SUBLANES, SUBLANES, LANES), b.reshape(t // SUBLANES, SUBLANES, LANES)]
    y = jnp.swapaxes(jnp.stack(parts, axis=1), 1, 2).reshape(t, N_ATT_HEADS, LANES)
    return y[:, :, 0:ATT_HEAD_DIM]


def _attn_sample_kernel(q_ref, kn_ref, vn_ref, ckt_ref, cvt_ref, u2_ref, o_ref, c_scr, acc_scr, *, l, tk, nch):
    ch = pl.program_id(1)
    m = N_ATT_HEADS * l
    hm = _head_lane_masks(D_ATT, BF16)
    q = q_ref[0]
    qs = jnp.concatenate([q * mk for mk in hm], axis=0)
    u2 = u2_ref[...]

    @pl.when(ch == 0)
    def _():
        pad = jnp.zeros((tk - l, D_ATT), BF16)
        kn = jnp.concatenate([kn_ref[0], pad], axis=0)
        vn = jnp.concatenate([vn_ref[0], pad], axis=0)
        row = jnp.concatenate([lax.broadcasted_iota(jnp.int32, (l, tk), 0)] * N_ATT_HEADS, axis=0)
        col = lax.broadcasted_iota(jnp.int32, (m, tk), 1)
        w, c = _sb_weights(_dot_nt(qs, kn), jnp.zeros((m, 1), F32), u2, col < row)
        c_scr[...] = c
        acc_scr[...] = _dot(w.astype(BF16), vn)

    @pl.when(jnp.max(c_scr[...]) >= SB_EXHAUSTED)
    def _():
        chunk = ckt_ref.shape[-1]
        kt = ckt_ref[0, 0].reshape(D_ATT, chunk).astype(BF16)
        vt = cvt_ref[0, 0].reshape(D_ATT, chunk).astype(BF16)
        c, acc = c_scr[...], acc_scr[...]
        for t in reversed(range(chunk // tk)):
            w, c = _sb_weights(_dot(qs, kt[:, t * tk:(t + 1) * tk]), c, u2, None)
            acc = acc + _dot_nt(w.astype(BF16), vt[:, t * tk:(t + 1) * tk])
        c_scr[...] = c
        acc_scr[...] = acc

    @pl.when(ch == nch - 1)
    def _():
        acc = acc_scr[...]
        hmf = _head_lane_masks(D_ATT, F32)
        out = acc[0:l] * hmf[0]
        for h in range(1, N_ATT_HEADS):
            out = out + acc[h * l:(h + 1) * l] * hmf[h]
        o_ref[0] = out


def _attn_sample(qb, kb_new, vb_new, cache_k, cache_v, layer):
    b, l, _ = qb.shape
    past = cache_k.shape[2]
    cache_k = jnp.transpose(cache_k, (0, 1, 3, 4, 2))
    cache_v = jnp.transpose(cache_v, (0, 1, 3, 4, 2))
    tk = ATT_TK
    chunk = min(ATT_CACHE_CHUNK, past)
    assert past % chunk == 0 and chunk % tk == 0 and l <= tk and l % 16 == 0
    nch = past // chunk
    u2 = _u2_matrix(tk)
    kernel = functools.partial(_attn_sample_kernel, l=l, tk=tk, nch=nch)
    new = pl.BlockSpec((1, l, D_ATT), lambda bi, ch: (bi, 0, 0))
    old = pl.BlockSpec((1, 1, N_ATT_HEADS, ATT_HEAD_DIM, chunk), lambda bi, ch: (layer, bi, 0, 0, nch - 1 - ch))
    return pl.pallas_call(
        kernel,
        grid=(b, nch),
        in_specs=[new, new, new, old, old, pl.BlockSpec(u2.shape, lambda bi, ch: (0, 0))],
        out_specs=new,
        out_shape=jax.ShapeDtypeStruct((b, l, D_ATT), F32),
        scratch_shapes=[pltpu.VMEM((N_ATT_HEADS * l, 1), F32), pltpu.VMEM((N_ATT_HEADS * l, D_ATT), F32)],
        compiler_params=pltpu.CompilerParams(
            dimension_semantics=("arbitrary", "arbitrary"), vmem_limit_bytes=VMEM_LIMIT),
        name="attn_sample",
    )(qb, kb_new, vb_new, cache_k, cache_v, u2)


def _ssd_kernel(xbc_ref, z_ref, dt_ref, conv0_ref, s0_ref, cw_ref, cb_ref, alog_ref, dskip_ref, g_ref,
                tri_ref, y_ref, sfin_ref, cbuf, state, *, q):
    c = pl.program_id(1)

    @pl.when(c == 0)
    def _():
        cbuf[0:SUBLANES, :] = conv0_ref[0]
        zero = jnp.zeros((SSM_HEAD_DIM, SSM_STATE), F32)
        for pr in range(N_PAIRS):
            top = jnp.concatenate([s0_ref[0, 2 * pr], zero], axis=1)
            bot = jnp.concatenate([zero, s0_ref[0, 2 * pr + 1]], axis=1)
            state[pr] = jnp.concatenate([top, bot], axis=0)

    cbuf[SUBLANES:SUBLANES + q, :] = xbc_ref[0]
    xc = cb_ref[...]
    for i in range(SSM_CONV):
        xc = xc + cbuf[pl.ds(SUBLANES - (SSM_CONV - 1) + i, q), :] * cw_ref[i:i + 1, :]
    xc = _silu(xc)
    cbuf[0:SUBLANES, :] = cbuf[q:q + SUBLANES, :]

    lane = lax.broadcasted_iota(jnp.int32, (1, LANES), 1)
    lo_half = lane < SSM_STATE
    bmat = xc[:, D_SSM:D_SSM + LANES]
    cmat = xc[:, D_SSM + LANES:D_SSM + 2 * LANES]
    b_sw = pltpu.roll(bmat, SSM_STATE, axis=1)
    c_sw = pltpu.roll(cmat, SSM_STATE, axis=1)
    bdup = [jnp.where(lo_half, bmat, b_sw), jnp.where(lo_half, b_sw, bmat)]
    cdup = [jnp.where(lo_half, cmat, c_sw), jnp.where(lo_half, c_sw, cmat)]
    gmask = [jnp.where(lo_half, 1.0, 0.0), jnp.where(lo_half, 0.0, 1.0)]
    bmat_b = bmat.astype(BF16)
    cb_g = [_dot_nt((cmat * gmask[g]).astype(BF16), bmat_b) for g in range(SSM_GROUPS)]

    dt = dt_ref[0]
    da = dt * (-jnp.exp(alog_ref[...]))
    tri = tri_ref[...]
    hi, mid, lo = _split3(da)
    acum = _dot(tri, hi) + _dot(tri, mid) + _dot(tri, lo)
    acum_t = acum.T
    dt_t = dt.T
    a_end = acum[q - 1:q, :]
    trow = lax.broadcasted_iota(jnp.int32, (q, q), 0)
    tcol = lax.broadcasted_iota(jnp.int32, (q, q), 1)
    causal = tcol <= trow
    bd_r = lax.broadcasted_iota(jnp.int32, (LANES, LANES), 0) < SSM_STATE
    bd_c = lax.broadcasted_iota(jnp.int32, (LANES, LANES), 1) < SSM_STATE
    block_diag = bd_r == bd_c
    hm_b = [jnp.where(lo_half, 1.0, 0.0).astype(BF16), jnp.where(lo_half, 0.0, 1.0).astype(BF16)]

    ys = []
    ssq = jnp.zeros((q, 1), F32)
    for pr in range(N_PAIRS):
        g = pr // (N_PAIRS // SSM_GROUPS)
        x_pair = xc[:, pr * LANES:(pr + 1) * LANES]
        x_b = x_pair.astype(BF16)
        ms = []
        for hh in range(HEADS_PER_LANE_TILE):
            h = pr * HEADS_PER_LANE_TILE + hh
            seg = acum[:, h:h + 1] - acum_t[h:h + 1, :]
            lmat = jnp.where(causal, jnp.exp(seg), 0.0)
            ms.append((cb_g[g] * lmat * dt_t[h:h + 1, :]).astype(BF16))
        h0 = pr * HEADS_PER_LANE_TILE
        acol = jnp.where(lo_half, acum[:, h0:h0 + 1], acum[:, h0 + 1:h0 + 2])
        dcol = jnp.where(lo_half, dt[:, h0:h0 + 1], dt[:, h0 + 1:h0 + 2])
        aend = jnp.where(lo_half, a_end[:, h0:h0 + 1], a_end[:, h0 + 1:h0 + 2])
        s_pair = state[pr]
        y_diag = _dot(jnp.concatenate(ms, axis=1), jnp.concatenate([x_b * hm_b[0], x_b * hm_b[1]], axis=0))
        y_off = _dot_nt((cdup[g] * jnp.exp(acol)).astype(BF16), s_pair.astype(BF16))
        new = _dot_tn(x_b, (bdup[g] * (jnp.exp(aend - acol) * dcol)).astype(BF16))
        state[pr] = s_pair * jnp.exp(aend) + jnp.where(block_diag, new, 0.0)
        y = y_diag + y_off + dskip_ref[:, pr * LANES:(pr + 1) * LANES] * x_pair
        yz = y * _silu(z_ref[0, :, pr * LANES:(pr + 1) * LANES])
        ssq = ssq + jnp.sum(yz * yz, axis=1, keepdims=True)
        ys.append(yz)

    inv = lax.rsqrt(ssq * (1.0 / D_SSM) + NORM_EPS)
    for pr in range(N_PAIRS):
        y_ref[0, :, pr * LANES:(pr + 1) * LANES] = (
            ys[pr] * inv * g_ref[:, pr * LANES:(pr + 1) * LANES]).astype(BF16)

    @pl.when(c == pl.num_programs(1) - 1)
    def _():
        for pr in range(N_PAIRS):
            sfin_ref[0, 2 * pr] = state[pr, 0:SSM_HEAD_DIM, 0:SSM_STATE]
            sfin_ref[0, 2 * pr + 1] = state[pr, SSM_HEAD_DIM:LANES, SSM_STATE:LANES]


def _ssd(xbc, z, dt, conv0, s0, conv_w, conv_b, a_log_pad, dskip_lanes, g_ssm):
    b, l, _ = xbc.shape
    q = min(SSD_Q, l)
    assert l % q == 0 and q % SUBLANES == 0
    tri = (jnp.arange(q)[None, :] <= jnp.arange(q)[:, None]).astype(BF16)
    seq = lambda n: pl.BlockSpec((1, q, n), lambda bi, ci: (bi, ci, 0))
    per_b = lambda a: pl.BlockSpec((1,) + a.shape[1:], lambda bi, ci: (bi,) + (0,) * (a.ndim - 1))
    full = lambda a: pl.BlockSpec(a.shape, lambda bi, ci: (0,) * a.ndim)
    kernel = functools.partial(_ssd_kernel, q=q)
    return pl.pallas_call(
        kernel,
        grid=(b, l // q),
        in_specs=[seq(CONV_DIM), seq(D_SSM), seq(LANES), per_b(conv0), per_b(s0),
                  full(conv_w), full(conv_b), full(a_log_pad), full(dskip_lanes), full(g_ssm), full(tri)],
        out_specs=(seq(D_SSM), per_b(s0)),
        out_shape=(jax.ShapeDtypeStruct((b, l, D_SSM), BF16), jax.ShapeDtypeStruct(s0.shape, F32)),
        scratch_shapes=[pltpu.VMEM((SUBLANES + q, CONV_DIM), F32),
                        pltpu.VMEM((N_PAIRS, LANES, LANES), F32)],
        compiler_params=pltpu.CompilerParams(
            dimension_semantics=("arbitrary", "arbitrary"), vmem_limit_bytes=VMEM_LIMIT),
        name="ssd",
    )(xbc, z, dt, conv0, s0, conv_w, conv_b, a_log_pad, dskip_lanes, g_ssm, tri)


def _gelu_tanh(x):
    return 0.5 * x * (1.0 + jnp.tanh(math.sqrt(2.0 / math.pi) * (x + 0.044715 * (x * x * x))))


def _out_ffn_kernel(x_ref, attn_ref, ys_ref, fc0_ref, ga_ref, gpost_ref, gpre_ref, gfpost_ref,
                    woa_ref, wos_ref, wgu_ref, wd_ref, cw_ref,
                    y_ref, fcn_ref, h2_buf, acc, carry, g_buf, u_buf, *, n_seq, lt):
    t = pl.program_id(1)

    @pl.when(t == 0)
    def _():
        carry[...] = fc0_ref[0]

    tm = x_ref.shape[0]
    for r0 in range(0, tm, FFN_ROWS):
        rows = slice(r0, r0 + FFN_ROWS)
        an = _rmsnorm(attn_ref[rows, :], ga_ref[...]).astype(BF16)
        m = _dot(an, woa_ref[...]) + _dot(ys_ref[rows, :], wos_ref[...])
        x1 = x_ref[rows, :] + _rmsnorm(m, gpost_ref[...])
        y_ref[rows, :] = x1
        h2_buf[rows, :] = _rmsnorm(x1, gpre_ref[...]).astype(BF16)
    row = lax.broadcasted_iota(jnp.int32, (lt, FFN_F), 0)

    def cols(ci, base=0):
        start = base + ci * FFN_F
        return pl.ds(start if isinstance(ci, int) else pl.multiple_of(start, FFN_F), FFN_F)

    def stage1(ci, slot):
        h2 = h2_buf[...]
        g_buf[slot] = _dot(h2, wgu_ref[:, cols(ci)])
        u_buf[slot] = _dot(h2, wgu_ref[:, cols(ci, D_FF)])

    def stage2(ci, slot):
        gate = g_buf[slot]
        cw = cw_ref[:, cols(ci)]
        acts = []
        for s in range(n_seq):
            gs = gate[s * lt:(s + 1) * lt]
            prev = carry[ci, s * SUBLANES:(s + 1) * SUBLANES, :]
            p1 = prev[SUBLANES - 1:SUBLANES, :]
            p2 = prev[SUBLANES - 2:SUBLANES - 1, :]
            g1 = jnp.where(row == 0, p1, pltpu.roll(gs, 1, axis=0))
            g2 = jnp.where(row == 0, p2, jnp.where(row == 1, p1, pltpu.roll(gs, 2, axis=0)))
            gc = cw[3:4, :] + g2 * cw[0:1, :] + g1 * cw[1:2, :] + gs * cw[2:3, :]
            carry[ci, s * SUBLANES:(s + 1) * SUBLANES, :] = gs[lt - SUBLANES:lt]
            acts.append(_gelu_tanh(gc))
        act = acts[0] if n_seq == 1 else jnp.concatenate(acts, axis=0)
        acc[...] += _dot((act * u_buf[slot]).astype(BF16), wd_ref[cols(ci), :])

    acc[...] = jnp.zeros_like(acc)
    stage1(0, 0)

    def body(p, _):
        c = 2 * p + 1
        stage1(c, 1)
        stage2(c - 1, 0)
        stage1(c + 1, 0)
        stage2(c, 1)
        return 0

    assert FFN_NC % 2 == 1
    lax.fori_loop(0, FFN_NC // 2, body, 0)
    stage2(FFN_NC - 1, 0)
    for r0 in range(0, tm, FFN_ROWS):
        rows = slice(r0, r0 + FFN_ROWS)
        y_ref[rows, :] = y_ref[rows, :] + _rmsnorm(acc[rows, :], gfpost_ref[...])
    fcn_ref[0] = carry[...]


def _out_ffn(x2d, attn2d, ys2d, fc0, n_seq, lt, gains, weights):
    t = x2d.shape[0]
    tm = n_seq * lt
    n_groups = fc0.shape[0]
    tiles = t // (tm * n_groups)
    assert tiles * tm * n_groups == t
    row = lambda n: pl.BlockSpec((tm, n), lambda gi, ti: (gi * tiles + ti, 0))
    full = lambda a: pl.BlockSpec(a.shape, lambda gi, ti: (0,) * a.ndim, pipeline_mode=pl.Buffered(1))
    fc_spec = pl.BlockSpec((1,) + fc0.shape[1:], lambda gi, ti: (gi, 0, 0, 0))
    kernel = functools.partial(_out_ffn_kernel, n_seq=n_seq, lt=lt)
    return pl.pallas_call(
        kernel,
        grid=(n_groups, tiles),
        in_specs=[row(D_MODEL), row(D_ATT), row(D_SSM), fc_spec] + [full(a) for a in gains]
                 + [full(a) for a in weights],
        out_specs=(row(D_MODEL), fc_spec),
        out_shape=(jax.ShapeDtypeStruct((t, D_MODEL), F32), jax.ShapeDtypeStruct(fc0.shape, F32)),
        scratch_shapes=[pltpu.VMEM((tm, D_MODEL), BF16), pltpu.VMEM((tm, D_MODEL), F32),
                        pltpu.VMEM(fc0.shape[1:], F32), pltpu.VMEM((2, tm, FFN_F), F32),
                        pltpu.VMEM((2, tm, FFN_F), F32)],
        compiler_params=pltpu.CompilerParams(
            dimension_semantics=("arbitrary", "arbitrary"), vmem_limit_bytes=VMEM_LIMIT),
        name="out_ffn",
    )(x2d, attn2d, ys2d, fc0, *gains, *weights)


def _ffn_state_to_chunks(st, n_seq):
    b = st.shape[0]
    s = st.reshape(b // n_seq, n_seq, FFN_CONV - 1, FFN_NC, FFN_F)
    s = jnp.pad(s, ((0, 0), (0, 0), (SUBLANES - (FFN_CONV - 1), 0), (0, 0), (0, 0)))
    return jnp.transpose(s, (0, 3, 1, 2, 4)).reshape(b // n_seq, FFN_NC, n_seq * SUBLANES, FFN_F)


def _chunks_to_ffn_state(ch, n_seq):
    g = ch.shape[0]
    s = ch.reshape(g, FFN_NC, n_seq, SUBLANES, FFN_F)[:, :, :, SUBLANES - (FFN_CONV - 1):, :]
    return jnp.transpose(s, (0, 2, 3, 1, 4)).reshape(g * n_seq, FFN_CONV - 1, D_FF)


def _layer(x, caches, ssm_h0, ssm_conv_prev, ffn_conv_prev, p, is_prompt):
    b, l, _ = x.shape
    t = b * l
    x2d = x.reshape(t, D_MODEL)
    r3 = lambda a: a.reshape(b, l, a.shape[-1])
    proj_w = (p['g_mix_pre'], p['w_in_t'], p['w_dt_t'], p['dt_bias'])
    if is_prompt:
        qb, kb, vb, kt, vt, z, xbc, dt = _in_proj(x, *proj_w, kv_transposed=True)
        attn = _attn_prompt(qb, kb, vb)
        k, v = jnp.transpose(kt, (0, 3, 1, 2)), jnp.transpose(vt, (0, 3, 1, 2))
    else:
        qb, kb, vb, k, v, z, xbc, dt = _in_proj(x2d[None], *proj_w, kv_transposed=False)
        attn = _attn_sample(r3(qb), r3(kb), r3(vb), *caches)

    conv0 = jnp.pad(ssm_conv_prev, ((0, 0), (SUBLANES - (SSM_CONV - 1), 0), (0, 0)))
    l_ssd = l if l % SSD_Q == 0 else SSD_Q * pl.cdiv(l, SSD_Q)
    pad_t = lambda a: jnp.pad(r3(a), ((0, 0), (0, l_ssd - l), (0, 0)))
    ys, s_fin = _ssd(pad_t(xbc), pad_t(z), pad_t(dt), conv0, ssm_h0,
                     p['ssm_conv_w'], p['ssm_conv_b'], p['a_log'], p['d_skip'], p['g_ssm_out'])
    ys = ys[:, :l, :]
    ssm_conv_new = r3(xbc)[:, l - (SSM_CONV - 1):, :]

    if is_prompt:
        n_seq, lt = 1, min(FFN_TM, l)
    else:
        n_seq, lt = b, l
    fc0 = _ffn_state_to_chunks(ffn_conv_prev, n_seq)
    gains = (p['g_attn_out'], p['g_mix_post'], p['g_ffn_pre'], p['g_ffn_post'])
    weights = (p['w_out_a'], p['w_out_s'], p['w_gu'], p['w_down'], p['ffn_cw'])
    y2d, fcn = _out_ffn(x2d, attn.reshape(t, D_ATT), ys.reshape(t, D_SSM), fc0, n_seq, lt, gains, weights)
    return (y2d.reshape(b, l, D_MODEL), k.reshape(b, l, N_ATT_HEADS, ATT_HEAD_DIM),
            v.reshape(b, l, N_ATT_HEADS, ATT_HEAD_DIM), s_fin, ssm_conv_new,
            _chunks_to_ffn_state(fcn, n_seq))


def _prep_params(i, g_mix_pre, g_mix_post, w_in, ssm_conv_w, ssm_conv_b, dt_bias, a_log, d_skip,
                 g_ssm_out, g_attn_out, w_out, g_ffn_pre, g_ffn_post, w_up, ffn_conv_w, ffn_conv_b, w_down):
    row = lambda a: a[i].reshape(1, -1).astype(F32)
    pad_lanes = lambda a: jnp.pad(a, ((0, 0), (0, LANES - a.shape[1])))
    wi_t = w_in[i].T.astype(BF16)
    ffn_cw = jnp.concatenate([ffn_conv_w[i], ffn_conv_b[i][None, :],
                              jnp.zeros((SUBLANES - FFN_CONV - 1, D_FF), F32)], axis=0)
    return {
        'g_mix_pre': row(g_mix_pre), 'g_mix_post': row(g_mix_post),
        'g_ffn_pre': row(g_ffn_pre), 'g_ffn_post': row(g_ffn_post),
        'g_attn_out': row(g_attn_out), 'g_ssm_out': row(g_ssm_out),
        'w_in_t': wi_t,
        'w_dt_t': jnp.pad(wi_t[D_MAIN_PROJ:], ((0, LANES - (wi_t.shape[0] - D_MAIN_PROJ)), (0, 0))),
        'dt_bias': pad_lanes(row(dt_bias)),
        'ssm_conv_w': ssm_conv_w[i].astype(F32), 'ssm_conv_b': row(ssm_conv_b),
        'a_log': pad_lanes(row(a_log)),
        'd_skip': jnp.repeat(d_skip[i].astype(F32), SSM_HEAD_DIM).reshape(1, D_SSM),
        'w_out_a': w_out[i][:D_ATT].astype(BF16), 'w_out_s': w_out[i][D_ATT:].astype(BF16),
        'w_gu': w_up[i].astype(BF16),
        'w_down': w_down[i].astype(BF16),
        'ffn_cw': ffn_cw,
    }


def kernel(x_prompt, x_sample, cache_k, cache_v, state_ssm, state_ssm_conv, state_ffn_conv, g_mix_pre, g_mix_post, w_in, ssm_conv_w, ssm_conv_b, dt_bias, a_log, d_skip, g_ssm_out, g_attn_out, w_out, g_ffn_pre, g_ffn_post, w_up, ffn_conv_w, ffn_conv_b, w_down):
    depth = w_in.shape[0]
    bp = x_prompt.shape[0]
    dtp = x_prompt.dtype
    zh = jnp.zeros((bp, N_SSM_HEADS, SSM_HEAD_DIM, SSM_STATE), dtp)
    zcs = jnp.zeros((bp, SSM_CONV - 1, CONV_DIM), dtp)
    zcf = jnp.zeros((bp, FFN_CONV - 1, D_FF), dtp)
    y_p, y_s = x_prompt, x_sample
    outs_p, outs_s = [], []
    for i in range(depth):
        p = _prep_params(i, g_mix_pre, g_mix_post, w_in, ssm_conv_w, ssm_conv_b, dt_bias, a_log, d_skip,
                         g_ssm_out, g_attn_out, w_out, g_ffn_pre, g_ffn_post, w_up, ffn_conv_w, ffn_conv_b,
                         w_down)
        rp = _layer(y_p, None, zh, zcs, zcf, p, True)
        rs = _layer(y_s, (cache_k, cache_v, i), state_ssm[i], state_ssm_conv[i], state_ffn_conv[i], p, False)
        y_p, y_s = rp[0], rs[0]
        outs_p.append(rp[1:])
        outs_s.append(rs[1:])
    stack = lambda outs, j: jnp.stack([o[j] for o in outs])
    return (y_p, y_s) + tuple(stack(outs_p, j) for j in range(5)) + tuple(stack(outs_s, j) for j in range(5))
```

```python
import functools
import math

import jax
import jax.numpy as jnp
from jax import lax
from jax.experimental import pallas as pl
from jax.experimental.pallas import tpu as pltpu

F32 = jnp.float32
BF16 = jnp.bfloat16

D_MODEL = 1024
D_ATT = 512
N_ATT_HEADS = 8
ATT_HEAD_DIM = 64
D_SSM = 512
N_SSM_HEADS = 8
SSM_HEAD_DIM = 64
SSM_STATE = 64
SSM_GROUPS = 2
SSM_CONV = 4
CONV_DIM = D_SSM + 2 * SSM_GROUPS * SSM_STATE
D_FF = 2816
FFN_CONV = 3
NORM_EPS = 1e-6
D_MAIN_PROJ = 3 * D_ATT + D_SSM + CONV_DIM
ATT_SCALE = ATT_HEAD_DIM ** -0.5
LOG2E = math.log2(math.e)
SB_EXHAUSTED = -120.0

LANES = 128
SUBLANES = 8
HEADS_PER_LANE_TILE = LANES // ATT_HEAD_DIM
N_PAIRS = N_SSM_HEADS // HEADS_PER_LANE_TILE

PROJ_TM = 512
ATT_TQ = 256
ATT_TK = 128
ATT_CACHE_CHUNK = 512
SSD_Q = 128
FFN_TM = 512
FFN_F = 256
FFN_ROWS = 128
FFN_NC = D_FF // FFN_F
VMEM_LIMIT = 56 * 1024 * 1024


def _rmsnorm(x, g):
    y = x * lax.rsqrt(jnp.mean(x * x, axis=-1, keepdims=True) + NORM_EPS)
    return y * g


def _softplus(x):
    return jnp.maximum(x, 0.0) + jnp.log1p(jnp.exp(-jnp.abs(x)))


def _silu(x):
    return x * (1.0 / (1.0 + jnp.exp(-x)))


def _dot(a, b):
    return jnp.dot(a, b, preferred_element_type=F32)


def _dot_nt(a, b):
    return lax.dot_general(a, b, (((1,), (1,)), ((), ())), preferred_element_type=F32)


def _dot_tn(a, b):
    return lax.dot_general(a, b, (((0,), (0,)), ((), ())), preferred_element_type=F32)


def _split2(x):
    hi = x.astype(BF16)
    lo = (x - hi.astype(F32)).astype(BF16)
    return hi, lo


def _split3(x):
    hi = x.astype(BF16)
    r1 = x - hi.astype(F32)
    mid = r1.astype(BF16)
    lo = (r1 - mid.astype(F32)).astype(BF16)
    return hi, mid, lo


def _in_proj_kernel(x_ref, g_ref, wt_ref, wdt_ref, dtb_ref,
                    qb_ref, kb_ref, vb_ref, k_ref, v_ref, z_ref, xbc_ref, dt_ref, *, kv_transposed):
    h = _rmsnorm(x_ref[0], g_ref[...]).astype(BF16)

    def proj(lo, hi):
        return _dot_nt(h, wt_ref[lo:hi, :])

    qb_ref[0] = (proj(0, D_ATT) * ATT_SCALE).astype(BF16)
    v = proj(2 * D_ATT, 3 * D_ATT)
    vb_ref[0] = v.astype(BF16)
    if kv_transposed:
        tm = v.shape[0]
        kt = _dot_nt(wt_ref[D_ATT:2 * D_ATT, :], h)
        kb_ref[0] = kt.astype(BF16)
        k_ref[0] = kt.reshape(N_ATT_HEADS, ATT_HEAD_DIM, tm)
        v_ref[0] = v.T.reshape(N_ATT_HEADS, ATT_HEAD_DIM, tm)
    else:
        k = proj(D_ATT, 2 * D_ATT)
        kb_ref[0] = k.astype(BF16)
        k_ref[0] = _lanes_to_heads(k)
        v_ref[0] = _lanes_to_heads(v)
    z_ref[0] = proj(3 * D_ATT, 3 * D_ATT + D_SSM)
    xbc_ref[0] = proj(3 * D_ATT + D_SSM, D_MAIN_PROJ)
    dt_ref[0] = _softplus(_dot_nt(h, wdt_ref[...]) + dtb_ref[...])


def _in_proj(x, g, w_t, w_dt, dt_bias, kv_transposed):
    b, l, _ = x.shape
    tm = min(PROJ_TM, l)
    assert l % tm == 0
    row = lambda n: pl.BlockSpec((1, tm, n), lambda bi, i: (bi, i, 0))
    full = lambda a: pl.BlockSpec(a.shape, lambda bi, i: (0,) * a.ndim)
    sds = jax.ShapeDtypeStruct
    if kv_transposed:
        kb_shape, kb_spec = sds((b, D_ATT, l), BF16), pl.BlockSpec((1, D_ATT, tm), lambda bi, i: (bi, 0, i))
        kv_shape = sds((b, N_ATT_HEADS, ATT_HEAD_DIM, l), F32)
        kv_spec = pl.BlockSpec((1, N_ATT_HEADS, ATT_HEAD_DIM, tm), lambda bi, i: (bi, 0, 0, i))
    else:
        kb_shape, kb_spec = sds((b, l, D_ATT), BF16), row(D_ATT)
        kv_shape = sds((b, l, N_ATT_HEADS, ATT_HEAD_DIM), F32)
        kv_spec = pl.BlockSpec((1, tm, N_ATT_HEADS, ATT_HEAD_DIM), lambda bi, i: (bi, i, 0, 0))
    out_shape = (sds((b, l, D_ATT), BF16), kb_shape, sds((b, l, D_ATT), BF16), kv_shape, kv_shape,
                 sds((b, l, D_SSM), F32), sds((b, l, CONV_DIM), F32), sds((b, l, LANES), F32))
    return pl.pallas_call(
        functools.partial(_in_proj_kernel, kv_transposed=kv_transposed),
        grid=(b, l // tm),
        in_specs=[row(D_MODEL), full(g), full(w_t), full(w_dt), full(dt_bias)],
        out_specs=(row(D_ATT), kb_spec, row(D_ATT), kv_spec, kv_spec,
                   row(D_SSM), row(CONV_DIM), row(LANES)),
        out_shape=out_shape,
        compiler_params=pltpu.CompilerParams(
            dimension_semantics=("arbitrary", "arbitrary"), vmem_limit_bytes=VMEM_LIMIT),
        name="in_proj",
    )(x, g, w_t, w_dt, dt_bias)


def _sb_weights(s, c, u2, mask):
    lk = -(jnp.maximum(s, 0.0) + jnp.log(1.0 + jnp.exp(-jnp.abs(s))))
    if mask is not None:
        lk = jnp.where(mask, lk, 0.0)
    hi, lo = _split2(lk)
    r = _dot(jnp.concatenate([hi, lo], axis=1), u2) + c
    w = jnp.exp(s + r)
    if mask is not None:
        w = jnp.where(mask, w, 0.0)
    return w, c + jnp.sum(lk, axis=1, keepdims=True)


def _head_lane_masks(n_lanes, dtype):
    lane = lax.broadcasted_iota(jnp.int32, (1, n_lanes), 1)
    return [jnp.where((lane >= h * ATT_HEAD_DIM) & (lane < (h + 1) * ATT_HEAD_DIM), 1.0, 0.0).astype(dtype)
            for h in range(n_lanes // ATT_HEAD_DIM)]


def _attn_prompt_kernel(q_ref, kt_ref, v_ref, u2x_ref, o_ref, s_buf, hl_buf, mask_buf, qs_buf, vcat_buf,
                        out_buf, *, tq, nq):
    tk = tq // 2
    hm = _head_lane_masks(LANES, BF16)
    big = 1e30
    causal = lax.broadcasted_iota(jnp.int32, (tq, tq), 1) < lax.broadcasted_iota(jnp.int32, (tq, tq), 0)
    mask_buf[0] = jnp.full((tq, tq), -jnp.inf, F32)
    mask_buf[1] = jnp.where(causal, -jnp.inf, big)
    for i in range(nq):
        qn = -q_ref[0, i * tq:(i + 1) * tq, :]
        vsb = v_ref[0, i * tq:(i + 1) * tq, :]
        for h in range(HEADS_PER_LANE_TILE):
            qs_buf[i, h * tq:(h + 1) * tq, :] = qn * hm[h]
            for n, half in enumerate((1, 0)):
                r0 = (n * HEADS_PER_LANE_TILE + h) * tk
                vcat_buf[i, r0:r0 + tk, :] = vsb[half * tk:(half + 1) * tk] * hm[h]

    def advance(i, j, skip_rest):
        last = (j == 0) | skip_rest
        return jnp.where(last, i + 1, i), jnp.where(last, i + 1, j - 1)

    def exhausted(c):
        return jnp.max(c) < SB_EXHAUSTED

    def row_off(idx):
        return pl.multiple_of(jnp.minimum(idx, nq - 1) * tq, tq)

    def stage_a(i, j, slot):
        kb = kt_ref[0, :, pl.ds(row_off(j), tq)]
        t = _dot(qs_buf[jnp.minimum(i, nq - 1)], kb)
        floor = mask_buf[(i == j).astype(jnp.int32)]
        for h in range(HEADS_PER_LANE_TILE):
            th = jnp.maximum(t[h * tq:(h + 1) * tq], floor)
            s_buf[slot, h * tq:(h + 1) * tq, :] = th
            e = jnp.exp2(jnp.abs(th) * (-LOG2E))
            lk = jnp.minimum(th, 0.0) - jnp.log(1.0 + e)
            hi, lo = _split2(lk)
            for half in range(2):
                r0 = half * 2 * tq + h * tq
                hl_buf[slot, r0:r0 + tq, 0:tk] = hi[:, half * tk:(half + 1) * tk]
                hl_buf[slot, r0:r0 + tq, tk:tq] = lo[:, half * tk:(half + 1) * tk]

    def stage_b(i, j, slot, c, acc):
        rr = _dot(hl_buf[slot], u2x_ref[...])
        first = i == j
        c = jnp.where(first, 0.0, c)
        acc = jnp.where(first, 0.0, acc)
        ws = []
        for half in (1, 0):
            r0 = half * 2 * tq
            r = rr[r0:r0 + 2 * tq, 0:tk] + c
            th = s_buf[slot, :, half * tk:(half + 1) * tk]
            w = jnp.exp(r - th).astype(BF16)
            c = c + rr[r0:r0 + 2 * tq, tk:tq]
            ws += [w[0:tq], w[tq:2 * tq]]
        acc = acc + _dot(jnp.concatenate(ws, axis=1), vcat_buf[jnp.minimum(j, nq - 1)])
        out_buf[jnp.minimum(i, nq)] = acc
        return c, acc

    def two_items(carry):
        ib, jb, ia, ja, c, acc = carry
        stage_a(ia, ja, 1)
        c, acc = stage_b(ib, jb, 0, c, acc)
        i2, j2 = advance(ia, ja, exhausted(c) & (ib == ia))
        stage_a(i2, j2, 0)
        c, acc = stage_b(ia, ja, 1, c, acc)
        i3, j3 = advance(i2, j2, exhausted(c) & (ia == i2))
        return i2, j2, i3, j3, c, acc

    def body(carry):
        return two_items(two_items(carry))

    z = jnp.int32(0)
    stage_a(z, z, 0)
    init = (z, z, z + 1, z + 1, jnp.zeros((2 * tq, tk), F32), jnp.zeros((tq, LANES), F32))
    lax.while_loop(lambda carry: carry[0] < nq, body, init)
    for i in range(nq):
        o_ref[0, i * tq:(i + 1) * tq, :] = out_buf[i]


def _u2_matrix(tk):
    j = jnp.arange(2 * tk)[:, None] % tk
    s = jnp.arange(tk)[None, :]
    return (j >= s).astype(BF16)


def _attn_prompt(qb, kb, vb):
    b, l, _ = qb.shape
    tq = min(ATT_TQ, l)
    assert l % tq == 0 and tq == 2 * ATT_TK
    u2x = jnp.concatenate([_u2_matrix(ATT_TK), jnp.ones((tq, ATT_TK), BF16)], axis=1)
    kernel = functools.partial(_attn_prompt_kernel, tq=tq, nq=l // tq)
    seq = pl.BlockSpec((1, l, LANES), lambda bi, hp: (bi, 0, hp))
    return pl.pallas_call(
        kernel,
        grid=(b, D_ATT // LANES),
        in_specs=[seq, pl.BlockSpec((1, LANES, l), lambda bi, hp: (bi, hp, 0)), seq,
                  pl.BlockSpec(u2x.shape, lambda bi, hp: (0, 0))],
        out_specs=seq,
        out_shape=jax.ShapeDtypeStruct((b, l, D_ATT), F32),
        scratch_shapes=[pltpu.VMEM((2, 2 * tq, tq), F32), pltpu.VMEM((2, 4 * tq, tq), BF16),
                        pltpu.VMEM((2, tq, tq), F32), pltpu.VMEM((l // tq, 2 * tq, LANES), BF16),
                        pltpu.VMEM((l // tq, 2 * tq, LANES), BF16), pltpu.VMEM((l // tq + 1, tq, LANES), F32)],
        compiler_params=pltpu.CompilerParams(
            dimension_semantics=("arbitrary", "arbitrary"), vmem_limit_bytes=VMEM_LIMIT),
        name="attn_prompt",
    )(qb, kb, vb, u2x)


def _heads_to_lanes(x):
    t = x.shape[0]
    y = jnp.swapaxes(x.reshape(t // SUBLANES, SUBLANES, N_ATT_HEADS, ATT_HEAD_DIM), 1, 2)
    return jnp.concatenate([y[:, v].reshape(t, ATT_HEAD_DIM) for v in range(N_ATT_HEADS)], axis=1)


def _lanes_to_heads(x):
    t = x.shape[0]
    parts = []
    for g in range(D_ATT // LANES):
        a = x[:, g * LANES:(g + 1) * LANES]
        b = pltpu.roll(a, ATT_HEAD_DIM, axis=1)
        parts += [a.reshape(t // SUBLANES, SUBLANES, LANES), b.reshape(t // SUBLANES, SUBLANES, LANES)]
    y = jnp.swapaxes(jnp.stack(parts, axis=1), 1, 2).reshape(t, N_ATT_HEADS, LANES)
    return y[:, :, 0:ATT_HEAD_DIM]


def _attn_sample_kernel(q_ref, kn_ref, vn_ref, ck_hbm, cv_hbm, u2_ref, o_ref, kbuf, vbuf, sem,
                        *, l, tk, chunk, nch, layer):
    bi = pl.program_id(0)
    m = N_ATT_HEADS * l
    hm = _head_lane_masks(D_ATT, BF16)
    q = q_ref[0]
    qs = jnp.concatenate([q * mk for mk in hm], axis=0)
    u2 = u2_ref[...]

    def copies(ch, slot):
        keys = pl.ds(pl.multiple_of((nch - 1 - ch) * chunk, chunk), chunk)
        return (pltpu.make_async_copy(ck_hbm.at[layer, bi, :, :, keys], kbuf.at[slot], sem.at[0, slot]),
                pltpu.make_async_copy(cv_hbm.at[layer, bi, :, :, keys], vbuf.at[slot], sem.at[1, slot]))

    def start(ch, slot):
        for cp in copies(ch, slot):
            cp.start()

    def wait(ch, slot):
        for cp in copies(ch, slot):
            cp.wait()

    start(0, 0)
    pad = jnp.zeros((tk - l, D_ATT), BF16)
    kn = jnp.concatenate([kn_ref[0], pad], axis=0)
    vn = jnp.concatenate([vn_ref[0], pad], axis=0)
    row = jnp.concatenate([lax.broadcasted_iota(jnp.int32, (l, tk), 0)] * N_ATT_HEADS, axis=0)
    col = lax.broadcasted_iota(jnp.int32, (m, tk), 1)
    w, c = _sb_weights(_dot_nt(qs, kn), jnp.zeros((m, 1), F32), u2, col < row)
    acc = _dot(w.astype(BF16), vn)

    def live(c):
        return jnp.max(c) >= SB_EXHAUSTED

    def body(carry):
        ch, c, acc = carry
        slot = ch % 2
        wait(ch, slot)

        @pl.when(ch + 1 < nch)
        def _():
            start(ch + 1, 1 - slot)

        kt = kbuf[slot].reshape(D_ATT, chunk).astype(BF16)
        vt = vbuf[slot].reshape(D_ATT, chunk).astype(BF16)
        for t in reversed(range(chunk // tk)):
            w, c = _sb_weights(_dot(qs, kt[:, t * tk:(t + 1) * tk]), c, u2, None)
            acc = acc + _dot_nt(w.astype(BF16), vt[:, t * tk:(t + 1) * tk])
        return ch + 1, c, acc

    ch, c, acc = lax.while_loop(lambda carry: (carry[0] < nch) & live(carry[1]), body, (jnp.int32(0), c, acc))

    @pl.when(ch < nch)
    def _():
        wait(ch, ch % 2)

    hmf = _head_lane_masks(D_ATT, F32)
    out = acc[0:l] * hmf[0]
    for h in range(1, N_ATT_HEADS):
        out = out + acc[h * l:(h + 1) * l] * hmf[h]
    o_ref[0] = out


def _attn_sample(qb, kb_new, vb_new, cache_k, cache_v, layer):
    b, l, _ = qb.shape
    past = cache_k.shape[2]
    cache_k = jnp.transpose(cache_k, (0, 1, 3, 4, 2))
    cache_v = jnp.transpose(cache_v, (0, 1, 3, 4, 2))
    tk = ATT_TK
    chunk = min(ATT_CACHE_CHUNK, past)
    assert past % chunk == 0 and chunk % tk == 0 and l <= tk and l % 16 == 0
    u2 = _u2_matrix(tk)
    kernel = functools.partial(_attn_sample_kernel, l=l, tk=tk, chunk=chunk, nch=past // chunk, layer=layer)
    new = pl.BlockSpec((1, l, D_ATT), lambda bi: (bi, 0, 0))
    hbm = pl.BlockSpec(memory_space=pl.ANY)
    return pl.pallas_call(
        kernel,
        grid=(b,),
        in_specs=[new, new, new, hbm, hbm, pl.BlockSpec(u2.shape, lambda bi: (0, 0))],
        out_specs=new,
        out_shape=jax.ShapeDtypeStruct((b, l, D_ATT), F32),
        scratch_shapes=[pltpu.VMEM((2, N_ATT_HEADS, ATT_HEAD_DIM, chunk), F32),
                        pltpu.VMEM((2, N_ATT_HEADS, ATT_HEAD_DIM, chunk), F32),
                        pltpu.SemaphoreType.DMA((2, 2))],
        compiler_params=pltpu.CompilerParams(
            dimension_semantics=("arbitrary",), vmem_limit_bytes=VMEM_LIMIT),
        name="attn_sample",
    )(qb, kb_new, vb_new, cache_k, cache_v, u2)


def _ssd_kernel(xbc_ref, z_ref, dt_ref, conv0_ref, s0_ref, cw_ref, cb_ref, alog_ref, dskip_ref, g_ref,
                tri_ref, y_ref, sfin_ref, cbuf, state, *, q):
    c = pl.program_id(1)

    @pl.when(c == 0)
    def _():
        cbuf[...] = conv0_ref[0]
        zero = jnp.zeros((SSM_HEAD_DIM, SSM_STATE), F32)
        for pr in range(N_PAIRS):
            top = jnp.concatenate([s0_ref[0, 2 * pr], zero], axis=1)
            bot = jnp.concatenate([zero, s0_ref[0, 2 * pr + 1]], axis=1)
            state[pr] = jnp.concatenate([top, bot], axis=0)

    x = xbc_ref[0]
    prev = cbuf[...]
    row8 = lax.broadcasted_iota(jnp.int32, (SUBLANES, CONV_DIM), 0)
    xc = cb_ref[...] + x * cw_ref[SSM_CONV - 1:SSM_CONV, :]
    for k in range(1, SSM_CONV):
        r = pltpu.roll(x, k, axis=0)
        head = r[0:SUBLANES]
        for j in range(k):
            head = jnp.where(row8 == j, prev[SUBLANES - k + j:SUBLANES - k + j + 1, :], head)
        r = jnp.concatenate([head, r[SUBLANES:]], axis=0)
        xc = xc + r * cw_ref[SSM_CONV - 1 - k:SSM_CONV - k, :]
    xc = _silu(xc)
    cbuf[...] = x[q - SUBLANES:q]

    lane = lax.broadcasted_iota(jnp.int32, (1, LANES), 1)
    lo_half = lane < SSM_STATE
    bmat = xc[:, D_SSM:D_SSM + LANES]
    cmat = xc[:, D_SSM + LANES:D_SSM + 2 * LANES]
    b_sw = pltpu.roll(bmat, SSM_STATE, axis=1)
    c_sw = pltpu.roll(cmat, SSM_STATE, axis=1)
    bdup = [jnp.where(lo_half, bmat, b_sw), jnp.where(lo_half, b_sw, bmat)]
    cdup = [jnp.where(lo_half, cmat, c_sw), jnp.where(lo_half, c_sw, cmat)]
    gmask = [jnp.where(lo_half, 1.0, 0.0), jnp.where(lo_half, 0.0, 1.0)]
    bmat_b = bmat.astype(BF16)
    cb_g = [_dot_nt((cmat * gmask[g]).astype(BF16), bmat_b) for g in range(SSM_GROUPS)]

    dt = dt_ref[0]
    da = dt * (-jnp.exp(alog_ref[...]))
    tri = tri_ref[...]
    hi, mid, lo = _split3(da)
    acum = _dot(tri, hi) + _dot(tri, mid) + _dot(tri, lo)
    acum_t = acum.T
    dt_t = dt.T
    a_end = acum[q - 1:q, :]
    trow = lax.broadcasted_iota(jnp.int32, (q, q), 0)
    tcol = lax.broadcasted_iota(jnp.int32, (q, q), 1)
    causal = tcol <= trow
    bd_r = lax.broadcasted_iota(jnp.int32, (LANES, LANES), 0) < SSM_STATE
    bd_c = lax.broadcasted_iota(jnp.int32, (LANES, LANES), 1) < SSM_STATE
    block_diag = bd_r == bd_c
    hm_b = [jnp.where(lo_half, 1.0, 0.0).astype(BF16), jnp.where(lo_half, 0.0, 1.0).astype(BF16)]

    ys = []
    ssq = jnp.zeros((q, 1), F32)
    for pr in range(N_PAIRS):
        g = pr // (N_PAIRS // SSM_GROUPS)
        x_pair = xc[:, pr * LANES:(pr + 1) * LANES]
        x_b = x_pair.astype(BF16)
        ms = []
        for hh in range(HEADS_PER_LANE_TILE):
            h = pr * HEADS_PER_LANE_TILE + hh
            seg = acum[:, h:h + 1] - acum_t[h:h + 1, :]
            lmat = jnp.where(causal, jnp.exp(seg), 0.0)
            ms.append((cb_g[g] * lmat * dt_t[h:h + 1, :]).astype(BF16))
        h0 = pr * HEADS_PER_LANE_TILE
        acol = jnp.where(lo_half, acum[:, h0:h0 + 1], acum[:, h0 + 1:h0 + 2])
        dcol = jnp.where(lo_half, dt[:, h0:h0 + 1], dt[:, h0 + 1:h0 + 2])
        aend = jnp.where(lo_half, a_end[:, h0:h0 + 1], a_end[:, h0 + 1:h0 + 2])
        s_pair = state[pr]
        y_diag = _dot(jnp.concatenate(ms, axis=1), jnp.concatenate([x_b * hm_b[0], x_b * hm_b[1]], axis=0))
        y_off = _dot_nt((cdup[g] * jnp.exp(acol)).astype(BF16), s_pair.astype(BF16))
        new = _dot_tn(x_b, (bdup[g] * (jnp.exp(aend - acol) * dcol)).astype(BF16))
        state[pr] = s_pair * jnp.exp(aend) + jnp.where(block_diag, new, 0.0)
        y = y_diag + y_off + dskip_ref[:, pr * LANES:(pr + 1) * LANES] * x_pair
        yz = y * _silu(z_ref[0, :, pr * LANES:(pr + 1) * LANES])
        ssq = ssq + jnp.sum(yz * yz, axis=1, keepdims=True)
        ys.append(yz)

    inv = lax.rsqrt(ssq * (1.0 / D_SSM) + NORM_EPS)
    for pr in range(N_PAIRS):
        y_ref[0, :, pr * LANES:(pr + 1) * LANES] = (
            ys[pr] * inv * g_ref[:, pr * LANES:(pr + 1) * LANES]).astype(BF16)

    @pl.when(c == pl.num_programs(1) - 1)
    def _():
        for pr in range(N_PAIRS):
            sfin_ref[0, 2 * pr] = state[pr, 0:SSM_HEAD_DIM, 0:SSM_STATE]
            sfin_ref[0, 2 * pr + 1] = state[pr, SSM_HEAD_DIM:LANES, SSM_STATE:LANES]


def _ssd(xbc, z, dt, conv0, s0, conv_w, conv_b, a_log_pad, dskip_lanes, g_ssm):
    b, l, _ = xbc.shape
    q = min(SSD_Q, l)
    assert l % q == 0 and q % SUBLANES == 0
    tri = (jnp.arange(q)[None, :] <= jnp.arange(q)[:, None]).astype(BF16)
    seq = lambda n: pl.BlockSpec((1, q, n), lambda bi, ci: (bi, ci, 0))
    per_b = lambda a: pl.BlockSpec((1,) + a.shape[1:], lambda bi, ci: (bi,) + (0,) * (a.ndim - 1))
    full = lambda a: pl.BlockSpec(a.shape, lambda bi, ci: (0,) * a.ndim)
    kernel = functools.partial(_ssd_kernel, q=q)
    return pl.pallas_call(
        kernel,
        grid=(b, l // q),
        in_specs=[seq(CONV_DIM), seq(D_SSM), seq(LANES), per_b(conv0), per_b(s0),
                  full(conv_w), full(conv_b), full(a_log_pad), full(dskip_lanes), full(g_ssm), full(tri)],
        out_specs=(seq(D_SSM), per_b(s0)),
        out_shape=(jax.ShapeDtypeStruct((b, l, D_SSM), BF16), jax.ShapeDtypeStruct(s0.shape, F32)),
        scratch_shapes=[pltpu.VMEM((SUBLANES, CONV_DIM), F32),
                        pltpu.VMEM((N_PAIRS, LANES, LANES), F32)],
        compiler_params=pltpu.CompilerParams(
            dimension_semantics=("arbitrary", "arbitrary"), vmem_limit_bytes=VMEM_LIMIT),
        name="ssd",
    )(xbc, z, dt, conv0, s0, conv_w, conv_b, a_log_pad, dskip_lanes, g_ssm, tri)


def _gelu_tanh(x):
    return 0.5 * x * (1.0 + jnp.tanh(math.sqrt(2.0 / math.pi) * (x + 0.044715 * (x * x * x))))


def _out_ffn_kernel(x_ref, attn_ref, ys_ref, fc0_ref, ga_ref, gpost_ref, gpre_ref, gfpost_ref,
                    woa_ref, wos_ref, wgu_ref, wd_ref, cw_ref,
                    y_ref, fcn_ref, h2_buf, acc, carry, g_buf, u_buf, *, n_seq, lt):
    t = pl.program_id(1)

    @pl.when(t == 0)
    def _():
        carry[...] = fc0_ref[0]

    tm = x_ref.shape[0]
    for r0 in range(0, tm, FFN_ROWS):
        rows = slice(r0, r0 + FFN_ROWS)
        an = _rmsnorm(attn_ref[rows, :], ga_ref[...]).astype(BF16)
        m = _dot(an, woa_ref[...]) + _dot(ys_ref[rows, :], wos_ref[...])
        x1 = x_ref[rows, :] + _rmsnorm(m, gpost_ref[...])
        y_ref[rows, :] = x1
        h2_buf[rows, :] = _rmsnorm(x1, gpre_ref[...]).astype(BF16)
    row = lax.broadcasted_iota(jnp.int32, (lt, FFN_F), 0)

    def cols(ci, base=0):
        start = base + ci * FFN_F
        return pl.ds(start if isinstance(ci, int) else pl.multiple_of(start, FFN_F), FFN_F)

    def stage1(ci, slot):
        h2 = h2_buf[...]
        g_buf[slot] = _dot(h2, wgu_ref[:, cols(ci)])
        u_buf[slot] = _dot(h2, wgu_ref[:, cols(ci, D_FF)])

    def stage2(ci, slot):
        gate = g_buf[slot]
        cw = cw_ref[:, cols(ci)]
        acts = []
        for s in range(n_seq):
            gs = gate[s * lt:(s + 1) * lt]
            prev = carry[ci, s * SUBLANES:(s + 1) * SUBLANES, :]
            p1 = prev[SUBLANES - 1:SUBLANES, :]
            p2 = prev[SUBLANES - 2:SUBLANES - 1, :]
            g1 = jnp.where(row == 0, p1, pltpu.roll(gs, 1, axis=0))
            g2 = jnp.where(row == 0, p2, jnp.where(row == 1, p1, pltpu.roll(gs, 2, axis=0)))
            gc = cw[3:4, :] + g2 * cw[0:1, :] + g1 * cw[1:2, :] + gs * cw[2:3, :]
            carry[ci, s * SUBLANES:(s + 1) * SUBLANES, :] = gs[lt - SUBLANES:lt]
            acts.append(_gelu_tanh(gc))
        act = acts[0] if n_seq == 1 else jnp.concatenate(acts, axis=0)
        acc[...] += _dot((act * u_buf[slot]).astype(BF16), wd_ref[cols(ci), :])

    acc[...] = jnp.zeros_like(acc)
    stage1(0, 0)

    def body(p, _):
        c = 2 * p + 1
        stage1(c, 1)
        stage2(c - 1, 0)
        stage1(c + 1, 0)
        stage2(c, 1)
        return 0

    assert FFN_NC % 2 == 1
    lax.fori_loop(0, FFN_NC // 2, body, 0)
    stage2(FFN_NC - 1, 0)
    for r0 in range(0, tm, FFN_ROWS):
        rows = slice(r0, r0 + FFN_ROWS)
        y_ref[rows, :] = y_ref[rows, :] + _rmsnorm(acc[rows, :], gfpost_ref[...])
    fcn_ref[0] = carry[...]


def _out_ffn(x2d, attn2d, ys2d, fc0, n_seq, lt, gains, weights):
    t = x2d.shape[0]
    tm = n_seq * lt
    n_groups = fc0.shape[0]
    tiles = t // (tm * n_groups)
    assert tiles * tm * n_groups == t
    row = lambda n: pl.BlockSpec((tm, n), lambda gi, ti: (gi * tiles + ti, 0))
    full = lambda a: pl.BlockSpec(a.shape, lambda gi, ti: (0,) * a.ndim, pipeline_mode=pl.Buffered(1))
    fc_spec = pl.BlockSpec((1,) + fc0.shape[1:], lambda gi, ti: (gi, 0, 0, 0))
    kernel = functools.partial(_out_ffn_kernel, n_seq=n_seq, lt=lt)
    return pl.pallas_call(
        kernel,
        grid=(n_groups, tiles),
        in_specs=[row(D_MODEL), row(D_ATT), row(D_SSM), fc_spec] + [full(a) for a in gains]
                 + [full(a) for a in weights],
        out_specs=(row(D_MODEL), fc_spec),
        out_shape=(jax.ShapeDtypeStruct((t, D_MODEL), F32), jax.ShapeDtypeStruct(fc0.shape, F32)),
        scratch_shapes=[pltpu.VMEM((tm, D_MODEL), BF16), pltpu.VMEM((tm, D_MODEL), F32),
                        pltpu.VMEM(fc0.shape[1:], F32), pltpu.VMEM((2, tm, FFN_F), F32),
                        pltpu.VMEM((2, tm, FFN_F), F32)],
        compiler_params=pltpu.CompilerParams(
            dimension_semantics=("arbitrary", "arbitrary"), vmem_limit_bytes=VMEM_LIMIT),
        name="out_ffn",
    )(x2d, attn2d, ys2d, fc0, *gains, *weights)


def _ffn_state_to_chunks(st, n_seq):
    b = st.shape[0]
    s = st.reshape(b // n_seq, n_seq, FFN_CONV - 1, FFN_NC, FFN_F)
    s = jnp.pad(s, ((0, 0), (0, 0), (SUBLANES - (FFN_CONV - 1), 0), (0, 0), (0, 0)))
    return jnp.transpose(s, (0, 3, 1, 2, 4)).reshape(b // n_seq, FFN_NC, n_seq * SUBLANES, FFN_F)


def _chunks_to_ffn_state(ch, n_seq):
    g = ch.shape[0]
    s = ch.reshape(g, FFN_NC, n_seq, SUBLANES, FFN_F)[:, :, :, SUBLANES - (FFN_CONV - 1):, :]
    return jnp.transpose(s, (0, 2, 3, 1, 4)).reshape(g * n_seq, FFN_CONV - 1, D_FF)


def _layer(x, caches, ssm_h0, ssm_conv_prev, ffn_conv_prev, p, is_prompt):
    b, l, _ = x.shape
    t = b * l
    x2d = x.reshape(t, D_MODEL)
    r3 = lambda a: a.reshape(b, l, a.shape[-1])
    proj_w = (p['g_mix_pre'], p['w_in_t'], p['w_dt_t'], p['dt_bias'])
    if is_prompt:
        qb, kb, vb, kt, vt, z, xbc, dt = _in_proj(x, *proj_w, kv_transposed=True)
        attn = _attn_prompt(qb, kb, vb)
        k, v = jnp.transpose(kt, (0, 3, 1, 2)), jnp.transpose(vt, (0, 3, 1, 2))
    else:
        qb, kb, vb, k, v, z, xbc, dt = _in_proj(x2d[None], *proj_w, kv_transposed=False)
        attn = _attn_sample(r3(qb), r3(kb), r3(vb), *caches)

    conv0 = jnp.pad(ssm_conv_prev, ((0, 0), (SUBLANES - (SSM_CONV - 1), 0), (0, 0)))
    l_ssd = l if l % SSD_Q == 0 else SSD_Q * pl.cdiv(l, SSD_Q)
    pad_t = lambda a: jnp.pad(r3(a), ((0, 0), (0, l_ssd - l), (0, 0)))
    ys, s_fin = _ssd(pad_t(xbc), pad_t(z), pad_t(dt), conv0, ssm_h0,
                     p['ssm_conv_w'], p['ssm_conv_b'], p['a_log'], p['d_skip'], p['g_ssm_out'])
    ys = ys[:, :l, :]
    ssm_conv_new = r3(xbc)[:, l - (SSM_CONV - 1):, :]

    if is_prompt:
        n_seq, lt = 1, min(FFN_TM, l)
    else:
        n_seq, lt = b, l
    fc0 = _ffn_state_to_chunks(ffn_conv_prev, n_seq)
    gains = (p['g_attn_out'], p['g_mix_post'], p['g_ffn_pre'], p['g_ffn_post'])
    weights = (p['w_out_a'], p['w_out_s'], p['w_gu'], p['w_down'], p['ffn_cw'])
    y2d, fcn = _out_ffn(x2d, attn.reshape(t, D_ATT), ys.reshape(t, D_SSM), fc0, n_seq, lt, gains, weights)
    return (y2d.reshape(b, l, D_MODEL), k.reshape(b, l, N_ATT_HEADS, ATT_HEAD_DIM),
            v.reshape(b, l, N_ATT_HEADS, ATT_HEAD_DIM), s_fin, ssm_conv_new,
            _chunks_to_ffn_state(fcn, n_seq))


def _prep_params(i, g_mix_pre, g_mix_post, w_in, ssm_conv_w, ssm_conv_b, dt_bias, a_log, d_skip,
                 g_ssm_out, g_attn_out, w_out, g_ffn_pre, g_ffn_post, w_up, ffn_conv_w, ffn_conv_b, w_down):
    row = lambda a: a[i].reshape(1, -1).astype(F32)
    pad_lanes = lambda a: jnp.pad(a, ((0, 0), (0, LANES - a.shape[1])))
    wi_t = w_in[i].T.astype(BF16)
    ffn_cw = jnp.concatenate([ffn_conv_w[i], ffn_conv_b[i][None, :],
                              jnp.zeros((SUBLANES - FFN_CONV - 1, D_FF), F32)], axis=0)
    return {
        'g_mix_pre': row(g_mix_pre), 'g_mix_post': row(g_mix_post),
        'g_ffn_pre': row(g_ffn_pre), 'g_ffn_post': row(g_ffn_post),
        'g_attn_out': row(g_attn_out), 'g_ssm_out': row(g_ssm_out),
        'w_in_t': wi_t,
        'w_dt_t': jnp.pad(wi_t[D_MAIN_PROJ:], ((0, LANES - (wi_t.shape[0] - D_MAIN_PROJ)), (0, 0))),
        'dt_bias': pad_lanes(row(dt_bias)),
        'ssm_conv_w': ssm_conv_w[i].astype(F32), 'ssm_conv_b': row(ssm_conv_b),
        'a_log': pad_lanes(row(a_log)),
        'd_skip': jnp.repeat(d_skip[i].astype(F32), SSM_HEAD_DIM).reshape(1, D_SSM),
        'w_out_a': w_out[i][:D_ATT].astype(BF16), 'w_out_s': w_out[i][D_ATT:].astype(BF16),
        'w_gu': w_up[i].astype(BF16),
        'w_down': w_down[i].astype(BF16),
        'ffn_cw': ffn_cw,
    }


def kernel(x_prompt, x_sample, cache_k, cache_v, state_ssm, state_ssm_conv, state_ffn_conv, g_mix_pre, g_mix_post, w_in, ssm_conv_w, ssm_conv_b, dt_bias, a_log, d_skip, g_ssm_out, g_attn_out, w_out, g_ffn_pre, g_ffn_post, w_up, ffn_conv_w, ffn_conv_b, w_down):
    depth = w_in.shape[0]
    bp = x_prompt.shape[0]
    dtp = x_prompt.dtype
    zh = jnp.zeros((bp, N_SSM_HEADS, SSM_HEAD_DIM, SSM_STATE), dtp)
    zcs = jnp.zeros((bp, SSM_CONV - 1, CONV_DIM), dtp)
    zcf = jnp.zeros((bp, FFN_CONV - 1, D_FF), dtp)
    y_p, y_s = x_prompt, x_sample
    outs_p, outs_s = [], []
    for i in range(depth):
        p = _prep_params(i, g_mix_pre, g_mix_post, w_in, ssm_conv_w, ssm_conv_b, dt_bias, a_log, d_skip,
                         g_ssm_out, g_attn_out, w_out, g_ffn_pre, g_ffn_post, w_up, ffn_conv_w, ffn_conv_b,
                         w_down)
        rp = _layer(y_p, None, zh, zcs, zcf, p, True)
        rs = _layer(y_s, (cache_k, cache_v, i), state_ssm[i], state_ssm_conv[i], state_ffn_conv[i], p, False)
        y_p, y_s = rp[0], rs[0]
        outs_p.append(rp[1:])
        outs_s.append(rs[1:])
    stack = lambda outs, j: jnp.stack([o[j] for o in outs])
    return (y_p, y_s) + tuple(stack(outs_p, j) for j in range(5)) + tuple(stack(outs_s, j) for j in range(5))
```

```python
import functools
import math

import jax
import jax.numpy as jnp
from jax import lax
from jax.experimental import pallas as pl
from jax.experimental.pallas import tpu as pltpu

F32 = jnp.float32
BF16 = jnp.bfloat16

D_MODEL = 1024
D_ATT = 512
N_ATT_HEADS = 8
ATT_HEAD_DIM = 64
D_SSM = 512
N_SSM_HEADS = 8
SSM_HEAD_DIM = 64
SSM_STATE = 64
SSM_GROUPS = 2
SSM_CONV = 4
CONV_DIM = D_SSM + 2 * SSM_GROUPS * SSM_STATE
D_FF = 2816
FFN_CONV = 3
NORM_EPS = 1e-6
D_MAIN_PROJ = 3 * D_ATT + D_SSM + CONV_DIM
ATT_SCALE = ATT_HEAD_DIM ** -0.5
LOG2E = math.log2(math.e)
SB_EXHAUSTED = -120.0

LANES = 128
SUBLANES = 8
HEADS_PER_LANE_TILE = LANES // ATT_HEAD_DIM
N_PAIRS = N_SSM_HEADS // HEADS_PER_LANE_TILE

PROJ_TM = 512
ATT_TQ = 256
ATT_TK = 128
ATT_CACHE_CHUNK = 512
SSD_Q = 128
FFN_TM = 512
FFN_F = 256
FFN_ROWS = 128
FFN_NC = D_FF // FFN_F
VMEM_LIMIT = 56 * 1024 * 1024


def _rmsnorm(x, g):
    y = x * lax.rsqrt(jnp.mean(x * x, axis=-1, keepdims=True) + NORM_EPS)
    return y * g


def _softplus(x):
    return jnp.maximum(x, 0.0) + jnp.log1p(jnp.exp(-jnp.abs(x)))


def _silu(x):
    return x * (1.0 / (1.0 + jnp.exp(-x)))


def _dot(a, b):
    return jnp.dot(a, b, preferred_element_type=F32)


def _dot_nt(a, b):
    return lax.dot_general(a, b, (((1,), (1,)), ((), ())), preferred_element_type=F32)


def _dot_tn(a, b):
    return lax.dot_general(a, b, (((0,), (0,)), ((), ())), preferred_element_type=F32)


def _split2(x):
    hi = x.astype(BF16)
    lo = (x - hi.astype(F32)).astype(BF16)
    return hi, lo


def _split3(x):
    hi = x.astype(BF16)
    r1 = x - hi.astype(F32)
    mid = r1.astype(BF16)
    lo = (r1 - mid.astype(F32)).astype(BF16)
    return hi, mid, lo


def _in_proj_kernel(x_ref, g_ref, wt_ref, wdt_ref, dtb_ref,
                    qb_ref, kb_ref, vb_ref, k_ref, v_ref, z_ref, xbc_ref, dt_ref, *, kv_transposed):
    h = _rmsnorm(x_ref[0], g_ref[...]).astype(BF16)

    def proj(lo, hi):
        return _dot_nt(h, wt_ref[lo:hi, :])

    qb_ref[0] = (proj(0, D_ATT) * ATT_SCALE).astype(BF16)
    v = proj(2 * D_ATT, 3 * D_ATT)
    vb_ref[0] = v.astype(BF16)
    if kv_transposed:
        tm = v.shape[0]
        kt = _dot_nt(wt_ref[D_ATT:2 * D_ATT, :], h)
        kb_ref[0] = kt.astype(BF16)
        k_ref[0] = kt.reshape(N_ATT_HEADS, ATT_HEAD_DIM, tm)
        v_ref[0] = v.T.reshape(N_ATT_HEADS, ATT_HEAD_DIM, tm)
    else:
        k = proj(D_ATT, 2 * D_ATT)
        kb_ref[0] = k.astype(BF16)
        k_ref[0] = _lanes_to_heads(k)
        v_ref[0] = _lanes_to_heads(v)
    z_ref[0] = proj(3 * D_ATT, 3 * D_ATT + D_SSM)
    xbc_ref[0] = proj(3 * D_ATT + D_SSM, D_MAIN_PROJ)
    dt_ref[0] = _softplus(_dot_nt(h, wdt_ref[...]) + dtb_ref[...])


def _in_proj(x, g, w_t, w_dt, dt_bias, kv_transposed):
    b, l, _ = x.shape
    tm = min(PROJ_TM, l)
    assert l % tm == 0
    row = lambda n: pl.BlockSpec((1, tm, n), lambda bi, i: (bi, i, 0))
    full = lambda a: pl.BlockSpec(a.shape, lambda bi, i: (0,) * a.ndim)
    sds = jax.ShapeDtypeStruct
    if kv_transposed:
        kb_shape, kb_spec = sds((b, D_ATT, l), BF16), pl.BlockSpec((1, D_ATT, tm), lambda bi, i: (bi, 0, i))
        kv_shape = sds((b, N_ATT_HEADS, ATT_HEAD_DIM, l), F32)
        kv_spec = pl.BlockSpec((1, N_ATT_HEADS, ATT_HEAD_DIM, tm), lambda bi, i: (bi, 0, 0, i))
    else:
        kb_shape, kb_spec = sds((b, l, D_ATT), BF16), row(D_ATT)
        kv_shape = sds((b, l, N_ATT_HEADS, ATT_HEAD_DIM), F32)
        kv_spec = pl.BlockSpec((1, tm, N_ATT_HEADS, ATT_HEAD_DIM), lambda bi, i: (bi, i, 0, 0))
    out_shape = (sds((b, l, D_ATT), BF16), kb_shape, sds((b, l, D_ATT), BF16), kv_shape, kv_shape,
                 sds((b, l, D_SSM), F32), sds((b, l, CONV_DIM), F32), sds((b, l, LANES), F32))
    return pl.pallas_call(
        functools.partial(_in_proj_kernel, kv_transposed=kv_transposed),
        grid=(b, l // tm),
        in_specs=[row(D_MODEL), full(g), full(w_t), full(w_dt), full(dt_bias)],
        out_specs=(row(D_ATT), kb_spec, row(D_ATT), kv_spec, kv_spec,
                   row(D_SSM), row(CONV_DIM), row(LANES)),
        out_shape=out_shape,
        compiler_params=pltpu.CompilerParams(
            dimension_semantics=("arbitrary", "arbitrary"), vmem_limit_bytes=VMEM_LIMIT),
        name="in_proj",
    )(x, g, w_t, w_dt, dt_bias)


def _sb_weights(s, c, u2, mask):
    lk = -(jnp.maximum(s, 0.0) + jnp.log(1.0 + jnp.exp(-jnp.abs(s))))
    if mask is not None:
        lk = jnp.where(mask, lk, 0.0)
    hi, lo = _split2(lk)
    r = _dot(jnp.concatenate([hi, lo], axis=1), u2) + c
    w = jnp.exp(s + r)
    if mask is not None:
        w = jnp.where(mask, w, 0.0)
    return w, c + jnp.sum(lk, axis=1, keepdims=True)


def _head_lane_masks(n_lanes, dtype):
    lane = lax.broadcasted_iota(jnp.int32, (1, n_lanes), 1)
    return [jnp.where((lane >= h * ATT_HEAD_DIM) & (lane < (h + 1) * ATT_HEAD_DIM), 1.0, 0.0).astype(dtype)
            for h in range(n_lanes // ATT_HEAD_DIM)]


def _attn_prompt_kernel(q_ref, kt_ref, v_ref, u2x_ref, o_ref, s_buf, hl_buf, mask_buf, qs_buf, vcat_buf,
                        out_buf, *, tq, nq):
    tk = tq // 2
    hm = _head_lane_masks(LANES, BF16)
    big = 1e30
    causal = lax.broadcasted_iota(jnp.int32, (tq, tq), 1) < lax.broadcasted_iota(jnp.int32, (tq, tq), 0)
    mask_buf[0] = jnp.full((tq, tq), -jnp.inf, F32)
    mask_buf[1] = jnp.where(causal, -jnp.inf, big)
    for i in range(nq):
        qn = -q_ref[0, i * tq:(i + 1) * tq, :]
        vsb = v_ref[0, i * tq:(i + 1) * tq, :]
        for h in range(HEADS_PER_LANE_TILE):
            qs_buf[i, h * tq:(h + 1) * tq, :] = qn * hm[h]
            for n, half in enumerate((1, 0)):
                r0 = (n * HEADS_PER_LANE_TILE + h) * tk
                vcat_buf[i, r0:r0 + tk, :] = vsb[half * tk:(half + 1) * tk] * hm[h]

    def advance(i, j, skip_rest):
        last = (j == 0) | skip_rest
        return jnp.where(last, i + 1, i), jnp.where(last, i + 1, j - 1)

    def exhausted_after(c_prev, row_sums, first):
        carry = jnp.where(first, 0.0, c_prev[:, 0:1]) + row_sums
        return jnp.max(carry) < SB_EXHAUSTED

    def row_off(idx):
        return pl.multiple_of(jnp.minimum(idx, nq - 1) * tq, tq)

    def stage_a(i, j, slot):
        kb = kt_ref[0, :, pl.ds(row_off(j), tq)]
        t = _dot(qs_buf[jnp.minimum(i, nq - 1)], kb)
        floor = mask_buf[(i == j).astype(jnp.int32)]
        row_sums = []
        for h in range(HEADS_PER_LANE_TILE):
            th = jnp.maximum(t[h * tq:(h + 1) * tq], floor)
            s_buf[slot, h * tq:(h + 1) * tq, :] = th
            e = jnp.exp2(jnp.abs(th) * (-LOG2E))
            lk = jnp.minimum(th, 0.0) - jnp.log(1.0 + e)
            row_sums.append(jnp.sum(lk, axis=1, keepdims=True))
            hi, lo = _split2(lk)
            for half in range(2):
                r0 = half * 2 * tq + h * tq
                hl_buf[slot, r0:r0 + tq, 0:tk] = hi[:, half * tk:(half + 1) * tk]
                hl_buf[slot, r0:r0 + tq, tk:tq] = lo[:, half * tk:(half + 1) * tk]
        return jnp.concatenate(row_sums, axis=0)

    def stage_b(i, j, slot, c, acc):
        rr = _dot(hl_buf[slot], u2x_ref[...])
        first = i == j
        c = jnp.where(first, 0.0, c)
        acc = jnp.where(first, 0.0, acc)
        ws = []
        for half in (1, 0):
            r0 = half * 2 * tq
            r = rr[r0:r0 + 2 * tq, 0:tk] + c
            th = s_buf[slot, :, half * tk:(half + 1) * tk]
            w = jnp.exp(r - th).astype(BF16)
            c = c + rr[r0:r0 + 2 * tq, tk:tq]
            ws += [w[0:tq], w[tq:2 * tq]]
        acc = acc + _dot(jnp.concatenate(ws, axis=1), vcat_buf[jnp.minimum(j, nq - 1)])
        out_buf[jnp.minimum(i, nq)] = acc
        return c, acc

    def two_items(carry):
        ib, jb, ia, ja, c, acc = carry
        rs = stage_a(ia, ja, 1)
        c, acc = stage_b(ib, jb, 0, c, acc)
        i2, j2 = advance(ia, ja, exhausted_after(c, rs, ia == ja))
        rs = stage_a(i2, j2, 0)
        c, acc = stage_b(ia, ja, 1, c, acc)
        i3, j3 = advance(i2, j2, exhausted_after(c, rs, i2 == j2))
        return i2, j2, i3, j3, c, acc

    def body(carry):
        return two_items(two_items(carry))

    z = jnp.int32(0)
    stage_a(z, z, 0)
    init = (z, z, z + 1, z + 1, jnp.zeros((2 * tq, tk), F32), jnp.zeros((tq, LANES), F32))
    lax.while_loop(lambda carry: carry[0] < nq, body, init)
    for i in range(nq):
        o_ref[0, i * tq:(i + 1) * tq, :] = out_buf[i]


def _u2_matrix(tk):
    j = jnp.arange(2 * tk)[:, None] % tk
    s = jnp.arange(tk)[None, :]
    return (j >= s).astype(BF16)


def _attn_prompt(qb, kb, vb):
    b, l, _ = qb.shape
    tq = min(ATT_TQ, l)
    assert l % tq == 0 and tq == 2 * ATT_TK
    u2x = jnp.concatenate([_u2_matrix(ATT_TK), jnp.ones((tq, ATT_TK), BF16)], axis=1)
    kernel = functools.partial(_attn_prompt_kernel, tq=tq, nq=l // tq)
    seq = pl.BlockSpec((1, l, LANES), lambda bi, hp: (bi, 0, hp))
    return pl.pallas_call(
        kernel,
        grid=(b, D_ATT // LANES),
        in_specs=[seq, pl.BlockSpec((1, LANES, l), lambda bi, hp: (bi, hp, 0)), seq,
                  pl.BlockSpec(u2x.shape, lambda bi, hp: (0, 0))],
        out_specs=seq,
        out_shape=jax.ShapeDtypeStruct((b, l, D_ATT), F32),
        scratch_shapes=[pltpu.VMEM((2, 2 * tq, tq), F32), pltpu.VMEM((2, 4 * tq, tq), BF16),
                        pltpu.VMEM((2, tq, tq), F32), pltpu.VMEM((l // tq, 2 * tq, LANES), BF16),
                        pltpu.VMEM((l // tq, 2 * tq, LANES), BF16), pltpu.VMEM((l // tq + 1, tq, LANES), F32)],
        compiler_params=pltpu.CompilerParams(
            dimension_semantics=("arbitrary", "arbitrary"), vmem_limit_bytes=VMEM_LIMIT),
        name="attn_prompt",
    )(qb, kb, vb, u2x)


def _heads_to_lanes(x):
    t = x.shape[0]
    y = jnp.swapaxes(x.reshape(t // SUBLANES, SUBLANES, N_ATT_HEADS, ATT_HEAD_DIM), 1, 2)
    return jnp.concatenate([y[:, v].reshape(t, ATT_HEAD_DIM) for v in range(N_ATT_HEADS)], axis=1)


def _lanes_to_heads(x):
    t = x.shape[0]
    parts = []
    for g in range(D_ATT // LANES):
        a = x[:, g * LANES:(g + 1) * LANES]
        b = pltpu.roll(a, ATT_HEAD_DIM, axis=1)
        parts += [a.reshape(t // SUBLANES, SUBLANES, LANES), b.reshape(t // SUBLANES, SUBLANES, LANES)]
    y = jnp.swapaxes(jnp.stack(parts, axis=1), 1, 2).reshape(t, N_ATT_HEADS, LANES)
    return y[:, :, 0:ATT_HEAD_DIM]


def _attn_sample_kernel(q_ref, kn_ref, vn_ref, ck_hbm, cv_hbm, u2_ref, o_ref, kbuf, vbuf, sem,
                        *, l, tk, chunk, nch, layer):
    bi = pl.program_id(0)
    m = N_ATT_HEADS * l
    hm = _head_lane_masks(D_ATT, BF16)
    q = q_ref[0]
    qs = jnp.concatenate([q * mk for mk in hm], axis=0)
    u2 = u2_ref[...]

    def copies(ch, slot):
        keys = pl.ds(pl.multiple_of((nch - 1 - ch) * chunk, chunk), chunk)
        return (pltpu.make_async_copy(ck_hbm.at[layer, bi, :, :, keys], kbuf.at[slot], sem.at[0, slot]),
                pltpu.make_async_copy(cv_hbm.at[layer, bi, :, :, keys], vbuf.at[slot], sem.at[1, slot]))

    def start(ch, slot):
        for cp in copies(ch, slot):
            cp.start()

    def wait(ch, slot):
        for cp in copies(ch, slot):
            cp.wait()

    start(0, 0)
    pad = jnp.zeros((tk - l, D_ATT), BF16)
    kn = jnp.concatenate([kn_ref[0], pad], axis=0)
    vn = jnp.concatenate([vn_ref[0], pad], axis=0)
    row = jnp.concatenate([lax.broadcasted_iota(jnp.int32, (l, tk), 0)] * N_ATT_HEADS, axis=0)
    col = lax.broadcasted_iota(jnp.int32, (m, tk), 1)
    w, c = _sb_weights(_dot_nt(qs, kn), jnp.zeros((m, 1), F32), u2, col < row)
    acc = _dot(w.astype(BF16), vn)

    def live(c):
        return jnp.max(c) >= SB_EXHAUSTED

    def body(carry):
        ch, c, acc = carry
        slot = ch % 2
        wait(ch, slot)

        @pl.when(ch + 1 < nch)
        def _():
            start(ch + 1, 1 - slot)

        kt = kbuf[slot].reshape(D_ATT, chunk).astype(BF16)
        vt = vbuf[slot].reshape(D_ATT, chunk).astype(BF16)
        for t in reversed(range(chunk // tk)):
            w, c = _sb_weights(_dot(qs, kt[:, t * tk:(t + 1) * tk]), c, u2, None)
            acc = acc + _dot_nt(w.astype(BF16), vt[:, t * tk:(t + 1) * tk])
        return ch + 1, c, acc

    ch, c, acc = lax.while_loop(lambda carry: (carry[0] < nch) & live(carry[1]), body, (jnp.int32(0), c, acc))

    @pl.when(ch < nch)
    def _():
        wait(ch, ch % 2)

    hmf = _head_lane_masks(D_ATT, F32)
    out = acc[0:l] * hmf[0]
    for h in range(1, N_ATT_HEADS):
        out = out + acc[h * l:(h + 1) * l] * hmf[h]
    o_ref[0] = out


def _attn_sample(qb, kb_new, vb_new, cache_k, cache_v, layer):
    b, l, _ = qb.shape
    past = cache_k.shape[2]
    cache_k = jnp.transpose(cache_k, (0, 1, 3, 4, 2))
    cache_v = jnp.transpose(cache_v, (0, 1, 3, 4, 2))
    tk = ATT_TK
    chunk = min(ATT_CACHE_CHUNK, past)
    assert past % chunk == 0 and chunk % tk == 0 and l <= tk and l % 16 == 0
    u2 = _u2_matrix(tk)
    kernel = functools.partial(_attn_sample_kernel, l=l, tk=tk, chunk=chunk, nch=past // chunk, layer=layer)
    new = pl.BlockSpec((1, l, D_ATT), lambda bi: (bi, 0, 0))
    hbm = pl.BlockSpec(memory_space=pl.ANY)
    return pl.pallas_call(
        kernel,
        grid=(b,),
        in_specs=[new, new, new, hbm, hbm, pl.BlockSpec(u2.shape, lambda bi: (0, 0))],
        out_specs=new,
        out_shape=jax.ShapeDtypeStruct((b, l, D_ATT), F32),
        scratch_shapes=[pltpu.VMEM((2, N_ATT_HEADS, ATT_HEAD_DIM, chunk), F32),
                        pltpu.VMEM((2, N_ATT_HEADS, ATT_HEAD_DIM, chunk), F32),
                        pltpu.SemaphoreType.DMA((2, 2))],
        compiler_params=pltpu.CompilerParams(
            dimension_semantics=("arbitrary",), vmem_limit_bytes=VMEM_LIMIT),
        name="attn_sample",
    )(qb, kb_new, vb_new, cache_k, cache_v, u2)


def _ssd_kernel(xbc_ref, z_ref, dt_ref, conv0_ref, s0_ref, cw_ref, cb_ref, alog_ref, dskip_ref, g_ref,
                tri_ref, y_ref, sfin_ref, cbuf, state, *, q):
    c = pl.program_id(1)

    @pl.when(c == 0)
    def _():
        cbuf[...] = conv0_ref[0]
        zero = jnp.zeros((SSM_HEAD_DIM, SSM_STATE), F32)
        for pr in range(N_PAIRS):
            top = jnp.concatenate([s0_ref[0, 2 * pr], zero], axis=1)
            bot = jnp.concatenate([zero, s0_ref[0, 2 * pr + 1]], axis=1)
            state[pr] = jnp.concatenate([top, bot], axis=0)

    x = xbc_ref[0]
    prev = cbuf[...]
    row8 = lax.broadcasted_iota(jnp.int32, (SUBLANES, CONV_DIM), 0)
    xc = cb_ref[...] + x * cw_ref[SSM_CONV - 1:SSM_CONV, :]
    for k in range(1, SSM_CONV):
        r = pltpu.roll(x, k, axis=0)
        head = r[0:SUBLANES]
        for j in range(k):
            head = jnp.where(row8 == j, prev[SUBLANES - k + j:SUBLANES - k + j + 1, :], head)
        r = jnp.concatenate([head, r[SUBLANES:]], axis=0)
        xc = xc + r * cw_ref[SSM_CONV - 1 - k:SSM_CONV - k, :]
    xc = _silu(xc)
    cbuf[...] = x[q - SUBLANES:q]

    lane = lax.broadcasted_iota(jnp.int32, (1, LANES), 1)
    lo_half = lane < SSM_STATE
    bmat = xc[:, D_SSM:D_SSM + LANES]
    cmat = xc[:, D_SSM + LANES:D_SSM + 2 * LANES]
    b_sw = pltpu.roll(bmat, SSM_STATE, axis=1)
    c_sw = pltpu.roll(cmat, SSM_STATE, axis=1)
    bdup = [jnp.where(lo_half, bmat, b_sw), jnp.where(lo_half, b_sw, bmat)]
    cdup = [jnp.where(lo_half, cmat, c_sw), jnp.where(lo_half, c_sw, cmat)]
    gmask = [jnp.where(lo_half, 1.0, 0.0), jnp.where(lo_half, 0.0, 1.0)]
    bmat_b = bmat.astype(BF16)
    cb_g = [_dot_nt((cmat * gmask[g]).astype(BF16), bmat_b) for g in range(SSM_GROUPS)]

    dt = dt_ref[0]
    da = dt * (-jnp.exp(alog_ref[...]))
    tri = tri_ref[...]
    hi, mid, lo = _split3(da)
    acum = _dot(tri, hi) + _dot(tri, mid) + _dot(tri, lo)
    acum_t = acum.T
    dt_t = dt.T
    a_end = acum[q - 1:q, :]
    trow = lax.broadcasted_iota(jnp.int32, (q, q), 0)
    tcol = lax.broadcasted_iota(jnp.int32, (q, q), 1)
    causal = tcol <= trow
    bd_r = lax.broadcasted_iota(jnp.int32, (LANES, LANES), 0) < SSM_STATE
    bd_c = lax.broadcasted_iota(jnp.int32, (LANES, LANES), 1) < SSM_STATE
    block_diag = bd_r == bd_c
    hm_b = [jnp.where(lo_half, 1.0, 0.0).astype(BF16), jnp.where(lo_half, 0.0, 1.0).astype(BF16)]

    ys = []
    ssq = jnp.zeros((q, 1), F32)
    for pr in range(N_PAIRS):
        g = pr // (N_PAIRS // SSM_GROUPS)
        x_pair = xc[:, pr * LANES:(pr + 1) * LANES]
        x_b = x_pair.astype(BF16)
        ms = []
        for hh in range(HEADS_PER_LANE_TILE):
            h = pr * HEADS_PER_LANE_TILE + hh
            seg = acum[:, h:h + 1] - acum_t[h:h + 1, :]
            lmat = jnp.where(causal, jnp.exp(seg), 0.0)
            ms.append((cb_g[g] * lmat * dt_t[h:h + 1, :]).astype(BF16))
        h0 = pr * HEADS_PER_LANE_TILE
        acol = jnp.where(lo_half, acum[:, h0:h0 + 1], acum[:, h0 + 1:h0 + 2])
        dcol = jnp.where(lo_half, dt[:, h0:h0 + 1], dt[:, h0 + 1:h0 + 2])
        aend = jnp.where(lo_half, a_end[:, h0:h0 + 1], a_end[:, h0 + 1:h0 + 2])
        s_pair = state[pr]
        y_diag = _dot(jnp.concatenate(ms, axis=1), jnp.concatenate([x_b * hm_b[0], x_b * hm_b[1]], axis=0))
        y_off = _dot_nt((cdup[g] * jnp.exp(acol)).astype(BF16), s_pair.astype(BF16))
        new = _dot_tn(x_b, (bdup[g] * (jnp.exp(aend - acol) * dcol)).astype(BF16))
        state[pr] = s_pair * jnp.exp(aend) + jnp.where(block_diag, new, 0.0)
        y = y_diag + y_off + dskip_ref[:, pr * LANES:(pr + 1) * LANES] * x_pair
        yz = y * _silu(z_ref[0, :, pr * LANES:(pr + 1) * LANES])
        ssq = ssq + jnp.sum(yz * yz, axis=1, keepdims=True)
        ys.append(yz)

    inv = lax.rsqrt(ssq * (1.0 / D_SSM) + NORM_EPS)
    for pr in range(N_PAIRS):
        y_ref[0, :, pr * LANES:(pr + 1) * LANES] = (
            ys[pr] * inv * g_ref[:, pr * LANES:(pr + 1) * LANES]).astype(BF16)

    @pl.when(c == pl.num_programs(1) - 1)
    def _():
        for pr in range(N_PAIRS):
            sfin_ref[0, 2 * pr] = state[pr, 0:SSM_HEAD_DIM, 0:SSM_STATE]
            sfin_ref[0, 2 * pr + 1] = state[pr, SSM_HEAD_DIM:LANES, SSM_STATE:LANES]


def _ssd(xbc, z, dt, conv0, s0, conv_w, conv_b, a_log_pad, dskip_lanes, g_ssm):
    b, l, _ = xbc.shape
    q = min(SSD_Q, l)
    assert l % q == 0 and q % SUBLANES == 0
    tri = (jnp.arange(q)[None, :] <= jnp.arange(q)[:, None]).astype(BF16)
    seq = lambda n: pl.BlockSpec((1, q, n), lambda bi, ci: (bi, ci, 0))
    per_b = lambda a: pl.BlockSpec((1,) + a.shape[1:], lambda bi, ci: (bi,) + (0,) * (a.ndim - 1))
    full = lambda a: pl.BlockSpec(a.shape, lambda bi, ci: (0,) * a.ndim)
    kernel = functools.partial(_ssd_kernel, q=q)
    return pl.pallas_call(
        kernel,
        grid=(b, l // q),
        in_specs=[seq(CONV_DIM), seq(D_SSM), seq(LANES), per_b(conv0), per_b(s0),
                  full(conv_w), full(conv_b), full(a_log_pad), full(dskip_lanes), full(g_ssm), full(tri)],
        out_specs=(seq(D_SSM), per_b(s0)),
        out_shape=(jax.ShapeDtypeStruct((b, l, D_SSM), BF16), jax.ShapeDtypeStruct(s0.shape, F32)),
        scratch_shapes=[pltpu.VMEM((SUBLANES, CONV_DIM), F32),
                        pltpu.VMEM((N_PAIRS, LANES, LANES), F32)],
        compiler_params=pltpu.CompilerParams(
            dimension_semantics=("arbitrary", "arbitrary"), vmem_limit_bytes=VMEM_LIMIT),
        name="ssd",
    )(xbc, z, dt, conv0, s0, conv_w, conv_b, a_log_pad, dskip_lanes, g_ssm, tri)


def _gelu_tanh(x):
    return 0.5 * x * (1.0 + jnp.tanh(math.sqrt(2.0 / math.pi) * (x + 0.044715 * (x * x * x))))


def _out_ffn_kernel(x_ref, attn_ref, ys_ref, fc0_ref, ga_ref, gpost_ref, gpre_ref, gfpost_ref,
                    woa_ref, wos_ref, wgu_ref, wd_ref, cw_ref,
                    y_ref, fcn_ref, h2_buf, acc, carry, g_buf, u_buf, *, n_seq, lt):
    t = pl.program_id(1)

    @pl.when(t == 0)
    def _():
        carry[...] = fc0_ref[0]

    tm = x_ref.shape[0]
    for r0 in range(0, tm, FFN_ROWS):
        rows = slice(r0, r0 + FFN_ROWS)
        an = _rmsnorm(attn_ref[rows, :], ga_ref[...]).astype(BF16)
        m = _dot(an, woa_ref[...]) + _dot(ys_ref[rows, :], wos_ref[...])
        x1 = x_ref[rows, :] + _rmsnorm(m, gpost_ref[...])
        y_ref[rows, :] = x1
        h2_buf[rows, :] = _rmsnorm(x1, gpre_ref[...]).astype(BF16)
    row = lax.broadcasted_iota(jnp.int32, (lt, FFN_F), 0)

    def cols(ci, base=0):
        start = base + ci * FFN_F
        return pl.ds(start if isinstance(ci, int) else pl.multiple_of(start, FFN_F), FFN_F)

    def stage1(ci, slot):
        h2 = h2_buf[...]
        g_buf[slot] = _dot(h2, wgu_ref[:, cols(ci)])
        u_buf[slot] = _dot(h2, wgu_ref[:, cols(ci, D_FF)])

    def stage2(ci, slot):
        gate = g_buf[slot]
        cw = cw_ref[:, cols(ci)]
        acts = []
        for s in range(n_seq):
            gs = gate[s * lt:(s + 1) * lt]
            prev = carry[ci, s * SUBLANES:(s + 1) * SUBLANES, :]
            p1 = prev[SUBLANES - 1:SUBLANES, :]
            p2 = prev[SUBLANES - 2:SUBLANES - 1, :]
            g1 = jnp.where(row == 0, p1, pltpu.roll(gs, 1, axis=0))
            g2 = jnp.where(row == 0, p2, jnp.where(row == 1, p1, pltpu.roll(gs, 2, axis=0)))
            gc = cw[3:4, :] + g2 * cw[0:1, :] + g1 * cw[1:2, :] + gs * cw[2:3, :]
            carry[ci, s * SUBLANES:(s + 1) * SUBLANES, :] = gs[lt - SUBLANES:lt]
            acts.append(_gelu_tanh(gc))
        act = acts[0] if n_seq == 1 else jnp.concatenate(acts, axis=0)
        acc[...] += _dot((act * u_buf[slot]).astype(BF16), wd_ref[cols(ci), :])

    acc[...] = jnp.zeros_like(acc)
    stage1(0, 0)

    def body(p, _):
        c = 2 * p + 1
        stage1(c, 1)
        stage2(c - 1, 0)
        stage1(c + 1, 0)
        stage2(c, 1)
        return 0

    assert FFN_NC % 2 == 1
    lax.fori_loop(0, FFN_NC // 2, body, 0)
    stage2(FFN_NC - 1, 0)
    for r0 in range(0, tm, FFN_ROWS):
        rows = slice(r0, r0 + FFN_ROWS)
        y_ref[rows, :] = y_ref[rows, :] + _rmsnorm(acc[rows, :], gfpost_ref[...])
    fcn_ref[0] = carry[...]


def _out_ffn(x2d, attn2d, ys2d, fc0, n_seq, lt, gains, weights):
    t = x2d.shape[0]
    tm = n_seq * lt
    n_groups = fc0.shape[0]
    tiles = t // (tm * n_groups)
    assert tiles * tm * n_groups == t
    row = lambda n: pl.BlockSpec((tm, n), lambda gi, ti: (gi * tiles + ti, 0))
    full = lambda a: pl.BlockSpec(a.shape, lambda gi, ti: (0,) * a.ndim, pipeline_mode=pl.Buffered(1))
    fc_spec = pl.BlockSpec((1,) + fc0.shape[1:], lambda gi, ti: (gi, 0, 0, 0))
    kernel = functools.partial(_out_ffn_kernel, n_seq=n_seq, lt=lt)
    return pl.pallas_call(
        kernel,
        grid=(n_groups, tiles),
        in_specs=[row(D_MODEL), row(D_ATT), row(D_SSM), fc_spec] + [full(a) for a in gains]
                 + [full(a) for a in weights],
        out_specs=(row(D_MODEL), fc_spec),
        out_shape=(jax.ShapeDtypeStruct((t, D_MODEL), F32), jax.ShapeDtypeStruct(fc0.shape, F32)),
        scratch_shapes=[pltpu.VMEM((tm, D_MODEL), BF16), pltpu.VMEM((tm, D_MODEL), F32),
                        pltpu.VMEM(fc0.shape[1:], F32), pltpu.VMEM((2, tm, FFN_F), F32),
                        pltpu.VMEM((2, tm, FFN_F), F32)],
        compiler_params=pltpu.CompilerParams(
            dimension_semantics=("arbitrary", "arbitrary"), vmem_limit_bytes=VMEM_LIMIT),
        name="out_ffn",
    )(x2d, attn2d, ys2d, fc0, *gains, *weights)


def _ffn_state_to_chunks(st, n_seq):
    b = st.shape[0]
    s = st.reshape(b // n_seq, n_seq, FFN_CONV - 1, FFN_NC, FFN_F)
    s = jnp.pad(s, ((0, 0), (0, 0), (SUBLANES - (FFN_CONV - 1), 0), (0, 0), (0, 0)))
    return jnp.transpose(s, (0, 3, 1, 2, 4)).reshape(b // n_seq, FFN_NC, n_seq * SUBLANES, FFN_F)


def _chunks_to_ffn_state(ch, n_seq):
    g = ch.shape[0]
    s = ch.reshape(g, FFN_NC, n_seq, SUBLANES, FFN_F)[:, :, :, SUBLANES - (FFN_CONV - 1):, :]
    return jnp.transpose(s, (0, 2, 3, 1, 4)).reshape(g * n_seq, FFN_CONV - 1, D_FF)


def _layer(x, caches, ssm_h0, ssm_conv_prev, ffn_conv_prev, p, is_prompt):
    b, l, _ = x.shape
    t = b * l
    x2d = x.reshape(t, D_MODEL)
    r3 = lambda a: a.reshape(b, l, a.shape[-1])
    proj_w = (p['g_mix_pre'], p['w_in_t'], p['w_dt_t'], p['dt_bias'])
    if is_prompt:
        qb, kb, vb, kt, vt, z, xbc, dt = _in_proj(x, *proj_w, kv_transposed=True)
        attn = _attn_prompt(qb, kb, vb)
        k, v = jnp.transpose(kt, (0, 3, 1, 2)), jnp.transpose(vt, (0, 3, 1, 2))
    else:
        qb, kb, vb, k, v, z, xbc, dt = _in_proj(x2d[None], *proj_w, kv_transposed=False)
        attn = _attn_sample(r3(qb), r3(kb), r3(vb), *caches)

    conv0 = jnp.pad(ssm_conv_prev, ((0, 0), (SUBLANES - (SSM_CONV - 1), 0), (0, 0)))
    l_ssd = l if l % SSD_Q == 0 else SSD_Q * pl.cdiv(l, SSD_Q)
    pad_t = lambda a: jnp.pad(r3(a), ((0, 0), (0, l_ssd - l), (0, 0)))
    ys, s_fin = _ssd(pad_t(xbc), pad_t(z), pad_t(dt), conv0, ssm_h0,
                     p['ssm_conv_w'], p['ssm_conv_b'], p['a_log'], p['d_skip'], p['g_ssm_out'])
    ys = ys[:, :l, :]
    ssm_conv_new = r3(xbc)[:, l - (SSM_CONV - 1):, :]

    if is_prompt:
        n_seq, lt = 1, min(FFN_TM, l)
    else:
        n_seq, lt = b, l
    fc0 = _ffn_state_to_chunks(ffn_conv_prev, n_seq)
    gains = (p['g_attn_out'], p['g_mix_post'], p['g_ffn_pre'], p['g_ffn_post'])
    weights = (p['w_out_a'], p['w_out_s'], p['w_gu'], p['w_down'], p['ffn_cw'])
    y2d, fcn = _out_ffn(x2d, attn.reshape(t, D_ATT), ys.reshape(t, D_SSM), fc0, n_seq, lt, gains, weights)
    return (y2d.reshape(b, l, D_MODEL), k.reshape(b, l, N_ATT_HEADS, ATT_HEAD_DIM),
            v.reshape(b, l, N_ATT_HEADS, ATT_HEAD_DIM), s_fin, ssm_conv_new,
            _chunks_to_ffn_state(fcn, n_seq))


def _prep_params(i, g_mix_pre, g_mix_post, w_in, ssm_conv_w, ssm_conv_b, dt_bias, a_log, d_skip,
                 g_ssm_out, g_attn_out, w_out, g_ffn_pre, g_ffn_post, w_up, ffn_conv_w, ffn_conv_b, w_down):
    row = lambda a: a[i].reshape(1, -1).astype(F32)
    pad_lanes = lambda a: jnp.pad(a, ((0, 0), (0, LANES - a.shape[1])))
    wi_t = w_in[i].T.astype(BF16)
    ffn_cw = jnp.concatenate([ffn_conv_w[i], ffn_conv_b[i][None, :],
                              jnp.zeros((SUBLANES - FFN_CONV - 1, D_FF), F32)], axis=0)
    return {
        'g_mix_pre': row(g_mix_pre), 'g_mix_post': row(g_mix_post),
        'g_ffn_pre': row(g_ffn_pre), 'g_ffn_post': row(g_ffn_post),
        'g_attn_out': row(g_attn_out), 'g_ssm_out': row(g_ssm_out),
        'w_in_t': wi_t,
        'w_dt_t': jnp.pad(wi_t[D_MAIN_PROJ:], ((0, LANES - (wi_t.shape[0] - D_MAIN_PROJ)), (0, 0))),
        'dt_bias': pad_lanes(row(dt_bias)),
        'ssm_conv_w': ssm_conv_w[i].astype(F32), 'ssm_conv_b': row(ssm_conv_b),
        'a_log': pad_lanes(row(a_log)),
        'd_skip': jnp.repeat(d_skip[i].astype(F32), SSM_HEAD_DIM).reshape(1, D_SSM),
        'w_out_a': w_out[i][:D_ATT].astype(BF16), 'w_out_s': w_out[i][D_ATT:].astype(BF16),
        'w_gu': w_up[i].astype(BF16),
        'w_down': w_down[i].astype(BF16),
        'ffn_cw': ffn_cw,
    }


def kernel(x_prompt, x_sample, cache_k, cache_v, state_ssm, state_ssm_conv, state_ffn_conv, g_mix_pre, g_mix_post, w_in, ssm_conv_w, ssm_conv_b, dt_bias, a_log, d_skip, g_ssm_out, g_attn_out, w_out, g_ffn_pre, g_ffn_post, w_up, ffn_conv_w, ffn_conv_b, w_down):
    depth = w_in.shape[0]
    bp = x_prompt.shape[0]
    dtp = x_prompt.dtype
    zh = jnp.zeros((bp, N_SSM_HEADS, SSM_HEAD_DIM, SSM_STATE), dtp)
    zcs = jnp.zeros((bp, SSM_CONV - 1, CONV_DIM), dtp)
    zcf = jnp.zeros((bp, FFN_CONV - 1, D_FF), dtp)
    y_p, y_s = x_prompt, x_sample
    outs_p, outs_s = [], []
    for i in range(depth):
        p = _prep_params(i, g_mix_pre, g_mix_post, w_in, ssm_conv_w, ssm_conv_b, dt_bias, a_log, d_skip,
                         g_ssm_out, g_attn_out, w_out, g_ffn_pre, g_ffn_post, w_up, ffn_conv_w, ffn_conv_b,
                         w_down)
        rp = _layer(y_p, None, zh, zcs, zcf, p, True)
        rs = _layer(y_s, (cache_k, cache_v, i), state_ssm[i], state_ssm_conv[i], state_ffn_conv[i], p, False)
        y_p, y_s = rp[0], rs[0]
        outs_p.append(rp[1:])
        outs_s.append(rs[1:])
    stack = lambda outs, j: jnp.stack([o[j] for o in outs])
    return (y_p, y_s) + tuple(stack(outs_p, j) for j in range(5)) + tuple(stack(outs_s, j) for j in range(5))
```

```python
import functools
import math

import jax
import jax.numpy as jnp
from jax import lax
from jax.experimental import pallas as pl
from jax.experimental.pallas import tpu as pltpu

F32 = jnp.float32
BF16 = jnp.bfloat16

D_MODEL = 1024
D_ATT = 512
N_ATT_HEADS = 8
ATT_HEAD_DIM = 64
D_SSM = 512
N_SSM_HEADS = 8
SSM_HEAD_DIM = 64
SSM_STATE = 64
SSM_GROUPS = 2
SSM_CONV = 4
CONV_DIM = D_SSM + 2 * SSM_GROUPS * SSM_STATE
D_FF = 2816
FFN_CONV = 3
NORM_EPS = 1e-6
D_MAIN_PROJ = 3 * D_ATT + D_SSM + CONV_DIM
ATT_SCALE = ATT_HEAD_DIM ** -0.5
LOG2E = math.log2(math.e)
SB_EXHAUSTED = -120.0

LANES = 128
SUBLANES = 8
HEADS_PER_LANE_TILE = LANES // ATT_HEAD_DIM
N_PAIRS = N_SSM_HEADS // HEADS_PER_LANE_TILE

PROJ_TM = 512
ATT_TQ = 256
ATT_TK = 128
ATT_CACHE_CHUNK = 512
SSD_Q = 128
FFN_TM = 512
FFN_F = 256
FFN_ROWS = 128
FFN_NC = D_FF // FFN_F
VMEM_LIMIT = 56 * 1024 * 1024


def _rmsnorm(x, g):
    y = x * lax.rsqrt(jnp.mean(x * x, axis=-1, keepdims=True) + NORM_EPS)
    return y * g


def _softplus(x):
    return jnp.maximum(x, 0.0) + jnp.log1p(jnp.exp(-jnp.abs(x)))


def _silu(x):
    return x * (1.0 / (1.0 + jnp.exp(-x)))


def _dot(a, b):
    return jnp.dot(a, b, preferred_element_type=F32)


def _dot_nt(a, b):
    return lax.dot_general(a, b, (((1,), (1,)), ((), ())), preferred_element_type=F32)


def _dot_tn(a, b):
    return lax.dot_general(a, b, (((0,), (0,)), ((), ())), preferred_element_type=F32)


def _split2(x):
    hi = x.astype(BF16)
    lo = (x - hi.astype(F32)).astype(BF16)
    return hi, lo


def _split3(x):
    hi = x.astype(BF16)
    r1 = x - hi.astype(F32)
    mid = r1.astype(BF16)
    lo = (r1 - mid.astype(F32)).astype(BF16)
    return hi, mid, lo


def _in_proj_kernel(x_ref, g_ref, wt_ref, wdt_ref, dtb_ref,
                    qb_ref, kb_ref, vb_ref, k_ref, v_ref, z_ref, xbc_ref, dt_ref, *, kv_transposed):
    h = _rmsnorm(x_ref[0], g_ref[...]).astype(BF16)

    def proj(lo, hi):
        return _dot_nt(h, wt_ref[lo:hi, :])

    qb_ref[0] = (proj(0, D_ATT) * ATT_SCALE).astype(BF16)
    v = proj(2 * D_ATT, 3 * D_ATT)
    vb_ref[0] = v.astype(BF16)
    if kv_transposed:
        tm = v.shape[0]
        kt = _dot_nt(wt_ref[D_ATT:2 * D_ATT, :], h)
        kb_ref[0] = kt.astype(BF16)
        k_ref[0] = kt.reshape(N_ATT_HEADS, ATT_HEAD_DIM, tm)
        v_ref[0] = v.T.reshape(N_ATT_HEADS, ATT_HEAD_DIM, tm)
    else:
        k = proj(D_ATT, 2 * D_ATT)
        kb_ref[0] = k.astype(BF16)
        k_ref[0] = _lanes_to_heads(k)
        v_ref[0] = _lanes_to_heads(v)
    z_ref[0] = proj(3 * D_ATT, 3 * D_ATT + D_SSM)
    xbc_ref[0] = proj(3 * D_ATT + D_SSM, D_MAIN_PROJ)
    dt_ref[0] = _softplus(_dot_nt(h, wdt_ref[...]) + dtb_ref[...])


def _in_proj(x, g, w_t, w_dt, dt_bias, kv_transposed):
    b, l, _ = x.shape
    tm = min(PROJ_TM, l)
    assert l % tm == 0
    row = lambda n: pl.BlockSpec((1, tm, n), lambda bi, i: (bi, i, 0))
    full = lambda a: pl.BlockSpec(a.shape, lambda bi, i: (0,) * a.ndim)
    sds = jax.ShapeDtypeStruct
    if kv_transposed:
        kb_shape, kb_spec = sds((b, D_ATT, l), BF16), pl.BlockSpec((1, D_ATT, tm), lambda bi, i: (bi, 0, i))
        kv_shape = sds((b, N_ATT_HEADS, ATT_HEAD_DIM, l), F32)
        kv_spec = pl.BlockSpec((1, N_ATT_HEADS, ATT_HEAD_DIM, tm), lambda bi, i: (bi, 0, 0, i))
    else:
        kb_shape, kb_spec = sds((b, l, D_ATT), BF16), row(D_ATT)
        kv_shape = sds((b, l, N_ATT_HEADS, ATT_HEAD_DIM), F32)
        kv_spec = pl.BlockSpec((1, tm, N_ATT_HEADS, ATT_HEAD_DIM), lambda bi, i: (bi, i, 0, 0))
    out_shape = (sds((b, l, D_ATT), BF16), kb_shape, sds((b, l, D_ATT), BF16), kv_shape, kv_shape,
                 sds((b, l, D_SSM), F32), sds((b, l, CONV_DIM), F32), sds((b, l, LANES), F32))
    return pl.pallas_call(
        functools.partial(_in_proj_kernel, kv_transposed=kv_transposed),
        grid=(b, l // tm),
        in_specs=[row(D_MODEL), full(g), full(w_t), full(w_dt), full(dt_bias)],
        out_specs=(row(D_ATT), kb_spec, row(D_ATT), kv_spec, kv_spec,
                   row(D_SSM), row(CONV_DIM), row(LANES)),
        out_shape=out_shape,
        compiler_params=pltpu.CompilerParams(
            dimension_semantics=("arbitrary", "arbitrary"), vmem_limit_bytes=VMEM_LIMIT),
        name="in_proj",
    )(x, g, w_t, w_dt, dt_bias)


def _sb_weights(s, c, u2, mask):
    lk = -(jnp.maximum(s, 0.0) + jnp.log(1.0 + jnp.exp(-jnp.abs(s))))
    if mask is not None:
        lk = jnp.where(mask, lk, 0.0)
    hi, lo = _split2(lk)
    r = _dot(jnp.concatenate([hi, lo], axis=1), u2) + c
    w = jnp.exp(s + r)
    if mask is not None:
        w = jnp.where(mask, w, 0.0)
    return w, c + jnp.sum(lk, axis=1, keepdims=True)


def _head_lane_masks(n_lanes, dtype):
    lane = lax.broadcasted_iota(jnp.int32, (1, n_lanes), 1)
    return [jnp.where((lane >= h * ATT_HEAD_DIM) & (lane < (h + 1) * ATT_HEAD_DIM), 1.0, 0.0).astype(dtype)
            for h in range(n_lanes // ATT_HEAD_DIM)]


def _attn_prompt_kernel(q_ref, kt_ref, v_ref, u2x_ref, o_ref, s_buf, hl_buf, mask_buf, qs_buf, vcat_buf,
                        out_buf, *, tq, nq):
    tk = tq // 2
    hm = _head_lane_masks(LANES, BF16)
    big = 1e30
    causal = lax.broadcasted_iota(jnp.int32, (tq, tq), 1) < lax.broadcasted_iota(jnp.int32, (tq, tq), 0)
    mask_buf[0] = jnp.full((tq, tq), -jnp.inf, F32)
    mask_buf[1] = jnp.where(causal, -jnp.inf, big)
    for i in range(nq):
        qn = -q_ref[0, i * tq:(i + 1) * tq, :]
        vsb = v_ref[0, i * tq:(i + 1) * tq, :]
        for h in range(HEADS_PER_LANE_TILE):
            qs_buf[i, h * tq:(h + 1) * tq, :] = qn * hm[h]
            for n, half in enumerate((1, 0)):
                r0 = (n * HEADS_PER_LANE_TILE + h) * tk
                vcat_buf[i, r0:r0 + tk, :] = vsb[half * tk:(half + 1) * tk] * hm[h]

    def advance(i, j, skip_rest):
        last = (j == 0) | skip_rest
        return jnp.where(last, i + 1, i), jnp.where(last, i + 1, j - 1)

    def exhausted_after(c_prev, row_sums, first):
        carry = jnp.where(first, 0.0, c_prev[:, 0:1]) + row_sums
        return jnp.max(carry) < SB_EXHAUSTED

    def row_off(idx):
        return pl.multiple_of(jnp.minimum(idx, nq - 1) * tq, tq)

    def stage_a(i, j, slot):
        kb = kt_ref[0, :, pl.ds(row_off(j), tq)]
        t = _dot(qs_buf[jnp.minimum(i, nq - 1)], kb)
        floor = mask_buf[(i == j).astype(jnp.int32)]
        row_sums = []
        for h in range(HEADS_PER_LANE_TILE):
            th = jnp.maximum(t[h * tq:(h + 1) * tq], floor)
            s_buf[slot, h * tq:(h + 1) * tq, :] = th
            e = jnp.exp2(jnp.abs(th) * (-LOG2E))
            lk = jnp.minimum(th, 0.0) - jnp.log(1.0 + e)
            row_sums.append(jnp.sum(lk, axis=1, keepdims=True))
            hi, lo = _split2(lk)
            for half in range(2):
                r0 = half * 2 * tq + h * tq
                hl_buf[slot, r0:r0 + tq, 0:tk] = hi[:, half * tk:(half + 1) * tk]
                hl_buf[slot, r0:r0 + tq, tk:tq] = lo[:, half * tk:(half + 1) * tk]
        return jnp.concatenate(row_sums, axis=0)

    def stage_b(i, j, slot, c, acc):
        rr = _dot(hl_buf[slot], u2x_ref[...])
        first = i == j
        c = jnp.where(first, 0.0, c)
        acc = jnp.where(first, 0.0, acc)
        ws = []
        for half in (1, 0):
            r0 = half * 2 * tq
            r = rr[r0:r0 + 2 * tq, 0:tk] + c
            th = s_buf[slot, :, half * tk:(half + 1) * tk]
            w = jnp.exp(r - th).astype(BF16)
            c = c + rr[r0:r0 + 2 * tq, tk:tq]
            ws += [w[0:tq], w[tq:2 * tq]]
        acc = acc + _dot(jnp.concatenate(ws, axis=1), vcat_buf[jnp.minimum(j, nq - 1)])
        out_buf[jnp.minimum(i, nq)] = acc
        return c, acc

    def two_items(carry):
        ib, jb, ia, ja, c, acc = carry
        rs = stage_a(ia, ja, 1)
        c, acc = stage_b(ib, jb, 0, c, acc)
        i2, j2 = advance(ia, ja, exhausted_after(c, rs, ia == ja))
        rs = stage_a(i2, j2, 0)
        c, acc = stage_b(ia, ja, 1, c, acc)
        i3, j3 = advance(i2, j2, exhausted_after(c, rs, i2 == j2))
        return i2, j2, i3, j3, c, acc

    def body(carry):
        return two_items(two_items(carry))

    z = jnp.int32(0)
    stage_a(z, z, 0)
    init = (z, z, z + 1, z + 1, jnp.zeros((2 * tq, tk), F32), jnp.zeros((tq, LANES), F32))
    lax.while_loop(lambda carry: carry[0] < nq, body, init)
    for i in range(nq):
        o_ref[0, i * tq:(i + 1) * tq, :] = out_buf[i]


def _u2_matrix(tk):
    j = jnp.arange(2 * tk)[:, None] % tk
    s = jnp.arange(tk)[None, :]
    return (j >= s).astype(BF16)


def _attn_prompt(qb, kb, vb):
    b, l, _ = qb.shape
    tq = min(ATT_TQ, l)
    assert l % tq == 0 and tq == 2 * ATT_TK
    u2x = jnp.concatenate([_u2_matrix(ATT_TK), jnp.ones((tq, ATT_TK), BF16)], axis=1)
    kernel = functools.partial(_attn_prompt_kernel, tq=tq, nq=l // tq)
    seq = pl.BlockSpec((1, l, LANES), lambda bi, hp: (bi, 0, hp))
    return pl.pallas_call(
        kernel,
        grid=(b, D_ATT // LANES),
        in_specs=[seq, pl.BlockSpec((1, LANES, l), lambda bi, hp: (bi, hp, 0)), seq,
                  pl.BlockSpec(u2x.shape, lambda bi, hp: (0, 0))],
        out_specs=seq,
        out_shape=jax.ShapeDtypeStruct((b, l, D_ATT), F32),
        scratch_shapes=[pltpu.VMEM((2, 2 * tq, tq), F32), pltpu.VMEM((2, 4 * tq, tq), BF16),
                        pltpu.VMEM((2, tq, tq), F32), pltpu.VMEM((l // tq, 2 * tq, LANES), BF16),
                        pltpu.VMEM((l // tq, 2 * tq, LANES), BF16), pltpu.VMEM((l // tq + 1, tq, LANES), F32)],
        compiler_params=pltpu.CompilerParams(
            dimension_semantics=("arbitrary", "arbitrary"), vmem_limit_bytes=VMEM_LIMIT),
        name="attn_prompt",
    )(qb, kb, vb, u2x)


def _heads_to_lanes(x):
    t = x.shape[0]
    y = jnp.swapaxes(x.reshape(t // SUBLANES, SUBLANES, N_ATT_HEADS, ATT_HEAD_DIM), 1, 2)
    return jnp.concatenate([y[:, v].reshape(t, ATT_HEAD_DIM) for v in range(N_ATT_HEADS)], axis=1)


def _lanes_to_heads(x):
    t = x.shape[0]
    parts = []
    for g in range(D_ATT // LANES):
        a = x[:, g * LANES:(g + 1) * LANES]
        b = pltpu.roll(a, ATT_HEAD_DIM, axis=1)
        parts += [a.reshape(t // SUBLANES, SUBLANES, LANES), b.reshape(t // SUBLANES, SUBLANES, LANES)]
    y = jnp.swapaxes(jnp.stack(parts, axis=1), 1, 2).reshape(t, N_ATT_HEADS, LANES)
    return y[:, :, 0:ATT_HEAD_DIM]


def _attn_sample_kernel(q_ref, kn_ref, vn_ref, ck_hbm, cv_hbm, u2_ref, o_ref, kbuf, vbuf, sem,
                        *, l, tk, chunk, nch, layer):
    bi = pl.program_id(0)
    m = N_ATT_HEADS * l
    hm = _head_lane_masks(D_ATT, BF16)
    q = q_ref[0]
    qs = jnp.concatenate([q * mk for mk in hm], axis=0)
    u2 = u2_ref[...]

    def copies(ch, slot):
        keys = pl.ds(pl.multiple_of((nch - 1 - ch) * chunk, chunk), chunk)
        return (pltpu.make_async_copy(ck_hbm.at[layer, bi, :, :, keys], kbuf.at[slot], sem.at[0, slot]),
                pltpu.make_async_copy(cv_hbm.at[layer, bi, :, :, keys], vbuf.at[slot], sem.at[1, slot]))

    def start(ch, slot):
        for cp in copies(ch, slot):
            cp.start()

    def wait(ch, slot):
        for cp in copies(ch, slot):
            cp.wait()

    start(0, 0)
    pad = jnp.zeros((tk - l, D_ATT), BF16)
    kn = jnp.concatenate([kn_ref[0], pad], axis=0)
    vn = jnp.concatenate([vn_ref[0], pad], axis=0)
    row = jnp.concatenate([lax.broadcasted_iota(jnp.int32, (l, tk), 0)] * N_ATT_HEADS, axis=0)
    col = lax.broadcasted_iota(jnp.int32, (m, tk), 1)
    w, c = _sb_weights(_dot_nt(qs, kn), jnp.zeros((m, 1), F32), u2, col < row)
    acc = _dot(w.astype(BF16), vn)

    def live(c):
        return jnp.max(c) >= SB_EXHAUSTED

    def body(carry):
        ch, c, acc = carry
        slot = ch % 2
        wait(ch, slot)

        @pl.when(ch + 1 < nch)
        def _():
            start(ch + 1, 1 - slot)

        kt = kbuf[slot].reshape(D_ATT, chunk).astype(BF16)
        vt = vbuf[slot].reshape(D_ATT, chunk).astype(BF16)
        for t in reversed(range(chunk // tk)):
            w, c = _sb_weights(_dot(qs, kt[:, t * tk:(t + 1) * tk]), c, u2, None)
            acc = acc + _dot_nt(w.astype(BF16), vt[:, t * tk:(t + 1) * tk])
        return ch + 1, c, acc

    ch, c, acc = lax.while_loop(lambda carry: (carry[0] < nch) & live(carry[1]), body, (jnp.int32(0), c, acc))

    @pl.when(ch < nch)
    def _():
        wait(ch, ch % 2)

    hmf = _head_lane_masks(D_ATT, F32)
    out = acc[0:l] * hmf[0]
    for h in range(1, N_ATT_HEADS):
        out = out + acc[h * l:(h + 1) * l] * hmf[h]
    o_ref[0] = out


def _attn_sample(qb, kb_new, vb_new, cache_k, cache_v, layer):
    b, l, _ = qb.shape
    past = cache_k.shape[2]
    cache_k = jnp.transpose(cache_k, (0, 1, 3, 4, 2))
    cache_v = jnp.transpose(cache_v, (0, 1, 3, 4, 2))
    tk = ATT_TK
    chunk = min(ATT_CACHE_CHUNK, past)
    assert past % chunk == 0 and chunk % tk == 0 and l <= tk and l % 16 == 0
    u2 = _u2_matrix(tk)
    kernel = functools.partial(_attn_sample_kernel, l=l, tk=tk, chunk=chunk, nch=past // chunk, layer=layer)
    new = pl.BlockSpec((1, l, D_ATT), lambda bi: (bi, 0, 0))
    hbm = pl.BlockSpec(memory_space=pl.ANY)
    return pl.pallas_call(
        kernel,
        grid=(b,),
        in_specs=[new, new, new, hbm, hbm, pl.BlockSpec(u2.shape, lambda bi: (0, 0))],
        out_specs=new,
        out_shape=jax.ShapeDtypeStruct((b, l, D_ATT), F32),
        scratch_shapes=[pltpu.VMEM((2, N_ATT_HEADS, ATT_HEAD_DIM, chunk), F32),
                        pltpu.VMEM((2, N_ATT_HEADS, ATT_HEAD_DIM, chunk), F32),
                        pltpu.SemaphoreType.DMA((2, 2))],
        compiler_params=pltpu.CompilerParams(
            dimension_semantics=("arbitrary",), vmem_limit_bytes=VMEM_LIMIT),
        name="attn_sample",
    )(qb, kb_new, vb_new, cache_k, cache_v, u2)


def _ssd_kernel(xbc_ref, z_ref, dt_ref, conv0_ref, s0_ref, cw_ref, cb_ref, alog_ref, dskip_ref, g_ref,
                tri_ref, y_ref, sfin_ref, cbuf, state, *, q):
    c = pl.program_id(1)

    @pl.when(c == 0)
    def _():
        cbuf[...] = conv0_ref[0]
        zero = jnp.zeros((SSM_HEAD_DIM, SSM_STATE), F32)
        for pr in range(N_PAIRS):
            top = jnp.concatenate([s0_ref[0, 2 * pr], zero], axis=1)
            bot = jnp.concatenate([zero, s0_ref[0, 2 * pr + 1]], axis=1)
            state[pr] = jnp.concatenate([top, bot], axis=0)

    x = xbc_ref[0]
    prev = cbuf[...]
    row8 = lax.broadcasted_iota(jnp.int32, (SUBLANES, CONV_DIM), 0)
    xc = cb_ref[...] + x * cw_ref[SSM_CONV - 1:SSM_CONV, :]
    for k in range(1, SSM_CONV):
        r = pltpu.roll(x, k, axis=0)
        head = r[0:SUBLANES]
        for j in range(k):
            head = jnp.where(row8 == j, prev[SUBLANES - k + j:SUBLANES - k + j + 1, :], head)
        r = jnp.concatenate([head, r[SUBLANES:]], axis=0)
        xc = xc + r * cw_ref[SSM_CONV - 1 - k:SSM_CONV - k, :]
    xc = _silu(xc)
    cbuf[...] = x[q - SUBLANES:q]

    lane = lax.broadcasted_iota(jnp.int32, (1, LANES), 1)
    lo_half = lane < SSM_STATE
    bmat = xc[:, D_SSM:D_SSM + LANES]
    cmat = xc[:, D_SSM + LANES:D_SSM + 2 * LANES]
    b_sw = pltpu.roll(bmat, SSM_STATE, axis=1)
    c_sw = pltpu.roll(cmat, SSM_STATE, axis=1)
    bdup = [jnp.where(lo_half, bmat, b_sw), jnp.where(lo_half, b_sw, bmat)]
    cdup = [jnp.where(lo_half, cmat, c_sw), jnp.where(lo_half, c_sw, cmat)]
    gmask = [jnp.where(lo_half, 1.0, 0.0), jnp.where(lo_half, 0.0, 1.0)]
    bmat_b = bmat.astype(BF16)
    cb_g = [_dot_nt((cmat * gmask[g]).astype(BF16), bmat_b) for g in range(SSM_GROUPS)]

    dt = dt_ref[0]
    da = dt * (-jnp.exp(alog_ref[...]))
    tri = tri_ref[...]
    hi, mid, lo = _split3(da)
    acum = _dot(tri, hi) + _dot(tri, mid) + _dot(tri, lo)
    acum_t = acum.T
    dt_t = dt.T
    a_end = acum[q - 1:q, :]
    trow = lax.broadcasted_iota(jnp.int32, (q, q), 0)
    tcol = lax.broadcasted_iota(jnp.int32, (q, q), 1)
    causal = tcol <= trow
    bd_r = lax.broadcasted_iota(jnp.int32, (LANES, LANES), 0) < SSM_STATE
    bd_c = lax.broadcasted_iota(jnp.int32, (LANES, LANES), 1) < SSM_STATE
    block_diag = bd_r == bd_c
    hm_b = [jnp.where(lo_half, 1.0, 0.0).astype(BF16), jnp.where(lo_half, 0.0, 1.0).astype(BF16)]

    ys = []
    ssq = jnp.zeros((q, 1), F32)
    for pr in range(N_PAIRS):
        g = pr // (N_PAIRS // SSM_GROUPS)
        x_pair = xc[:, pr * LANES:(pr + 1) * LANES]
        x_b = x_pair.astype(BF16)
        ms = []
        for hh in range(HEADS_PER_LANE_TILE):
            h = pr * HEADS_PER_LANE_TILE + hh
            seg = acum[:, h:h + 1] - acum_t[h:h + 1, :]
            lmat = jnp.where(causal, jnp.exp(seg), 0.0)
            ms.append((cb_g[g] * lmat * dt_t[h:h + 1, :]).astype(BF16))
        h0 = pr * HEADS_PER_LANE_TILE
        acol = jnp.where(lo_half, acum[:, h0:h0 + 1], acum[:, h0 + 1:h0 + 2])
        dcol = jnp.where(lo_half, dt[:, h0:h0 + 1], dt[:, h0 + 1:h0 + 2])
        aend = jnp.where(lo_half, a_end[:, h0:h0 + 1], a_end[:, h0 + 1:h0 + 2])
        s_pair = state[pr]
        y_diag = _dot(jnp.concatenate(ms, axis=1), jnp.concatenate([x_b * hm_b[0], x_b * hm_b[1]], axis=0))
        y_off = _dot_nt((cdup[g] * jnp.exp(acol)).astype(BF16), s_pair.astype(BF16))
        new = _dot_tn(x_b, (bdup[g] * (jnp.exp(aend - acol) * dcol)).astype(BF16))
        state[pr] = s_pair * jnp.exp(aend) + jnp.where(block_diag, new, 0.0)
        y = y_diag + y_off + dskip_ref[:, pr * LANES:(pr + 1) * LANES] * x_pair
        yz = y * _silu(z_ref[0, :, pr * LANES:(pr + 1) * LANES])
        ssq = ssq + jnp.sum(yz * yz, axis=1, keepdims=True)
        ys.append(yz)

    inv = lax.rsqrt(ssq * (1.0 / D_SSM) + NORM_EPS)
    for pr in range(N_PAIRS):
        y_ref[0, :, pr * LANES:(pr + 1) * LANES] = (
            ys[pr] * inv * g_ref[:, pr * LANES:(pr + 1) * LANES]).astype(BF16)

    @pl.when(c == pl.num_programs(1) - 1)
    def _():
        for pr in range(N_PAIRS):
            sfin_ref[0, 2 * pr] = state[pr, 0:SSM_HEAD_DIM, 0:SSM_STATE]
            sfin_ref[0, 2 * pr + 1] = state[pr, SSM_HEAD_DIM:LANES, SSM_STATE:LANES]


def _ssd(xbc, z, dt, conv0, s0, conv_w, conv_b, a_log_pad, dskip_lanes, g_ssm):
    b, l, _ = xbc.shape
    q = min(SSD_Q, l)
    assert l % q == 0 and q % SUBLANES == 0
    tri = (jnp.arange(q)[None, :] <= jnp.arange(q)[:, None]).astype(BF16)
    seq = lambda n: pl.BlockSpec((1, q, n), lambda bi, ci: (bi, ci, 0))
    per_b = lambda a: pl.BlockSpec((1,) + a.shape[1:], lambda bi, ci: (bi,) + (0,) * (a.ndim - 1))
    full = lambda a: pl.BlockSpec(a.shape, lambda bi, ci: (0,) * a.ndim)
    kernel = functools.partial(_ssd_kernel, q=q)
    return pl.pallas_call(
        kernel,
        grid=(b, l // q),
        in_specs=[seq(CONV_DIM), seq(D_SSM), seq(LANES), per_b(conv0), per_b(s0),
                  full(conv_w), full(conv_b), full(a_log_pad), full(dskip_lanes), full(g_ssm), full(tri)],
        out_specs=(seq(D_SSM), per_b(s0)),
        out_shape=(jax.ShapeDtypeStruct((b, l, D_SSM), BF16), jax.ShapeDtypeStruct(s0.shape, F32)),
        scratch_shapes=[pltpu.VMEM((SUBLANES, CONV_DIM), F32),
                        pltpu.VMEM((N_PAIRS, LANES, LANES), F32)],
        compiler_params=pltpu.CompilerParams(
            dimension_semantics=("arbitrary", "arbitrary"), vmem_limit_bytes=VMEM_LIMIT),
        name="ssd",
    )(xbc, z, dt, conv0, s0, conv_w, conv_b, a_log_pad, dskip_lanes, g_ssm, tri)


def _gelu_tanh(x):
    return 0.5 * x * (1.0 + jnp.tanh(math.sqrt(2.0 / math.pi) * (x + 0.044715 * (x * x * x))))


def _out_ffn_kernel(x_ref, attn_ref, ys_ref, fc0_ref, ga_ref, gpost_ref, gpre_ref, gfpost_ref,
                    woa_ref, wos_ref, wgu_ref, wd_ref, cw_ref,
                    y_ref, fcn_ref, h2_buf, acc, carry, g_buf, u_buf, *, n_seq, lt):
    t = pl.program_id(1)

    @pl.when(t == 0)
    def _():
        carry[...] = fc0_ref[0]

    tm = x_ref.shape[0]
    for r0 in range(0, tm, FFN_ROWS):
        rows = slice(r0, r0 + FFN_ROWS)
        an = _rmsnorm(attn_ref[rows, :], ga_ref[...]).astype(BF16)
        m = _dot(an, woa_ref[...]) + _dot(ys_ref[rows, :], wos_ref[...])
        x1 = x_ref[rows, :] + _rmsnorm(m, gpost_ref[...])
        y_ref[rows, :] = x1
        h2_buf[rows, :] = _rmsnorm(x1, gpre_ref[...]).astype(BF16)
    row = lax.broadcasted_iota(jnp.int32, (lt, FFN_F), 0)

    def cols(ci, base=0):
        start = base + ci * FFN_F
        return pl.ds(start if isinstance(ci, int) else pl.multiple_of(start, FFN_F), FFN_F)

    def stage1(ci, slot):
        h2 = h2_buf[...]
        g_buf[slot] = _dot(h2, wgu_ref[:, cols(ci)])
        u_buf[slot] = _dot(h2, wgu_ref[:, cols(ci, D_FF)])

    def stage2(ci, slot):
        gate = g_buf[slot]
        cw = cw_ref[:, cols(ci)]
        acts = []
        for s in range(n_seq):
            gs = gate[s * lt:(s + 1) * lt]
            prev = carry[ci, s * SUBLANES:(s + 1) * SUBLANES, :]
            p1 = prev[SUBLANES - 1:SUBLANES, :]
            p2 = prev[SUBLANES - 2:SUBLANES - 1, :]
            g1 = jnp.where(row == 0, p1, pltpu.roll(gs, 1, axis=0))
            g2 = jnp.where(row == 0, p2, jnp.where(row == 1, p1, pltpu.roll(gs, 2, axis=0)))
            gc = cw[3:4, :] + g2 * cw[0:1, :] + g1 * cw[1:2, :] + gs * cw[2:3, :]
            carry[ci, s * SUBLANES:(s + 1) * SUBLANES, :] = gs[lt - SUBLANES:lt]
            acts.append(_gelu_tanh(gc))
        act = acts[0] if n_seq == 1 else jnp.concatenate(acts, axis=0)
        acc[...] += _dot((act * u_buf[slot]).astype(BF16), wd_ref[cols(ci), :])

    acc[...] = jnp.zeros_like(acc)
    stage1(0, 0)

    def body(p, _):
        c = 2 * p + 1
        stage1(c, 1)
        stage2(c - 1, 0)
        stage1(c + 1, 0)
        stage2(c, 1)
        return 0

    assert FFN_NC % 2 == 1
    lax.fori_loop(0, FFN_NC // 2, body, 0)
    stage2(FFN_NC - 1, 0)
    for r0 in range(0, tm, FFN_ROWS):
        rows = slice(r0, r0 + FFN_ROWS)
        y_ref[rows, :] = y_ref[rows, :] + _rmsnorm(acc[rows, :], gfpost_ref[...])
    fcn_ref[0] = carry[...]


def _out_ffn(x2d, attn2d, ys2d, fc0, n_seq, lt, gains, weights):
    t = x2d.shape[0]
    tm = n_seq * lt
    n_groups = fc0.shape[0]
    tiles = t // (tm * n_groups)
    assert tiles * tm * n_groups == t
    row = lambda n: pl.BlockSpec((tm, n), lambda gi, ti: (gi * tiles + ti, 0))
    full = lambda a: pl.BlockSpec(a.shape, lambda gi, ti: (0,) * a.ndim, pipeline_mode=pl.Buffered(1))
    fc_spec = pl.BlockSpec((1,) + fc0.shape[1:], lambda gi, ti: (gi, 0, 0, 0))
    kernel = functools.partial(_out_ffn_kernel, n_seq=n_seq, lt=lt)
    return pl.pallas_call(
        kernel,
        grid=(n_groups, tiles),
        in_specs=[row(D_MODEL), row(D_ATT), row(D_SSM), fc_spec] + [full(a) for a in gains]
                 + [full(a) for a in weights],
        out_specs=(row(D_MODEL), fc_spec),
        out_shape=(jax.ShapeDtypeStruct((t, D_MODEL), F32), jax.ShapeDtypeStruct(fc0.shape, F32)),
        scratch_shapes=[pltpu.VMEM((tm, D_MODEL), BF16), pltpu.VMEM((tm, D_MODEL), F32),
                        pltpu.VMEM(fc0.shape[1:], F32), pltpu.VMEM((2, tm, FFN_F), F32),
                        pltpu.VMEM((2, tm, FFN_F), F32)],
        compiler_params=pltpu.CompilerParams(
            dimension_semantics=("arbitrary", "arbitrary"), vmem_limit_bytes=VMEM_LIMIT),
        name="out_ffn",
    )(x2d, attn2d, ys2d, fc0, *gains, *weights)


def _ffn_state_to_chunks(st, n_seq):
    b = st.shape[0]
    s = st.reshape(b // n_seq, n_seq, FFN_CONV - 1, FFN_NC, FFN_F)
    s = jnp.pad(s, ((0, 0), (0, 0), (SUBLANES - (FFN_CONV - 1), 0), (0, 0), (0, 0)))
    return jnp.transpose(s, (0, 3, 1, 2, 4)).reshape(b // n_seq, FFN_NC, n_seq * SUBLANES, FFN_F)


def _chunks_to_ffn_state(ch, n_seq):
    g = ch.shape[0]
    s = ch.reshape(g, FFN_NC, n_seq, SUBLANES, FFN_F)[:, :, :, SUBLANES - (FFN_CONV - 1):, :]
    return jnp.transpose(s, (0, 2, 3, 1, 4)).reshape(g * n_seq, FFN_CONV - 1, D_FF)


def _layer(x, caches, ssm_h0, ssm_conv_prev, ffn_conv_prev, p, is_prompt):
    b, l, _ = x.shape
    t = b * l
    x2d = x.reshape(t, D_MODEL)
    r3 = lambda a: a.reshape(b, l, a.shape[-1])
    proj_w = (p['g_mix_pre'], p['w_in_t'], p['w_dt_t'], p['dt_bias'])
    if is_prompt:
        qb, kb, vb, kt, vt, z, xbc, dt = _in_proj(x, *proj_w, kv_transposed=True)
        attn = _attn_prompt(qb, kb, vb)
        k, v = jnp.transpose(kt, (0, 3, 1, 2)), jnp.transpose(vt, (0, 3, 1, 2))
    else:
        qb, kb, vb, k, v, z, xbc, dt = _in_proj(x2d[None], *proj_w, kv_transposed=False)
        attn = _attn_sample(r3(qb), r3(kb), r3(vb), *caches)

    conv0 = jnp.pad(ssm_conv_prev, ((0, 0), (SUBLANES - (SSM_CONV - 1), 0), (0, 0)))
    ys, s_fin = _ssd(r3(xbc), r3(z), r3(dt), conv0, ssm_h0,
                     p['ssm_conv_w'], p['ssm_conv_b'], p['a_log'], p['d_skip'], p['g_ssm_out'])
    ssm_conv_new = r3(xbc)[:, l - (SSM_CONV - 1):, :]

    if is_prompt:
        n_seq, lt = 1, min(FFN_TM, l)
    else:
        n_seq, lt = b, l
    fc0 = _ffn_state_to_chunks(ffn_conv_prev, n_seq)
    gains = (p['g_attn_out'], p['g_mix_post'], p['g_ffn_pre'], p['g_ffn_post'])
    weights = (p['w_out_a'], p['w_out_s'], p['w_gu'], p['w_down'], p['ffn_cw'])
    y2d, fcn = _out_ffn(x2d, attn.reshape(t, D_ATT), ys.reshape(t, D_SSM), fc0, n_seq, lt, gains, weights)
    return (y2d.reshape(b, l, D_MODEL), k.reshape(b, l, N_ATT_HEADS, ATT_HEAD_DIM),
            v.reshape(b, l, N_ATT_HEADS, ATT_HEAD_DIM), s_fin, ssm_conv_new,
            _chunks_to_ffn_state(fcn, n_seq))


def _prep_params(i, g_mix_pre, g_mix_post, w_in, ssm_conv_w, ssm_conv_b, dt_bias, a_log, d_skip,
                 g_ssm_out, g_attn_out, w_out, g_ffn_pre, g_ffn_post, w_up, ffn_conv_w, ffn_conv_b, w_down):
    row = lambda a: a[i].reshape(1, -1).astype(F32)
    pad_lanes = lambda a: jnp.pad(a, ((0, 0), (0, LANES - a.shape[1])))
    wi_t = w_in[i].T.astype(BF16)
    ffn_cw = jnp.concatenate([ffn_conv_w[i], ffn_conv_b[i][None, :],
                              jnp.zeros((SUBLANES - FFN_CONV - 1, D_FF), F32)], axis=0)
    return {
        'g_mix_pre': row(g_mix_pre), 'g_mix_post': row(g_mix_post),
        'g_ffn_pre': row(g_ffn_pre), 'g_ffn_post': row(g_ffn_post),
        'g_attn_out': row(g_attn_out), 'g_ssm_out': row(g_ssm_out),
        'w_in_t': wi_t,
        'w_dt_t': jnp.pad(wi_t[D_MAIN_PROJ:], ((0, LANES - (wi_t.shape[0] - D_MAIN_PROJ)), (0, 0))),
        'dt_bias': pad_lanes(row(dt_bias)),
        'ssm_conv_w': ssm_conv_w[i].astype(F32), 'ssm_conv_b': row(ssm_conv_b),
        'a_log': pad_lanes(row(a_log)),
        'd_skip': jnp.repeat(d_skip[i].astype(F32), SSM_HEAD_DIM).reshape(1, D_SSM),
        'w_out_a': w_out[i][:D_ATT].astype(BF16), 'w_out_s': w_out[i][D_ATT:].astype(BF16),
        'w_gu': w_up[i].astype(BF16),
        'w_down': w_down[i].astype(BF16),
        'ffn_cw': ffn_cw,
    }


def kernel(x_prompt, x_sample, cache_k, cache_v, state_ssm, state_ssm_conv, state_ffn_conv, g_mix_pre, g_mix_post, w_in, ssm_conv_w, ssm_conv_b, dt_bias, a_log, d_skip, g_ssm_out, g_attn_out, w_out, g_ffn_pre, g_ffn_post, w_up, ffn_conv_w, ffn_conv_b, w_down):
    depth = w_in.shape[0]
    bp = x_prompt.shape[0]
    dtp = x_prompt.dtype
    zh = jnp.zeros((bp, N_SSM_HEADS, SSM_HEAD_DIM, SSM_STATE), dtp)
    zcs = jnp.zeros((bp, SSM_CONV - 1, CONV_DIM), dtp)
    zcf = jnp.zeros((bp, FFN_CONV - 1, D_FF), dtp)
    y_p, y_s = x_prompt, x_sample
    outs_p, outs_s = [], []
    for i in range(depth):
        p = _prep_params(i, g_mix_pre, g_mix_post, w_in, ssm_conv_w, ssm_conv_b, dt_bias, a_log, d_skip,
                         g_ssm_out, g_attn_out, w_out, g_ffn_pre, g_ffn_post, w_up, ffn_conv_w, ffn_conv_b,
                         w_down)
        rp = _layer(y_p, None, zh, zcs, zcf, p, True)
        rs = _layer(y_s, (cache_k, cache_v, i), state_ssm[i], state_ssm_conv[i], state_ffn_conv[i], p, False)
        y_p, y_s = rp[0], rs[0]
        outs_p.append(rp[1:])
        outs_s.append(rs[1:])
    stack = lambda outs, j: jnp.stack([o[j] for o in outs])
    return (y_p, y_s) + tuple(stack(outs_p, j) for j in range(5)) + tuple(stack(outs_s, j) for j in range(5))
```

```python
import functools
import math

import jax
import jax.numpy as jnp
from jax import lax
from jax.experimental import pallas as pl
from jax.experimental.pallas import tpu as pltpu

F32 = jnp.float32
BF16 = jnp.bfloat16

D_MODEL = 1024
D_ATT = 512
N_ATT_HEADS = 8
ATT_HEAD_DIM = 64
D_SSM = 512
N_SSM_HEADS = 8
SSM_HEAD_DIM = 64
SSM_STATE = 64
SSM_GROUPS = 2
SSM_CONV = 4
CONV_DIM = D_SSM + 2 * SSM_GROUPS * SSM_STATE
D_FF = 2816
FFN_CONV = 3
NORM_EPS = 1e-6
D_MAIN_PROJ = 3 * D_ATT + D_SSM + CONV_DIM
ATT_SCALE = ATT_HEAD_DIM ** -0.5
LOG2E = math.log2(math.e)
SB_EXHAUSTED = -120.0

LANES = 128
SUBLANES = 8
HEADS_PER_LANE_TILE = LANES // ATT_HEAD_DIM
N_PAIRS = N_SSM_HEADS // HEADS_PER_LANE_TILE

PROJ_TM = 512
ATT_TQ = 256
ATT_TK = 128
ATT_CACHE_CHUNK = 512
SSD_Q = 128
FFN_TM = 512
FFN_F = 256
FFN_ROWS = 128
FFN_NC = D_FF // FFN_F
VMEM_LIMIT = 56 * 1024 * 1024


def _rmsnorm(x, g):
    y = x * lax.rsqrt(jnp.mean(x * x, axis=-1, keepdims=True) + NORM_EPS)
    return y * g


def _softplus(x):
    return jnp.maximum(x, 0.0) + jnp.log1p(jnp.exp(-jnp.abs(x)))


def _silu(x):
    return x * (1.0 / (1.0 + jnp.exp(-x)))


def _dot(a, b):
    return jnp.dot(a, b, preferred_element_type=F32)


def _dot_nt(a, b):
    return lax.dot_general(a, b, (((1,), (1,)), ((), ())), preferred_element_type=F32)


def _dot_tn(a, b):
    return lax.dot_general(a, b, (((0,), (0,)), ((), ())), preferred_element_type=F32)


def _split2(x):
    hi = x.astype(BF16)
    lo = (x - hi.astype(F32)).astype(BF16)
    return hi, lo


def _split3(x):
    hi = x.astype(BF16)
    r1 = x - hi.astype(F32)
    mid = r1.astype(BF16)
    lo = (r1 - mid.astype(F32)).astype(BF16)
    return hi, mid, lo


def _in_proj_kernel(x_ref, g_ref, wt_ref, wdt_ref, dtb_ref,
                    qb_ref, kb_ref, vb_ref, k_ref, v_ref, z_ref, xbc_ref, dt_ref, *, kv_transposed):
    h = _rmsnorm(x_ref[0], g_ref[...]).astype(BF16)

    def proj(lo, hi):
        return _dot_nt(h, wt_ref[lo:hi, :])

    qb_ref[0] = (proj(0, D_ATT) * ATT_SCALE).astype(BF16)
    v = proj(2 * D_ATT, 3 * D_ATT)
    vb_ref[0] = v.astype(BF16)
    if kv_transposed:
        tm = v.shape[0]
        kt = _dot_nt(wt_ref[D_ATT:2 * D_ATT, :], h)
        kb_ref[0] = kt.astype(BF16)
        k_ref[0] = kt.reshape(N_ATT_HEADS, ATT_HEAD_DIM, tm)
        v_ref[0] = v.T.reshape(N_ATT_HEADS, ATT_HEAD_DIM, tm)
    else:
        k = proj(D_ATT, 2 * D_ATT)
        kb_ref[0] = k.astype(BF16)
        k_ref[0] = _lanes_to_heads(k)
        v_ref[0] = _lanes_to_heads(v)
    z_ref[0] = proj(3 * D_ATT, 3 * D_ATT + D_SSM)
    xbc_ref[0] = proj(3 * D_ATT + D_SSM, D_MAIN_PROJ)
    dt_ref[0] = _softplus(_dot_nt(h, wdt_ref[...]) + dtb_ref[...])


def _in_proj(x, g, w_t, w_dt, dt_bias, kv_transposed):
    b, l, _ = x.shape
    tm = min(PROJ_TM, l)
    assert l % tm == 0
    row = lambda n: pl.BlockSpec((1, tm, n), lambda bi, i: (bi, i, 0))
    full = lambda a: pl.BlockSpec(a.shape, lambda bi, i: (0,) * a.ndim)
    sds = jax.ShapeDtypeStruct
    if kv_transposed:
        kb_shape, kb_spec = sds((b, D_ATT, l), BF16), pl.BlockSpec((1, D_ATT, tm), lambda bi, i: (bi, 0, i))
        kv_shape = sds((b, N_ATT_HEADS, ATT_HEAD_DIM, l), F32)
        kv_spec = pl.BlockSpec((1, N_ATT_HEADS, ATT_HEAD_DIM, tm), lambda bi, i: (bi, 0, 0, i))
    else:
        kb_shape, kb_spec = sds((b, l, D_ATT), BF16), row(D_ATT)
        kv_shape = sds((b, l, N_ATT_HEADS, ATT_HEAD_DIM), F32)
        kv_spec = pl.BlockSpec((1, tm, N_ATT_HEADS, ATT_HEAD_DIM), lambda bi, i: (bi, i, 0, 0))
    out_shape = (sds((b, l, D_ATT), BF16), kb_shape, sds((b, l, D_ATT), BF16), kv_shape, kv_shape,
                 sds((b, l, D_SSM), F32), sds((b, l, CONV_DIM), F32), sds((b, l, LANES), F32))
    return pl.pallas_call(
        functools.partial(_in_proj_kernel, kv_transposed=kv_transposed),
        grid=(b, l // tm),
        in_specs=[row(D_MODEL), full(g), full(w_t), full(w_dt), full(dt_bias)],
        out_specs=(row(D_ATT), kb_spec, row(D_ATT), kv_spec, kv_spec,
                   row(D_SSM), row(CONV_DIM), row(LANES)),
        out_shape=out_shape,
        compiler_params=pltpu.CompilerParams(
            dimension_semantics=("arbitrary", "arbitrary"), vmem_limit_bytes=VMEM_LIMIT),
        name="in_proj",
    )(x, g, w_t, w_dt, dt_bias)


def _sb_weights(s, c, u2, mask):
    lk = -(jnp.maximum(s, 0.0) + jnp.log(1.0 + jnp.exp(-jnp.abs(s))))
    if mask is not None:
        lk = jnp.where(mask, lk, 0.0)
    hi, lo = _split2(lk)
    r = _dot(jnp.concatenate([hi, lo], axis=1), u2) + c
    w = jnp.exp(s + r)
    if mask is not None:
        w = jnp.where(mask, w, 0.0)
    return w, c + jnp.sum(lk, axis=1, keepdims=True)


def _head_lane_masks(n_lanes, dtype):
    lane = lax.broadcasted_iota(jnp.int32, (1, n_lanes), 1)
    return [jnp.where((lane >= h * ATT_HEAD_DIM) & (lane < (h + 1) * ATT_HEAD_DIM), 1.0, 0.0).astype(dtype)
            for h in range(n_lanes // ATT_HEAD_DIM)]


def _attn_prompt_kernel(q_ref, kt_ref, v_ref, u2x_ref, o_ref, s_buf, hl_buf, mask_buf, qs_buf, vcat_buf,
                        out_buf, *, tq, nq):
    tk = tq // 2
    hm = _head_lane_masks(LANES, BF16)
    big = 1e30
    causal = lax.broadcasted_iota(jnp.int32, (tq, tq), 1) < lax.broadcasted_iota(jnp.int32, (tq, tq), 0)
    mask_buf[0] = jnp.full((tq, tq), -jnp.inf, F32)
    mask_buf[1] = jnp.where(causal, -jnp.inf, big)
    for i in range(nq):
        qn = -q_ref[0, i * tq:(i + 1) * tq, :]
        vsb = v_ref[0, i * tq:(i + 1) * tq, :]
        for h in range(HEADS_PER_LANE_TILE):
            qs_buf[i, h * tq:(h + 1) * tq, :] = qn * hm[h]
            for n, half in enumerate((1, 0)):
                r0 = (n * HEADS_PER_LANE_TILE + h) * tk
                vcat_buf[i, r0:r0 + tk, :] = vsb[half * tk:(half + 1) * tk] * hm[h]

    def advance(i, j, skip_rest):
        last = (j == 0) | skip_rest
        return jnp.where(last, i + 1, i), jnp.where(last, i + 1, j - 1)

    def exhausted_after(c_prev, row_sums, first):
        carry = jnp.where(first, 0.0, c_prev[:, 0:1]) + row_sums
        return jnp.max(carry) < SB_EXHAUSTED

    def row_off(idx):
        return pl.multiple_of(jnp.minimum(idx, nq - 1) * tq, tq)

    def stage_a(i, j, slot):
        kb = kt_ref[0, :, pl.ds(row_off(j), tq)]
        t = _dot(qs_buf[jnp.minimum(i, nq - 1)], kb)
        floor = mask_buf[(i == j).astype(jnp.int32)]
        row_sums = []
        for h in range(HEADS_PER_LANE_TILE):
            th = jnp.maximum(t[h * tq:(h + 1) * tq], floor)
            s_buf[slot, h * tq:(h + 1) * tq, :] = th
            e = jnp.exp2(jnp.abs(th) * (-LOG2E))
            lk = jnp.minimum(th, 0.0) - jnp.log(1.0 + e)
            row_sums.append(jnp.sum(lk, axis=1, keepdims=True))
            hi, lo = _split2(lk)
            for half in range(2):
                r0 = half * 2 * tq + h * tq
                hl_buf[slot, r0:r0 + tq, 0:tk] = hi[:, half * tk:(half + 1) * tk]
                hl_buf[slot, r0:r0 + tq, tk:tq] = lo[:, half * tk:(half + 1) * tk]
        return jnp.concatenate(row_sums, axis=0)

    def stage_b(i, j, slot, c, acc):
        rr = _dot(hl_buf[slot], u2x_ref[...])
        first = i == j
        c = jnp.where(first, 0.0, c)
        acc = jnp.where(first, 0.0, acc)
        ws = []
        for half in (1, 0):
            r0 = half * 2 * tq
            r = rr[r0:r0 + 2 * tq, 0:tk] + c
            th = s_buf[slot, :, half * tk:(half + 1) * tk]
            w = jnp.exp(r - th).astype(BF16)
            c = c + rr[r0:r0 + 2 * tq, tk:tq]
            ws += [w[0:tq], w[tq:2 * tq]]
        acc = acc + _dot(jnp.concatenate(ws, axis=1), vcat_buf[jnp.minimum(j, nq - 1)])
        out_buf[jnp.minimum(i, nq)] = acc
        return c, acc

    def two_items(carry):
        ib, jb, ia, ja, c, acc = carry
        rs = stage_a(ia, ja, 1)
        c, acc = stage_b(ib, jb, 0, c, acc)
        i2, j2 = advance(ia, ja, exhausted_after(c, rs, ia == ja))
        rs = stage_a(i2, j2, 0)
        c, acc = stage_b(ia, ja, 1, c, acc)
        i3, j3 = advance(i2, j2, exhausted_after(c, rs, i2 == j2))
        return i2, j2, i3, j3, c, acc

    def body(carry):
        return two_items(two_items(carry))

    z = jnp.int32(0)
    stage_a(z, z, 0)
    init = (z, z, z + 1, z + 1, jnp.zeros((2 * tq, tk), F32), jnp.zeros((tq, LANES), F32))
    lax.while_loop(lambda carry: carry[0] < nq, body, init)
    for i in range(nq):
        o_ref[0, i * tq:(i + 1) * tq, :] = out_buf[i]


def _u2_matrix(tk):
    j = jnp.arange(2 * tk)[:, None] % tk
    s = jnp.arange(tk)[None, :]
    return (j >= s).astype(BF16)


def _attn_prompt(qb, kb, vb):
    b, l, _ = qb.shape
    tq = min(ATT_TQ, l)
    assert l % tq == 0 and tq == 2 * ATT_TK
    u2x = jnp.concatenate([_u2_matrix(ATT_TK), jnp.ones((tq, ATT_TK), BF16)], axis=1)
    kernel = functools.partial(_attn_prompt_kernel, tq=tq, nq=l // tq)
    seq = pl.BlockSpec((1, l, LANES), lambda bi, hp: (bi, 0, hp))
    return pl.pallas_call(
        kernel,
        grid=(b, D_ATT // LANES),
        in_specs=[seq, pl.BlockSpec((1, LANES, l), lambda bi, hp: (bi, hp, 0)), seq,
                  pl.BlockSpec(u2x.shape, lambda bi, hp: (0, 0))],
        out_specs=seq,
        out_shape=jax.ShapeDtypeStruct((b, l, D_ATT), F32),
        scratch_shapes=[pltpu.VMEM((2, 2 * tq, tq), F32), pltpu.VMEM((2, 4 * tq, tq), BF16),
                        pltpu.VMEM((2, tq, tq), F32), pltpu.VMEM((l // tq, 2 * tq, LANES), BF16),
                        pltpu.VMEM((l // tq, 2 * tq, LANES), BF16), pltpu.VMEM((l // tq + 1, tq, LANES), F32)],
        compiler_params=pltpu.CompilerParams(
            dimension_semantics=("arbitrary", "arbitrary"), vmem_limit_bytes=VMEM_LIMIT),
        name="attn_prompt",
    )(qb, kb, vb, u2x)


def _heads_to_lanes(x):
    t = x.shape[0]
    y = jnp.swapaxes(x.reshape(t // SUBLANES, SUBLANES, N_ATT_HEADS, ATT_HEAD_DIM), 1, 2)
    return jnp.concatenate([y[:, v].reshape(t, ATT_HEAD_DIM) for v in range(N_ATT_HEADS)], axis=1)


def _lanes_to_heads(x):
    t = x.shape[0]
    parts = []
    for g in range(D_ATT // LANES):
        a = x[:, g * LANES:(g + 1) * LANES]
        b = pltpu.roll(a, ATT_HEAD_DIM, axis=1)
        parts += [a.reshape(t // SUBLANES, SUBLANES, LANES), b.reshape(t // SUBLANES, SUBLANES, LANES)]
    y = jnp.swapaxes(jnp.stack(parts, axis=1), 1, 2).reshape(t, N_ATT_HEADS, LANES)
    return y[:, :, 0:ATT_HEAD_DIM]


def _attn_sample_kernel(q_ref, kn_ref, vn_ref, ck_hbm, cv_hbm, u2_ref, o_ref, kbuf, vbuf, sem,
                        *, l, tk, chunk, nch, layer):
    bi = pl.program_id(0)
    m = N_ATT_HEADS * l
    hm = _head_lane_masks(D_ATT, BF16)
    q = q_ref[0]
    qs = jnp.concatenate([q * mk for mk in hm], axis=0)
    u2 = u2_ref[...]

    def copies(ch, slot):
        keys = pl.ds(pl.multiple_of((nch - 1 - ch) * chunk, chunk), chunk)
        return (pltpu.make_async_copy(ck_hbm.at[layer, bi, :, :, keys], kbuf.at[slot], sem.at[0, slot]),
                pltpu.make_async_copy(cv_hbm.at[layer, bi, :, :, keys], vbuf.at[slot], sem.at[1, slot]))

    def start(ch, slot):
        for cp in copies(ch, slot):
            cp.start()

    def wait(ch, slot):
        for cp in copies(ch, slot):
            cp.wait()

    start(0, 0)
    pad = jnp.zeros((tk - l, D_ATT), BF16)
    kn = jnp.concatenate([kn_ref[0], pad], axis=0)
    vn = jnp.concatenate([vn_ref[0], pad], axis=0)
    row = jnp.concatenate([lax.broadcasted_iota(jnp.int32, (l, tk), 0)] * N_ATT_HEADS, axis=0)
    col = lax.broadcasted_iota(jnp.int32, (m, tk), 1)
    w, c = _sb_weights(_dot_nt(qs, kn), jnp.zeros((m, 1), F32), u2, col < row)
    acc = _dot(w.astype(BF16), vn)

    def live(c):
        return jnp.max(c) >= SB_EXHAUSTED

    def body(carry):
        ch, c, acc = carry
        slot = ch % 2
        wait(ch, slot)

        @pl.when(ch + 1 < nch)
        def _():
            start(ch + 1, 1 - slot)

        kt = kbuf[slot].reshape(D_ATT, chunk).astype(BF16)
        vt = vbuf[slot].reshape(D_ATT, chunk).astype(BF16)
        for t in reversed(range(chunk // tk)):
            w, c = _sb_weights(_dot(qs, kt[:, t * tk:(t + 1) * tk]), c, u2, None)
            acc = acc + _dot_nt(w.astype(BF16), vt[:, t * tk:(t + 1) * tk])
        return ch + 1, c, acc

    ch, c, acc = lax.while_loop(lambda carry: (carry[0] < nch) & live(carry[1]), body, (jnp.int32(0), c, acc))

    @pl.when(ch < nch)
    def _():
        wait(ch, ch % 2)

    hmf = _head_lane_masks(D_ATT, F32)
    out = acc[0:l] * hmf[0]
    for h in range(1, N_ATT_HEADS):
        out = out + acc[h * l:(h + 1) * l] * hmf[h]
    o_ref[0] = out


def _attn_sample(qb, kb_new, vb_new, cache_k, cache_v, layer):
    b, l, _ = qb.shape
    past = cache_k.shape[2]
    cache_k = jnp.transpose(cache_k, (0, 1, 3, 4, 2))
    cache_v = jnp.transpose(cache_v, (0, 1, 3, 4, 2))
    tk = ATT_TK
    chunk = min(ATT_CACHE_CHUNK, past)
    assert past % chunk == 0 and chunk % tk == 0 and l <= tk and l % 16 == 0
    u2 = _u2_matrix(tk)
    kernel = functools.partial(_attn_sample_kernel, l=l, tk=tk, chunk=chunk, nch=past // chunk, layer=layer)
    new = pl.BlockSpec((1, l, D_ATT), lambda bi: (bi, 0, 0))
    hbm = pl.BlockSpec(memory_space=pl.ANY)
    return pl.pallas_call(
        kernel,
        grid=(b,),
        in_specs=[new, new, new, hbm, hbm, pl.BlockSpec(u2.shape, lambda bi: (0, 0))],
        out_specs=new,
        out_shape=jax.ShapeDtypeStruct((b, l, D_ATT), F32),
        scratch_shapes=[pltpu.VMEM((2, N_ATT_HEADS, ATT_HEAD_DIM, chunk), F32),
                        pltpu.VMEM((2, N_ATT_HEADS, ATT_HEAD_DIM, chunk), F32),
                        pltpu.SemaphoreType.DMA((2, 2))],
        compiler_params=pltpu.CompilerParams(
            dimension_semantics=("arbitrary",), vmem_limit_bytes=VMEM_LIMIT),
        name="attn_sample",
    )(qb, kb_new, vb_new, cache_k, cache_v, u2)


def _ssd_kernel(xbc_ref, z_ref, dt_ref, conv0_ref, s0_ref, cw_ref, cb_ref, alog_ref, dskip_ref, g_ref,
                tri_ref, y_ref, sfin_ref, cbuf, state, *, q):
    c = pl.program_id(1)

    @pl.when(c == 0)
    def _():
        cbuf[...] = conv0_ref[0]
        zero = jnp.zeros((SSM_HEAD_DIM, SSM_STATE), F32)
        for pr in range(N_PAIRS):
            top = jnp.concatenate([s0_ref[0, 2 * pr], zero], axis=1)
            bot = jnp.concatenate([zero, s0_ref[0, 2 * pr + 1]], axis=1)
            state[pr] = jnp.concatenate([top, bot], axis=0)

    x = xbc_ref[0]
    prev = cbuf[...]
    row8 = lax.broadcasted_iota(jnp.int32, (SUBLANES, CONV_DIM), 0)
    xc = cb_ref[...] + x * cw_ref[SSM_CONV - 1:SSM_CONV, :]
    for k in range(1, SSM_CONV):
        r = pltpu.roll(x, k, axis=0)
        head = r[0:SUBLANES]
        for j in range(k):
            head = jnp.where(row8 == j, prev[SUBLANES - k + j:SUBLANES - k + j + 1, :], head)
        r = jnp.concatenate([head, r[SUBLANES:]], axis=0)
        xc = xc + r * cw_ref[SSM_CONV - 1 - k:SSM_CONV - k, :]
    xc = _silu(xc)
    cbuf[...] = x[q - SUBLANES:q]

    lane = lax.broadcasted_iota(jnp.int32, (1, LANES), 1)
    lo_half = lane < SSM_STATE
    bmat = xc[:, D_SSM:D_SSM + LANES]
    cmat = xc[:, D_SSM + LANES:D_SSM + 2 * LANES]
    b_sw = pltpu.roll(bmat, SSM_STATE, axis=1)
    c_sw = pltpu.roll(cmat, SSM_STATE, axis=1)
    bdup = [jnp.where(lo_half, bmat, b_sw), jnp.where(lo_half, b_sw, bmat)]
    cdup = [jnp.where(lo_half, cmat, c_sw), jnp.where(lo_half, c_sw, cmat)]
    gmask = [jnp.where(lo_half, 1.0, 0.0), jnp.where(lo_half, 0.0, 1.0)]
    bmat_b = bmat.astype(BF16)
    cb_g = [_dot_nt((cmat * gmask[g]).astype(BF16), bmat_b) for g in range(SSM_GROUPS)]

    dt = dt_ref[0]
    da = dt * (-jnp.exp(alog_ref[...]))
    tri = tri_ref[...]
    hi, mid, lo = _split3(da)
    acum = _dot(tri, hi) + _dot(tri, mid) + _dot(tri, lo)
    acum_t = acum.T
    dt_t = dt.T
    a_end = acum[q - 1:q, :]
    trow = lax.broadcasted_iota(jnp.int32, (q, q), 0)
    tcol = lax.broadcasted_iota(jnp.int32, (q, q), 1)
    causal = tcol <= trow
    bd_r = lax.broadcasted_iota(jnp.int32, (LANES, LANES), 0) < SSM_STATE
    bd_c = lax.broadcasted_iota(jnp.int32, (LANES, LANES), 1) < SSM_STATE
    block_diag = bd_r == bd_c
    hm_b = [jnp.where(lo_half, 1.0, 0.0).astype(BF16), jnp.where(lo_half, 0.0, 1.0).astype(BF16)]

    pairs = range(N_PAIRS)
    group = [pr // (N_PAIRS // SSM_GROUPS) for pr in pairs]
    x_pair = [xc[:, pr * LANES:(pr + 1) * LANES] for pr in pairs]
    x_b = [v.astype(BF16) for v in x_pair]
    s_pair = [state[pr] for pr in pairs]
    lhs_diag, lhs_off, upd = [], [], []
    for pr in pairs:
        h0 = pr * HEADS_PER_LANE_TILE
        ms = []
        for h in (h0, h0 + 1):
            seg = acum[:, h:h + 1] - acum_t[h:h + 1, :]
            lmat = jnp.where(causal, jnp.exp(seg), 0.0)
            ms.append((cb_g[group[pr]] * lmat * dt_t[h:h + 1, :]).astype(BF16))
        acol = jnp.where(lo_half, acum[:, h0:h0 + 1], acum[:, h0 + 1:h0 + 2])
        dcol = jnp.where(lo_half, dt[:, h0:h0 + 1], dt[:, h0 + 1:h0 + 2])
        aend = jnp.where(lo_half, a_end[:, h0:h0 + 1], a_end[:, h0 + 1:h0 + 2])
        lhs_diag.append(jnp.concatenate(ms, axis=1))
        lhs_off.append((cdup[group[pr]] * jnp.exp(acol)).astype(BF16))
        upd.append((aend, (bdup[group[pr]] * (jnp.exp(aend - acol) * dcol)).astype(BF16)))
    y_diag = [_dot(lhs_diag[pr], jnp.concatenate([x_b[pr] * hm_b[0], x_b[pr] * hm_b[1]], axis=0)) for pr in pairs]
    y_off = [_dot_nt(lhs_off[pr], s_pair[pr].astype(BF16)) for pr in pairs]
    ys = []
    ssq = jnp.zeros((q, 1), F32)
    for pr in pairs:
        y = y_diag[pr] + y_off[pr] + dskip_ref[:, pr * LANES:(pr + 1) * LANES] * x_pair[pr]
        yz = y * _silu(z_ref[0, :, pr * LANES:(pr + 1) * LANES])
        ssq = ssq + jnp.sum(yz * yz, axis=1, keepdims=True)
        ys.append(yz)

    inv = lax.rsqrt(ssq * (1.0 / D_SSM) + NORM_EPS)
    for pr in pairs:
        aend, bw = upd[pr]
        state[pr] = s_pair[pr] * jnp.exp(aend) + jnp.where(block_diag, _dot_tn(x_b[pr], bw), 0.0)
    for pr in range(N_PAIRS):
        y_ref[0, :, pr * LANES:(pr + 1) * LANES] = (
            ys[pr] * inv * g_ref[:, pr * LANES:(pr + 1) * LANES]).astype(BF16)

    @pl.when(c == pl.num_programs(1) - 1)
    def _():
        for pr in range(N_PAIRS):
            sfin_ref[0, 2 * pr] = state[pr, 0:SSM_HEAD_DIM, 0:SSM_STATE]
            sfin_ref[0, 2 * pr + 1] = state[pr, SSM_HEAD_DIM:LANES, SSM_STATE:LANES]


def _ssd(xbc, z, dt, conv0, s0, conv_w, conv_b, a_log_pad, dskip_lanes, g_ssm):
    b, l, _ = xbc.shape
    q = min(SSD_Q, l)
    assert l % q == 0 and q % SUBLANES == 0
    tri = (jnp.arange(q)[None, :] <= jnp.arange(q)[:, None]).astype(BF16)
    seq = lambda n: pl.BlockSpec((1, q, n), lambda bi, ci: (bi, ci, 0))
    per_b = lambda a: pl.BlockSpec((1,) + a.shape[1:], lambda bi, ci: (bi,) + (0,) * (a.ndim - 1))
    full = lambda a: pl.BlockSpec(a.shape, lambda bi, ci: (0,) * a.ndim)
    kernel = functools.partial(_ssd_kernel, q=q)
    return pl.pallas_call(
        kernel,
        grid=(b, l // q),
        in_specs=[seq(CONV_DIM), seq(D_SSM), seq(LANES), per_b(conv0), per_b(s0),
                  full(conv_w), full(conv_b), full(a_log_pad), full(dskip_lanes), full(g_ssm), full(tri)],
        out_specs=(seq(D_SSM), per_b(s0)),
        out_shape=(jax.ShapeDtypeStruct((b, l, D_SSM), BF16), jax.ShapeDtypeStruct(s0.shape, F32)),
        scratch_shapes=[pltpu.VMEM((SUBLANES, CONV_DIM), F32),
                        pltpu.VMEM((N_PAIRS, LANES, LANES), F32)],
        compiler_params=pltpu.CompilerParams(
            dimension_semantics=("arbitrary", "arbitrary"), vmem_limit_bytes=VMEM_LIMIT),
        name="ssd",
    )(xbc, z, dt, conv0, s0, conv_w, conv_b, a_log_pad, dskip_lanes, g_ssm, tri)


def _gelu_tanh(x):
    return 0.5 * x * (1.0 + jnp.tanh(math.sqrt(2.0 / math.pi) * (x + 0.044715 * (x * x * x))))


def _out_ffn_kernel(x_ref, attn_ref, ys_ref, fc0_ref, ga_ref, gpost_ref, gpre_ref, gfpost_ref,
                    woa_ref, wos_ref, wgu_ref, wd_ref, cw_ref,
                    y_ref, fcn_ref, h2_buf, acc, carry, g_buf, u_buf, *, n_seq, lt):
    t = pl.program_id(1)

    @pl.when(t == 0)
    def _():
        carry[...] = fc0_ref[0]

    tm = x_ref.shape[0]
    for r0 in range(0, tm, FFN_ROWS):
        rows = slice(r0, r0 + FFN_ROWS)
        an = _rmsnorm(attn_ref[rows, :], ga_ref[...]).astype(BF16)
        m = _dot(an, woa_ref[...]) + _dot(ys_ref[rows, :], wos_ref[...])
        x1 = x_ref[rows, :] + _rmsnorm(m, gpost_ref[...])
        y_ref[rows, :] = x1
        h2_buf[rows, :] = _rmsnorm(x1, gpre_ref[...]).astype(BF16)
    row = lax.broadcasted_iota(jnp.int32, (lt, FFN_F), 0)

    def cols(ci, base=0):
        start = base + ci * FFN_F
        return pl.ds(start if isinstance(ci, int) else pl.multiple_of(start, FFN_F), FFN_F)

    def stage1(ci, slot):
        h2 = h2_buf[...]
        g_buf[slot] = _dot(h2, wgu_ref[:, cols(ci)])
        u_buf[slot] = _dot(h2, wgu_ref[:, cols(ci, D_FF)])

    def stage2(ci, slot):
        gate = g_buf[slot]
        cw = cw_ref[:, cols(ci)]
        acts = []
        for s in range(n_seq):
            gs = gate[s * lt:(s + 1) * lt]
            prev = carry[ci, s * SUBLANES:(s + 1) * SUBLANES, :]
            p1 = prev[SUBLANES - 1:SUBLANES, :]
            p2 = prev[SUBLANES - 2:SUBLANES - 1, :]
            g1 = jnp.where(row == 0, p1, pltpu.roll(gs, 1, axis=0))
            g2 = jnp.where(row == 0, p2, jnp.where(row == 1, p1, pltpu.roll(gs, 2, axis=0)))
            gc = cw[3:4, :] + g2 * cw[0:1, :] + g1 * cw[1:2, :] + gs * cw[2:3, :]
            carry[ci, s * SUBLANES:(s + 1) * SUBLANES, :] = gs[lt - SUBLANES:lt]
            acts.append(_gelu_tanh(gc))
        act = acts[0] if n_seq == 1 else jnp.concatenate(acts, axis=0)
        acc[...] += _dot((act * u_buf[slot]).astype(BF16), wd_ref[cols(ci), :])

    acc[...] = jnp.zeros_like(acc)
    stage1(0, 0)

    def body(p, _):
        c = 2 * p + 1
        stage1(c, 1)
        stage2(c - 1, 0)
        stage1(c + 1, 0)
        stage2(c, 1)
        return 0

    assert FFN_NC % 2 == 1
    lax.fori_loop(0, FFN_NC // 2, body, 0)
    stage2(FFN_NC - 1, 0)
    for r0 in range(0, tm, FFN_ROWS):
        rows = slice(r0, r0 + FFN_ROWS)
        y_ref[rows, :] = y_ref[rows, :] + _rmsnorm(acc[rows, :], gfpost_ref[...])
    fcn_ref[0] = carry[...]


def _out_ffn(x2d, attn2d, ys2d, fc0, n_seq, lt, gains, weights):
    t = x2d.shape[0]
    tm = n_seq * lt
    n_groups = fc0.shape[0]
    tiles = t // (tm * n_groups)
    assert tiles * tm * n_groups == t
    row = lambda n: pl.BlockSpec((tm, n), lambda gi, ti: (gi * tiles + ti, 0))
    full = lambda a: pl.BlockSpec(a.shape, lambda gi, ti: (0,) * a.ndim, pipeline_mode=pl.Buffered(1))
    fc_spec = pl.BlockSpec((1,) + fc0.shape[1:], lambda gi, ti: (gi, 0, 0, 0))
    kernel = functools.partial(_out_ffn_kernel, n_seq=n_seq, lt=lt)
    return pl.pallas_call(
        kernel,
        grid=(n_groups, tiles),
        in_specs=[row(D_MODEL), row(D_ATT), row(D_SSM), fc_spec] + [full(a) for a in gains]
                 + [full(a) for a in weights],
        out_specs=(row(D_MODEL), fc_spec),
        out_shape=(jax.ShapeDtypeStruct((t, D_MODEL), F32), jax.ShapeDtypeStruct(fc0.shape, F32)),
        scratch_shapes=[pltpu.VMEM((tm, D_MODEL), BF16), pltpu.VMEM((tm, D_MODEL), F32),
                        pltpu.VMEM(fc0.shape[1:], F32), pltpu.VMEM((2, tm, FFN_F), F32),
                        pltpu.VMEM((2, tm, FFN_F), F32)],
        compiler_params=pltpu.CompilerParams(
            dimension_semantics=("arbitrary", "arbitrary"), vmem_limit_bytes=VMEM_LIMIT),
        name="out_ffn",
    )(x2d, attn2d, ys2d, fc0, *gains, *weights)


def _ffn_state_to_chunks(st, n_seq):
    b = st.shape[0]
    s = st.reshape(b // n_seq, n_seq, FFN_CONV - 1, FFN_NC, FFN_F)
    s = jnp.pad(s, ((0, 0), (0, 0), (SUBLANES - (FFN_CONV - 1), 0), (0, 0), (0, 0)))
    return jnp.transpose(s, (0, 3, 1, 2, 4)).reshape(b // n_seq, FFN_NC, n_seq * SUBLANES, FFN_F)


def _chunks_to_ffn_state(ch, n_seq):
    g = ch.shape[0]
    s = ch.reshape(g, FFN_NC, n_seq, SUBLANES, FFN_F)[:, :, :, SUBLANES - (FFN_CONV - 1):, :]
    return jnp.transpose(s, (0, 2, 3, 1, 4)).reshape(g * n_seq, FFN_CONV - 1, D_FF)


def _layer(x, caches, ssm_h0, ssm_conv_prev, ffn_conv_prev, p, is_prompt):
    b, l, _ = x.shape
    t = b * l
    x2d = x.reshape(t, D_MODEL)
    r3 = lambda a: a.reshape(b, l, a.shape[-1])
    proj_w = (p['g_mix_pre'], p['w_in_t'], p['w_dt_t'], p['dt_bias'])
    if is_prompt:
        qb, kb, vb, kt, vt, z, xbc, dt = _in_proj(x, *proj_w, kv_transposed=True)
        attn = _attn_prompt(qb, kb, vb)
        k, v = jnp.transpose(kt, (0, 3, 1, 2)), jnp.transpose(vt, (0, 3, 1, 2))
    else:
        qb, kb, vb, k, v, z, xbc, dt = _in_proj(x2d[None], *proj_w, kv_transposed=False)
        attn = _attn_sample(r3(qb), r3(kb), r3(vb), *caches)

    conv0 = jnp.pad(ssm_conv_prev, ((0, 0), (SUBLANES - (SSM_CONV - 1), 0), (0, 0)))
    ys, s_fin = _ssd(r3(xbc), r3(z), r3(dt), conv0, ssm_h0,
                     p['ssm_conv_w'], p['ssm_conv_b'], p['a_log'], p['d_skip'], p['g_ssm_out'])
    ssm_conv_new = r3(xbc)[:, l - (SSM_CONV - 1):, :]

    if is_prompt:
        n_seq, lt = 1, min(FFN_TM, l)
    else:
        n_seq, lt = b, l
    fc0 = _ffn_state_to_chunks(ffn_conv_prev, n_seq)
    gains = (p['g_attn_out'], p['g_mix_post'], p['g_ffn_pre'], p['g_ffn_post'])
    weights = (p['w_out_a'], p['w_out_s'], p['w_gu'], p['w_down'], p['ffn_cw'])
    y2d, fcn = _out_ffn(x2d, attn.reshape(t, D_ATT), ys.reshape(t, D_SSM), fc0, n_seq, lt, gains, weights)
    return (y2d.reshape(b, l, D_MODEL), k.reshape(b, l, N_ATT_HEADS, ATT_HEAD_DIM),
            v.reshape(b, l, N_ATT_HEADS, ATT_HEAD_DIM), s_fin, ssm_conv_new,
            _chunks_to_ffn_state(fcn, n_seq))


def _prep_params(i, g_mix_pre, g_mix_post, w_in, ssm_conv_w, ssm_conv_b, dt_bias, a_log, d_skip,
                 g_ssm_out, g_attn_out, w_out, g_ffn_pre, g_ffn_post, w_up, ffn_conv_w, ffn_conv_b, w_down):
    row = lambda a: a[i].reshape(1, -1).astype(F32)
    pad_lanes = lambda a: jnp.pad(a, ((0, 0), (0, LANES - a.shape[1])))
    wi_t = w_in[i].T.astype(BF16)
    ffn_cw = jnp.concatenate([ffn_conv_w[i], ffn_conv_b[i][None, :],
                              jnp.zeros((SUBLANES - FFN_CONV - 1, D_FF), F32)], axis=0)
    return {
        'g_mix_pre': row(g_mix_pre), 'g_mix_post': row(g_mix_post),
        'g_ffn_pre': row(g_ffn_pre), 'g_ffn_post': row(g_ffn_post),
        'g_attn_out': row(g_attn_out), 'g_ssm_out': row(g_ssm_out),
        'w_in_t': wi_t,
        'w_dt_t': jnp.pad(wi_t[D_MAIN_PROJ:], ((0, LANES - (wi_t.shape[0] - D_MAIN_PROJ)), (0, 0))),
        'dt_bias': pad_lanes(row(dt_bias)),
        'ssm_conv_w': ssm_conv_w[i].astype(F32), 'ssm_conv_b': row(ssm_conv_b),
        'a_log': pad_lanes(row(a_log)),
        'd_skip': jnp.repeat(d_skip[i].astype(F32), SSM_HEAD_DIM).reshape(1, D_SSM),
        'w_out_a': w_out[i][:D_ATT].astype(BF16), 'w_out_s': w_out[i][D_ATT:].astype(BF16),
        'w_gu': w_up[i].astype(BF16),
        'w_down': w_down[i].astype(BF16),
        'ffn_cw': ffn_cw,
    }


def kernel(x_prompt, x_sample, cache_k, cache_v, state_ssm, state_ssm_conv, state_ffn_conv, g_mix_pre, g_mix_post, w_in, ssm_conv_w, ssm_conv_b, dt_bias, a_log, d_skip, g_ssm_out, g_attn_out, w_out, g_ffn_pre, g_ffn_post, w_up, ffn_conv_w, ffn_conv_b, w_down):
    depth = w_in.shape[0]
    bp = x_prompt.shape[0]
    dtp = x_prompt.dtype
    zh = jnp.zeros((bp, N_SSM_HEADS, SSM_HEAD_DIM, SSM_STATE), dtp)
    zcs = jnp.zeros((bp, SSM_CONV - 1, CONV_DIM), dtp)
    zcf = jnp.zeros((bp, FFN_CONV - 1, D_FF), dtp)
    y_p, y_s = x_prompt, x_sample
    outs_p, outs_s = [], []
    for i in range(depth):
        p = _prep_params(i, g_mix_pre, g_mix_post, w_in, ssm_conv_w, ssm_conv_b, dt_bias, a_log, d_skip,
                         g_ssm_out, g_attn_out, w_out, g_ffn_pre, g_ffn_post, w_up, ffn_conv_w, ffn_conv_b,
                         w_down)
        rp = _layer(y_p, None, zh, zcs, zcf, p, True)
        rs = _layer(y_s, (cache_k, cache_v, i), state_ssm[i], state_ssm_conv[i], state_ffn_conv[i], p, False)
        y_p, y_s = rp[0], rs[0]
        outs_p.append(rp[1:])
        outs_s.append(rs[1:])
    stack = lambda outs, j: jnp.stack([o[j] for o in outs])
    return (y_p, y_s) + tuple(stack(outs_p, j) for j in range(5)) + tuple(stack(outs_s, j) for j in range(5))
```

```python
import functools
import math

import jax
import jax.numpy as jnp
from jax import lax
from jax.experimental import pallas as pl
from jax.experimental.pallas import tpu as pltpu

F32 = jnp.float32
BF16 = jnp.bfloat16

D_MODEL = 1024
D_ATT = 512
N_ATT_HEADS = 8
ATT_HEAD_DIM = 64
D_SSM = 512
N_SSM_HEADS = 8
SSM_HEAD_DIM = 64
SSM_STATE = 64
SSM_GROUPS = 2
SSM_CONV = 4
CONV_DIM = D_SSM + 2 * SSM_GROUPS * SSM_STATE
D_FF = 2816
FFN_CONV = 3
NORM_EPS = 1e-6
D_MAIN_PROJ = 3 * D_ATT + D_SSM + CONV_DIM
ATT_SCALE = ATT_HEAD_DIM ** -0.5
LOG2E = math.log2(math.e)
SB_EXHAUSTED = -120.0

LANES = 128
SUBLANES = 8
HEADS_PER_LANE_TILE = LANES // ATT_HEAD_DIM
N_PAIRS = N_SSM_HEADS // HEADS_PER_LANE_TILE

PROJ_TM = 512
ATT_TQ = 256
ATT_TK = 128
ATT_CACHE_CHUNK = 256
SSD_Q = 128
FFN_TM = 512
FFN_F = 256
FFN_ROWS = 128
FFN_NC = D_FF // FFN_F
VMEM_LIMIT = 56 * 1024 * 1024


def _rmsnorm(x, g):
    y = x * lax.rsqrt(jnp.mean(x * x, axis=-1, keepdims=True) + NORM_EPS)
    return y * g


def _softplus(x):
    return jnp.maximum(x, 0.0) + jnp.log1p(jnp.exp(-jnp.abs(x)))


def _silu(x):
    return x * (1.0 / (1.0 + jnp.exp(-x)))


def _dot(a, b):
    return jnp.dot(a, b, preferred_element_type=F32)


def _dot_nt(a, b):
    return lax.dot_general(a, b, (((1,), (1,)), ((), ())), preferred_element_type=F32)


def _dot_tn(a, b):
    return lax.dot_general(a, b, (((0,), (0,)), ((), ())), preferred_element_type=F32)


def _split2(x):
    hi = x.astype(BF16)
    lo = (x - hi.astype(F32)).astype(BF16)
    return hi, lo


def _split3(x):
    hi = x.astype(BF16)
    r1 = x - hi.astype(F32)
    mid = r1.astype(BF16)
    lo = (r1 - mid.astype(F32)).astype(BF16)
    return hi, mid, lo


def _in_proj_kernel(x_ref, g_ref, wt_ref, wdt_ref, dtb_ref,
                    qb_ref, kb_ref, vb_ref, k_ref, v_ref, z_ref, xbc_ref, dt_ref, *, kv_transposed):
    h = _rmsnorm(x_ref[0], g_ref[...]).astype(BF16)

    def proj(lo, hi):
        return _dot_nt(h, wt_ref[lo:hi, :])

    qb_ref[0] = (proj(0, D_ATT) * ATT_SCALE).astype(BF16)
    v = proj(2 * D_ATT, 3 * D_ATT)
    vb_ref[0] = v.astype(BF16)
    if kv_transposed:
        tm = v.shape[0]
        kt = _dot_nt(wt_ref[D_ATT:2 * D_ATT, :], h)
        kb_ref[0] = kt.astype(BF16)
        k_ref[0] = kt.reshape(N_ATT_HEADS, ATT_HEAD_DIM, tm)
        v_ref[0] = v.T.reshape(N_ATT_HEADS, ATT_HEAD_DIM, tm)
    else:
        k = proj(D_ATT, 2 * D_ATT)
        kb_ref[0] = k.astype(BF16)
        k_ref[0] = _lanes_to_heads(k)
        v_ref[0] = _lanes_to_heads(v)
    z_ref[0] = proj(3 * D_ATT, 3 * D_ATT + D_SSM)
    xbc_ref[0] = proj(3 * D_ATT + D_SSM, D_MAIN_PROJ)
    dt_ref[0] = _softplus(_dot_nt(h, wdt_ref[...]) + dtb_ref[...])


def _in_proj(x, g, w_t, w_dt, dt_bias, kv_transposed):
    b, l, _ = x.shape
    tm = min(PROJ_TM, l)
    assert l % tm == 0
    row = lambda n: pl.BlockSpec((1, tm, n), lambda bi, i: (bi, i, 0))
    full = lambda a: pl.BlockSpec(a.shape, lambda bi, i: (0,) * a.ndim)
    sds = jax.ShapeDtypeStruct
    if kv_transposed:
        kb_shape, kb_spec = sds((b, D_ATT, l), BF16), pl.BlockSpec((1, D_ATT, tm), lambda bi, i: (bi, 0, i))
        kv_shape = sds((b, N_ATT_HEADS, ATT_HEAD_DIM, l), F32)
        kv_spec = pl.BlockSpec((1, N_ATT_HEADS, ATT_HEAD_DIM, tm), lambda bi, i: (bi, 0, 0, i))
    else:
        kb_shape, kb_spec = sds((b, l, D_ATT), BF16), row(D_ATT)
        kv_shape = sds((b, l, N_ATT_HEADS, ATT_HEAD_DIM), F32)
        kv_spec = pl.BlockSpec((1, tm, N_ATT_HEADS, ATT_HEAD_DIM), lambda bi, i: (bi, i, 0, 0))
    out_shape = (sds((b, l, D_ATT), BF16), kb_shape, sds((b, l, D_ATT), BF16), kv_shape, kv_shape,
                 sds((b, l, D_SSM), F32), sds((b, l, CONV_DIM), F32), sds((b, l, LANES), F32))
    return pl.pallas_call(
        functools.partial(_in_proj_kernel, kv_transposed=kv_transposed),
        grid=(b, l // tm),
        in_specs=[row(D_MODEL), full(g), full(w_t), full(w_dt), full(dt_bias)],
        out_specs=(row(D_ATT), kb_spec, row(D_ATT), kv_spec, kv_spec,
                   row(D_SSM), row(CONV_DIM), row(LANES)),
        out_shape=out_shape,
        compiler_params=pltpu.CompilerParams(
            dimension_semantics=("arbitrary", "arbitrary"), vmem_limit_bytes=VMEM_LIMIT),
        name="in_proj",
    )(x, g, w_t, w_dt, dt_bias)


def _sb_weights(s, c, u2, mask):
    lk = -(jnp.maximum(s, 0.0) + jnp.log(1.0 + jnp.exp(-jnp.abs(s))))
    if mask is not None:
        lk = jnp.where(mask, lk, 0.0)
    hi, lo = _split2(lk)
    r = _dot(jnp.concatenate([hi, lo], axis=1), u2) + c
    w = jnp.exp(s + r)
    if mask is not None:
        w = jnp.where(mask, w, 0.0)
    return w, c + jnp.sum(lk, axis=1, keepdims=True)


def _head_lane_masks(n_lanes, dtype):
    lane = lax.broadcasted_iota(jnp.int32, (1, n_lanes), 1)
    return [jnp.where((lane >= h * ATT_HEAD_DIM) & (lane < (h + 1) * ATT_HEAD_DIM), 1.0, 0.0).astype(dtype)
            for h in range(n_lanes // ATT_HEAD_DIM)]


def _attn_prompt_kernel(q_ref, kt_ref, v_ref, u2x_ref, o_ref, s_buf, hl_buf, mask_buf, qs_buf, vcat_buf,
                        out_buf, *, tq, nq):
    tk = tq // 2
    hm = _head_lane_masks(LANES, BF16)
    big = 1e30
    causal = lax.broadcasted_iota(jnp.int32, (tq, tq), 1) < lax.broadcasted_iota(jnp.int32, (tq, tq), 0)
    mask_buf[0] = jnp.full((tq, tq), -jnp.inf, F32)
    mask_buf[1] = jnp.where(causal, -jnp.inf, big)
    for i in range(nq):
        qn = -q_ref[0, i * tq:(i + 1) * tq, :]
        vsb = v_ref[0, i * tq:(i + 1) * tq, :]
        for h in range(HEADS_PER_LANE_TILE):
            qs_buf[i, h * tq:(h + 1) * tq, :] = qn * hm[h]
            for n, half in enumerate((1, 0)):
                r0 = (n * HEADS_PER_LANE_TILE + h) * tk
                vcat_buf[i, r0:r0 + tk, :] = vsb[half * tk:(half + 1) * tk] * hm[h]

    def advance(i, j, skip_rest):
        last = (j == 0) | skip_rest
        return jnp.where(last, i + 1, i), jnp.where(last, i + 1, j - 1)

    def exhausted_after(c_prev, row_sums, first):
        carry = jnp.where(first, 0.0, c_prev[:, 0:1]) + row_sums
        return jnp.max(carry) < SB_EXHAUSTED

    def row_off(idx):
        return pl.multiple_of(jnp.minimum(idx, nq - 1) * tq, tq)

    def stage_a(i, j, slot):
        kb = kt_ref[0, :, pl.ds(row_off(j), tq)]
        t = _dot(qs_buf[jnp.minimum(i, nq - 1)], kb)
        floor = mask_buf[(i == j).astype(jnp.int32)]
        row_sums = []
        for h in range(HEADS_PER_LANE_TILE):
            th = jnp.maximum(t[h * tq:(h + 1) * tq], floor)
            s_buf[slot, h * tq:(h + 1) * tq, :] = th
            e = jnp.exp2(jnp.abs(th) * (-LOG2E))
            lk = jnp.minimum(th, 0.0) - jnp.log(1.0 + e)
            row_sums.append(jnp.sum(lk, axis=1, keepdims=True))
            hi, lo = _split2(lk)
            for half in range(2):
                r0 = half * 2 * tq + h * tq
                hl_buf[slot, r0:r0 + tq, 0:tk] = hi[:, half * tk:(half + 1) * tk]
                hl_buf[slot, r0:r0 + tq, tk:tq] = lo[:, half * tk:(half + 1) * tk]
        return jnp.concatenate(row_sums, axis=0)

    def stage_b(i, j, slot, c, acc):
        rr = _dot(hl_buf[slot], u2x_ref[...])
        first = i == j
        c = jnp.where(first, 0.0, c)
        acc = jnp.where(first, 0.0, acc)
        ws = []
        for half in (1, 0):
            r0 = half * 2 * tq
            r = rr[r0:r0 + 2 * tq, 0:tk] + c
            th = s_buf[slot, :, half * tk:(half + 1) * tk]
            w = jnp.exp(r - th).astype(BF16)
            c = c + rr[r0:r0 + 2 * tq, tk:tq]
            ws += [w[0:tq], w[tq:2 * tq]]
        acc = acc + _dot(jnp.concatenate(ws, axis=1), vcat_buf[jnp.minimum(j, nq - 1)])
        out_buf[jnp.minimum(i, nq)] = acc
        return c, acc

    def two_items(carry):
        ib, jb, ia, ja, c, acc = carry
        rs = stage_a(ia, ja, 1)
        c, acc = stage_b(ib, jb, 0, c, acc)
        i2, j2 = advance(ia, ja, exhausted_after(c, rs, ia == ja))
        rs = stage_a(i2, j2, 0)
        c, acc = stage_b(ia, ja, 1, c, acc)
        i3, j3 = advance(i2, j2, exhausted_after(c, rs, i2 == j2))
        return i2, j2, i3, j3, c, acc

    def body(carry):
        return two_items(two_items(carry))

    z = jnp.int32(0)
    stage_a(z, z, 0)
    init = (z, z, z + 1, z + 1, jnp.zeros((2 * tq, tk), F32), jnp.zeros((tq, LANES), F32))
    lax.while_loop(lambda carry: carry[0] < nq, body, init)
    for i in range(nq):
        o_ref[0, i * tq:(i + 1) * tq, :] = out_buf[i]


def _u2_matrix(tk):
    j = jnp.arange(2 * tk)[:, None] % tk
    s = jnp.arange(tk)[None, :]
    return (j >= s).astype(BF16)


def _attn_prompt(qb, kb, vb):
    b, l, _ = qb.shape
    tq = min(ATT_TQ, l)
    assert l % tq == 0 and tq == 2 * ATT_TK
    u2x = jnp.concatenate([_u2_matrix(ATT_TK), jnp.ones((tq, ATT_TK), BF16)], axis=1)
    kernel = functools.partial(_attn_prompt_kernel, tq=tq, nq=l // tq)
    seq = pl.BlockSpec((1, l, LANES), lambda bi, hp: (bi, 0, hp))
    return pl.pallas_call(
        kernel,
        grid=(b, D_ATT // LANES),
        in_specs=[seq, pl.BlockSpec((1, LANES, l), lambda bi, hp: (bi, hp, 0)), seq,
                  pl.BlockSpec(u2x.shape, lambda bi, hp: (0, 0))],
        out_specs=seq,
        out_shape=jax.ShapeDtypeStruct((b, l, D_ATT), F32),
        scratch_shapes=[pltpu.VMEM((2, 2 * tq, tq), F32), pltpu.VMEM((2, 4 * tq, tq), BF16),
                        pltpu.VMEM((2, tq, tq), F32), pltpu.VMEM((l // tq, 2 * tq, LANES), BF16),
                        pltpu.VMEM((l // tq, 2 * tq, LANES), BF16), pltpu.VMEM((l // tq + 1, tq, LANES), F32)],
        compiler_params=pltpu.CompilerParams(
            dimension_semantics=("arbitrary", "arbitrary"), vmem_limit_bytes=VMEM_LIMIT),
        name="attn_prompt",
    )(qb, kb, vb, u2x)


def _heads_to_lanes(x):
    t = x.shape[0]
    y = jnp.swapaxes(x.reshape(t // SUBLANES, SUBLANES, N_ATT_HEADS, ATT_HEAD_DIM), 1, 2)
    return jnp.concatenate([y[:, v].reshape(t, ATT_HEAD_DIM) for v in range(N_ATT_HEADS)], axis=1)


def _lanes_to_heads(x):
    t = x.shape[0]
    parts = []
    for g in range(D_ATT // LANES):
        a = x[:, g * LANES:(g + 1) * LANES]
        b = pltpu.roll(a, ATT_HEAD_DIM, axis=1)
        parts += [a.reshape(t // SUBLANES, SUBLANES, LANES), b.reshape(t // SUBLANES, SUBLANES, LANES)]
    y = jnp.swapaxes(jnp.stack(parts, axis=1), 1, 2).reshape(t, N_ATT_HEADS, LANES)
    return y[:, :, 0:ATT_HEAD_DIM]


def _attn_sample_kernel(q_ref, kn_ref, vn_ref, ck_hbm, cv_hbm, u2_ref, o_ref, kbuf, vbuf, sem,
                        *, l, tk, chunk, nch, layer):
    bi = pl.program_id(0)
    m = N_ATT_HEADS * l
    hm = _head_lane_masks(D_ATT, BF16)
    q = q_ref[0]
    qs = jnp.concatenate([q * mk for mk in hm], axis=0)
    u2 = u2_ref[...]

    def copies(ch, slot):
        keys = pl.ds(pl.multiple_of((nch - 1 - ch) * chunk, chunk), chunk)
        return (pltpu.make_async_copy(ck_hbm.at[layer, bi, :, :, keys], kbuf.at[slot], sem.at[0, slot]),
                pltpu.make_async_copy(cv_hbm.at[layer, bi, :, :, keys], vbuf.at[slot], sem.at[1, slot]))

    def start(ch, slot):
        for cp in copies(ch, slot):
            cp.start()

    def wait(ch, slot):
        for cp in copies(ch, slot):
            cp.wait()

    start(0, 0)
    pad = jnp.zeros((tk - l, D_ATT), BF16)
    kn = jnp.concatenate([kn_ref[0], pad], axis=0)
    vn = jnp.concatenate([vn_ref[0], pad], axis=0)
    row = jnp.concatenate([lax.broadcasted_iota(jnp.int32, (l, tk), 0)] * N_ATT_HEADS, axis=0)
    col = lax.broadcasted_iota(jnp.int32, (m, tk), 1)
    w, c = _sb_weights(_dot_nt(qs, kn), jnp.zeros((m, 1), F32), u2, col < row)
    acc = _dot(w.astype(BF16), vn)

    def live(c):
        return jnp.max(c) >= SB_EXHAUSTED

    def body(carry):
        ch, c, acc = carry
        slot = ch % 2
        wait(ch, slot)

        @pl.when(ch + 1 < nch)
        def _():
            start(ch + 1, 1 - slot)

        kt = kbuf[slot].reshape(D_ATT, chunk).astype(BF16)
        vt = vbuf[slot].reshape(D_ATT, chunk).astype(BF16)
        for t in reversed(range(chunk // tk)):
            w, c = _sb_weights(_dot(qs, kt[:, t * tk:(t + 1) * tk]), c, u2, None)
            acc = acc + _dot_nt(w.astype(BF16), vt[:, t * tk:(t + 1) * tk])
        return ch + 1, c, acc

    ch, c, acc = lax.while_loop(lambda carry: (carry[0] < nch) & live(carry[1]), body, (jnp.int32(0), c, acc))

    @pl.when(ch < nch)
    def _():
        wait(ch, ch % 2)

    hmf = _head_lane_masks(D_ATT, F32)
    out = acc[0:l] * hmf[0]
    for h in range(1, N_ATT_HEADS):
        out = out + acc[h * l:(h + 1) * l] * hmf[h]
    o_ref[0] = out


def _attn_sample(qb, kb_new, vb_new, cache_k, cache_v, layer):
    b, l, _ = qb.shape
    past = cache_k.shape[2]
    cache_k = jnp.transpose(cache_k, (0, 1, 3, 4, 2))
    cache_v = jnp.transpose(cache_v, (0, 1, 3, 4, 2))
    tk = ATT_TK
    chunk = min(ATT_CACHE_CHUNK, past)
    assert past % chunk == 0 and chunk % tk == 0 and l <= tk and l % 16 == 0
    u2 = _u2_matrix(tk)
    kernel = functools.partial(_attn_sample_kernel, l=l, tk=tk, chunk=chunk, nch=past // chunk, layer=layer)
    new = pl.BlockSpec((1, l, D_ATT), lambda bi: (bi, 0, 0))
    hbm = pl.BlockSpec(memory_space=pl.ANY)
    return pl.pallas_call(
        kernel,
        grid=(b,),
        in_specs=[new, new, new, hbm, hbm, pl.BlockSpec(u2.shape, lambda bi: (0, 0))],
        out_specs=new,
        out_shape=jax.ShapeDtypeStruct((b, l, D_ATT), F32),
        scratch_shapes=[pltpu.VMEM((2, N_ATT_HEADS, ATT_HEAD_DIM, chunk), F32),
                        pltpu.VMEM((2, N_ATT_HEADS, ATT_HEAD_DIM, chunk), F32),
                        pltpu.SemaphoreType.DMA((2, 2))],
        compiler_params=pltpu.CompilerParams(
            dimension_semantics=("arbitrary",), vmem_limit_bytes=VMEM_LIMIT),
        name="attn_sample",
    )(qb, kb_new, vb_new, cache_k, cache_v, u2)


def _ssd_kernel(xbc_ref, z_ref, dt_ref, conv0_ref, s0_ref, cw_ref, cb_ref, alog_ref, dskip_ref, g_ref,
                tri_ref, y_ref, sfin_ref, cbuf, state, *, q):
    c = pl.program_id(1)

    @pl.when(c == 0)
    def _():
        cbuf[...] = conv0_ref[0]
        zero = jnp.zeros((SSM_HEAD_DIM, SSM_STATE), F32)
        for pr in range(N_PAIRS):
            top = jnp.concatenate([s0_ref[0, 2 * pr], zero], axis=1)
            bot = jnp.concatenate([zero, s0_ref[0, 2 * pr + 1]], axis=1)
            state[pr] = jnp.concatenate([top, bot], axis=0)

    dt = dt_ref[0]
    da = dt * (-jnp.exp(alog_ref[...]))
    tri = tri_ref[...]
    hi, mid, lo = _split3(da)
    acum = _dot(tri, hi) + _dot(tri, mid) + _dot(tri, lo)
    acum_t = acum.T
    dt_t = dt.T
    a_end = acum[q - 1:q, :]
    trow = lax.broadcasted_iota(jnp.int32, (q, q), 0)
    tcol = lax.broadcasted_iota(jnp.int32, (q, q), 1)
    causal = tcol <= trow
    bd_r = lax.broadcasted_iota(jnp.int32, (LANES, LANES), 0) < SSM_STATE
    bd_c = lax.broadcasted_iota(jnp.int32, (LANES, LANES), 1) < SSM_STATE
    block_diag = bd_r == bd_c

    x = xbc_ref[0]
    prev = cbuf[...]
    row8 = lax.broadcasted_iota(jnp.int32, (SUBLANES, CONV_DIM), 0)
    xc = cb_ref[...] + x * cw_ref[SSM_CONV - 1:SSM_CONV, :]
    for k in range(1, SSM_CONV):
        r = pltpu.roll(x, k, axis=0)
        head = r[0:SUBLANES]
        for j in range(k):
            head = jnp.where(row8 == j, prev[SUBLANES - k + j:SUBLANES - k + j + 1, :], head)
        r = jnp.concatenate([head, r[SUBLANES:]], axis=0)
        xc = xc + r * cw_ref[SSM_CONV - 1 - k:SSM_CONV - k, :]
    xc = _silu(xc)
    cbuf[...] = x[q - SUBLANES:q]

    lane = lax.broadcasted_iota(jnp.int32, (1, LANES), 1)
    lo_half = lane < SSM_STATE
    bmat = xc[:, D_SSM:D_SSM + LANES]
    cmat = xc[:, D_SSM + LANES:D_SSM + 2 * LANES]
    b_sw = pltpu.roll(bmat, SSM_STATE, axis=1)
    c_sw = pltpu.roll(cmat, SSM_STATE, axis=1)
    bdup = [jnp.where(lo_half, bmat, b_sw), jnp.where(lo_half, b_sw, bmat)]
    cdup = [jnp.where(lo_half, cmat, c_sw), jnp.where(lo_half, c_sw, cmat)]
    gmask = [jnp.where(lo_half, 1.0, 0.0), jnp.where(lo_half, 0.0, 1.0)]
    bmat_b = bmat.astype(BF16)
    cb_g = [_dot_nt((cmat * gmask[g]).astype(BF16), bmat_b) for g in range(SSM_GROUPS)]

    hm_b = [jnp.where(lo_half, 1.0, 0.0).astype(BF16), jnp.where(lo_half, 0.0, 1.0).astype(BF16)]

    pairs = range(N_PAIRS)
    group = [pr // (N_PAIRS // SSM_GROUPS) for pr in pairs]
    x_pair = [xc[:, pr * LANES:(pr + 1) * LANES] for pr in pairs]
    x_b = [v.astype(BF16) for v in x_pair]
    s_pair = [state[pr] for pr in pairs]
    lhs_diag, lhs_off, upd = [], [], []
    for pr in pairs:
        h0 = pr * HEADS_PER_LANE_TILE
        ms = []
        for h in (h0, h0 + 1):
            seg = acum[:, h:h + 1] - acum_t[h:h + 1, :]
            lmat = jnp.where(causal, jnp.exp(seg), 0.0)
            ms.append((cb_g[group[pr]] * lmat * dt_t[h:h + 1, :]).astype(BF16))
        acol = jnp.where(lo_half, acum[:, h0:h0 + 1], acum[:, h0 + 1:h0 + 2])
        dcol = jnp.where(lo_half, dt[:, h0:h0 + 1], dt[:, h0 + 1:h0 + 2])
        aend = jnp.where(lo_half, a_end[:, h0:h0 + 1], a_end[:, h0 + 1:h0 + 2])
        lhs_diag.append(jnp.concatenate(ms, axis=1))
        lhs_off.append((cdup[group[pr]] * jnp.exp(acol)).astype(BF16))
        upd.append((aend, (bdup[group[pr]] * (jnp.exp(aend - acol) * dcol)).astype(BF16)))
    y_diag = [_dot(lhs_diag[pr], jnp.concatenate([x_b[pr] * hm_b[0], x_b[pr] * hm_b[1]], axis=0)) for pr in pairs]
    y_off = [_dot_nt(lhs_off[pr], s_pair[pr].astype(BF16)) for pr in pairs]
    ys = []
    ssq = jnp.zeros((q, 1), F32)
    for pr in pairs:
        y = y_diag[pr] + y_off[pr] + dskip_ref[:, pr * LANES:(pr + 1) * LANES] * x_pair[pr]
        yz = y * _silu(z_ref[0, :, pr * LANES:(pr + 1) * LANES])
        ssq = ssq + jnp.sum(yz * yz, axis=1, keepdims=True)
        ys.append(yz)

    inv = lax.rsqrt(ssq * (1.0 / D_SSM) + NORM_EPS)
    for pr in pairs:
        aend, bw = upd[pr]
        state[pr] = s_pair[pr] * jnp.exp(aend) + jnp.where(block_diag, _dot_tn(x_b[pr], bw), 0.0)
    for pr in range(N_PAIRS):
        y_ref[0, :, pr * LANES:(pr + 1) * LANES] = (
            ys[pr] * inv * g_ref[:, pr * LANES:(pr + 1) * LANES]).astype(BF16)

    @pl.when(c == pl.num_programs(1) - 1)
    def _():
        for pr in range(N_PAIRS):
            sfin_ref[0, 2 * pr] = state[pr, 0:SSM_HEAD_DIM, 0:SSM_STATE]
            sfin_ref[0, 2 * pr + 1] = state[pr, SSM_HEAD_DIM:LANES, SSM_STATE:LANES]


def _ssd(xbc, z, dt, conv0, s0, conv_w, conv_b, a_log_pad, dskip_lanes, g_ssm):
    b, l, _ = xbc.shape
    q = min(SSD_Q, l)
    assert l % q == 0 and q % SUBLANES == 0
    tri = (jnp.arange(q)[None, :] <= jnp.arange(q)[:, None]).astype(BF16)
    seq = lambda n: pl.BlockSpec((1, q, n), lambda bi, ci: (bi, ci, 0))
    per_b = lambda a: pl.BlockSpec((1,) + a.shape[1:], lambda bi, ci: (bi,) + (0,) * (a.ndim - 1))
    full = lambda a: pl.BlockSpec(a.shape, lambda bi, ci: (0,) * a.ndim)
    kernel = functools.partial(_ssd_kernel, q=q)
    return pl.pallas_call(
        kernel,
        grid=(b, l // q),
        in_specs=[seq(CONV_DIM), seq(D_SSM), seq(LANES), per_b(conv0), per_b(s0),
                  full(conv_w), full(conv_b), full(a_log_pad), full(dskip_lanes), full(g_ssm), full(tri)],
        out_specs=(seq(D_SSM), per_b(s0)),
        out_shape=(jax.ShapeDtypeStruct((b, l, D_SSM), BF16), jax.ShapeDtypeStruct(s0.shape, F32)),
        scratch_shapes=[pltpu.VMEM((SUBLANES, CONV_DIM), F32),
                        pltpu.VMEM((N_PAIRS, LANES, LANES), F32)],
        compiler_params=pltpu.CompilerParams(
            dimension_semantics=("arbitrary", "arbitrary"), vmem_limit_bytes=VMEM_LIMIT),
        name="ssd",
    )(xbc, z, dt, conv0, s0, conv_w, conv_b, a_log_pad, dskip_lanes, g_ssm, tri)


def _gelu_tanh(x):
    return 0.5 * x * (1.0 + jnp.tanh(math.sqrt(2.0 / math.pi) * (x + 0.044715 * (x * x * x))))


def _out_ffn_kernel(x_ref, attn_ref, ys_ref, fc0_ref, ga_ref, gpost_ref, gpre_ref, gfpost_ref,
                    woa_ref, wos_ref, wgu_ref, wd_ref, cw_ref,
                    y_ref, fcn_ref, h2_buf, acc, carry, g_buf, u_buf, *, n_seq, lt):
    t = pl.program_id(1)

    @pl.when(t == 0)
    def _():
        carry[...] = fc0_ref[0]

    tm = x_ref.shape[0]
    for r0 in range(0, tm, FFN_ROWS):
        rows = slice(r0, r0 + FFN_ROWS)
        an = _rmsnorm(attn_ref[rows, :], ga_ref[...]).astype(BF16)
        m = _dot(an, woa_ref[...]) + _dot(ys_ref[rows, :], wos_ref[...])
        x1 = x_ref[rows, :] + _rmsnorm(m, gpost_ref[...])
        y_ref[rows, :] = x1
        h2_buf[rows, :] = _rmsnorm(x1, gpre_ref[...]).astype(BF16)
    row = lax.broadcasted_iota(jnp.int32, (lt, FFN_F), 0)

    def cols(ci, base=0):
        start = base + ci * FFN_F
        return pl.ds(start if isinstance(ci, int) else pl.multiple_of(start, FFN_F), FFN_F)

    def stage1(ci, slot):
        h2 = h2_buf[...]
        g_buf[slot] = _dot(h2, wgu_ref[:, cols(ci)])
        u_buf[slot] = _dot(h2, wgu_ref[:, cols(ci, D_FF)])

    def stage2(ci, slot):
        gate = g_buf[slot]
        cw = cw_ref[:, cols(ci)]
        acts = []
        for s in range(n_seq):
            gs = gate[s * lt:(s + 1) * lt]
            prev = carry[ci, s * SUBLANES:(s + 1) * SUBLANES, :]
            p1 = prev[SUBLANES - 1:SUBLANES, :]
            p2 = prev[SUBLANES - 2:SUBLANES - 1, :]
            g1 = jnp.where(row == 0, p1, pltpu.roll(gs, 1, axis=0))
            g2 = jnp.where(row == 0, p2, jnp.where(row == 1, p1, pltpu.roll(gs, 2, axis=0)))
            gc = cw[3:4, :] + g2 * cw[0:1, :] + g1 * cw[1:2, :] + gs * cw[2:3, :]
            carry[ci, s * SUBLANES:(s + 1) * SUBLANES, :] = gs[lt - SUBLANES:lt]
            acts.append(_gelu_tanh(gc))
        act = acts[0] if n_seq == 1 else jnp.concatenate(acts, axis=0)
        acc[...] += _dot((act * u_buf[slot]).astype(BF16), wd_ref[cols(ci), :])

    acc[...] = jnp.zeros_like(acc)
    stage1(0, 0)

    def body(p, _):
        c = 2 * p + 1
        stage1(c, 1)
        stage2(c - 1, 0)
        stage1(c + 1, 0)
        stage2(c, 1)
        return 0

    assert FFN_NC % 2 == 1
    lax.fori_loop(0, FFN_NC // 2, body, 0)
    stage2(FFN_NC - 1, 0)
    for r0 in range(0, tm, FFN_ROWS):
        rows = slice(r0, r0 + FFN_ROWS)
        y_ref[rows, :] = y_ref[rows, :] + _rmsnorm(acc[rows, :], gfpost_ref[...])
    fcn_ref[0] = carry[...]


def _out_ffn(x2d, attn2d, ys2d, fc0, n_seq, lt, gains, weights):
    t = x2d.shape[0]
    tm = n_seq * lt
    n_groups = fc0.shape[0]
    tiles = t // (tm * n_groups)
    assert tiles * tm * n_groups == t
    row = lambda n: pl.BlockSpec((tm, n), lambda gi, ti: (gi * tiles + ti, 0))
    full = lambda a: pl.BlockSpec(a.shape, lambda gi, ti: (0,) * a.ndim, pipeline_mode=pl.Buffered(1))
    fc_spec = pl.BlockSpec((1,) + fc0.shape[1:], lambda gi, ti: (gi, 0, 0, 0))
    kernel = functools.partial(_out_ffn_kernel, n_seq=n_seq, lt=lt)
    return pl.pallas_call(
        kernel,
        grid=(n_groups, tiles),
        in_specs=[row(D_MODEL), row(D_ATT), row(D_SSM), fc_spec] + [full(a) for a in gains]
                 + [full(a) for a in weights],
        out_specs=(row(D_MODEL), fc_spec),
        out_shape=(jax.ShapeDtypeStruct((t, D_MODEL), F32), jax.ShapeDtypeStruct(fc0.shape, F32)),
        scratch_shapes=[pltpu.VMEM((tm, D_MODEL), BF16), pltpu.VMEM((tm, D_MODEL), F32),
                        pltpu.VMEM(fc0.shape[1:], F32), pltpu.VMEM((2, tm, FFN_F), F32),
                        pltpu.VMEM((2, tm, FFN_F), F32)],
        compiler_params=pltpu.CompilerParams(
            dimension_semantics=("arbitrary", "arbitrary"), vmem_limit_bytes=VMEM_LIMIT),
        name="out_ffn",
    )(x2d, attn2d, ys2d, fc0, *gains, *weights)


def _ffn_state_to_chunks(st, n_seq):
    b = st.shape[0]
    s = st.reshape(b // n_seq, n_seq, FFN_CONV - 1, FFN_NC, FFN_F)
    s = jnp.pad(s, ((0, 0), (0, 0), (SUBLANES - (FFN_CONV - 1), 0), (0, 0), (0, 0)))
    return jnp.transpose(s, (0, 3, 1, 2, 4)).reshape(b // n_seq, FFN_NC, n_seq * SUBLANES, FFN_F)


def _chunks_to_ffn_state(ch, n_seq):
    g = ch.shape[0]
    s = ch.reshape(g, FFN_NC, n_seq, SUBLANES, FFN_F)[:, :, :, SUBLANES - (FFN_CONV - 1):, :]
    return jnp.transpose(s, (0, 2, 3, 1, 4)).reshape(g * n_seq, FFN_CONV - 1, D_FF)


def _layer(x, caches, ssm_h0, ssm_conv_prev, ffn_conv_prev, p, is_prompt):
    b, l, _ = x.shape
    t = b * l
    x2d = x.reshape(t, D_MODEL)
    r3 = lambda a: a.reshape(b, l, a.shape[-1])
    proj_w = (p['g_mix_pre'], p['w_in_t'], p['w_dt_t'], p['dt_bias'])
    if is_prompt:
        qb, kb, vb, kt, vt, z, xbc, dt = _in_proj(x, *proj_w, kv_transposed=True)
        attn = _attn_prompt(qb, kb, vb)
        k, v = jnp.transpose(kt, (0, 3, 1, 2)), jnp.transpose(vt, (0, 3, 1, 2))
    else:
        qb, kb, vb, k, v, z, xbc, dt = _in_proj(x2d[None], *proj_w, kv_transposed=False)
        attn = _attn_sample(r3(qb), r3(kb), r3(vb), *caches)

    conv0 = jnp.pad(ssm_conv_prev, ((0, 0), (SUBLANES - (SSM_CONV - 1), 0), (0, 0)))
    ys, s_fin = _ssd(r3(xbc), r3(z), r3(dt), conv0, ssm_h0,
                     p['ssm_conv_w'], p['ssm_conv_b'], p['a_log'], p['d_skip'], p['g_ssm_out'])
    ssm_conv_new = r3(xbc)[:, l - (SSM_CONV - 1):, :]

    if is_prompt:
        n_seq, lt = 1, min(FFN_TM, l)
    else:
        n_seq, lt = b, l
    fc0 = _ffn_state_to_chunks(ffn_conv_prev, n_seq)
    gains = (p['g_attn_out'], p['g_mix_post'], p['g_ffn_pre'], p['g_ffn_post'])
    weights = (p['w_out_a'], p['w_out_s'], p['w_gu'], p['w_down'], p['ffn_cw'])
    y2d, fcn = _out_ffn(x2d, attn.reshape(t, D_ATT), ys.reshape(t, D_SSM), fc0, n_seq, lt, gains, weights)
    return (y2d.reshape(b, l, D_MODEL), k.reshape(b, l, N_ATT_HEADS, ATT_HEAD_DIM),
            v.reshape(b, l, N_ATT_HEADS, ATT_HEAD_DIM), s_fin, ssm_conv_new,
            _chunks_to_ffn_state(fcn, n_seq))


def _prep_params(i, g_mix_pre, g_mix_post, w_in, ssm_conv_w, ssm_conv_b, dt_bias, a_log, d_skip,
                 g_ssm_out, g_attn_out, w_out, g_ffn_pre, g_ffn_post, w_up, ffn_conv_w, ffn_conv_b, w_down):
    row = lambda a: a[i].reshape(1, -1).astype(F32)
    pad_lanes = lambda a: jnp.pad(a, ((0, 0), (0, LANES - a.shape[1])))
    wi_t = w_in[i].T.astype(BF16)
    ffn_cw = jnp.concatenate([ffn_conv_w[i], ffn_conv_b[i][None, :],
                              jnp.zeros((SUBLANES - FFN_CONV - 1, D_FF), F32)], axis=0)
    return {
        'g_mix_pre': row(g_mix_pre), 'g_mix_post': row(g_mix_post),
        'g_ffn_pre': row(g_ffn_pre), 'g_ffn_post': row(g_ffn_post),
        'g_attn_out': row(g_attn_out), 'g_ssm_out': row(g_ssm_out),
        'w_in_t': wi_t,
        'w_dt_t': jnp.pad(wi_t[D_MAIN_PROJ:], ((0, LANES - (wi_t.shape[0] - D_MAIN_PROJ)), (0, 0))),
        'dt_bias': pad_lanes(row(dt_bias)),
        'ssm_conv_w': ssm_conv_w[i].astype(F32), 'ssm_conv_b': row(ssm_conv_b),
        'a_log': pad_lanes(row(a_log)),
        'd_skip': jnp.repeat(d_skip[i].astype(F32), SSM_HEAD_DIM).reshape(1, D_SSM),
        'w_out_a': w_out[i][:D_ATT].astype(BF16), 'w_out_s': w_out[i][D_ATT:].astype(BF16),
        'w_gu': w_up[i].astype(BF16),
        'w_down': w_down[i].astype(BF16),
        'ffn_cw': ffn_cw,
    }


def kernel(x_prompt, x_sample, cache_k, cache_v, state_ssm, state_ssm_conv, state_ffn_conv, g_mix_pre, g_mix_post, w_in, ssm_conv_w, ssm_conv_b, dt_bias, a_log, d_skip, g_ssm_out, g_attn_out, w_out, g_ffn_pre, g_ffn_post, w_up, ffn_conv_w, ffn_conv_b, w_down):
    depth = w_in.shape[0]
    bp = x_prompt.shape[0]
    dtp = x_prompt.dtype
    zh = jnp.zeros((bp, N_SSM_HEADS, SSM_HEAD_DIM, SSM_STATE), dtp)
    zcs = jnp.zeros((bp, SSM_CONV - 1, CONV_DIM), dtp)
    zcf = jnp.zeros((bp, FFN_CONV - 1, D_FF), dtp)
    y_p, y_s = x_prompt, x_sample
    outs_p, outs_s = [], []
    for i in range(depth):
        p = _prep_params(i, g_mix_pre, g_mix_post, w_in, ssm_conv_w, ssm_conv_b, dt_bias, a_log, d_skip,
                         g_ssm_out, g_attn_out, w_out, g_ffn_pre, g_ffn_post, w_up, ffn_conv_w, ffn_conv_b,
                         w_down)
        rp = _layer(y_p, None, zh, zcs, zcf, p, True)
        rs = _layer(y_s, (cache_k, cache_v, i), state_ssm[i], state_ssm_conv[i], state_ffn_conv[i], p, False)
        y_p, y_s = rp[0], rs[0]
        outs_p.append(rp[1:])
        outs_s.append(rs[1:])
    stack = lambda outs, j: jnp.stack([o[j] for o in outs])
    return (y_p, y_s) + tuple(stack(outs_p, j) for j in range(5)) + tuple(stack(outs_s, j) for j in range(5))
```

```python
import functools
import math

import jax
import jax.numpy as jnp
from jax import lax
from jax.experimental import pallas as pl
from jax.experimental.pallas import tpu as pltpu

F32 = jnp.float32
BF16 = jnp.bfloat16

D_MODEL = 1024
D_ATT = 512
N_ATT_HEADS = 8
ATT_HEAD_DIM = 64
D_SSM = 512
N_SSM_HEADS = 8
SSM_HEAD_DIM = 64
SSM_STATE = 64
SSM_GROUPS = 2
SSM_CONV = 4
CONV_DIM = D_SSM + 2 * SSM_GROUPS * SSM_STATE
D_FF = 2816
FFN_CONV = 3
NORM_EPS = 1e-6
D_MAIN_PROJ = 3 * D_ATT + D_SSM + CONV_DIM
ATT_SCALE = ATT_HEAD_DIM ** -0.5
LOG2E = math.log2(math.e)
SB_EXHAUSTED = -120.0

LANES = 128
SUBLANES = 8
HEADS_PER_LANE_TILE = LANES // ATT_HEAD_DIM
N_PAIRS = N_SSM_HEADS // HEADS_PER_LANE_TILE

PROJ_TM = 512
ATT_TQ = 256
ATT_TK = 128
ATT_CACHE_CHUNK = 256
SSD_Q = 128
FFN_TM = 512
FFN_F = 256
FFN_ROWS = 128
FFN_NC = D_FF // FFN_F
VMEM_LIMIT = 56 * 1024 * 1024


def _rmsnorm(x, g):
    y = x * lax.rsqrt(jnp.mean(x * x, axis=-1, keepdims=True) + NORM_EPS)
    return y * g


def _softplus(x):
    return jnp.maximum(x, 0.0) + jnp.log1p(jnp.exp(-jnp.abs(x)))


def _silu(x):
    return x * (1.0 / (1.0 + jnp.exp(-x)))


def _dot(a, b):
    return jnp.dot(a, b, preferred_element_type=F32)


def _dot_nt(a, b):
    return lax.dot_general(a, b, (((1,), (1,)), ((), ())), preferred_element_type=F32)


def _dot_tn(a, b):
    return lax.dot_general(a, b, (((0,), (0,)), ((), ())), preferred_element_type=F32)


def _split2(x):
    hi = x.astype(BF16)
    lo = (x - hi.astype(F32)).astype(BF16)
    return hi, lo


def _split3(x):
    hi = x.astype(BF16)
    r1 = x - hi.astype(F32)
    mid = r1.astype(BF16)
    lo = (r1 - mid.astype(F32)).astype(BF16)
    return hi, mid, lo


def _store_attention_tiles(qs_ref, vc_ref, qn, vb):
    tq, tk = ATT_TQ, ATT_TQ // 2
    hm = _head_lane_masks(LANES, BF16)
    for p in range(D_ATT // LANES):
        for t in range(qn.shape[0] // tq):
            qt = qn[t * tq:(t + 1) * tq, p * LANES:(p + 1) * LANES]
            vt = vb[t * tq:(t + 1) * tq, p * LANES:(p + 1) * LANES]
            for h in range(HEADS_PER_LANE_TILE):
                qs_ref[0, p, t, h * tq:(h + 1) * tq, :] = qt * hm[h]
                for n, half in enumerate((1, 0)):
                    r0 = (n * HEADS_PER_LANE_TILE + h) * tk
                    vc_ref[0, p, t, r0:r0 + tk, :] = vt[half * tk:(half + 1) * tk] * hm[h]


def _in_proj_kernel(x_ref, g_ref, wt_ref, wdt_ref, dtb_ref,
                    qb_ref, kb_ref, vb_ref, k_ref, v_ref, z_ref, xbc_ref, dt_ref, *, kv_transposed):
    h = _rmsnorm(x_ref[0], g_ref[...]).astype(BF16)

    def proj(lo, hi):
        return _dot_nt(h, wt_ref[lo:hi, :])

    q = proj(0, D_ATT)
    v = proj(2 * D_ATT, 3 * D_ATT)
    if kv_transposed:
        tm = v.shape[0]
        _store_attention_tiles(qb_ref, vb_ref, (q * -ATT_SCALE).astype(BF16), v.astype(BF16))
        kt = _dot_nt(wt_ref[D_ATT:2 * D_ATT, :], h)
        kb_ref[0] = kt.astype(BF16)
        k_ref[0] = kt.reshape(N_ATT_HEADS, ATT_HEAD_DIM, tm)
        v_ref[0] = v.T.reshape(N_ATT_HEADS, ATT_HEAD_DIM, tm)
    else:
        qb_ref[0] = (q * ATT_SCALE).astype(BF16)
        vb_ref[0] = v.astype(BF16)
        k = proj(D_ATT, 2 * D_ATT)
        kb_ref[0] = k.astype(BF16)
        k_ref[0] = _lanes_to_heads(k)
        v_ref[0] = _lanes_to_heads(v)
    z_ref[0] = proj(3 * D_ATT, 3 * D_ATT + D_SSM)
    xbc_ref[0] = proj(3 * D_ATT + D_SSM, D_MAIN_PROJ)
    dt_ref[0] = _softplus(_dot_nt(h, wdt_ref[...]) + dtb_ref[...])


def _in_proj(x, g, w_t, w_dt, dt_bias, kv_transposed):
    b, l, _ = x.shape
    tm = min(PROJ_TM, l)
    assert l % tm == 0
    row = lambda n: pl.BlockSpec((1, tm, n), lambda bi, i: (bi, i, 0))
    full = lambda a: pl.BlockSpec(a.shape, lambda bi, i: (0,) * a.ndim)
    sds = jax.ShapeDtypeStruct
    if kv_transposed:
        kb_shape, kb_spec = sds((b, D_ATT, l), BF16), pl.BlockSpec((1, D_ATT, tm), lambda bi, i: (bi, 0, i))
        kv_shape = sds((b, N_ATT_HEADS, ATT_HEAD_DIM, l), F32)
        kv_spec = pl.BlockSpec((1, N_ATT_HEADS, ATT_HEAD_DIM, tm), lambda bi, i: (bi, 0, 0, i))
        assert tm % ATT_TQ == 0
        n_pairs, tiles = D_ATT // LANES, tm // ATT_TQ
        qv_shape = sds((b, n_pairs, l // ATT_TQ, 2 * ATT_TQ, LANES), BF16)
        qv_spec = pl.BlockSpec((1, n_pairs, tiles, 2 * ATT_TQ, LANES), lambda bi, i: (bi, 0, i, 0, 0))
    else:
        kb_shape, kb_spec = sds((b, l, D_ATT), BF16), row(D_ATT)
        kv_shape = sds((b, l, N_ATT_HEADS, ATT_HEAD_DIM), F32)
        kv_spec = pl.BlockSpec((1, tm, N_ATT_HEADS, ATT_HEAD_DIM), lambda bi, i: (bi, i, 0, 0))
        qv_shape, qv_spec = sds((b, l, D_ATT), BF16), row(D_ATT)
    out_shape = (qv_shape, kb_shape, qv_shape, kv_shape, kv_shape,
                 sds((b, l, D_SSM), F32), sds((b, l, CONV_DIM), F32), sds((b, l, LANES), F32))
    return pl.pallas_call(
        functools.partial(_in_proj_kernel, kv_transposed=kv_transposed),
        grid=(b, l // tm),
        in_specs=[row(D_MODEL), full(g), full(w_t), full(w_dt), full(dt_bias)],
        out_specs=(qv_spec, kb_spec, qv_spec, kv_spec, kv_spec,
                   row(D_SSM), row(CONV_DIM), row(LANES)),
        out_shape=out_shape,
        compiler_params=pltpu.CompilerParams(
            dimension_semantics=("arbitrary", "arbitrary"), vmem_limit_bytes=VMEM_LIMIT),
        name="in_proj",
    )(x, g, w_t, w_dt, dt_bias)


def _sb_weights(s, c, u2, mask):
    lk = -(jnp.maximum(s, 0.0) + jnp.log(1.0 + jnp.exp(-jnp.abs(s))))
    if mask is not None:
        lk = jnp.where(mask, lk, 0.0)
    hi, lo = _split2(lk)
    r = _dot(jnp.concatenate([hi, lo], axis=1), u2) + c
    w = jnp.exp(s + r)
    if mask is not None:
        w = jnp.where(mask, w, 0.0)
    return w, c + jnp.sum(lk, axis=1, keepdims=True)


def _head_lane_masks(n_lanes, dtype):
    lane = lax.broadcasted_iota(jnp.int32, (1, n_lanes), 1)
    return [jnp.where((lane >= h * ATT_HEAD_DIM) & (lane < (h + 1) * ATT_HEAD_DIM), 1.0, 0.0).astype(dtype)
            for h in range(n_lanes // ATT_HEAD_DIM)]


def _attn_prompt_kernel(qs_ref, kt_ref, vc_ref, u2x_ref, o_ref, s_buf, hl_buf, mask_buf, out_buf, *, tq, nq):
    tk = tq // 2
    big = 1e30
    causal = lax.broadcasted_iota(jnp.int32, (tq, tq), 1) < lax.broadcasted_iota(jnp.int32, (tq, tq), 0)
    mask_buf[0] = jnp.full((tq, tq), -jnp.inf, F32)
    mask_buf[1] = jnp.where(causal, -jnp.inf, big)

    def advance(i, j, skip_rest):
        last = (j == 0) | skip_rest
        return jnp.where(last, i + 1, i), jnp.where(last, i + 1, j - 1)

    def exhausted_after(c_prev, row_sums, first):
        carry = jnp.where(first, 0.0, c_prev[:, 0:1]) + row_sums
        return jnp.max(carry) < SB_EXHAUSTED

    def row_off(idx):
        return pl.multiple_of(jnp.minimum(idx, nq - 1) * tq, tq)

    def stage_a(i, j, slot):
        kb = kt_ref[0, :, pl.ds(row_off(j), tq)]
        t = _dot(qs_ref[0, 0, jnp.minimum(i, nq - 1)], kb)
        floor = mask_buf[(i == j).astype(jnp.int32)]
        row_sums = []
        for h in range(HEADS_PER_LANE_TILE):
            th = jnp.maximum(t[h * tq:(h + 1) * tq], floor)
            s_buf[slot, h * tq:(h + 1) * tq, :] = th
            e = jnp.exp2(jnp.abs(th) * (-LOG2E))
            lk = jnp.minimum(th, 0.0) - jnp.log(1.0 + e)
            row_sums.append(jnp.sum(lk, axis=1, keepdims=True))
            hi, lo = _split2(lk)
            for half in range(2):
                r0 = half * 2 * tq + h * tq
                hl_buf[slot, r0:r0 + tq, 0:tk] = hi[:, half * tk:(half + 1) * tk]
                hl_buf[slot, r0:r0 + tq, tk:tq] = lo[:, half * tk:(half + 1) * tk]
        return jnp.concatenate(row_sums, axis=0)

    def stage_b(i, j, slot, c, acc):
        rr = _dot(hl_buf[slot], u2x_ref[...])
        first = i == j
        c = jnp.where(first, 0.0, c)
        acc = jnp.where(first, 0.0, acc)
        ws = []
        for half in (1, 0):
            r0 = half * 2 * tq
            r = rr[r0:r0 + 2 * tq, 0:tk] + c
            th = s_buf[slot, :, half * tk:(half + 1) * tk]
            w = jnp.exp(r - th).astype(BF16)
            c = c + rr[r0:r0 + 2 * tq, tk:tq]
            ws += [w[0:tq], w[tq:2 * tq]]
        acc = acc + _dot(jnp.concatenate(ws, axis=1), vc_ref[0, 0, jnp.minimum(j, nq - 1)])
        out_buf[jnp.minimum(i, nq)] = acc
        return c, acc

    def two_items(carry):
        ib, jb, ia, ja, c, acc = carry
        rs = stage_a(ia, ja, 1)
        c, acc = stage_b(ib, jb, 0, c, acc)
        i2, j2 = advance(ia, ja, exhausted_after(c, rs, ia == ja))
        rs = stage_a(i2, j2, 0)
        c, acc = stage_b(ia, ja, 1, c, acc)
        i3, j3 = advance(i2, j2, exhausted_after(c, rs, i2 == j2))
        return i2, j2, i3, j3, c, acc

    def body(carry):
        return two_items(two_items(carry))

    z = jnp.int32(0)
    stage_a(z, z, 0)
    init = (z, z, z + 1, z + 1, jnp.zeros((2 * tq, tk), F32), jnp.zeros((tq, LANES), F32))
    lax.while_loop(lambda carry: carry[0] < nq, body, init)
    for i in range(nq):
        o_ref[0, i * tq:(i + 1) * tq, :] = out_buf[i]


def _u2_matrix(tk):
    j = jnp.arange(2 * tk)[:, None] % tk
    s = jnp.arange(tk)[None, :]
    return (j >= s).astype(BF16)


def _attn_prompt(qs, kt, vc):
    b, n_pairs, nq, _, _ = qs.shape
    tq = ATT_TQ
    l = nq * tq
    assert tq == 2 * ATT_TK and kt.shape == (b, D_ATT, l)
    u2x = jnp.concatenate([_u2_matrix(ATT_TK), jnp.ones((tq, ATT_TK), BF16)], axis=1)
    kernel = functools.partial(_attn_prompt_kernel, tq=tq, nq=nq)
    tiles = pl.BlockSpec((1, 1, nq, 2 * tq, LANES), lambda bi, hp: (bi, hp, 0, 0, 0))
    return pl.pallas_call(
        kernel,
        grid=(b, n_pairs),
        in_specs=[tiles, pl.BlockSpec((1, LANES, l), lambda bi, hp: (bi, hp, 0)), tiles,
                  pl.BlockSpec(u2x.shape, lambda bi, hp: (0, 0))],
        out_specs=pl.BlockSpec((1, l, LANES), lambda bi, hp: (bi, 0, hp)),
        out_shape=jax.ShapeDtypeStruct((b, l, D_ATT), F32),
        scratch_shapes=[pltpu.VMEM((2, 2 * tq, tq), F32), pltpu.VMEM((2, 4 * tq, tq), BF16),
                        pltpu.VMEM((2, tq, tq), F32), pltpu.VMEM((nq + 1, tq, LANES), F32)],
        compiler_params=pltpu.CompilerParams(
            dimension_semantics=("arbitrary", "arbitrary"), vmem_limit_bytes=VMEM_LIMIT),
        name="attn_prompt",
    )(qs, kt, vc, u2x)


def _heads_to_lanes(x):
    t = x.shape[0]
    y = jnp.swapaxes(x.reshape(t // SUBLANES, SUBLANES, N_ATT_HEADS, ATT_HEAD_DIM), 1, 2)
    return jnp.concatenate([y[:, v].reshape(t, ATT_HEAD_DIM) for v in range(N_ATT_HEADS)], axis=1)


def _lanes_to_heads(x):
    t = x.shape[0]
    parts = []
    for g in range(D_ATT // LANES):
        a = x[:, g * LANES:(g + 1) * LANES]
        b = pltpu.roll(a, ATT_HEAD_DIM, axis=1)
        parts += [a.reshape(t // SUBLANES, SUBLANES, LANES), b.reshape(t // SUBLANES, SUBLANES, LANES)]
    y = jnp.swapaxes(jnp.stack(parts, axis=1), 1, 2).reshape(t, N_ATT_HEADS, LANES)
    return y[:, :, 0:ATT_HEAD_DIM]


def _attn_sample_kernel(q_ref, kn_ref, vn_ref, ck_hbm, cv_hbm, u2_ref, o_ref, kbuf, vbuf, sem,
                        *, l, tk, chunk, nch, layer):
    bi = pl.program_id(0)
    m = N_ATT_HEADS * l
    hm = _head_lane_masks(D_ATT, BF16)
    q = q_ref[0]
    qs = jnp.concatenate([q * mk for mk in hm], axis=0)
    u2 = u2_ref[...]

    def copies(ch, slot):
        keys = pl.ds(pl.multiple_of((nch - 1 - ch) * chunk, chunk), chunk)
        return (pltpu.make_async_copy(ck_hbm.at[layer, bi, :, :, keys], kbuf.at[slot], sem.at[0, slot]),
                pltpu.make_async_copy(cv_hbm.at[layer, bi, :, :, keys], vbuf.at[slot], sem.at[1, slot]))

    def start(ch, slot):
        for cp in copies(ch, slot):
            cp.start()

    def wait(ch, slot):
        for cp in copies(ch, slot):
            cp.wait()

    start(0, 0)
    pad = jnp.zeros((tk - l, D_ATT), BF16)
    kn = jnp.concatenate([kn_ref[0], pad], axis=0)
    vn = jnp.concatenate([vn_ref[0], pad], axis=0)
    row = jnp.concatenate([lax.broadcasted_iota(jnp.int32, (l, tk), 0)] * N_ATT_HEADS, axis=0)
    col = lax.broadcasted_iota(jnp.int32, (m, tk), 1)
    w, c = _sb_weights(_dot_nt(qs, kn), jnp.zeros((m, 1), F32), u2, col < row)
    acc = _dot(w.astype(BF16), vn)

    def live(c):
        return jnp.max(c) >= SB_EXHAUSTED

    def body(carry):
        ch, c, acc = carry
        slot = ch % 2
        wait(ch, slot)

        @pl.when(ch + 1 < nch)
        def _():
            start(ch + 1, 1 - slot)

        kt = kbuf[slot].reshape(D_ATT, chunk).astype(BF16)
        vt = vbuf[slot].reshape(D_ATT, chunk).astype(BF16)
        for t in reversed(range(chunk // tk)):
            w, c = _sb_weights(_dot(qs, kt[:, t * tk:(t + 1) * tk]), c, u2, None)
            acc = acc + _dot_nt(w.astype(BF16), vt[:, t * tk:(t + 1) * tk])
        return ch + 1, c, acc

    ch, c, acc = lax.while_loop(lambda carry: (carry[0] < nch) & live(carry[1]), body, (jnp.int32(0), c, acc))

    @pl.when(ch < nch)
    def _():
        wait(ch, ch % 2)

    hmf = _head_lane_masks(D_ATT, F32)
    out = acc[0:l] * hmf[0]
    for h in range(1, N_ATT_HEADS):
        out = out + acc[h * l:(h + 1) * l] * hmf[h]
    o_ref[0] = out


def _attn_sample(qb, kb_new, vb_new, cache_k, cache_v, layer):
    b, l, _ = qb.shape
    past = cache_k.shape[2]
    cache_k = jnp.transpose(cache_k, (0, 1, 3, 4, 2))
    cache_v = jnp.transpose(cache_v, (0, 1, 3, 4, 2))
    tk = ATT_TK
    chunk = min(ATT_CACHE_CHUNK, past)
    assert past % chunk == 0 and chunk % tk == 0 and l <= tk and l % 16 == 0
    u2 = _u2_matrix(tk)
    kernel = functools.partial(_attn_sample_kernel, l=l, tk=tk, chunk=chunk, nch=past // chunk, layer=layer)
    new = pl.BlockSpec((1, l, D_ATT), lambda bi: (bi, 0, 0))
    hbm = pl.BlockSpec(memory_space=pl.ANY)
    return pl.pallas_call(
        kernel,
        grid=(b,),
        in_specs=[new, new, new, hbm, hbm, pl.BlockSpec(u2.shape, lambda bi: (0, 0))],
        out_specs=new,
        out_shape=jax.ShapeDtypeStruct((b, l, D_ATT), F32),
        scratch_shapes=[pltpu.VMEM((2, N_ATT_HEADS, ATT_HEAD_DIM, chunk), F32),
                        pltpu.VMEM((2, N_ATT_HEADS, ATT_HEAD_DIM, chunk), F32),
                        pltpu.SemaphoreType.DMA((2, 2))],
        compiler_params=pltpu.CompilerParams(
            dimension_semantics=("arbitrary",), vmem_limit_bytes=VMEM_LIMIT),
        name="attn_sample",
    )(qb, kb_new, vb_new, cache_k, cache_v, u2)


def _ssd_kernel(xbc_ref, z_ref, dt_ref, conv0_ref, s0_ref, cw_ref, cb_ref, alog_ref, dskip_ref, g_ref,
                tri_ref, y_ref, sfin_ref, cbuf, state, *, q):
    c = pl.program_id(1)

    @pl.when(c == 0)
    def _():
        cbuf[...] = conv0_ref[0]
        zero = jnp.zeros((SSM_HEAD_DIM, SSM_STATE), F32)
        for pr in range(N_PAIRS):
            top = jnp.concatenate([s0_ref[0, 2 * pr], zero], axis=1)
            bot = jnp.concatenate([zero, s0_ref[0, 2 * pr + 1]], axis=1)
            state[pr] = jnp.concatenate([top, bot], axis=0)

    dt = dt_ref[0]
    da = dt * (-jnp.exp(alog_ref[...]))
    tri = tri_ref[...]
    hi, mid, lo = _split3(da)
    acum = _dot(tri, hi) + _dot(tri, mid) + _dot(tri, lo)
    acum_t = acum.T
    dt_t = dt.T
    a_end = acum[q - 1:q, :]
    trow = lax.broadcasted_iota(jnp.int32, (q, q), 0)
    tcol = lax.broadcasted_iota(jnp.int32, (q, q), 1)
    causal = tcol <= trow
    bd_r = lax.broadcasted_iota(jnp.int32, (LANES, LANES), 0) < SSM_STATE
    bd_c = lax.broadcasted_iota(jnp.int32, (LANES, LANES), 1) < SSM_STATE
    block_diag = bd_r == bd_c

    x = xbc_ref[0]
    prev = cbuf[...]
    row8 = lax.broadcasted_iota(jnp.int32, (SUBLANES, CONV_DIM), 0)
    xc = cb_ref[...] + x * cw_ref[SSM_CONV - 1:SSM_CONV, :]
    for k in range(1, SSM_CONV):
        r = pltpu.roll(x, k, axis=0)
        head = r[0:SUBLANES]
        for j in range(k):
            head = jnp.where(row8 == j, prev[SUBLANES - k + j:SUBLANES - k + j + 1, :], head)
        r = jnp.concatenate([head, r[SUBLANES:]], axis=0)
        xc = xc + r * cw_ref[SSM_CONV - 1 - k:SSM_CONV - k, :]
    xc = _silu(xc)
    cbuf[...] = x[q - SUBLANES:q]

    lane = lax.broadcasted_iota(jnp.int32, (1, LANES), 1)
    lo_half = lane < SSM_STATE
    bmat = xc[:, D_SSM:D_SSM + LANES]
    cmat = xc[:, D_SSM + LANES:D_SSM + 2 * LANES]
    b_sw = pltpu.roll(bmat, SSM_STATE, axis=1)
    c_sw = pltpu.roll(cmat, SSM_STATE, axis=1)
    bdup = [jnp.where(lo_half, bmat, b_sw), jnp.where(lo_half, b_sw, bmat)]
    cdup = [jnp.where(lo_half, cmat, c_sw), jnp.where(lo_half, c_sw, cmat)]
    gmask = [jnp.where(lo_half, 1.0, 0.0), jnp.where(lo_half, 0.0, 1.0)]
    bmat_b = bmat.astype(BF16)
    cb_g = [_dot_nt((cmat * gmask[g]).astype(BF16), bmat_b) for g in range(SSM_GROUPS)]

    hm_b = [jnp.where(lo_half, 1.0, 0.0).astype(BF16), jnp.where(lo_half, 0.0, 1.0).astype(BF16)]

    pairs = range(N_PAIRS)
    group = [pr // (N_PAIRS // SSM_GROUPS) for pr in pairs]
    x_pair = [xc[:, pr * LANES:(pr + 1) * LANES] for pr in pairs]
    x_b = [v.astype(BF16) for v in x_pair]
    s_pair = [state[pr] for pr in pairs]
    lhs_diag, lhs_off, upd = [], [], []
    for pr in pairs:
        h0 = pr * HEADS_PER_LANE_TILE
        ms = []
        for h in (h0, h0 + 1):
            seg = acum[:, h:h + 1] - acum_t[h:h + 1, :]
            lmat = jnp.where(causal, jnp.exp(seg), 0.0)
            ms.append((cb_g[group[pr]] * lmat * dt_t[h:h + 1, :]).astype(BF16))
        acol = jnp.where(lo_half, acum[:, h0:h0 + 1], acum[:, h0 + 1:h0 + 2])
        dcol = jnp.where(lo_half, dt[:, h0:h0 + 1], dt[:, h0 + 1:h0 + 2])
        aend = jnp.where(lo_half, a_end[:, h0:h0 + 1], a_end[:, h0 + 1:h0 + 2])
        lhs_diag.append(jnp.concatenate(ms, axis=1))
        lhs_off.append((cdup[group[pr]] * jnp.exp(acol)).astype(BF16))
        upd.append((aend, (bdup[group[pr]] * (jnp.exp(aend - acol) * dcol)).astype(BF16)))
    y_diag = [_dot(lhs_diag[pr], jnp.concatenate([x_b[pr] * hm_b[0], x_b[pr] * hm_b[1]], axis=0)) for pr in pairs]
    y_off = [_dot_nt(lhs_off[pr], s_pair[pr].astype(BF16)) for pr in pairs]
    ys = []
    ssq = jnp.zeros((q, 1), F32)
    for pr in pairs:
        y = y_diag[pr] + y_off[pr] + dskip_ref[:, pr * LANES:(pr + 1) * LANES] * x_pair[pr]
        yz = y * _silu(z_ref[0, :, pr * LANES:(pr + 1) * LANES])
        ssq = ssq + jnp.sum(yz * yz, axis=1, keepdims=True)
        ys.append(yz)

    inv = lax.rsqrt(ssq * (1.0 / D_SSM) + NORM_EPS)
    for pr in pairs:
        aend, bw = upd[pr]
        state[pr] = s_pair[pr] * jnp.exp(aend) + jnp.where(block_diag, _dot_tn(x_b[pr], bw), 0.0)
    for pr in range(N_PAIRS):
        y_ref[0, :, pr * LANES:(pr + 1) * LANES] = (
            ys[pr] * inv * g_ref[:, pr * LANES:(pr + 1) * LANES]).astype(BF16)

    @pl.when(c == pl.num_programs(1) - 1)
    def _():
        for pr in range(N_PAIRS):
            sfin_ref[0, 2 * pr] = state[pr, 0:SSM_HEAD_DIM, 0:SSM_STATE]
            sfin_ref[0, 2 * pr + 1] = state[pr, SSM_HEAD_DIM:LANES, SSM_STATE:LANES]


def _ssd(xbc, z, dt, conv0, s0, conv_w, conv_b, a_log_pad, dskip_lanes, g_ssm):
    b, l, _ = xbc.shape
    q = min(SSD_Q, l)
    assert l % q == 0 and q % SUBLANES == 0
    tri = (jnp.arange(q)[None, :] <= jnp.arange(q)[:, None]).astype(BF16)
    seq = lambda n: pl.BlockSpec((1, q, n), lambda bi, ci: (bi, ci, 0))
    per_b = lambda a: pl.BlockSpec((1,) + a.shape[1:], lambda bi, ci: (bi,) + (0,) * (a.ndim - 1))
    full = lambda a: pl.BlockSpec(a.shape, lambda bi, ci: (0,) * a.ndim)
    kernel = functools.partial(_ssd_kernel, q=q)
    return pl.pallas_call(
        kernel,
        grid=(b, l // q),
        in_specs=[seq(CONV_DIM), seq(D_SSM), seq(LANES), per_b(conv0), per_b(s0),
                  full(conv_w), full(conv_b), full(a_log_pad), full(dskip_lanes), full(g_ssm), full(tri)],
        out_specs=(seq(D_SSM), per_b(s0)),
        out_shape=(jax.ShapeDtypeStruct((b, l, D_SSM), BF16), jax.ShapeDtypeStruct(s0.shape, F32)),
        scratch_shapes=[pltpu.VMEM((SUBLANES, CONV_DIM), F32),
                        pltpu.VMEM((N_PAIRS, LANES, LANES), F32)],
        compiler_params=pltpu.CompilerParams(
            dimension_semantics=("arbitrary", "arbitrary"), vmem_limit_bytes=VMEM_LIMIT),
        name="ssd",
    )(xbc, z, dt, conv0, s0, conv_w, conv_b, a_log_pad, dskip_lanes, g_ssm, tri)


def _gelu_tanh(x):
    return 0.5 * x * (1.0 + jnp.tanh(math.sqrt(2.0 / math.pi) * (x + 0.044715 * (x * x * x))))


def _out_ffn_kernel(x_ref, attn_ref, ys_ref, fc0_ref, ga_ref, gpost_ref, gpre_ref, gfpost_ref,
                    woa_ref, wos_ref, wgu_ref, wd_ref, cw_ref,
                    y_ref, fcn_ref, h2_buf, acc, carry, g_buf, u_buf, *, n_seq, lt):
    t = pl.program_id(1)

    @pl.when(t == 0)
    def _():
        carry[...] = fc0_ref[0]

    tm = x_ref.shape[0]
    for r0 in range(0, tm, FFN_ROWS):
        rows = slice(r0, r0 + FFN_ROWS)
        an = _rmsnorm(attn_ref[rows, :], ga_ref[...]).astype(BF16)
        m = _dot(an, woa_ref[...]) + _dot(ys_ref[rows, :], wos_ref[...])
        x1 = x_ref[rows, :] + _rmsnorm(m, gpost_ref[...])
        y_ref[rows, :] = x1
        h2_buf[rows, :] = _rmsnorm(x1, gpre_ref[...]).astype(BF16)
    row = lax.broadcasted_iota(jnp.int32, (lt, FFN_F), 0)

    def cols(ci, base=0):
        start = base + ci * FFN_F
        return pl.ds(start if isinstance(ci, int) else pl.multiple_of(start, FFN_F), FFN_F)

    def stage1(ci, slot):
        h2 = h2_buf[...]
        g_buf[slot] = _dot(h2, wgu_ref[:, cols(ci)])
        u_buf[slot] = _dot(h2, wgu_ref[:, cols(ci, D_FF)])

    def stage2(ci, slot):
        gate = g_buf[slot]
        cw = cw_ref[:, cols(ci)]
        acts = []
        for s in range(n_seq):
            gs = gate[s * lt:(s + 1) * lt]
            prev = carry[ci, s * SUBLANES:(s + 1) * SUBLANES, :]
            p1 = prev[SUBLANES - 1:SUBLANES, :]
            p2 = prev[SUBLANES - 2:SUBLANES - 1, :]
            g1 = jnp.where(row == 0, p1, pltpu.roll(gs, 1, axis=0))
            g2 = jnp.where(row == 0, p2, jnp.where(row == 1, p1, pltpu.roll(gs, 2, axis=0)))
            gc = cw[3:4, :] + g2 * cw[0:1, :] + g1 * cw[1:2, :] + gs * cw[2:3, :]
            carry[ci, s * SUBLANES:(s + 1) * SUBLANES, :] = gs[lt - SUBLANES:lt]
            acts.append(_gelu_tanh(gc))
        act = acts[0] if n_seq == 1 else jnp.concatenate(acts, axis=0)
        acc[...] += _dot((act * u_buf[slot]).astype(BF16), wd_ref[cols(ci), :])

    acc[...] = jnp.zeros_like(acc)
    stage1(0, 0)

    def body(p, _):
        c = 2 * p + 1
        stage1(c, 1)
        stage2(c - 1, 0)
        stage1(c + 1, 0)
        stage2(c, 1)
        return 0

    assert FFN_NC % 2 == 1
    lax.fori_loop(0, FFN_NC // 2, body, 0)
    stage2(FFN_NC - 1, 0)
    for r0 in range(0, tm, FFN_ROWS):
        rows = slice(r0, r0 + FFN_ROWS)
        y_ref[rows, :] = y_ref[rows, :] + _rmsnorm(acc[rows, :], gfpost_ref[...])
    fcn_ref[0] = carry[...]


def _out_ffn(x2d, attn2d, ys2d, fc0, n_seq, lt, gains, weights):
    t = x2d.shape[0]
    tm = n_seq * lt
    n_groups = fc0.shape[0]
    tiles = t // (tm * n_groups)
    assert tiles * tm * n_groups == t
    row = lambda n: pl.BlockSpec((tm, n), lambda gi, ti: (gi * tiles + ti, 0))
    full = lambda a: pl.BlockSpec(a.shape, lambda gi, ti: (0,) * a.ndim, pipeline_mode=pl.Buffered(1))
    fc_spec = pl.BlockSpec((1,) + fc0.shape[1:], lambda gi, ti: (gi, 0, 0, 0))
    kernel = functools.partial(_out_ffn_kernel, n_seq=n_seq, lt=lt)
    return pl.pallas_call(
        kernel,
        grid=(n_groups, tiles),
        in_specs=[row(D_MODEL), row(D_ATT), row(D_SSM), fc_spec] + [full(a) for a in gains]
                 + [full(a) for a in weights],
        out_specs=(row(D_MODEL), fc_spec),
        out_shape=(jax.ShapeDtypeStruct((t, D_MODEL), F32), jax.ShapeDtypeStruct(fc0.shape, F32)),
        scratch_shapes=[pltpu.VMEM((tm, D_MODEL), BF16), pltpu.VMEM((tm, D_MODEL), F32),
                        pltpu.VMEM(fc0.shape[1:], F32), pltpu.VMEM((2, tm, FFN_F), F32),
                        pltpu.VMEM((2, tm, FFN_F), F32)],
        compiler_params=pltpu.CompilerParams(
            dimension_semantics=("arbitrary", "arbitrary"), vmem_limit_bytes=VMEM_LIMIT),
        name="out_ffn",
    )(x2d, attn2d, ys2d, fc0, *gains, *weights)


def _ffn_state_to_chunks(st, n_seq):
    b = st.shape[0]
    s = st.reshape(b // n_seq, n_seq, FFN_CONV - 1, FFN_NC, FFN_F)
    s = jnp.pad(s, ((0, 0), (0, 0), (SUBLANES - (FFN_CONV - 1), 0), (0, 0), (0, 0)))
    return jnp.transpose(s, (0, 3, 1, 2, 4)).reshape(b // n_seq, FFN_NC, n_seq * SUBLANES, FFN_F)


def _chunks_to_ffn_state(ch, n_seq):
    g = ch.shape[0]
    s = ch.reshape(g, FFN_NC, n_seq, SUBLANES, FFN_F)[:, :, :, SUBLANES - (FFN_CONV - 1):, :]
    return jnp.transpose(s, (0, 2, 3, 1, 4)).reshape(g * n_seq, FFN_CONV - 1, D_FF)


def _layer(x, caches, ssm_h0, ssm_conv_prev, ffn_conv_prev, p, is_prompt):
    b, l, _ = x.shape
    t = b * l
    x2d = x.reshape(t, D_MODEL)
    r3 = lambda a: a.reshape(b, l, a.shape[-1])
    proj_w = (p['g_mix_pre'], p['w_in_t'], p['w_dt_t'], p['dt_bias'])
    if is_prompt:
        qb, kb, vb, kt, vt, z, xbc, dt = _in_proj(x, *proj_w, kv_transposed=True)
        attn = _attn_prompt(qb, kb, vb)
        k, v = jnp.transpose(kt, (0, 3, 1, 2)), jnp.transpose(vt, (0, 3, 1, 2))
    else:
        qb, kb, vb, k, v, z, xbc, dt = _in_proj(x2d[None], *proj_w, kv_transposed=False)
        attn = _attn_sample(r3(qb), r3(kb), r3(vb), *caches)

    conv0 = jnp.pad(ssm_conv_prev, ((0, 0), (SUBLANES - (SSM_CONV - 1), 0), (0, 0)))
    ys, s_fin = _ssd(r3(xbc), r3(z), r3(dt), conv0, ssm_h0,
                     p['ssm_conv_w'], p['ssm_conv_b'], p['a_log'], p['d_skip'], p['g_ssm_out'])
    ssm_conv_new = r3(xbc)[:, l - (SSM_CONV - 1):, :]

    if is_prompt:
        n_seq, lt = 1, min(FFN_TM, l)
    else:
        n_seq, lt = b, l
    fc0 = _ffn_state_to_chunks(ffn_conv_prev, n_seq)
    gains = (p['g_attn_out'], p['g_mix_post'], p['g_ffn_pre'], p['g_ffn_post'])
    weights = (p['w_out_a'], p['w_out_s'], p['w_gu'], p['w_down'], p['ffn_cw'])
    y2d, fcn = _out_ffn(x2d, attn.reshape(t, D_ATT), ys.reshape(t, D_SSM), fc0, n_seq, lt, gains, weights)
    return (y2d.reshape(b, l, D_MODEL), k.reshape(b, l, N_ATT_HEADS, ATT_HEAD_DIM),
            v.reshape(b, l, N_ATT_HEADS, ATT_HEAD_DIM), s_fin, ssm_conv_new,
            _chunks_to_ffn_state(fcn, n_seq))


def _prep_params(i, g_mix_pre, g_mix_post, w_in, ssm_conv_w, ssm_conv_b, dt_bias, a_log, d_skip,
                 g_ssm_out, g_attn_out, w_out, g_ffn_pre, g_ffn_post, w_up, ffn_conv_w, ffn_conv_b, w_down):
    row = lambda a: a[i].reshape(1, -1).astype(F32)
    pad_lanes = lambda a: jnp.pad(a, ((0, 0), (0, LANES - a.shape[1])))
    wi_t = w_in[i].T.astype(BF16)
    ffn_cw = jnp.concatenate([ffn_conv_w[i], ffn_conv_b[i][None, :],
                              jnp.zeros((SUBLANES - FFN_CONV - 1, D_FF), F32)], axis=0)
    return {
        'g_mix_pre': row(g_mix_pre), 'g_mix_post': row(g_mix_post),
        'g_ffn_pre': row(g_ffn_pre), 'g_ffn_post': row(g_ffn_post),
        'g_attn_out': row(g_attn_out), 'g_ssm_out': row(g_ssm_out),
        'w_in_t': wi_t,
        'w_dt_t': jnp.pad(wi_t[D_MAIN_PROJ:], ((0, LANES - (wi_t.shape[0] - D_MAIN_PROJ)), (0, 0))),
        'dt_bias': pad_lanes(row(dt_bias)),
        'ssm_conv_w': ssm_conv_w[i].astype(F32), 'ssm_conv_b': row(ssm_conv_b),
        'a_log': pad_lanes(row(a_log)),
        'd_skip': jnp.repeat(d_skip[i].astype(F32), SSM_HEAD_DIM).reshape(1, D_SSM),
        'w_out_a': w_out[i][:D_ATT].astype(BF16), 'w_out_s': w_out[i][D_ATT:].astype(BF16),
        'w_gu': w_up[i].astype(BF16),
        'w_down': w_down[i].astype(BF16),
        'ffn_cw': ffn_cw,
    }


def kernel(x_prompt, x_sample, cache_k, cache_v, state_ssm, state_ssm_conv, state_ffn_conv, g_mix_pre, g_mix_post, w_in, ssm_conv_w, ssm_conv_b, dt_bias, a_log, d_skip, g_ssm_out, g_attn_out, w_out, g_ffn_pre, g_ffn_post, w_up, ffn_conv_w, ffn_conv_b, w_down):
    depth = w_in.shape[0]
    bp = x_prompt.shape[0]
    dtp = x_prompt.dtype
    zh = jnp.zeros((bp, N_SSM_HEADS, SSM_HEAD_DIM, SSM_STATE), dtp)
    zcs = jnp.zeros((bp, SSM_CONV - 1, CONV_DIM), dtp)
    zcf = jnp.zeros((bp, FFN_CONV - 1, D_FF), dtp)
    y_p, y_s = x_prompt, x_sample
    outs_p, outs_s = [], []
    for i in range(depth):
        p = _prep_params(i, g_mix_pre, g_mix_post, w_in, ssm_conv_w, ssm_conv_b, dt_bias, a_log, d_skip,
                         g_ssm_out, g_attn_out, w_out, g_ffn_pre, g_ffn_post, w_up, ffn_conv_w, ffn_conv_b,
                         w_down)
        rp = _layer(y_p, None, zh, zcs, zcf, p, True)
        rs = _layer(y_s, (cache_k, cache_v, i), state_ssm[i], state_ssm_conv[i], state_ffn_conv[i], p, False)
        y_p, y_s = rp[0], rs[0]
        outs_p.append(rp[1:])
        outs_s.append(rs[1:])
    stack = lambda outs, j: jnp.stack([o[j] for o in outs])
    return (y_p, y_s) + tuple(stack(outs_p, j) for j in range(5)) + tuple(stack(outs_s, j) for j in range(5))
```

```python
import functools
import math

import jax
import jax.numpy as jnp
from jax import lax
from jax.experimental import pallas as pl
from jax.experimental.pallas import tpu as pltpu

F32 = jnp.float32
BF16 = jnp.bfloat16

D_MODEL = 1024
D_ATT = 512
N_ATT_HEADS = 8
ATT_HEAD_DIM = 64
D_SSM = 512
N_SSM_HEADS = 8
SSM_HEAD_DIM = 64
SSM_STATE = 64
SSM_GROUPS = 2
SSM_CONV = 4
CONV_DIM = D_SSM + 2 * SSM_GROUPS * SSM_STATE
D_FF = 2816
FFN_CONV = 3
NORM_EPS = 1e-6
D_MAIN_PROJ = 3 * D_ATT + D_SSM + CONV_DIM
ATT_SCALE = ATT_HEAD_DIM ** -0.5
LOG2E = math.log2(math.e)
SB_EXHAUSTED = -120.0

LANES = 128
SUBLANES = 8
HEADS_PER_LANE_TILE = LANES // ATT_HEAD_DIM
N_PAIRS = N_SSM_HEADS // HEADS_PER_LANE_TILE

PROJ_TM = 512
ATT_TQ = 256
ATT_TK = 128
ATT_CACHE_CHUNK = 256
SSD_Q = 128
FFN_TM = 512
FFN_F = 256
FFN_ROWS = 128
FFN_NC = D_FF // FFN_F
VMEM_LIMIT = 56 * 1024 * 1024


def _rmsnorm(x, g):
    y = x * lax.rsqrt(jnp.mean(x * x, axis=-1, keepdims=True) + NORM_EPS)
    return y * g


def _softplus(x):
    return jnp.maximum(x, 0.0) + jnp.log1p(jnp.exp(-jnp.abs(x)))


def _silu(x):
    return x * (1.0 / (1.0 + jnp.exp(-x)))


def _dot(a, b):
    return jnp.dot(a, b, preferred_element_type=F32)


def _dot_nt(a, b):
    return lax.dot_general(a, b, (((1,), (1,)), ((), ())), preferred_element_type=F32)


def _dot_tn(a, b):
    return lax.dot_general(a, b, (((0,), (0,)), ((), ())), preferred_element_type=F32)


def _split2(x):
    hi = x.astype(BF16)
    lo = (x - hi.astype(F32)).astype(BF16)
    return hi, lo


def _split3(x):
    hi = x.astype(BF16)
    r1 = x - hi.astype(F32)
    mid = r1.astype(BF16)
    lo = (r1 - mid.astype(F32)).astype(BF16)
    return hi, mid, lo


def _store_attention_tiles(qs_ref, vc_ref, qn, vb):
    tq, tk = ATT_TQ, ATT_TQ // 2
    hm = _head_lane_masks(LANES, BF16)
    for p in range(D_ATT // LANES):
        for t in range(qn.shape[0] // tq):
            qt = qn[t * tq:(t + 1) * tq, p * LANES:(p + 1) * LANES]
            vt = vb[t * tq:(t + 1) * tq, p * LANES:(p + 1) * LANES]
            for h in range(HEADS_PER_LANE_TILE):
                qs_ref[0, p, t, h * tq:(h + 1) * tq, :] = qt * hm[h]
                for n, half in enumerate((1, 0)):
                    r0 = (n * HEADS_PER_LANE_TILE + h) * tk
                    vc_ref[0, p, t, r0:r0 + tk, :] = vt[half * tk:(half + 1) * tk] * hm[h]


def _in_proj_kernel(x_ref, g_ref, wt_ref, wdt_ref, dtb_ref,
                    qb_ref, kb_ref, vb_ref, k_ref, v_ref, z_ref, xbc_ref, dt_ref, *, kv_transposed):
    h = _rmsnorm(x_ref[0], g_ref[...]).astype(BF16)

    def proj(lo, hi):
        return _dot_nt(h, wt_ref[lo:hi, :])

    q = proj(0, D_ATT)
    v = proj(2 * D_ATT, 3 * D_ATT)
    if kv_transposed:
        tm = v.shape[0]
        _store_attention_tiles(qb_ref, vb_ref, (q * -ATT_SCALE).astype(BF16), v.astype(BF16))
        kt = _dot_nt(wt_ref[D_ATT:2 * D_ATT, :], h)
        kb_ref[0] = kt.astype(BF16)
        k_ref[0] = kt.reshape(N_ATT_HEADS, ATT_HEAD_DIM, tm)
        v_ref[0] = v.T.reshape(N_ATT_HEADS, ATT_HEAD_DIM, tm)
    else:
        qb_ref[0] = (q * ATT_SCALE).astype(BF16)
        vb_ref[0] = v.astype(BF16)
        k = proj(D_ATT, 2 * D_ATT)
        kb_ref[0] = k.astype(BF16)
        k_ref[0] = _lanes_to_heads(k)
        v_ref[0] = _lanes_to_heads(v)
    z_ref[0] = proj(3 * D_ATT, 3 * D_ATT + D_SSM)
    xbc_ref[0] = proj(3 * D_ATT + D_SSM, D_MAIN_PROJ)
    dt_ref[0] = _softplus(_dot_nt(h, wdt_ref[...]) + dtb_ref[...])


def _in_proj(x, g, w_t, w_dt, dt_bias, kv_transposed):
    b, l, _ = x.shape
    tm = min(PROJ_TM, l)
    assert l % tm == 0
    row = lambda n: pl.BlockSpec((1, tm, n), lambda bi, i: (bi, i, 0))
    full = lambda a: pl.BlockSpec(a.shape, lambda bi, i: (0,) * a.ndim)
    sds = jax.ShapeDtypeStruct
    if kv_transposed:
        kb_shape, kb_spec = sds((b, D_ATT, l), BF16), pl.BlockSpec((1, D_ATT, tm), lambda bi, i: (bi, 0, i))
        kv_shape = sds((b, N_ATT_HEADS, ATT_HEAD_DIM, l), F32)
        kv_spec = pl.BlockSpec((1, N_ATT_HEADS, ATT_HEAD_DIM, tm), lambda bi, i: (bi, 0, 0, i))
        assert tm % ATT_TQ == 0
        n_pairs, tiles = D_ATT // LANES, tm // ATT_TQ
        qv_shape = sds((b, n_pairs, l // ATT_TQ, 2 * ATT_TQ, LANES), BF16)
        qv_spec = pl.BlockSpec((1, n_pairs, tiles, 2 * ATT_TQ, LANES), lambda bi, i: (bi, 0, i, 0, 0))
    else:
        kb_shape, kb_spec = sds((b, l, D_ATT), BF16), row(D_ATT)
        kv_shape = sds((b, l, N_ATT_HEADS, ATT_HEAD_DIM), F32)
        kv_spec = pl.BlockSpec((1, tm, N_ATT_HEADS, ATT_HEAD_DIM), lambda bi, i: (bi, i, 0, 0))
        qv_shape, qv_spec = sds((b, l, D_ATT), BF16), row(D_ATT)
    out_shape = (qv_shape, kb_shape, qv_shape, kv_shape, kv_shape,
                 sds((b, l, D_SSM), F32), sds((b, l, CONV_DIM), F32), sds((b, l, LANES), F32))
    return pl.pallas_call(
        functools.partial(_in_proj_kernel, kv_transposed=kv_transposed),
        grid=(b, l // tm),
        in_specs=[row(D_MODEL), full(g), full(w_t), full(w_dt), full(dt_bias)],
        out_specs=(qv_spec, kb_spec, qv_spec, kv_spec, kv_spec,
                   row(D_SSM), row(CONV_DIM), row(LANES)),
        out_shape=out_shape,
        compiler_params=pltpu.CompilerParams(
            dimension_semantics=("arbitrary", "arbitrary"), vmem_limit_bytes=VMEM_LIMIT),
        name="in_proj",
    )(x, g, w_t, w_dt, dt_bias)


def _sb_weights(s, c, u2, mask):
    lk = -(jnp.maximum(s, 0.0) + jnp.log(1.0 + jnp.exp(-jnp.abs(s))))
    if mask is not None:
        lk = jnp.where(mask, lk, 0.0)
    hi, lo = _split2(lk)
    r = _dot(jnp.concatenate([hi, lo], axis=1), u2) + c
    w = jnp.exp(s + r)
    if mask is not None:
        w = jnp.where(mask, w, 0.0)
    return w, c + jnp.sum(lk, axis=1, keepdims=True)


def _head_lane_masks(n_lanes, dtype):
    lane = lax.broadcasted_iota(jnp.int32, (1, n_lanes), 1)
    return [jnp.where((lane >= h * ATT_HEAD_DIM) & (lane < (h + 1) * ATT_HEAD_DIM), 1.0, 0.0).astype(dtype)
            for h in range(n_lanes // ATT_HEAD_DIM)]


def _attn_prompt_kernel(qs_ref, kt_ref, vc_ref, u2x_ref, o_ref, s_buf, hl_buf, mask_buf, out_buf, *, tq, nq):
    tk = tq // 2
    big = 1e30
    causal = lax.broadcasted_iota(jnp.int32, (tq, tq), 1) < lax.broadcasted_iota(jnp.int32, (tq, tq), 0)
    mask_buf[0] = jnp.full((tq, tq), -jnp.inf, F32)
    mask_buf[1] = jnp.where(causal, -jnp.inf, big)

    def advance(i, j, skip_rest):
        last = (j == 0) | skip_rest
        return jnp.where(last, i + 1, i), jnp.where(last, i + 1, j - 1)

    def exhausted_after(c_prev, row_sums, first):
        carry = jnp.where(first, 0.0, c_prev[:, 0:1]) + row_sums
        return jnp.max(carry) < SB_EXHAUSTED

    def row_off(idx):
        return pl.multiple_of(jnp.minimum(idx, nq - 1) * tq, tq)

    def stage_a(i, j, slot):
        kb = kt_ref[0, :, pl.ds(row_off(j), tq)]
        t = _dot(qs_ref[0, 0, jnp.minimum(i, nq - 1)], kb)
        floor = mask_buf[(i == j).astype(jnp.int32)]
        row_sums = []
        for h in range(HEADS_PER_LANE_TILE):
            th = jnp.maximum(t[h * tq:(h + 1) * tq], floor)
            s_buf[slot, h * tq:(h + 1) * tq, :] = th
            e = jnp.exp2(jnp.abs(th) * (-LOG2E))
            lk = jnp.minimum(th, 0.0) - jnp.log(1.0 + e)
            row_sums.append(jnp.sum(lk, axis=1, keepdims=True))
            hi, lo = _split2(lk)
            for half in range(2):
                r0 = half * 2 * tq + h * tq
                hl_buf[slot, r0:r0 + tq, 0:tk] = hi[:, half * tk:(half + 1) * tk]
                hl_buf[slot, r0:r0 + tq, tk:tq] = lo[:, half * tk:(half + 1) * tk]
        return jnp.concatenate(row_sums, axis=0)

    def stage_b(i, j, slot, c, acc):
        rr = _dot(hl_buf[slot], u2x_ref[...])
        first = i == j
        c = jnp.where(first, 0.0, c)
        acc = jnp.where(first, 0.0, acc)
        ws = []
        for half in (1, 0):
            r0 = half * 2 * tq
            r = rr[r0:r0 + 2 * tq, 0:tk] + c
            th = s_buf[slot, :, half * tk:(half + 1) * tk]
            w = jnp.exp(r - th).astype(BF16)
            c = c + rr[r0:r0 + 2 * tq, tk:tq]
            ws += [w[0:tq], w[tq:2 * tq]]
        acc = acc + _dot(jnp.concatenate(ws, axis=1), vc_ref[0, 0, jnp.minimum(j, nq - 1)])
        out_buf[jnp.minimum(i, nq)] = acc
        return c, acc

    def two_items(carry):
        ib, jb, ia, ja, c, acc = carry
        rs = stage_a(ia, ja, 1)
        c, acc = stage_b(ib, jb, 0, c, acc)
        i2, j2 = advance(ia, ja, exhausted_after(c, rs, ia == ja))
        rs = stage_a(i2, j2, 0)
        c, acc = stage_b(ia, ja, 1, c, acc)
        i3, j3 = advance(i2, j2, exhausted_after(c, rs, i2 == j2))
        return i2, j2, i3, j3, c, acc

    def body(carry):
        return two_items(two_items(carry))

    z = jnp.int32(0)
    stage_a(z, z, 0)
    init = (z, z, z + 1, z + 1, jnp.zeros((2 * tq, tk), F32), jnp.zeros((tq, LANES), F32))
    lax.while_loop(lambda carry: carry[0] < nq, body, init)
    for i in range(nq):
        o_ref[0, i * tq:(i + 1) * tq, :] = out_buf[i]


def _u2_matrix(tk):
    j = jnp.arange(2 * tk)[:, None] % tk
    s = jnp.arange(tk)[None, :]
    return (j >= s).astype(BF16)


def _attn_prompt(qs, kt, vc):
    b, n_pairs, nq, _, _ = qs.shape
    tq = ATT_TQ
    l = nq * tq
    assert tq == 2 * ATT_TK and kt.shape == (b, D_ATT, l)
    u2x = jnp.concatenate([_u2_matrix(ATT_TK), jnp.ones((tq, ATT_TK), BF16)], axis=1)
    kernel = functools.partial(_attn_prompt_kernel, tq=tq, nq=nq)
    tiles = pl.BlockSpec((1, 1, nq, 2 * tq, LANES), lambda bi, hp: (bi, hp, 0, 0, 0))
    return pl.pallas_call(
        kernel,
        grid=(b, n_pairs),
        in_specs=[tiles, pl.BlockSpec((1, LANES, l), lambda bi, hp: (bi, hp, 0)), tiles,
                  pl.BlockSpec(u2x.shape, lambda bi, hp: (0, 0))],
        out_specs=pl.BlockSpec((1, l, LANES), lambda bi, hp: (bi, 0, hp)),
        out_shape=jax.ShapeDtypeStruct((b, l, D_ATT), F32),
        scratch_shapes=[pltpu.VMEM((2, 2 * tq, tq), F32), pltpu.VMEM((2, 4 * tq, tq), BF16),
                        pltpu.VMEM((2, tq, tq), F32), pltpu.VMEM((nq + 1, tq, LANES), F32)],
        compiler_params=pltpu.CompilerParams(
            dimension_semantics=("arbitrary", "arbitrary"), vmem_limit_bytes=VMEM_LIMIT),
        name="attn_prompt",
    )(qs, kt, vc, u2x)


def _lanes_to_heads(x):
    t = x.shape[0]
    parts = []
    for g in range(D_ATT // LANES):
        a = x[:, g * LANES:(g + 1) * LANES]
        b = pltpu.roll(a, ATT_HEAD_DIM, axis=1)
        parts += [a.reshape(t // SUBLANES, SUBLANES, LANES), b.reshape(t // SUBLANES, SUBLANES, LANES)]
    y = jnp.swapaxes(jnp.stack(parts, axis=1), 1, 2).reshape(t, N_ATT_HEADS, LANES)
    return y[:, :, 0:ATT_HEAD_DIM]


def _attn_sample_kernel(q_ref, kn_ref, vn_ref, ck_hbm, cv_hbm, u2_ref, o_ref, kbuf, vbuf, sem,
                        *, l, tk, chunk, nch, layer):
    bi = pl.program_id(0)
    m = N_ATT_HEADS * l
    hm = _head_lane_masks(D_ATT, BF16)
    q = q_ref[0]
    qs = jnp.concatenate([q * mk for mk in hm], axis=0)
    u2 = u2_ref[...]

    def copies(ch, slot):
        keys = pl.ds(pl.multiple_of((nch - 1 - ch) * chunk, chunk), chunk)
        return (pltpu.make_async_copy(ck_hbm.at[layer, bi, :, :, keys], kbuf.at[slot], sem.at[0, slot]),
                pltpu.make_async_copy(cv_hbm.at[layer, bi, :, :, keys], vbuf.at[slot], sem.at[1, slot]))

    def start(ch, slot):
        for cp in copies(ch, slot):
            cp.start()

    def wait(ch, slot):
        for cp in copies(ch, slot):
            cp.wait()

    start(0, 0)
    pad = jnp.zeros((tk - l, D_ATT), BF16)
    kn = jnp.concatenate([kn_ref[0], pad], axis=0)
    vn = jnp.concatenate([vn_ref[0], pad], axis=0)
    row = jnp.concatenate([lax.broadcasted_iota(jnp.int32, (l, tk), 0)] * N_ATT_HEADS, axis=0)
    col = lax.broadcasted_iota(jnp.int32, (m, tk), 1)
    w, c = _sb_weights(_dot_nt(qs, kn), jnp.zeros((m, 1), F32), u2, col < row)
    acc = _dot(w.astype(BF16), vn)

    def live(c):
        return jnp.max(c) >= SB_EXHAUSTED

    def body(carry):
        ch, c, acc = carry
        slot = ch % 2
        wait(ch, slot)

        @pl.when(ch + 1 < nch)
        def _():
            start(ch + 1, 1 - slot)

        kt = kbuf[slot].reshape(D_ATT, chunk).astype(BF16)
        vt = vbuf[slot].reshape(D_ATT, chunk).astype(BF16)
        for t in reversed(range(chunk // tk)):
            w, c = _sb_weights(_dot(qs, kt[:, t * tk:(t + 1) * tk]), c, u2, None)
            acc = acc + _dot_nt(w.astype(BF16), vt[:, t * tk:(t + 1) * tk])
        return ch + 1, c, acc

    ch, c, acc = lax.while_loop(lambda carry: (carry[0] < nch) & live(carry[1]), body, (jnp.int32(0), c, acc))

    @pl.when(ch < nch)
    def _():
        wait(ch, ch % 2)

    hmf = _head_lane_masks(D_ATT, F32)
    out = acc[0:l] * hmf[0]
    for h in range(1, N_ATT_HEADS):
        out = out + acc[h * l:(h + 1) * l] * hmf[h]
    o_ref[0] = out


def _attn_sample(qb, kb_new, vb_new, cache_k, cache_v, layer):
    b, l, _ = qb.shape
    past = cache_k.shape[2]
    cache_k = jnp.transpose(cache_k, (0, 1, 3, 4, 2))
    cache_v = jnp.transpose(cache_v, (0, 1, 3, 4, 2))
    tk = ATT_TK
    chunk = min(ATT_CACHE_CHUNK, past)
    assert past % chunk == 0 and chunk % tk == 0 and l <= tk and l % 16 == 0
    u2 = _u2_matrix(tk)
    kernel = functools.partial(_attn_sample_kernel, l=l, tk=tk, chunk=chunk, nch=past // chunk, layer=layer)
    new = pl.BlockSpec((1, l, D_ATT), lambda bi: (bi, 0, 0))
    hbm = pl.BlockSpec(memory_space=pl.ANY)
    return pl.pallas_call(
        kernel,
        grid=(b,),
        in_specs=[new, new, new, hbm, hbm, pl.BlockSpec(u2.shape, lambda bi: (0, 0))],
        out_specs=new,
        out_shape=jax.ShapeDtypeStruct((b, l, D_ATT), F32),
        scratch_shapes=[pltpu.VMEM((2, N_ATT_HEADS, ATT_HEAD_DIM, chunk), F32),
                        pltpu.VMEM((2, N_ATT_HEADS, ATT_HEAD_DIM, chunk), F32),
                        pltpu.SemaphoreType.DMA((2, 2))],
        compiler_params=pltpu.CompilerParams(
            dimension_semantics=("arbitrary",), vmem_limit_bytes=VMEM_LIMIT),
        name="attn_sample",
    )(qb, kb_new, vb_new, cache_k, cache_v, u2)


def _ssd_kernel(xbc_ref, z_ref, dt_ref, conv0_ref, s0_ref, cw_ref, cb_ref, alog_ref, dskip_ref, g_ref,
                tri_ref, y_ref, sfin_ref, cbuf, state, *, q):
    c = pl.program_id(1)

    @pl.when(c == 0)
    def _():
        cbuf[...] = conv0_ref[0]
        zero = jnp.zeros((SSM_HEAD_DIM, SSM_STATE), F32)
        for pr in range(N_PAIRS):
            top = jnp.concatenate([s0_ref[0, 2 * pr], zero], axis=1)
            bot = jnp.concatenate([zero, s0_ref[0, 2 * pr + 1]], axis=1)
            state[pr] = jnp.concatenate([top, bot], axis=0)

    dt = dt_ref[0]
    da = dt * (-jnp.exp(alog_ref[...]))
    tri = tri_ref[...]
    hi, mid, lo = _split3(da)
    acum = _dot(tri, hi) + _dot(tri, mid) + _dot(tri, lo)
    acum_t = acum.T
    dt_t = dt.T
    a_end = acum[q - 1:q, :]
    trow = lax.broadcasted_iota(jnp.int32, (q, q), 0)
    tcol = lax.broadcasted_iota(jnp.int32, (q, q), 1)
    causal = tcol <= trow
    bd_r = lax.broadcasted_iota(jnp.int32, (LANES, LANES), 0) < SSM_STATE
    bd_c = lax.broadcasted_iota(jnp.int32, (LANES, LANES), 1) < SSM_STATE
    block_diag = bd_r == bd_c

    x = xbc_ref[0]
    prev = cbuf[...]
    row8 = lax.broadcasted_iota(jnp.int32, (SUBLANES, CONV_DIM), 0)
    xc = cb_ref[...] + x * cw_ref[SSM_CONV - 1:SSM_CONV, :]
    for k in range(1, SSM_CONV):
        r = pltpu.roll(x, k, axis=0)
        head = r[0:SUBLANES]
        for j in range(k):
            head = jnp.where(row8 == j, prev[SUBLANES - k + j:SUBLANES - k + j + 1, :], head)
        r = jnp.concatenate([head, r[SUBLANES:]], axis=0)
        xc = xc + r * cw_ref[SSM_CONV - 1 - k:SSM_CONV - k, :]
    xc = _silu(xc)
    cbuf[...] = x[q - SUBLANES:q]

    lane = lax.broadcasted_iota(jnp.int32, (1, LANES), 1)
    lo_half = lane < SSM_STATE
    bmat = xc[:, D_SSM:D_SSM + LANES]
    cmat = xc[:, D_SSM + LANES:D_SSM + 2 * LANES]
    b_sw = pltpu.roll(bmat, SSM_STATE, axis=1)
    c_sw = pltpu.roll(cmat, SSM_STATE, axis=1)
    bdup = [jnp.where(lo_half, bmat, b_sw), jnp.where(lo_half, b_sw, bmat)]
    cdup = [jnp.where(lo_half, cmat, c_sw), jnp.where(lo_half, c_sw, cmat)]
    gmask = [jnp.where(lo_half, 1.0, 0.0), jnp.where(lo_half, 0.0, 1.0)]
    bmat_b = bmat.astype(BF16)
    cb_g = [_dot_nt((cmat * gmask[g]).astype(BF16), bmat_b) for g in range(SSM_GROUPS)]

    hm_b = [jnp.where(lo_half, 1.0, 0.0).astype(BF16), jnp.where(lo_half, 0.0, 1.0).astype(BF16)]

    pairs = range(N_PAIRS)
    group = [pr // (N_PAIRS // SSM_GROUPS) for pr in pairs]
    x_pair = [xc[:, pr * LANES:(pr + 1) * LANES] for pr in pairs]
    x_b = [v.astype(BF16) for v in x_pair]
    s_pair = [state[pr] for pr in pairs]
    lhs_diag, lhs_off, upd = [], [], []
    for pr in pairs:
        h0 = pr * HEADS_PER_LANE_TILE
        ms = []
        for h in (h0, h0 + 1):
            seg = acum[:, h:h + 1] - acum_t[h:h + 1, :]
            lmat = jnp.where(causal, jnp.exp(seg), 0.0)
            ms.append((cb_g[group[pr]] * lmat * dt_t[h:h + 1, :]).astype(BF16))
        acol = jnp.where(lo_half, acum[:, h0:h0 + 1], acum[:, h0 + 1:h0 + 2])
        dcol = jnp.where(lo_half, dt[:, h0:h0 + 1], dt[:, h0 + 1:h0 + 2])
        aend = jnp.where(lo_half, a_end[:, h0:h0 + 1], a_end[:, h0 + 1:h0 + 2])
        lhs_diag.append(jnp.concatenate(ms, axis=1))
        lhs_off.append((cdup[group[pr]] * jnp.exp(acol)).astype(BF16))
        upd.append((aend, (bdup[group[pr]] * (jnp.exp(aend - acol) * dcol)).astype(BF16)))
    y_diag = [_dot(lhs_diag[pr], jnp.concatenate([x_b[pr] * hm_b[0], x_b[pr] * hm_b[1]], axis=0)) for pr in pairs]
    y_off = [_dot_nt(lhs_off[pr], s_pair[pr].astype(BF16)) for pr in pairs]
    ys = []
    ssq = jnp.zeros((q, 1), F32)
    for pr in pairs:
        y = y_diag[pr] + y_off[pr] + dskip_ref[:, pr * LANES:(pr + 1) * LANES] * x_pair[pr]
        yz = y * _silu(z_ref[0, :, pr * LANES:(pr + 1) * LANES])
        ssq = ssq + jnp.sum(yz * yz, axis=1, keepdims=True)
        ys.append(yz)

    inv = lax.rsqrt(ssq * (1.0 / D_SSM) + NORM_EPS)
    for pr in pairs:
        aend, bw = upd[pr]
        state[pr] = s_pair[pr] * jnp.exp(aend) + jnp.where(block_diag, _dot_tn(x_b[pr], bw), 0.0)
    for pr in range(N_PAIRS):
        y_ref[0, :, pr * LANES:(pr + 1) * LANES] = (
            ys[pr] * inv * g_ref[:, pr * LANES:(pr + 1) * LANES]).astype(BF16)

    @pl.when(c == pl.num_programs(1) - 1)
    def _():
        for pr in range(N_PAIRS):
            sfin_ref[0, 2 * pr] = state[pr, 0:SSM_HEAD_DIM, 0:SSM_STATE]
            sfin_ref[0, 2 * pr + 1] = state[pr, SSM_HEAD_DIM:LANES, SSM_STATE:LANES]


def _ssd(xbc, z, dt, conv0, s0, conv_w, conv_b, a_log_pad, dskip_lanes, g_ssm):
    b, l, _ = xbc.shape
    q = min(SSD_Q, l)
    assert l % q == 0 and q % SUBLANES == 0
    tri = (jnp.arange(q)[None, :] <= jnp.arange(q)[:, None]).astype(BF16)
    seq = lambda n: pl.BlockSpec((1, q, n), lambda bi, ci: (bi, ci, 0))
    per_b = lambda a: pl.BlockSpec((1,) + a.shape[1:], lambda bi, ci: (bi,) + (0,) * (a.ndim - 1))
    full = lambda a: pl.BlockSpec(a.shape, lambda bi, ci: (0,) * a.ndim)
    kernel = functools.partial(_ssd_kernel, q=q)
    return pl.pallas_call(
        kernel,
        grid=(b, l // q),
        in_specs=[seq(CONV_DIM), seq(D_SSM), seq(LANES), per_b(conv0), per_b(s0),
                  full(conv_w), full(conv_b), full(a_log_pad), full(dskip_lanes), full(g_ssm), full(tri)],
        out_specs=(seq(D_SSM), per_b(s0)),
        out_shape=(jax.ShapeDtypeStruct((b, l, D_SSM), BF16), jax.ShapeDtypeStruct(s0.shape, F32)),
        scratch_shapes=[pltpu.VMEM((SUBLANES, CONV_DIM), F32),
                        pltpu.VMEM((N_PAIRS, LANES, LANES), F32)],
        compiler_params=pltpu.CompilerParams(
            dimension_semantics=("arbitrary", "arbitrary"), vmem_limit_bytes=VMEM_LIMIT),
        name="ssd",
    )(xbc, z, dt, conv0, s0, conv_w, conv_b, a_log_pad, dskip_lanes, g_ssm, tri)


def _gelu_tanh(x):
    return 0.5 * x * (1.0 + jnp.tanh(math.sqrt(2.0 / math.pi) * (x + 0.044715 * (x * x * x))))


def _out_ffn_kernel(x_ref, attn_ref, ys_ref, fc0_ref, ga_ref, gpost_ref, gpre_ref, gfpost_ref,
                    woa_ref, wos_ref, wgu_ref, wd_ref, cw_ref,
                    y_ref, fcn_ref, h2_buf, acc, carry, g_buf, u_buf, *, n_seq, lt):
    t = pl.program_id(1)

    @pl.when(t == 0)
    def _():
        carry[...] = fc0_ref[0]

    tm = x_ref.shape[0]
    pieces = [slice(r0, r0 + FFN_ROWS) for r0 in range(0, tm, FFN_ROWS)]
    an = [_rmsnorm(attn_ref[rows, :], ga_ref[...]).astype(BF16) for rows in pieces]
    m = [_dot(an[k], woa_ref[...]) + _dot(ys_ref[rows, :], wos_ref[...]) for k, rows in enumerate(pieces)]
    x1 = [x_ref[rows, :] + _rmsnorm(m[k], gpost_ref[...]) for k, rows in enumerate(pieces)]
    for k, rows in enumerate(pieces):
        y_ref[rows, :] = x1[k]
        h2_buf[rows, :] = _rmsnorm(x1[k], gpre_ref[...]).astype(BF16)
    row = lax.broadcasted_iota(jnp.int32, (lt, FFN_F), 0)

    def cols(ci, base=0):
        start = base + ci * FFN_F
        return pl.ds(start if isinstance(ci, int) else pl.multiple_of(start, FFN_F), FFN_F)

    def stage1(ci, slot):
        h2 = h2_buf[...]
        g_buf[slot] = _dot(h2, wgu_ref[:, cols(ci)])
        u_buf[slot] = _dot(h2, wgu_ref[:, cols(ci, D_FF)])

    def stage2(ci, slot):
        gate = g_buf[slot]
        cw = cw_ref[:, cols(ci)]
        acts = []
        for s in range(n_seq):
            gs = gate[s * lt:(s + 1) * lt]
            prev = carry[ci, s * SUBLANES:(s + 1) * SUBLANES, :]
            p1 = prev[SUBLANES - 1:SUBLANES, :]
            p2 = prev[SUBLANES - 2:SUBLANES - 1, :]
            g1 = jnp.where(row == 0, p1, pltpu.roll(gs, 1, axis=0))
            g2 = jnp.where(row == 0, p2, jnp.where(row == 1, p1, pltpu.roll(gs, 2, axis=0)))
            gc = cw[3:4, :] + g2 * cw[0:1, :] + g1 * cw[1:2, :] + gs * cw[2:3, :]
            carry[ci, s * SUBLANES:(s + 1) * SUBLANES, :] = gs[lt - SUBLANES:lt]
            acts.append(_gelu_tanh(gc))
        act = acts[0] if n_seq == 1 else jnp.concatenate(acts, axis=0)
        acc[...] += _dot((act * u_buf[slot]).astype(BF16), wd_ref[cols(ci), :])

    acc[...] = jnp.zeros_like(acc)
    stage1(0, 0)

    def body(p, _):
        c = 2 * p + 1
        stage1(c, 1)
        stage2(c - 1, 0)
        stage1(c + 1, 0)
        stage2(c, 1)
        return 0

    assert FFN_NC % 2 == 1
    lax.fori_loop(0, FFN_NC // 2, body, 0)
    stage2(FFN_NC - 1, 0)
    for r0 in range(0, tm, FFN_ROWS):
        rows = slice(r0, r0 + FFN_ROWS)
        y_ref[rows, :] = y_ref[rows, :] + _rmsnorm(acc[rows, :], gfpost_ref[...])
    fcn_ref[0] = carry[...]


def _out_ffn(x2d, attn2d, ys2d, fc0, n_seq, lt, gains, weights):
    t = x2d.shape[0]
    tm = n_seq * lt
    n_groups = fc0.shape[0]
    tiles = t // (tm * n_groups)
    assert tiles * tm * n_groups == t
    row = lambda n: pl.BlockSpec((tm, n), lambda gi, ti: (gi * tiles + ti, 0))
    full = lambda a: pl.BlockSpec(a.shape, lambda gi, ti: (0,) * a.ndim, pipeline_mode=pl.Buffered(1))
    fc_spec = pl.BlockSpec((1,) + fc0.shape[1:], lambda gi, ti: (gi, 0, 0, 0))
    kernel = functools.partial(_out_ffn_kernel, n_seq=n_seq, lt=lt)
    return pl.pallas_call(
        kernel,
        grid=(n_groups, tiles),
        in_specs=[row(D_MODEL), row(D_ATT), row(D_SSM), fc_spec] + [full(a) for a in gains]
                 + [full(a) for a in weights],
        out_specs=(row(D_MODEL), fc_spec),
        out_shape=(jax.ShapeDtypeStruct((t, D_MODEL), F32), jax.ShapeDtypeStruct(fc0.shape, F32)),
        scratch_shapes=[pltpu.VMEM((tm, D_MODEL), BF16), pltpu.VMEM((tm, D_MODEL), F32),
                        pltpu.VMEM(fc0.shape[1:], F32), pltpu.VMEM((2, tm, FFN_F), F32),
                        pltpu.VMEM((2, tm, FFN_F), F32)],
        compiler_params=pltpu.CompilerParams(
            dimension_semantics=("arbitrary", "arbitrary"), vmem_limit_bytes=VMEM_LIMIT),
        name="out_ffn",
    )(x2d, attn2d, ys2d, fc0, *gains, *weights)


def _ffn_state_to_chunks(st, n_seq):
    b = st.shape[0]
    s = st.reshape(b // n_seq, n_seq, FFN_CONV - 1, FFN_NC, FFN_F)
    s = jnp.pad(s, ((0, 0), (0, 0), (SUBLANES - (FFN_CONV - 1), 0), (0, 0), (0, 0)))
    return jnp.transpose(s, (0, 3, 1, 2, 4)).reshape(b // n_seq, FFN_NC, n_seq * SUBLANES, FFN_F)


def _chunks_to_ffn_state(ch, n_seq):
    g = ch.shape[0]
    s = ch.reshape(g, FFN_NC, n_seq, SUBLANES, FFN_F)[:, :, :, SUBLANES - (FFN_CONV - 1):, :]
    return jnp.transpose(s, (0, 2, 3, 1, 4)).reshape(g * n_seq, FFN_CONV - 1, D_FF)


def _layer(x, caches, ssm_h0, ssm_conv_prev, ffn_conv_prev, p, is_prompt):
    b, l, _ = x.shape
    t = b * l
    x2d = x.reshape(t, D_MODEL)
    r3 = lambda a: a.reshape(b, l, a.shape[-1])
    proj_w = (p['g_mix_pre'], p['w_in_t'], p['w_dt_t'], p['dt_bias'])
    if is_prompt:
        qb, kb, vb, kt, vt, z, xbc, dt = _in_proj(x, *proj_w, kv_transposed=True)
        attn = _attn_prompt(qb, kb, vb)
        k, v = jnp.transpose(kt, (0, 3, 1, 2)), jnp.transpose(vt, (0, 3, 1, 2))
    else:
        qb, kb, vb, k, v, z, xbc, dt = _in_proj(x2d[None], *proj_w, kv_transposed=False)
        attn = _attn_sample(r3(qb), r3(kb), r3(vb), *caches)

    conv0 = jnp.pad(ssm_conv_prev, ((0, 0), (SUBLANES - (SSM_CONV - 1), 0), (0, 0)))
    ys, s_fin = _ssd(r3(xbc), r3(z), r3(dt), conv0, ssm_h0,
                     p['ssm_conv_w'], p['ssm_conv_b'], p['a_log'], p['d_skip'], p['g_ssm_out'])
    ssm_conv_new = r3(xbc)[:, l - (SSM_CONV - 1):, :]

    if is_prompt:
        n_seq, lt = 1, min(FFN_TM, l)
    else:
        n_seq, lt = b, l
    fc0 = _ffn_state_to_chunks(ffn_conv_prev, n_seq)
    gains = (p['g_attn_out'], p['g_mix_post'], p['g_ffn_pre'], p['g_ffn_post'])
    weights = (p['w_out_a'], p['w_out_s'], p['w_gu'], p['w_down'], p['ffn_cw'])
    y2d, fcn = _out_ffn(x2d, attn.reshape(t, D_ATT), ys.reshape(t, D_SSM), fc0, n_seq, lt, gains, weights)
    return (y2d.reshape(b, l, D_MODEL), k.reshape(b, l, N_ATT_HEADS, ATT_HEAD_DIM),
            v.reshape(b, l, N_ATT_HEADS, ATT_HEAD_DIM), s_fin, ssm_conv_new,
            _chunks_to_ffn_state(fcn, n_seq))


def _prep_params(i, g_mix_pre, g_mix_post, w_in, ssm_conv_w, ssm_conv_b, dt_bias, a_log, d_skip,
                 g_ssm_out, g_attn_out, w_out, g_ffn_pre, g_ffn_post, w_up, ffn_conv_w, ffn_conv_b, w_down):
    row = lambda a: a[i].reshape(1, -1).astype(F32)
    pad_lanes = lambda a: jnp.pad(a, ((0, 0), (0, LANES - a.shape[1])))
    wi_t = w_in[i].T.astype(BF16)
    ffn_cw = jnp.concatenate([ffn_conv_w[i], ffn_conv_b[i][None, :],
                              jnp.zeros((SUBLANES - FFN_CONV - 1, D_FF), F32)], axis=0)
    return {
        'g_mix_pre': row(g_mix_pre), 'g_mix_post': row(g_mix_post),
        'g_ffn_pre': row(g_ffn_pre), 'g_ffn_post': row(g_ffn_post),
        'g_attn_out': row(g_attn_out), 'g_ssm_out': row(g_ssm_out),
        'w_in_t': wi_t,
        'w_dt_t': jnp.pad(wi_t[D_MAIN_PROJ:], ((0, LANES - (wi_t.shape[0] - D_MAIN_PROJ)), (0, 0))),
        'dt_bias': pad_lanes(row(dt_bias)),
        'ssm_conv_w': ssm_conv_w[i].astype(F32), 'ssm_conv_b': row(ssm_conv_b),
        'a_log': pad_lanes(row(a_log)),
        'd_skip': jnp.repeat(d_skip[i].astype(F32), SSM_HEAD_DIM).reshape(1, D_SSM),
        'w_out_a': w_out[i][:D_ATT].astype(BF16), 'w_out_s': w_out[i][D_ATT:].astype(BF16),
        'w_gu': w_up[i].astype(BF16),
        'w_down': w_down[i].astype(BF16),
        'ffn_cw': ffn_cw,
    }


def kernel(x_prompt, x_sample, cache_k, cache_v, state_ssm, state_ssm_conv, state_ffn_conv, g_mix_pre, g_mix_post, w_in, ssm_conv_w, ssm_conv_b, dt_bias, a_log, d_skip, g_ssm_out, g_attn_out, w_out, g_ffn_pre, g_ffn_post, w_up, ffn_conv_w, ffn_conv_b, w_down):
    depth = w_in.shape[0]
    bp = x_prompt.shape[0]
    dtp = x_prompt.dtype
    zh = jnp.zeros((bp, N_SSM_HEADS, SSM_HEAD_DIM, SSM_STATE), dtp)
    zcs = jnp.zeros((bp, SSM_CONV - 1, CONV_DIM), dtp)
    zcf = jnp.zeros((bp, FFN_CONV - 1, D_FF), dtp)
    y_p, y_s = x_prompt, x_sample
    outs_p, outs_s = [], []
    for i in range(depth):
        p = _prep_params(i, g_mix_pre, g_mix_post, w_in, ssm_conv_w, ssm_conv_b, dt_bias, a_log, d_skip,
                         g_ssm_out, g_attn_out, w_out, g_ffn_pre, g_ffn_post, w_up, ffn_conv_w, ffn_conv_b,
                         w_down)
        rp = _layer(y_p, None, zh, zcs, zcf, p, True)
        rs = _layer(y_s, (cache_k, cache_v, i), state_ssm[i], state_ssm_conv[i], state_ffn_conv[i], p, False)
        y_p, y_s = rp[0], rs[0]
        outs_p.append(rp[1:])
        outs_s.append(rs[1:])
    stack = lambda outs, j: jnp.stack([o[j] for o in outs])
    return (y_p, y_s) + tuple(stack(outs_p, j) for j in range(5)) + tuple(stack(outs_s, j) for j in range(5))
```

```python
import functools
import math

import jax
import jax.numpy as jnp
from jax import lax
from jax.experimental import pallas as pl
from jax.experimental.pallas import tpu as pltpu

F32 = jnp.float32
BF16 = jnp.bfloat16

D_MODEL = 1024
D_ATT = 512
N_ATT_HEADS = 8
ATT_HEAD_DIM = 64
D_SSM = 512
N_SSM_HEADS = 8
SSM_HEAD_DIM = 64
SSM_STATE = 64
SSM_GROUPS = 2
SSM_CONV = 4
CONV_DIM = D_SSM + 2 * SSM_GROUPS * SSM_STATE
D_FF = 2816
FFN_CONV = 3
NORM_EPS = 1e-6
D_MAIN_PROJ = 3 * D_ATT + D_SSM + CONV_DIM
ATT_SCALE = ATT_HEAD_DIM ** -0.5
LOG2E = math.log2(math.e)
SB_EXHAUSTED = -120.0

LANES = 128
SUBLANES = 8
HEADS_PER_LANE_TILE = LANES // ATT_HEAD_DIM
N_PAIRS = N_SSM_HEADS // HEADS_PER_LANE_TILE

PROJ_TM = 512
ATT_TQ = 256
ATT_TK = 128
ATT_CACHE_CHUNK = 256
SSD_Q = 128
FFN_TM = 512
FFN_F = 256
FFN_ROWS = 128
FFN_NC = D_FF // FFN_F
VMEM_LIMIT = 56 * 1024 * 1024


def _rmsnorm(x, g):
    y = x * lax.rsqrt(jnp.mean(x * x, axis=-1, keepdims=True) + NORM_EPS)
    return y * g


def _softplus(x):
    return jnp.maximum(x, 0.0) + jnp.log1p(jnp.exp(-jnp.abs(x)))


def _silu(x):
    return x * (1.0 / (1.0 + jnp.exp(-x)))


def _dot(a, b):
    return jnp.dot(a, b, preferred_element_type=F32)


def _dot_nt(a, b):
    return lax.dot_general(a, b, (((1,), (1,)), ((), ())), preferred_element_type=F32)


def _dot_tn(a, b):
    return lax.dot_general(a, b, (((0,), (0,)), ((), ())), preferred_element_type=F32)


def _split2(x):
    hi = x.astype(BF16)
    lo = (x - hi.astype(F32)).astype(BF16)
    return hi, lo


def _split3(x):
    hi = x.astype(BF16)
    r1 = x - hi.astype(F32)
    mid = r1.astype(BF16)
    lo = (r1 - mid.astype(F32)).astype(BF16)
    return hi, mid, lo


def _store_attention_tiles(qs_ref, vc_ref, qn, vb):
    tq, tk = ATT_TQ, ATT_TQ // 2
    hm = _head_lane_masks(LANES, BF16)
    for p in range(D_ATT // LANES):
        for t in range(qn.shape[0] // tq):
            qt = qn[t * tq:(t + 1) * tq, p * LANES:(p + 1) * LANES]
            vt = vb[t * tq:(t + 1) * tq, p * LANES:(p + 1) * LANES]
            for h in range(HEADS_PER_LANE_TILE):
                qs_ref[0, p, t, h * tq:(h + 1) * tq, :] = qt * hm[h]
                for n, half in enumerate((1, 0)):
                    r0 = (n * HEADS_PER_LANE_TILE + h) * tk
                    vc_ref[0, p, t, r0:r0 + tk, :] = vt[half * tk:(half + 1) * tk] * hm[h]


def _in_proj_kernel(x_ref, g_ref, wt_ref, wdt_ref, dtb_ref,
                    qb_ref, kb_ref, vb_ref, k_ref, v_ref, z_ref, xbc_ref, dt_ref, *, kv_transposed):
    h = _rmsnorm(x_ref[0], g_ref[...]).astype(BF16)

    def proj(lo, hi):
        return _dot_nt(h, wt_ref[lo:hi, :])

    q = proj(0, D_ATT)
    v = proj(2 * D_ATT, 3 * D_ATT)
    if kv_transposed:
        tm = v.shape[0]
        _store_attention_tiles(qb_ref, vb_ref, (q * -ATT_SCALE).astype(BF16), v.astype(BF16))
        kt = _dot_nt(wt_ref[D_ATT:2 * D_ATT, :], h)
        kb_ref[0] = kt.astype(BF16)
        k_ref[0] = kt.reshape(N_ATT_HEADS, ATT_HEAD_DIM, tm)
        v_ref[0] = v.T.reshape(N_ATT_HEADS, ATT_HEAD_DIM, tm)
    else:
        qb_ref[0] = (q * ATT_SCALE).astype(BF16)
        vb_ref[0] = v.astype(BF16)
        k = proj(D_ATT, 2 * D_ATT)
        kb_ref[0] = k.astype(BF16)
        k_ref[0] = _lanes_to_heads(k)
        v_ref[0] = _lanes_to_heads(v)
    z_ref[0] = proj(3 * D_ATT, 3 * D_ATT + D_SSM)
    xbc_ref[0] = proj(3 * D_ATT + D_SSM, D_MAIN_PROJ)
    dt_ref[0] = _softplus(_dot_nt(h, wdt_ref[...]) + dtb_ref[...])


def _in_proj(x, g, w_t, w_dt, dt_bias, kv_transposed):
    b, l, _ = x.shape
    tm = min(PROJ_TM, l)
    assert l % tm == 0
    row = lambda n: pl.BlockSpec((1, tm, n), lambda bi, i: (bi, i, 0))
    full = lambda a: pl.BlockSpec(a.shape, lambda bi, i: (0,) * a.ndim)
    sds = jax.ShapeDtypeStruct
    if kv_transposed:
        kb_shape, kb_spec = sds((b, D_ATT, l), BF16), pl.BlockSpec((1, D_ATT, tm), lambda bi, i: (bi, 0, i))
        kv_shape = sds((b, N_ATT_HEADS, ATT_HEAD_DIM, l), F32)
        kv_spec = pl.BlockSpec((1, N_ATT_HEADS, ATT_HEAD_DIM, tm), lambda bi, i: (bi, 0, 0, i))
        assert tm % ATT_TQ == 0
        n_pairs, tiles = D_ATT // LANES, tm // ATT_TQ
        qv_shape = sds((b, n_pairs, l // ATT_TQ, 2 * ATT_TQ, LANES), BF16)
        qv_spec = pl.BlockSpec((1, n_pairs, tiles, 2 * ATT_TQ, LANES), lambda bi, i: (bi, 0, i, 0, 0))
    else:
        kb_shape, kb_spec = sds((b, l, D_ATT), BF16), row(D_ATT)
        kv_shape = sds((b, l, N_ATT_HEADS, ATT_HEAD_DIM), F32)
        kv_spec = pl.BlockSpec((1, tm, N_ATT_HEADS, ATT_HEAD_DIM), lambda bi, i: (bi, i, 0, 0))
        qv_shape, qv_spec = sds((b, l, D_ATT), BF16), row(D_ATT)
    out_shape = (qv_shape, kb_shape, qv_shape, kv_shape, kv_shape,
                 sds((b, l, D_SSM), F32), sds((b, l, CONV_DIM), F32), sds((b, l, LANES), F32))
    return pl.pallas_call(
        functools.partial(_in_proj_kernel, kv_transposed=kv_transposed),
        grid=(b, l // tm),
        in_specs=[row(D_MODEL), full(g), full(w_t), full(w_dt), full(dt_bias)],
        out_specs=(qv_spec, kb_spec, qv_spec, kv_spec, kv_spec,
                   row(D_SSM), row(CONV_DIM), row(LANES)),
        out_shape=out_shape,
        compiler_params=pltpu.CompilerParams(
            dimension_semantics=("arbitrary", "arbitrary"), vmem_limit_bytes=VMEM_LIMIT),
        name="in_proj",
    )(x, g, w_t, w_dt, dt_bias)


def _sb_weights(s, c, u2, mask):
    lk = -(jnp.maximum(s, 0.0) + jnp.log(1.0 + jnp.exp(-jnp.abs(s))))
    if mask is not None:
        lk = jnp.where(mask, lk, 0.0)
    hi, lo = _split2(lk)
    r = _dot(jnp.concatenate([hi, lo], axis=1), u2) + c
    w = jnp.exp(s + r)
    if mask is not None:
        w = jnp.where(mask, w, 0.0)
    return w, c + jnp.sum(lk, axis=1, keepdims=True)


def _head_lane_masks(n_lanes, dtype):
    lane = lax.broadcasted_iota(jnp.int32, (1, n_lanes), 1)
    return [jnp.where((lane >= h * ATT_HEAD_DIM) & (lane < (h + 1) * ATT_HEAD_DIM), 1.0, 0.0).astype(dtype)
            for h in range(n_lanes // ATT_HEAD_DIM)]


def _attn_prompt_kernel(qs_ref, kt_ref, vc_ref, u2x_ref, o_ref, s_buf, hl_buf, mask_buf, out_buf, *, tq, nq):
    tk = tq // 2
    big = 1e30
    causal = lax.broadcasted_iota(jnp.int32, (tq, tq), 1) < lax.broadcasted_iota(jnp.int32, (tq, tq), 0)
    mask_buf[0] = jnp.full((tq, tq), -jnp.inf, F32)
    mask_buf[1] = jnp.where(causal, -jnp.inf, big)

    def advance(i, j, skip_rest):
        last = (j == 0) | skip_rest
        return jnp.where(last, i + 1, i), jnp.where(last, i + 1, j - 1)

    def exhausted_after(c_prev, row_sums, first):
        carry = jnp.where(first, 0.0, c_prev[:, 0:1]) + row_sums
        return jnp.max(carry) < SB_EXHAUSTED

    def row_off(idx):
        return pl.multiple_of(jnp.minimum(idx, nq - 1) * tq, tq)

    def stage_a(i, j, slot):
        kb = kt_ref[0, :, pl.ds(row_off(j), tq)]
        t = _dot(qs_ref[0, 0, jnp.minimum(i, nq - 1)], kb)
        floor = mask_buf[(i == j).astype(jnp.int32)]
        row_sums = []
        for h in range(HEADS_PER_LANE_TILE):
            th = jnp.maximum(t[h * tq:(h + 1) * tq], floor)
            s_buf[slot, h * tq:(h + 1) * tq, :] = th
            e = jnp.exp2(jnp.abs(th) * (-LOG2E))
            lk = jnp.minimum(th, 0.0) - jnp.log(1.0 + e)
            row_sums.append(jnp.sum(lk, axis=1, keepdims=True))
            hi, lo = _split2(lk)
            for half in range(2):
                r0 = half * 2 * tq + h * tq
                hl_buf[slot, r0:r0 + tq, 0:tk] = hi[:, half * tk:(half + 1) * tk]
                hl_buf[slot, r0:r0 + tq, tk:tq] = lo[:, half * tk:(half + 1) * tk]
        return jnp.concatenate(row_sums, axis=0)

    def stage_b(i, j, slot, c, acc):
        rr = _dot(hl_buf[slot], u2x_ref[...])
        first = i == j
        c = jnp.where(first, 0.0, c)
        acc = jnp.where(first, 0.0, acc)
        ws = []
        for half in (1, 0):
            r0 = half * 2 * tq
            r = rr[r0:r0 + 2 * tq, 0:tk] + c
            th = s_buf[slot, :, half * tk:(half + 1) * tk]
            w = jnp.exp(r - th).astype(BF16)
            c = c + rr[r0:r0 + 2 * tq, tk:tq]
            ws += [w[0:tq], w[tq:2 * tq]]
        acc = acc + _dot(jnp.concatenate(ws, axis=1), vc_ref[0, 0, jnp.minimum(j, nq - 1)])
        out_buf[jnp.minimum(i, nq)] = acc
        return c, acc

    def two_items(carry):
        ib, jb, ia, ja, c, acc = carry
        rs = stage_a(ia, ja, 1)
        c, acc = stage_b(ib, jb, 0, c, acc)
        i2, j2 = advance(ia, ja, exhausted_after(c, rs, ia == ja))
        rs = stage_a(i2, j2, 0)
        c, acc = stage_b(ia, ja, 1, c, acc)
        i3, j3 = advance(i2, j2, exhausted_after(c, rs, i2 == j2))
        return i2, j2, i3, j3, c, acc

    def body(carry):
        return two_items(two_items(two_items(two_items(carry))))

    z = jnp.int32(0)
    stage_a(z, z, 0)
    init = (z, z, z + 1, z + 1, jnp.zeros((2 * tq, tk), F32), jnp.zeros((tq, LANES), F32))
    lax.while_loop(lambda carry: carry[0] < nq, body, init)
    for i in range(nq):
        o_ref[0, i * tq:(i + 1) * tq, :] = out_buf[i]


def _u2_matrix(tk):
    j = jnp.arange(2 * tk)[:, None] % tk
    s = jnp.arange(tk)[None, :]
    return (j >= s).astype(BF16)


def _attn_prompt(qs, kt, vc):
    b, n_pairs, nq, _, _ = qs.shape
    tq = ATT_TQ
    l = nq * tq
    assert tq == 2 * ATT_TK and kt.shape == (b, D_ATT, l)
    u2x = jnp.concatenate([_u2_matrix(ATT_TK), jnp.ones((tq, ATT_TK), BF16)], axis=1)
    kernel = functools.partial(_attn_prompt_kernel, tq=tq, nq=nq)
    tiles = pl.BlockSpec((1, 1, nq, 2 * tq, LANES), lambda bi, hp: (bi, hp, 0, 0, 0))
    return pl.pallas_call(
        kernel,
        grid=(b, n_pairs),
        in_specs=[tiles, pl.BlockSpec((1, LANES, l), lambda bi, hp: (bi, hp, 0)), tiles,
                  pl.BlockSpec(u2x.shape, lambda bi, hp: (0, 0))],
        out_specs=pl.BlockSpec((1, l, LANES), lambda bi, hp: (bi, 0, hp)),
        out_shape=jax.ShapeDtypeStruct((b, l, D_ATT), F32),
        scratch_shapes=[pltpu.VMEM((2, 2 * tq, tq), F32), pltpu.VMEM((2, 4 * tq, tq), BF16),
                        pltpu.VMEM((2, tq, tq), F32), pltpu.VMEM((nq + 1, tq, LANES), F32)],
        compiler_params=pltpu.CompilerParams(
            dimension_semantics=("arbitrary", "arbitrary"), vmem_limit_bytes=VMEM_LIMIT),
        name="attn_prompt",
    )(qs, kt, vc, u2x)


def _lanes_to_heads(x):
    t = x.shape[0]
    parts = []
    for g in range(D_ATT // LANES):
        a = x[:, g * LANES:(g + 1) * LANES]
        b = pltpu.roll(a, ATT_HEAD_DIM, axis=1)
        parts += [a.reshape(t // SUBLANES, SUBLANES, LANES), b.reshape(t // SUBLANES, SUBLANES, LANES)]
    y = jnp.swapaxes(jnp.stack(parts, axis=1), 1, 2).reshape(t, N_ATT_HEADS, LANES)
    return y[:, :, 0:ATT_HEAD_DIM]


def _attn_sample_kernel(q_ref, kn_ref, vn_ref, ck_hbm, cv_hbm, u2_ref, o_ref, kbuf, vbuf, sem,
                        *, l, tk, chunk, nch, layer):
    bi = pl.program_id(0)
    m = N_ATT_HEADS * l
    hm = _head_lane_masks(D_ATT, BF16)
    q = q_ref[0]
    qs = jnp.concatenate([q * mk for mk in hm], axis=0)
    u2 = u2_ref[...]

    def copies(ch, slot):
        keys = pl.ds(pl.multiple_of((nch - 1 - ch) * chunk, chunk), chunk)
        return (pltpu.make_async_copy(ck_hbm.at[layer, bi, :, :, keys], kbuf.at[slot], sem.at[0, slot]),
                pltpu.make_async_copy(cv_hbm.at[layer, bi, :, :, keys], vbuf.at[slot], sem.at[1, slot]))

    def start(ch, slot):
        for cp in copies(ch, slot):
            cp.start()

    def wait(ch, slot):
        for cp in copies(ch, slot):
            cp.wait()

    start(0, 0)
    pad = jnp.zeros((tk - l, D_ATT), BF16)
    kn = jnp.concatenate([kn_ref[0], pad], axis=0)
    vn = jnp.concatenate([vn_ref[0], pad], axis=0)
    row = jnp.concatenate([lax.broadcasted_iota(jnp.int32, (l, tk), 0)] * N_ATT_HEADS, axis=0)
    col = lax.broadcasted_iota(jnp.int32, (m, tk), 1)
    w, c = _sb_weights(_dot_nt(qs, kn), jnp.zeros((m, 1), F32), u2, col < row)
    acc = _dot(w.astype(BF16), vn)

    def live(c):
        return jnp.max(c) >= SB_EXHAUSTED

    def body(carry):
        ch, c, acc = carry
        slot = ch % 2
        wait(ch, slot)

        @pl.when(ch + 1 < nch)
        def _():
            start(ch + 1, 1 - slot)

        kt = kbuf[slot].reshape(D_ATT, chunk).astype(BF16)
        vt = vbuf[slot].reshape(D_ATT, chunk).astype(BF16)
        for t in reversed(range(chunk // tk)):
            w, c = _sb_weights(_dot(qs, kt[:, t * tk:(t + 1) * tk]), c, u2, None)
            acc = acc + _dot_nt(w.astype(BF16), vt[:, t * tk:(t + 1) * tk])
        return ch + 1, c, acc

    ch, c, acc = lax.while_loop(lambda carry: (carry[0] < nch) & live(carry[1]), body, (jnp.int32(0), c, acc))

    @pl.when(ch < nch)
    def _():
        wait(ch, ch % 2)

    hmf = _head_lane_masks(D_ATT, F32)
    out = acc[0:l] * hmf[0]
    for h in range(1, N_ATT_HEADS):
        out = out + acc[h * l:(h + 1) * l] * hmf[h]
    o_ref[0] = out


def _attn_sample(qb, kb_new, vb_new, cache_k, cache_v, layer):
    b, l, _ = qb.shape
    past = cache_k.shape[2]
    cache_k = jnp.transpose(cache_k, (0, 1, 3, 4, 2))
    cache_v = jnp.transpose(cache_v, (0, 1, 3, 4, 2))
    tk = ATT_TK
    chunk = min(ATT_CACHE_CHUNK, past)
    assert past % chunk == 0 and chunk % tk == 0 and l <= tk and l % 16 == 0
    u2 = _u2_matrix(tk)
    kernel = functools.partial(_attn_sample_kernel, l=l, tk=tk, chunk=chunk, nch=past // chunk, layer=layer)
    new = pl.BlockSpec((1, l, D_ATT), lambda bi: (bi, 0, 0))
    hbm = pl.BlockSpec(memory_space=pl.ANY)
    return pl.pallas_call(
        kernel,
        grid=(b,),
        in_specs=[new, new, new, hbm, hbm, pl.BlockSpec(u2.shape, lambda bi: (0, 0))],
        out_specs=new,
        out_shape=jax.ShapeDtypeStruct((b, l, D_ATT), F32),
        scratch_shapes=[pltpu.VMEM((2, N_ATT_HEADS, ATT_HEAD_DIM, chunk), F32),
                        pltpu.VMEM((2, N_ATT_HEADS, ATT_HEAD_DIM, chunk), F32),
                        pltpu.SemaphoreType.DMA((2, 2))],
        compiler_params=pltpu.CompilerParams(
            dimension_semantics=("arbitrary",), vmem_limit_bytes=VMEM_LIMIT),
        name="attn_sample",
    )(qb, kb_new, vb_new, cache_k, cache_v, u2)


def _ssd_kernel(xbc_ref, z_ref, dt_ref, conv0_ref, s0_ref, cw_ref, cb_ref, alog_ref, dskip_ref, g_ref,
                tri_ref, y_ref, sfin_ref, cbuf, state, *, q):
    c = pl.program_id(1)

    @pl.when(c == 0)
    def _():
        cbuf[...] = conv0_ref[0]
        zero = jnp.zeros((SSM_HEAD_DIM, SSM_STATE), F32)
        for pr in range(N_PAIRS):
            top = jnp.concatenate([s0_ref[0, 2 * pr], zero], axis=1)
            bot = jnp.concatenate([zero, s0_ref[0, 2 * pr + 1]], axis=1)
            state[pr] = jnp.concatenate([top, bot], axis=0)

    dt = dt_ref[0]
    da = dt * (-jnp.exp(alog_ref[...]))
    tri = tri_ref[...]
    hi, mid, lo = _split3(da)
    acum = _dot(tri, hi) + _dot(tri, mid) + _dot(tri, lo)
    acum_t = acum.T
    dt_t = dt.T
    a_end = acum[q - 1:q, :]
    trow = lax.broadcasted_iota(jnp.int32, (q, q), 0)
    tcol = lax.broadcasted_iota(jnp.int32, (q, q), 1)
    causal = tcol <= trow
    bd_r = lax.broadcasted_iota(jnp.int32, (LANES, LANES), 0) < SSM_STATE
    bd_c = lax.broadcasted_iota(jnp.int32, (LANES, LANES), 1) < SSM_STATE
    block_diag = bd_r == bd_c

    x = xbc_ref[0]
    prev = cbuf[...]
    row8 = lax.broadcasted_iota(jnp.int32, (SUBLANES, CONV_DIM), 0)
    xc = cb_ref[...] + x * cw_ref[SSM_CONV - 1:SSM_CONV, :]
    for k in range(1, SSM_CONV):
        r = pltpu.roll(x, k, axis=0)
        head = r[0:SUBLANES]
        for j in range(k):
            head = jnp.where(row8 == j, prev[SUBLANES - k + j:SUBLANES - k + j + 1, :], head)
        r = jnp.concatenate([head, r[SUBLANES:]], axis=0)
        xc = xc + r * cw_ref[SSM_CONV - 1 - k:SSM_CONV - k, :]
    xc = _silu(xc)
    cbuf[...] = x[q - SUBLANES:q]

    lane = lax.broadcasted_iota(jnp.int32, (1, LANES), 1)
    lo_half = lane < SSM_STATE
    bmat = xc[:, D_SSM:D_SSM + LANES]
    cmat = xc[:, D_SSM + LANES:D_SSM + 2 * LANES]
    b_sw = pltpu.roll(bmat, SSM_STATE, axis=1)
    c_sw = pltpu.roll(cmat, SSM_STATE, axis=1)
    bdup = [jnp.where(lo_half, bmat, b_sw), jnp.where(lo_half, b_sw, bmat)]
    cdup = [jnp.where(lo_half, cmat, c_sw), jnp.where(lo_half, c_sw, cmat)]
    gmask = [jnp.where(lo_half, 1.0, 0.0), jnp.where(lo_half, 0.0, 1.0)]
    bmat_b = bmat.astype(BF16)
    cb_g = [_dot_nt((cmat * gmask[g]).astype(BF16), bmat_b) for g in range(SSM_GROUPS)]

    hm_b = [jnp.where(lo_half, 1.0, 0.0).astype(BF16), jnp.where(lo_half, 0.0, 1.0).astype(BF16)]

    pairs = range(N_PAIRS)
    group = [pr // (N_PAIRS // SSM_GROUPS) for pr in pairs]
    x_pair = [xc[:, pr * LANES:(pr + 1) * LANES] for pr in pairs]
    x_b = [v.astype(BF16) for v in x_pair]
    s_pair = [state[pr] for pr in pairs]
    lhs_diag, lhs_off, upd = [], [], []
    for pr in pairs:
        h0 = pr * HEADS_PER_LANE_TILE
        ms = []
        for h in (h0, h0 + 1):
            seg = acum[:, h:h + 1] - acum_t[h:h + 1, :]
            lmat = jnp.where(causal, jnp.exp(seg), 0.0)
            ms.append((cb_g[group[pr]] * lmat * dt_t[h:h + 1, :]).astype(BF16))
        acol = jnp.where(lo_half, acum[:, h0:h0 + 1], acum[:, h0 + 1:h0 + 2])
        dcol = jnp.where(lo_half, dt[:, h0:h0 + 1], dt[:, h0 + 1:h0 + 2])
        aend = jnp.where(lo_half, a_end[:, h0:h0 + 1], a_end[:, h0 + 1:h0 + 2])
        lhs_diag.append(jnp.concatenate(ms, axis=1))
        lhs_off.append((cdup[group[pr]] * jnp.exp(acol)).astype(BF16))
        upd.append((aend, (bdup[group[pr]] * (jnp.exp(aend - acol) * dcol)).astype(BF16)))
    y_diag = [_dot(lhs_diag[pr], jnp.concatenate([x_b[pr] * hm_b[0], x_b[pr] * hm_b[1]], axis=0)) for pr in pairs]
    y_off = [_dot_nt(lhs_off[pr], s_pair[pr].astype(BF16)) for pr in pairs]
    ys = []
    ssq = jnp.zeros((q, 1), F32)
    for pr in pairs:
        y = y_diag[pr] + y_off[pr] + dskip_ref[:, pr * LANES:(pr + 1) * LANES] * x_pair[pr]
        yz = y * _silu(z_ref[0, :, pr * LANES:(pr + 1) * LANES])
        ssq = ssq + jnp.sum(yz * yz, axis=1, keepdims=True)
        ys.append(yz)

    inv = lax.rsqrt(ssq * (1.0 / D_SSM) + NORM_EPS)
    for pr in pairs:
        aend, bw = upd[pr]
        state[pr] = s_pair[pr] * jnp.exp(aend) + jnp.where(block_diag, _dot_tn(x_b[pr], bw), 0.0)
    for pr in range(N_PAIRS):
        y_ref[0, :, pr * LANES:(pr + 1) * LANES] = (
            ys[pr] * inv * g_ref[:, pr * LANES:(pr + 1) * LANES]).astype(BF16)

    @pl.when(c == pl.num_programs(1) - 1)
    def _():
        for pr in range(N_PAIRS):
            sfin_ref[0, 2 * pr] = state[pr, 0:SSM_HEAD_DIM, 0:SSM_STATE]
            sfin_ref[0, 2 * pr + 1] = state[pr, SSM_HEAD_DIM:LANES, SSM_STATE:LANES]


def _ssd(xbc, z, dt, conv0, s0, conv_w, conv_b, a_log_pad, dskip_lanes, g_ssm):
    b, l, _ = xbc.shape
    q = min(SSD_Q, l)
    assert l % q == 0 and q % SUBLANES == 0
    tri = (jnp.arange(q)[None, :] <= jnp.arange(q)[:, None]).astype(BF16)
    seq = lambda n: pl.BlockSpec((1, q, n), lambda bi, ci: (bi, ci, 0))
    per_b = lambda a: pl.BlockSpec((1,) + a.shape[1:], lambda bi, ci: (bi,) + (0,) * (a.ndim - 1))
    full = lambda a: pl.BlockSpec(a.shape, lambda bi, ci: (0,) * a.ndim)
    kernel = functools.partial(_ssd_kernel, q=q)
    return pl.pallas_call(
        kernel,
        grid=(b, l // q),
        in_specs=[seq(CONV_DIM), seq(D_SSM), seq(LANES), per_b(conv0), per_b(s0),
                  full(conv_w), full(conv_b), full(a_log_pad), full(dskip_lanes), full(g_ssm), full(tri)],
        out_specs=(seq(D_SSM), per_b(s0)),
        out_shape=(jax.ShapeDtypeStruct((b, l, D_SSM), BF16), jax.ShapeDtypeStruct(s0.shape, F32)),
        scratch_shapes=[pltpu.VMEM((SUBLANES, CONV_DIM), F32),
                        pltpu.VMEM((N_PAIRS, LANES, LANES), F32)],
        compiler_params=pltpu.CompilerParams(
            dimension_semantics=("arbitrary", "arbitrary"), vmem_limit_bytes=VMEM_LIMIT),
        name="ssd",
    )(xbc, z, dt, conv0, s0, conv_w, conv_b, a_log_pad, dskip_lanes, g_ssm, tri)


def _gelu_tanh(x):
    return 0.5 * x * (1.0 + jnp.tanh(math.sqrt(2.0 / math.pi) * (x + 0.044715 * (x * x * x))))


def _out_ffn_kernel(x_ref, attn_ref, ys_ref, fc0_ref, ga_ref, gpost_ref, gpre_ref, gfpost_ref,
                    woa_ref, wos_ref, wgu_ref, wd_ref, cw_ref,
                    y_ref, fcn_ref, h2_buf, acc, carry, g_buf, u_buf, *, n_seq, lt):
    t = pl.program_id(1)

    @pl.when(t == 0)
    def _():
        carry[...] = fc0_ref[0]

    tm = x_ref.shape[0]
    pieces = [slice(r0, r0 + FFN_ROWS) for r0 in range(0, tm, FFN_ROWS)]
    an = [_rmsnorm(attn_ref[rows, :], ga_ref[...]).astype(BF16) for rows in pieces]
    m = [_dot(an[k], woa_ref[...]) + _dot(ys_ref[rows, :], wos_ref[...]) for k, rows in enumerate(pieces)]
    x1 = [x_ref[rows, :] + _rmsnorm(m[k], gpost_ref[...]) for k, rows in enumerate(pieces)]
    for k, rows in enumerate(pieces):
        y_ref[rows, :] = x1[k]
        h2_buf[rows, :] = _rmsnorm(x1[k], gpre_ref[...]).astype(BF16)
    row = lax.broadcasted_iota(jnp.int32, (lt, FFN_F), 0)

    def cols(ci, base=0):
        start = base + ci * FFN_F
        return pl.ds(start if isinstance(ci, int) else pl.multiple_of(start, FFN_F), FFN_F)

    def stage1(ci, slot):
        h2 = h2_buf[...]
        g_buf[slot] = _dot(h2, wgu_ref[:, cols(ci)])
        u_buf[slot] = _dot(h2, wgu_ref[:, cols(ci, D_FF)])

    def stage2(ci, slot):
        gate = g_buf[slot]
        cw = cw_ref[:, cols(ci)]
        acts = []
        for s in range(n_seq):
            gs = gate[s * lt:(s + 1) * lt]
            prev = carry[ci, s * SUBLANES:(s + 1) * SUBLANES, :]
            p1 = prev[SUBLANES - 1:SUBLANES, :]
            p2 = prev[SUBLANES - 2:SUBLANES - 1, :]
            g1 = jnp.where(row == 0, p1, pltpu.roll(gs, 1, axis=0))
            g2 = jnp.where(row == 0, p2, jnp.where(row == 1, p1, pltpu.roll(gs, 2, axis=0)))
            gc = cw[3:4, :] + g2 * cw[0:1, :] + g1 * cw[1:2, :] + gs * cw[2:3, :]
            carry[ci, s * SUBLANES:(s + 1) * SUBLANES, :] = gs[lt - SUBLANES:lt]
            acts.append(_gelu_tanh(gc))
        act = acts[0] if n_seq == 1 else jnp.concatenate(acts, axis=0)
        acc[...] += _dot((act * u_buf[slot]).astype(BF16), wd_ref[cols(ci), :])

    acc[...] = jnp.zeros_like(acc)
    stage1(0, 0)

    def body(p, _):
        c = 2 * p + 1
        stage1(c, 1)
        stage2(c - 1, 0)
        stage1(c + 1, 0)
        stage2(c, 1)
        return 0

    assert FFN_NC % 2 == 1
    lax.fori_loop(0, FFN_NC // 2, body, 0)
    stage2(FFN_NC - 1, 0)
    for r0 in range(0, tm, FFN_ROWS):
        rows = slice(r0, r0 + FFN_ROWS)
        y_ref[rows, :] = y_ref[rows, :] + _rmsnorm(acc[rows, :], gfpost_ref[...])
    fcn_ref[0] = carry[...]


def _out_ffn(x2d, attn2d, ys2d, fc0, n_seq, lt, gains, weights):
    t = x2d.shape[0]
    tm = n_seq * lt
    n_groups = fc0.shape[0]
    tiles = t // (tm * n_groups)
    assert tiles * tm * n_groups == t
    row = lambda n: pl.BlockSpec((tm, n), lambda gi, ti: (gi * tiles + ti, 0))
    full = lambda a: pl.BlockSpec(a.shape, lambda gi, ti: (0,) * a.ndim, pipeline_mode=pl.Buffered(1))
    fc_spec = pl.BlockSpec((1,) + fc0.shape[1:], lambda gi, ti: (gi, 0, 0, 0))
    kernel = functools.partial(_out_ffn_kernel, n_seq=n_seq, lt=lt)
    return pl.pallas_call(
        kernel,
        grid=(n_groups, tiles),
        in_specs=[row(D_MODEL), row(D_ATT), row(D_SSM), fc_spec] + [full(a) for a in gains]
                 + [full(a) for a in weights],
        out_specs=(row(D_MODEL), fc_spec),
        out_shape=(jax.ShapeDtypeStruct((t, D_MODEL), F32), jax.ShapeDtypeStruct(fc0.shape, F32)),
        scratch_shapes=[pltpu.VMEM((tm, D_MODEL), BF16), pltpu.VMEM((tm, D_MODEL), F32),
                        pltpu.VMEM(fc0.shape[1:], F32), pltpu.VMEM((2, tm, FFN_F), F32),
                        pltpu.VMEM((2, tm, FFN_F), F32)],
        compiler_params=pltpu.CompilerParams(
            dimension_semantics=("arbitrary", "arbitrary"), vmem_limit_bytes=VMEM_LIMIT),
        name="out_ffn",
    )(x2d, attn2d, ys2d, fc0, *gains, *weights)


def _ffn_state_to_chunks(st, n_seq):
    b = st.shape[0]
    s = st.reshape(b // n_seq, n_seq, FFN_CONV - 1, FFN_NC, FFN_F)
    s = jnp.pad(s, ((0, 0), (0, 0), (SUBLANES - (FFN_CONV - 1), 0), (0, 0), (0, 0)))
    return jnp.transpose(s, (0, 3, 1, 2, 4)).reshape(b // n_seq, FFN_NC, n_seq * SUBLANES, FFN_F)


def _chunks_to_ffn_state(ch, n_seq):
    g = ch.shape[0]
    s = ch.reshape(g, FFN_NC, n_seq, SUBLANES, FFN_F)[:, :, :, SUBLANES - (FFN_CONV - 1):, :]
    return jnp.transpose(s, (0, 2, 3, 1, 4)).reshape(g * n_seq, FFN_CONV - 1, D_FF)


def _layer(x, caches, ssm_h0, ssm_conv_prev, ffn_conv_prev, p, is_prompt):
    b, l, _ = x.shape
    t = b * l
    x2d = x.reshape(t, D_MODEL)
    r3 = lambda a: a.reshape(b, l, a.shape[-1])
    proj_w = (p['g_mix_pre'], p['w_in_t'], p['w_dt_t'], p['dt_bias'])
    if is_prompt:
        qb, kb, vb, kt, vt, z, xbc, dt = _in_proj(x, *proj_w, kv_transposed=True)
        attn = _attn_prompt(qb, kb, vb)
        k, v = jnp.transpose(kt, (0, 3, 1, 2)), jnp.transpose(vt, (0, 3, 1, 2))
    else:
        qb, kb, vb, k, v, z, xbc, dt = _in_proj(x2d[None], *proj_w, kv_transposed=False)
        attn = _attn_sample(r3(qb), r3(kb), r3(vb), *caches)

    conv0 = jnp.pad(ssm_conv_prev, ((0, 0), (SUBLANES - (SSM_CONV - 1), 0), (0, 0)))
    ys, s_fin = _ssd(r3(xbc), r3(z), r3(dt), conv0, ssm_h0,
                     p['ssm_conv_w'], p['ssm_conv_b'], p['a_log'], p['d_skip'], p['g_ssm_out'])
    ssm_conv_new = r3(xbc)[:, l - (SSM_CONV - 1):, :]

    if is_prompt:
        n_seq, lt = 1, min(FFN_TM, l)
    else:
        n_seq, lt = b, l
    fc0 = _ffn_state_to_chunks(ffn_conv_prev, n_seq)
    gains = (p['g_attn_out'], p['g_mix_post'], p['g_ffn_pre'], p['g_ffn_post'])
    weights = (p['w_out_a'], p['w_out_s'], p['w_gu'], p['w_down'], p['ffn_cw'])
    y2d, fcn = _out_ffn(x2d, attn.reshape(t, D_ATT), ys.reshape(t, D_SSM), fc0, n_seq, lt, gains, weights)
    return (y2d.reshape(b, l, D_MODEL), k.reshape(b, l, N_ATT_HEADS, ATT_HEAD_DIM),
            v.reshape(b, l, N_ATT_HEADS, ATT_HEAD_DIM), s_fin, ssm_conv_new,
            _chunks_to_ffn_state(fcn, n_seq))


def _prep_params(i, g_mix_pre, g_mix_post, w_in, ssm_conv_w, ssm_conv_b, dt_bias, a_log, d_skip,
                 g_ssm_out, g_attn_out, w_out, g_ffn_pre, g_ffn_post, w_up, ffn_conv_w, ffn_conv_b, w_down):
    row = lambda a: a[i].reshape(1, -1).astype(F32)
    pad_lanes = lambda a: jnp.pad(a, ((0, 0), (0, LANES - a.shape[1])))
    wi_t = w_in[i].T.astype(BF16)
    ffn_cw = jnp.concatenate([ffn_conv_w[i], ffn_conv_b[i][None, :],
                              jnp.zeros((SUBLANES - FFN_CONV - 1, D_FF), F32)], axis=0)
    return {
        'g_mix_pre': row(g_mix_pre), 'g_mix_post': row(g_mix_post),
        'g_ffn_pre': row(g_ffn_pre), 'g_ffn_post': row(g_ffn_post),
        'g_attn_out': row(g_attn_out), 'g_ssm_out': row(g_ssm_out),
        'w_in_t': wi_t,
        'w_dt_t': jnp.pad(wi_t[D_MAIN_PROJ:], ((0, LANES - (wi_t.shape[0] - D_MAIN_PROJ)), (0, 0))),
        'dt_bias': pad_lanes(row(dt_bias)),
        'ssm_conv_w': ssm_conv_w[i].astype(F32), 'ssm_conv_b': row(ssm_conv_b),
        'a_log': pad_lanes(row(a_log)),
        'd_skip': jnp.repeat(d_skip[i].astype(F32), SSM_HEAD_DIM).reshape(1, D_SSM),
        'w_out_a': w_out[i][:D_ATT].astype(BF16), 'w_out_s': w_out[i][D_ATT:].astype(BF16),
        'w_gu': w_up[i].astype(BF16),
        'w_down': w_down[i].astype(BF16),
        'ffn_cw': ffn_cw,
    }


def kernel(x_prompt, x_sample, cache_k, cache_v, state_ssm, state_ssm_conv, state_ffn_conv, g_mix_pre, g_mix_post, w_in, ssm_conv_w, ssm_conv_b, dt_bias, a_log, d_skip, g_ssm_out, g_attn_out, w_out, g_ffn_pre, g_ffn_post, w_up, ffn_conv_w, ffn_conv_b, w_down):
    depth = w_in.shape[0]
    bp = x_prompt.shape[0]
    dtp = x_prompt.dtype
    zh = jnp.zeros((bp, N_SSM_HEADS, SSM_HEAD_DIM, SSM_STATE), dtp)
    zcs = jnp.zeros((bp, SSM_CONV - 1, CONV_DIM), dtp)
    zcf = jnp.zeros((bp, FFN_CONV - 1, D_FF), dtp)
    y_p, y_s = x_prompt, x_sample
    outs_p, outs_s = [], []
    for i in range(depth):
        p = _prep_params(i, g_mix_pre, g_mix_post, w_in, ssm_conv_w, ssm_conv_b, dt_bias, a_log, d_skip,
                         g_ssm_out, g_attn_out, w_out, g_ffn_pre, g_ffn_post, w_up, ffn_conv_w, ffn_conv_b,
                         w_down)
        rp = _layer(y_p, None, zh, zcs, zcf, p, True)
        rs = _layer(y_s, (cache_k, cache_v, i), state_ssm[i], state_ssm_conv[i], state_ffn_conv[i], p, False)
        y_p, y_s = rp[0], rs[0]
        outs_p.append(rp[1:])
        outs_s.append(rs[1:])
    stack = lambda outs, j: jnp.stack([o[j] for o in outs])
    return (y_p, y_s) + tuple(stack(outs_p, j) for j in range(5)) + tuple(stack(outs_s, j) for j in range(5))
```

```python
import functools
import math

import jax
import jax.numpy as jnp
from jax import lax
from jax.experimental import pallas as pl
from jax.experimental.pallas import tpu as pltpu

F32 = jnp.float32
BF16 = jnp.bfloat16

D_MODEL = 1024
D_ATT = 512
N_ATT_HEADS = 8
ATT_HEAD_DIM = 64
D_SSM = 512
N_SSM_HEADS = 8
SSM_HEAD_DIM = 64
SSM_STATE = 64
SSM_GROUPS = 2
SSM_CONV = 4
CONV_DIM = D_SSM + 2 * SSM_GROUPS * SSM_STATE
D_FF = 2816
FFN_CONV = 3
NORM_EPS = 1e-6
D_MAIN_PROJ = 3 * D_ATT + D_SSM + CONV_DIM
ATT_SCALE = ATT_HEAD_DIM ** -0.5
LOG2E = math.log2(math.e)
SB_EXHAUSTED = -120.0

LANES = 128
SUBLANES = 8
HEADS_PER_LANE_TILE = LANES // ATT_HEAD_DIM
N_PAIRS = N_SSM_HEADS // HEADS_PER_LANE_TILE

PROJ_TM = 1024
ATT_TQ = 256
ATT_TK = 128
ATT_CACHE_CHUNK = 256
SSD_Q = 128
FFN_TM = 512
FFN_F = 256
FFN_ROWS = 128
FFN_NC = D_FF // FFN_F
VMEM_LIMIT = 56 * 1024 * 1024


def _rmsnorm(x, g):
    y = x * lax.rsqrt(jnp.mean(x * x, axis=-1, keepdims=True) + NORM_EPS)
    return y * g


def _softplus(x):
    return jnp.maximum(x, 0.0) + jnp.log1p(jnp.exp(-jnp.abs(x)))


def _silu(x):
    return x * (1.0 / (1.0 + jnp.exp(-x)))


def _dot(a, b):
    return jnp.dot(a, b, preferred_element_type=F32)


def _dot_nt(a, b):
    return lax.dot_general(a, b, (((1,), (1,)), ((), ())), preferred_element_type=F32)


def _dot_tn(a, b):
    return lax.dot_general(a, b, (((0,), (0,)), ((), ())), preferred_element_type=F32)


def _split2(x):
    hi = x.astype(BF16)
    lo = (x - hi.astype(F32)).astype(BF16)
    return hi, lo


def _split3(x):
    hi = x.astype(BF16)
    r1 = x - hi.astype(F32)
    mid = r1.astype(BF16)
    lo = (r1 - mid.astype(F32)).astype(BF16)
    return hi, mid, lo


def _store_attention_tiles(qs_ref, vc_ref, qn, vb):
    tq, tk = ATT_TQ, ATT_TQ // 2
    hm = _head_lane_masks(LANES, BF16)
    for p in range(D_ATT // LANES):
        for t in range(qn.shape[0] // tq):
            qt = qn[t * tq:(t + 1) * tq, p * LANES:(p + 1) * LANES]
            vt = vb[t * tq:(t + 1) * tq, p * LANES:(p + 1) * LANES]
            for h in range(HEADS_PER_LANE_TILE):
                qs_ref[0, p, t, h * tq:(h + 1) * tq, :] = qt * hm[h]
                for n, half in enumerate((1, 0)):
                    r0 = (n * HEADS_PER_LANE_TILE + h) * tk
                    vc_ref[0, p, t, r0:r0 + tk, :] = vt[half * tk:(half + 1) * tk] * hm[h]


def _in_proj_kernel(x_ref, g_ref, wt_ref, wdt_ref, dtb_ref,
                    qb_ref, kb_ref, vb_ref, k_ref, v_ref, z_ref, xbc_ref, dt_ref, *, kv_transposed):
    h = _rmsnorm(x_ref[0], g_ref[...]).astype(BF16)

    def proj(lo, hi):
        return _dot_nt(h, wt_ref[lo:hi, :])

    q = proj(0, D_ATT)
    v = proj(2 * D_ATT, 3 * D_ATT)
    if kv_transposed:
        tm = v.shape[0]
        _store_attention_tiles(qb_ref, vb_ref, (q * -ATT_SCALE).astype(BF16), v.astype(BF16))
        kt = _dot_nt(wt_ref[D_ATT:2 * D_ATT, :], h)
        kb_ref[0] = kt.astype(BF16)
        k_ref[0] = kt.reshape(N_ATT_HEADS, ATT_HEAD_DIM, tm)
        v_ref[0] = v.T.reshape(N_ATT_HEADS, ATT_HEAD_DIM, tm)
    else:
        qb_ref[0] = (q * ATT_SCALE).astype(BF16)
        vb_ref[0] = v.astype(BF16)
        k = proj(D_ATT, 2 * D_ATT)
        kb_ref[0] = k.astype(BF16)
        k_ref[0] = _lanes_to_heads(k)
        v_ref[0] = _lanes_to_heads(v)
    z_ref[0] = proj(3 * D_ATT, 3 * D_ATT + D_SSM)
    xbc_ref[0] = proj(3 * D_ATT + D_SSM, D_MAIN_PROJ)
    dt_ref[0] = _softplus(_dot_nt(h, wdt_ref[...]) + dtb_ref[...])


def _in_proj(x, g, w_t, w_dt, dt_bias, kv_transposed):
    b, l, _ = x.shape
    tm = min(PROJ_TM, l)
    assert l % tm == 0
    row = lambda n: pl.BlockSpec((1, tm, n), lambda bi, i: (bi, i, 0))
    full = lambda a: pl.BlockSpec(a.shape, lambda bi, i: (0,) * a.ndim)
    sds = jax.ShapeDtypeStruct
    if kv_transposed:
        kb_shape, kb_spec = sds((b, D_ATT, l), BF16), pl.BlockSpec((1, D_ATT, tm), lambda bi, i: (bi, 0, i))
        kv_shape = sds((b, N_ATT_HEADS, ATT_HEAD_DIM, l), F32)
        kv_spec = pl.BlockSpec((1, N_ATT_HEADS, ATT_HEAD_DIM, tm), lambda bi, i: (bi, 0, 0, i))
        assert tm % ATT_TQ == 0
        n_pairs, tiles = D_ATT // LANES, tm // ATT_TQ
        qv_shape = sds((b, n_pairs, l // ATT_TQ, 2 * ATT_TQ, LANES), BF16)
        qv_spec = pl.BlockSpec((1, n_pairs, tiles, 2 * ATT_TQ, LANES), lambda bi, i: (bi, 0, i, 0, 0))
    else:
        kb_shape, kb_spec = sds((b, l, D_ATT), BF16), row(D_ATT)
        kv_shape = sds((b, l, N_ATT_HEADS, ATT_HEAD_DIM), F32)
        kv_spec = pl.BlockSpec((1, tm, N_ATT_HEADS, ATT_HEAD_DIM), lambda bi, i: (bi, i, 0, 0))
        qv_shape, qv_spec = sds((b, l, D_ATT), BF16), row(D_ATT)
    out_shape = (qv_shape, kb_shape, qv_shape, kv_shape, kv_shape,
                 sds((b, l, D_SSM), F32), sds((b, l, CONV_DIM), F32), sds((b, l, LANES), F32))
    return pl.pallas_call(
        functools.partial(_in_proj_kernel, kv_transposed=kv_transposed),
        grid=(b, l // tm),
        in_specs=[row(D_MODEL), full(g), full(w_t), full(w_dt), full(dt_bias)],
        out_specs=(qv_spec, kb_spec, qv_spec, kv_spec, kv_spec,
                   row(D_SSM), row(CONV_DIM), row(LANES)),
        out_shape=out_shape,
        compiler_params=pltpu.CompilerParams(
            dimension_semantics=("arbitrary", "arbitrary"), vmem_limit_bytes=VMEM_LIMIT),
        name="in_proj",
    )(x, g, w_t, w_dt, dt_bias)


def _sb_weights(s, c, u2, mask):
    lk = -(jnp.maximum(s, 0.0) + jnp.log(1.0 + jnp.exp(-jnp.abs(s))))
    if mask is not None:
        lk = jnp.where(mask, lk, 0.0)
    hi, lo = _split2(lk)
    r = _dot(jnp.concatenate([hi, lo], axis=1), u2) + c
    w = jnp.exp(s + r)
    if mask is not None:
        w = jnp.where(mask, w, 0.0)
    return w, c + jnp.sum(lk, axis=1, keepdims=True)


def _head_lane_masks(n_lanes, dtype):
    lane = lax.broadcasted_iota(jnp.int32, (1, n_lanes), 1)
    return [jnp.where((lane >= h * ATT_HEAD_DIM) & (lane < (h + 1) * ATT_HEAD_DIM), 1.0, 0.0).astype(dtype)
            for h in range(n_lanes // ATT_HEAD_DIM)]


def _attn_prompt_kernel(qs_ref, kt_ref, vc_ref, u2x_ref, o_ref, s_buf, hl_buf, mask_buf, out_buf, *, tq, nq):
    tk = tq // 2
    big = 1e30
    causal = lax.broadcasted_iota(jnp.int32, (tq, tq), 1) < lax.broadcasted_iota(jnp.int32, (tq, tq), 0)
    mask_buf[0] = jnp.full((tq, tq), -jnp.inf, F32)
    mask_buf[1] = jnp.where(causal, -jnp.inf, big)

    def advance(i, j, skip_rest):
        last = (j == 0) | skip_rest
        return jnp.where(last, i + 1, i), jnp.where(last, i + 1, j - 1)

    def exhausted_after(c_prev, row_sums, first):
        carry = jnp.where(first, 0.0, c_prev[:, 0:1]) + row_sums
        return jnp.max(carry) < SB_EXHAUSTED

    def row_off(idx):
        return pl.multiple_of(jnp.minimum(idx, nq - 1) * tq, tq)

    def stage_a(i, j, slot):
        kb = kt_ref[0, :, pl.ds(row_off(j), tq)]
        t = _dot(qs_ref[0, 0, jnp.minimum(i, nq - 1)], kb)
        floor = mask_buf[(i == j).astype(jnp.int32)]
        row_sums = []
        for h in range(HEADS_PER_LANE_TILE):
            th = jnp.maximum(t[h * tq:(h + 1) * tq], floor)
            s_buf[slot, h * tq:(h + 1) * tq, :] = th
            e = jnp.exp2(jnp.abs(th) * (-LOG2E))
            lk = jnp.minimum(th, 0.0) - jnp.log(1.0 + e)
            row_sums.append(jnp.sum(lk, axis=1, keepdims=True))
            hi, lo = _split2(lk)
            for half in range(2):
                r0 = half * 2 * tq + h * tq
                hl_buf[slot, r0:r0 + tq, 0:tk] = hi[:, half * tk:(half + 1) * tk]
                hl_buf[slot, r0:r0 + tq, tk:tq] = lo[:, half * tk:(half + 1) * tk]
        return jnp.concatenate(row_sums, axis=0)

    def stage_b(i, j, slot, c, acc):
        rr = _dot(hl_buf[slot], u2x_ref[...])
        first = i == j
        c = jnp.where(first, 0.0, c)
        acc = jnp.where(first, 0.0, acc)
        ws = []
        for half in (1, 0):
            r0 = half * 2 * tq
            r = rr[r0:r0 + 2 * tq, 0:tk] + c
            th = s_buf[slot, :, half * tk:(half + 1) * tk]
            w = jnp.exp(r - th).astype(BF16)
            c = c + rr[r0:r0 + 2 * tq, tk:tq]
            ws += [w[0:tq], w[tq:2 * tq]]
        acc = acc + _dot(jnp.concatenate(ws, axis=1), vc_ref[0, 0, jnp.minimum(j, nq - 1)])
        out_buf[jnp.minimum(i, nq)] = acc
        return c, acc

    def two_items(carry):
        ib, jb, ia, ja, c, acc = carry
        rs = stage_a(ia, ja, 1)
        c, acc = stage_b(ib, jb, 0, c, acc)
        i2, j2 = advance(ia, ja, exhausted_after(c, rs, ia == ja))
        rs = stage_a(i2, j2, 0)
        c, acc = stage_b(ia, ja, 1, c, acc)
        i3, j3 = advance(i2, j2, exhausted_after(c, rs, i2 == j2))
        return i2, j2, i3, j3, c, acc

    def body(carry):
        return two_items(two_items(two_items(two_items(carry))))

    z = jnp.int32(0)
    stage_a(z, z, 0)
    init = (z, z, z + 1, z + 1, jnp.zeros((2 * tq, tk), F32), jnp.zeros((tq, LANES), F32))
    lax.while_loop(lambda carry: carry[0] < nq, body, init)
    for i in range(nq):
        o_ref[0, i * tq:(i + 1) * tq, :] = out_buf[i]


def _u2_matrix(tk):
    j = jnp.arange(2 * tk)[:, None] % tk
    s = jnp.arange(tk)[None, :]
    return (j >= s).astype(BF16)


def _attn_prompt(qs, kt, vc):
    b, n_pairs, nq, _, _ = qs.shape
    tq = ATT_TQ
    l = nq * tq
    assert tq == 2 * ATT_TK and kt.shape == (b, D_ATT, l)
    u2x = jnp.concatenate([_u2_matrix(ATT_TK), jnp.ones((tq, ATT_TK), BF16)], axis=1)
    kernel = functools.partial(_attn_prompt_kernel, tq=tq, nq=nq)
    tiles = pl.BlockSpec((1, 1, nq, 2 * tq, LANES), lambda bi, hp: (bi, hp, 0, 0, 0))
    return pl.pallas_call(
        kernel,
        grid=(b, n_pairs),
        in_specs=[tiles, pl.BlockSpec((1, LANES, l), lambda bi, hp: (bi, hp, 0)), tiles,
                  pl.BlockSpec(u2x.shape, lambda bi, hp: (0, 0))],
        out_specs=pl.BlockSpec((1, l, LANES), lambda bi, hp: (bi, 0, hp)),
        out_shape=jax.ShapeDtypeStruct((b, l, D_ATT), F32),
        scratch_shapes=[pltpu.VMEM((2, 2 * tq, tq), F32), pltpu.VMEM((2, 4 * tq, tq), BF16),
                        pltpu.VMEM((2, tq, tq), F32), pltpu.VMEM((nq + 1, tq, LANES), F32)],
        compiler_params=pltpu.CompilerParams(
            dimension_semantics=("arbitrary", "arbitrary"), vmem_limit_bytes=VMEM_LIMIT),
        name="attn_prompt",
    )(qs, kt, vc, u2x)


def _lanes_to_heads(x):
    t = x.shape[0]
    parts = []
    for g in range(D_ATT // LANES):
        a = x[:, g * LANES:(g + 1) * LANES]
        b = pltpu.roll(a, ATT_HEAD_DIM, axis=1)
        parts += [a.reshape(t // SUBLANES, SUBLANES, LANES), b.reshape(t // SUBLANES, SUBLANES, LANES)]
    y = jnp.swapaxes(jnp.stack(parts, axis=1), 1, 2).reshape(t, N_ATT_HEADS, LANES)
    return y[:, :, 0:ATT_HEAD_DIM]


def _attn_sample_kernel(q_ref, kn_ref, vn_ref, ck_hbm, cv_hbm, u2_ref, o_ref, kbuf, vbuf, sem,
                        *, l, tk, chunk, nch, layer):
    bi = pl.program_id(0)
    m = N_ATT_HEADS * l
    hm = _head_lane_masks(D_ATT, BF16)
    q = q_ref[0]
    qs = jnp.concatenate([q * mk for mk in hm], axis=0)
    u2 = u2_ref[...]

    def copies(ch, slot):
        keys = pl.ds(pl.multiple_of((nch - 1 - ch) * chunk, chunk), chunk)
        return (pltpu.make_async_copy(ck_hbm.at[layer, bi, :, :, keys], kbuf.at[slot], sem.at[0, slot]),
                pltpu.make_async_copy(cv_hbm.at[layer, bi, :, :, keys], vbuf.at[slot], sem.at[1, slot]))

    def start(ch, slot):
        for cp in copies(ch, slot):
            cp.start()

    def wait(ch, slot):
        for cp in copies(ch, slot):
            cp.wait()

    start(0, 0)
    pad = jnp.zeros((tk - l, D_ATT), BF16)
    kn = jnp.concatenate([kn_ref[0], pad], axis=0)
    vn = jnp.concatenate([vn_ref[0], pad], axis=0)
    row = jnp.concatenate([lax.broadcasted_iota(jnp.int32, (l, tk), 0)] * N_ATT_HEADS, axis=0)
    col = lax.broadcasted_iota(jnp.int32, (m, tk), 1)
    w, c = _sb_weights(_dot_nt(qs, kn), jnp.zeros((m, 1), F32), u2, col < row)
    acc = _dot(w.astype(BF16), vn)

    def live(c):
        return jnp.max(c) >= SB_EXHAUSTED

    def body(carry):
        ch, c, acc = carry
        slot = ch % 2
        wait(ch, slot)

        @pl.when(ch + 1 < nch)
        def _():
            start(ch + 1, 1 - slot)

        kt = kbuf[slot].reshape(D_ATT, chunk).astype(BF16)
        vt = vbuf[slot].reshape(D_ATT, chunk).astype(BF16)
        for t in reversed(range(chunk // tk)):
            w, c = _sb_weights(_dot(qs, kt[:, t * tk:(t + 1) * tk]), c, u2, None)
            acc = acc + _dot_nt(w.astype(BF16), vt[:, t * tk:(t + 1) * tk])
        return ch + 1, c, acc

    ch, c, acc = lax.while_loop(lambda carry: (carry[0] < nch) & live(carry[1]), body, (jnp.int32(0), c, acc))

    @pl.when(ch < nch)
    def _():
        wait(ch, ch % 2)

    hmf = _head_lane_masks(D_ATT, F32)
    out = acc[0:l] * hmf[0]
    for h in range(1, N_ATT_HEADS):
        out = out + acc[h * l:(h + 1) * l] * hmf[h]
    o_ref[0] = out


def _attn_sample(qb, kb_new, vb_new, cache_k, cache_v, layer):
    b, l, _ = qb.shape
    past = cache_k.shape[2]
    cache_k = jnp.transpose(cache_k, (0, 1, 3, 4, 2))
    cache_v = jnp.transpose(cache_v, (0, 1, 3, 4, 2))
    tk = ATT_TK
    chunk = min(ATT_CACHE_CHUNK, past)
    assert past % chunk == 0 and chunk % tk == 0 and l <= tk and l % 16 == 0
    u2 = _u2_matrix(tk)
    kernel = functools.partial(_attn_sample_kernel, l=l, tk=tk, chunk=chunk, nch=past // chunk, layer=layer)
    new = pl.BlockSpec((1, l, D_ATT), lambda bi: (bi, 0, 0))
    hbm = pl.BlockSpec(memory_space=pl.ANY)
    return pl.pallas_call(
        kernel,
        grid=(b,),
        in_specs=[new, new, new, hbm, hbm, pl.BlockSpec(u2.shape, lambda bi: (0, 0))],
        out_specs=new,
        out_shape=jax.ShapeDtypeStruct((b, l, D_ATT), F32),
        scratch_shapes=[pltpu.VMEM((2, N_ATT_HEADS, ATT_HEAD_DIM, chunk), F32),
                        pltpu.VMEM((2, N_ATT_HEADS, ATT_HEAD_DIM, chunk), F32),
                        pltpu.SemaphoreType.DMA((2, 2))],
        compiler_params=pltpu.CompilerParams(
            dimension_semantics=("arbitrary",), vmem_limit_bytes=VMEM_LIMIT),
        name="attn_sample",
    )(qb, kb_new, vb_new, cache_k, cache_v, u2)


def _ssd_kernel(xbc_ref, z_ref, dt_ref, conv0_ref, s0_ref, cw_ref, cb_ref, alog_ref, dskip_ref, g_ref,
                tri_ref, y_ref, sfin_ref, cbuf, state, *, q):
    c = pl.program_id(1)

    @pl.when(c == 0)
    def _():
        cbuf[...] = conv0_ref[0]
        zero = jnp.zeros((SSM_HEAD_DIM, SSM_STATE), F32)
        for pr in range(N_PAIRS):
            top = jnp.concatenate([s0_ref[0, 2 * pr], zero], axis=1)
            bot = jnp.concatenate([zero, s0_ref[0, 2 * pr + 1]], axis=1)
            state[pr] = jnp.concatenate([top, bot], axis=0)

    dt = dt_ref[0]
    da = dt * (-jnp.exp(alog_ref[...]))
    tri = tri_ref[...]
    hi, mid, lo = _split3(da)
    acum = _dot(tri, hi) + _dot(tri, mid) + _dot(tri, lo)
    acum_t = acum.T
    dt_t = dt.T
    a_end = acum[q - 1:q, :]
    trow = lax.broadcasted_iota(jnp.int32, (q, q), 0)
    tcol = lax.broadcasted_iota(jnp.int32, (q, q), 1)
    causal = tcol <= trow
    bd_r = lax.broadcasted_iota(jnp.int32, (LANES, LANES), 0) < SSM_STATE
    bd_c = lax.broadcasted_iota(jnp.int32, (LANES, LANES), 1) < SSM_STATE
    block_diag = bd_r == bd_c

    x = xbc_ref[0]
    prev = cbuf[...]
    row8 = lax.broadcasted_iota(jnp.int32, (SUBLANES, CONV_DIM), 0)
    xc = cb_ref[...] + x * cw_ref[SSM_CONV - 1:SSM_CONV, :]
    for k in range(1, SSM_CONV):
        r = pltpu.roll(x, k, axis=0)
        head = r[0:SUBLANES]
        for j in range(k):
            head = jnp.where(row8 == j, prev[SUBLANES - k + j:SUBLANES - k + j + 1, :], head)
        r = jnp.concatenate([head, r[SUBLANES:]], axis=0)
        xc = xc + r * cw_ref[SSM_CONV - 1 - k:SSM_CONV - k, :]
    xc = _silu(xc)
    cbuf[...] = x[q - SUBLANES:q]

    lane = lax.broadcasted_iota(jnp.int32, (1, LANES), 1)
    lo_half = lane < SSM_STATE
    bmat = xc[:, D_SSM:D_SSM + LANES]
    cmat = xc[:, D_SSM + LANES:D_SSM + 2 * LANES]
    b_sw = pltpu.roll(bmat, SSM_STATE, axis=1)
    c_sw = pltpu.roll(cmat, SSM_STATE, axis=1)
    bdup = [jnp.where(lo_half, bmat, b_sw), jnp.where(lo_half, b_sw, bmat)]
    cdup = [jnp.where(lo_half, cmat, c_sw), jnp.where(lo_half, c_sw, cmat)]
    gmask = [jnp.where(lo_half, 1.0, 0.0), jnp.where(lo_half, 0.0, 1.0)]
    bmat_b = bmat.astype(BF16)
    cb_g = [_dot_nt((cmat * gmask[g]).astype(BF16), bmat_b) for g in range(SSM_GROUPS)]

    hm_b = [jnp.where(lo_half, 1.0, 0.0).astype(BF16), jnp.where(lo_half, 0.0, 1.0).astype(BF16)]

    pairs = range(N_PAIRS)
    group = [pr // (N_PAIRS // SSM_GROUPS) for pr in pairs]
    x_pair = [xc[:, pr * LANES:(pr + 1) * LANES] for pr in pairs]
    x_b = [v.astype(BF16) for v in x_pair]
    s_pair = [state[pr] for pr in pairs]
    lhs_diag, lhs_off, upd = [], [], []
    for pr in pairs:
        h0 = pr * HEADS_PER_LANE_TILE
        ms = []
        for h in (h0, h0 + 1):
            seg = acum[:, h:h + 1] - acum_t[h:h + 1, :]
            lmat = jnp.where(causal, jnp.exp(seg), 0.0)
            ms.append((cb_g[group[pr]] * lmat * dt_t[h:h + 1, :]).astype(BF16))
        acol = jnp.where(lo_half, acum[:, h0:h0 + 1], acum[:, h0 + 1:h0 + 2])
        dcol = jnp.where(lo_half, dt[:, h0:h0 + 1], dt[:, h0 + 1:h0 + 2])
        aend = jnp.where(lo_half, a_end[:, h0:h0 + 1], a_end[:, h0 + 1:h0 + 2])
        lhs_diag.append(jnp.concatenate(ms, axis=1))
        lhs_off.append((cdup[group[pr]] * jnp.exp(acol)).astype(BF16))
        upd.append((aend, (bdup[group[pr]] * (jnp.exp(aend - acol) * dcol)).astype(BF16)))
    y_diag = [_dot(lhs_diag[pr], jnp.concatenate([x_b[pr] * hm_b[0], x_b[pr] * hm_b[1]], axis=0)) for pr in pairs]
    y_off = [_dot_nt(lhs_off[pr], s_pair[pr].astype(BF16)) for pr in pairs]
    ys = []
    ssq = jnp.zeros((q, 1), F32)
    for pr in pairs:
        y = y_diag[pr] + y_off[pr] + dskip_ref[:, pr * LANES:(pr + 1) * LANES] * x_pair[pr]
        yz = y * _silu(z_ref[0, :, pr * LANES:(pr + 1) * LANES])
        ssq = ssq + jnp.sum(yz * yz, axis=1, keepdims=True)
        ys.append(yz)

    inv = lax.rsqrt(ssq * (1.0 / D_SSM) + NORM_EPS)
    for pr in pairs:
        aend, bw = upd[pr]
        state[pr] = s_pair[pr] * jnp.exp(aend) + jnp.where(block_diag, _dot_tn(x_b[pr], bw), 0.0)
    for pr in range(N_PAIRS):
        y_ref[0, :, pr * LANES:(pr + 1) * LANES] = (
            ys[pr] * inv * g_ref[:, pr * LANES:(pr + 1) * LANES]).astype(BF16)

    @pl.when(c == pl.num_programs(1) - 1)
    def _():
        for pr in range(N_PAIRS):
            sfin_ref[0, 2 * pr] = state[pr, 0:SSM_HEAD_DIM, 0:SSM_STATE]
            sfin_ref[0, 2 * pr + 1] = state[pr, SSM_HEAD_DIM:LANES, SSM_STATE:LANES]


def _ssd(xbc, z, dt, conv0, s0, conv_w, conv_b, a_log_pad, dskip_lanes, g_ssm):
    b, l, _ = xbc.shape
    q = min(SSD_Q, l)
    assert l % q == 0 and q % SUBLANES == 0
    tri = (jnp.arange(q)[None, :] <= jnp.arange(q)[:, None]).astype(BF16)
    seq = lambda n: pl.BlockSpec((1, q, n), lambda bi, ci: (bi, ci, 0))
    per_b = lambda a: pl.BlockSpec((1,) + a.shape[1:], lambda bi, ci: (bi,) + (0,) * (a.ndim - 1))
    full = lambda a: pl.BlockSpec(a.shape, lambda bi, ci: (0,) * a.ndim)
    kernel = functools.partial(_ssd_kernel, q=q)
    return pl.pallas_call(
        kernel,
        grid=(b, l // q),
        in_specs=[seq(CONV_DIM), seq(D_SSM), seq(LANES), per_b(conv0), per_b(s0),
                  full(conv_w), full(conv_b), full(a_log_pad), full(dskip_lanes), full(g_ssm), full(tri)],
        out_specs=(seq(D_SSM), per_b(s0)),
        out_shape=(jax.ShapeDtypeStruct((b, l, D_SSM), BF16), jax.ShapeDtypeStruct(s0.shape, F32)),
        scratch_shapes=[pltpu.VMEM((SUBLANES, CONV_DIM), F32),
                        pltpu.VMEM((N_PAIRS, LANES, LANES), F32)],
        compiler_params=pltpu.CompilerParams(
            dimension_semantics=("arbitrary", "arbitrary"), vmem_limit_bytes=VMEM_LIMIT),
        name="ssd",
    )(xbc, z, dt, conv0, s0, conv_w, conv_b, a_log_pad, dskip_lanes, g_ssm, tri)


def _gelu_tanh(x):
    return 0.5 * x * (1.0 + jnp.tanh(math.sqrt(2.0 / math.pi) * (x + 0.044715 * (x * x * x))))


def _out_ffn_kernel(x_ref, attn_ref, ys_ref, fc0_ref, ga_ref, gpost_ref, gpre_ref, gfpost_ref,
                    woa_ref, wos_ref, wgu_ref, wd_ref, cw_ref,
                    y_ref, fcn_ref, h2_buf, acc, carry, g_buf, u_buf, *, n_seq, lt):
    t = pl.program_id(1)

    @pl.when(t == 0)
    def _():
        carry[...] = fc0_ref[0]

    tm = x_ref.shape[0]
    pieces = [slice(r0, r0 + FFN_ROWS) for r0 in range(0, tm, FFN_ROWS)]
    an = [_rmsnorm(attn_ref[rows, :], ga_ref[...]).astype(BF16) for rows in pieces]
    m = [_dot(an[k], woa_ref[...]) + _dot(ys_ref[rows, :], wos_ref[...]) for k, rows in enumerate(pieces)]
    x1 = [x_ref[rows, :] + _rmsnorm(m[k], gpost_ref[...]) for k, rows in enumerate(pieces)]
    for k, rows in enumerate(pieces):
        y_ref[rows, :] = x1[k]
        h2_buf[rows, :] = _rmsnorm(x1[k], gpre_ref[...]).astype(BF16)
    row = lax.broadcasted_iota(jnp.int32, (lt, FFN_F), 0)

    def cols(ci, base=0):
        start = base + ci * FFN_F
        return pl.ds(start if isinstance(ci, int) else pl.multiple_of(start, FFN_F), FFN_F)

    def stage1(ci, slot):
        h2 = h2_buf[...]
        g_buf[slot] = _dot(h2, wgu_ref[:, cols(ci)])
        u_buf[slot] = _dot(h2, wgu_ref[:, cols(ci, D_FF)])

    def stage2(ci, slot):
        gate = g_buf[slot]
        cw = cw_ref[:, cols(ci)]
        acts = []
        for s in range(n_seq):
            gs = gate[s * lt:(s + 1) * lt]
            prev = carry[ci, s * SUBLANES:(s + 1) * SUBLANES, :]
            p1 = prev[SUBLANES - 1:SUBLANES, :]
            p2 = prev[SUBLANES - 2:SUBLANES - 1, :]
            g1 = jnp.where(row == 0, p1, pltpu.roll(gs, 1, axis=0))
            g2 = jnp.where(row == 0, p2, jnp.where(row == 1, p1, pltpu.roll(gs, 2, axis=0)))
            gc = cw[3:4, :] + g2 * cw[0:1, :] + g1 * cw[1:2, :] + gs * cw[2:3, :]
            carry[ci, s * SUBLANES:(s + 1) * SUBLANES, :] = gs[lt - SUBLANES:lt]
            acts.append(_gelu_tanh(gc))
        act = acts[0] if n_seq == 1 else jnp.concatenate(acts, axis=0)
        acc[...] += _dot((act * u_buf[slot]).astype(BF16), wd_ref[cols(ci), :])

    acc[...] = jnp.zeros_like(acc)
    stage1(0, 0)

    def body(p, _):
        c = 2 * p + 1
        stage1(c, 1)
        stage2(c - 1, 0)
        stage1(c + 1, 0)
        stage2(c, 1)
        return 0

    assert FFN_NC % 2 == 1
    lax.fori_loop(0, FFN_NC // 2, body, 0)
    stage2(FFN_NC - 1, 0)
    for r0 in range(0, tm, FFN_ROWS):
        rows = slice(r0, r0 + FFN_ROWS)
        y_ref[rows, :] = y_ref[rows, :] + _rmsnorm(acc[rows, :], gfpost_ref[...])
    fcn_ref[0] = carry[...]


def _out_ffn(x2d, attn2d, ys2d, fc0, n_seq, lt, gains, weights):
    t = x2d.shape[0]
    tm = n_seq * lt
    n_groups = fc0.shape[0]
    tiles = t // (tm * n_groups)
    assert tiles * tm * n_groups == t
    row = lambda n: pl.BlockSpec((tm, n), lambda gi, ti: (gi * tiles + ti, 0))
    full = lambda a: pl.BlockSpec(a.shape, lambda gi, ti: (0,) * a.ndim, pipeline_mode=pl.Buffered(1))
    fc_spec = pl.BlockSpec((1,) + fc0.shape[1:], lambda gi, ti: (gi, 0, 0, 0))
    kernel = functools.partial(_out_ffn_kernel, n_seq=n_seq, lt=lt)
    return pl.pallas_call(
        kernel,
        grid=(n_groups, tiles),
        in_specs=[row(D_MODEL), row(D_ATT), row(D_SSM), fc_spec] + [full(a) for a in gains]
                 + [full(a) for a in weights],
        out_specs=(row(D_MODEL), fc_spec),
        out_shape=(jax.ShapeDtypeStruct((t, D_MODEL), F32), jax.ShapeDtypeStruct(fc0.shape, F32)),
        scratch_shapes=[pltpu.VMEM((tm, D_MODEL), BF16), pltpu.VMEM((tm, D_MODEL), F32),
                        pltpu.VMEM(fc0.shape[1:], F32), pltpu.VMEM((2, tm, FFN_F), F32),
                        pltpu.VMEM((2, tm, FFN_F), F32)],
        compiler_params=pltpu.CompilerParams(
            dimension_semantics=("arbitrary", "arbitrary"), vmem_limit_bytes=VMEM_LIMIT),
        name="out_ffn",
    )(x2d, attn2d, ys2d, fc0, *gains, *weights)


def _ffn_state_to_chunks(st, n_seq):
    b = st.shape[0]
    s = st.reshape(b // n_seq, n_seq, FFN_CONV - 1, FFN_NC, FFN_F)
    s = jnp.pad(s, ((0, 0), (0, 0), (SUBLANES - (FFN_CONV - 1), 0), (0, 0), (0, 0)))
    return jnp.transpose(s, (0, 3, 1, 2, 4)).reshape(b // n_seq, FFN_NC, n_seq * SUBLANES, FFN_F)


def _chunks_to_ffn_state(ch, n_seq):
    g = ch.shape[0]
    s = ch.reshape(g, FFN_NC, n_seq, SUBLANES, FFN_F)[:, :, :, SUBLANES - (FFN_CONV - 1):, :]
    return jnp.transpose(s, (0, 2, 3, 1, 4)).reshape(g * n_seq, FFN_CONV - 1, D_FF)


def _layer(x, caches, ssm_h0, ssm_conv_prev, ffn_conv_prev, p, is_prompt):
    b, l, _ = x.shape
    t = b * l
    x2d = x.reshape(t, D_MODEL)
    r3 = lambda a: a.reshape(b, l, a.shape[-1])
    proj_w = (p['g_mix_pre'], p['w_in_t'], p['w_dt_t'], p['dt_bias'])
    if is_prompt:
        qb, kb, vb, kt, vt, z, xbc, dt = _in_proj(x, *proj_w, kv_transposed=True)
        attn = _attn_prompt(qb, kb, vb)
        k, v = jnp.transpose(kt, (0, 3, 1, 2)), jnp.transpose(vt, (0, 3, 1, 2))
    else:
        qb, kb, vb, k, v, z, xbc, dt = _in_proj(x2d[None], *proj_w, kv_transposed=False)
        attn = _attn_sample(r3(qb), r3(kb), r3(vb), *caches)

    conv0 = jnp.pad(ssm_conv_prev, ((0, 0), (SUBLANES - (SSM_CONV - 1), 0), (0, 0)))
    ys, s_fin = _ssd(r3(xbc), r3(z), r3(dt), conv0, ssm_h0,
                     p['ssm_conv_w'], p['ssm_conv_b'], p['a_log'], p['d_skip'], p['g_ssm_out'])
    ssm_conv_new = r3(xbc)[:, l - (SSM_CONV - 1):, :]

    if is_prompt:
        n_seq, lt = 1, min(FFN_TM, l)
    else:
        n_seq, lt = b, l
    fc0 = _ffn_state_to_chunks(ffn_conv_prev, n_seq)
    gains = (p['g_attn_out'], p['g_mix_post'], p['g_ffn_pre'], p['g_ffn_post'])
    weights = (p['w_out_a'], p['w_out_s'], p['w_gu'], p['w_down'], p['ffn_cw'])
    y2d, fcn = _out_ffn(x2d, attn.reshape(t, D_ATT), ys.reshape(t, D_SSM), fc0, n_seq, lt, gains, weights)
    return (y2d.reshape(b, l, D_MODEL), k.reshape(b, l, N_ATT_HEADS, ATT_HEAD_DIM),
            v.reshape(b, l, N_ATT_HEADS, ATT_HEAD_DIM), s_fin, ssm_conv_new,
            _chunks_to_ffn_state(fcn, n_seq))


def _prep_params(i, g_mix_pre, g_mix_post, w_in, ssm_conv_w, ssm_conv_b, dt_bias, a_log, d_skip,
                 g_ssm_out, g_attn_out, w_out, g_ffn_pre, g_ffn_post, w_up, ffn_conv_w, ffn_conv_b, w_down):
    row = lambda a: a[i].reshape(1, -1).astype(F32)
    pad_lanes = lambda a: jnp.pad(a, ((0, 0), (0, LANES - a.shape[1])))
    wi_t = w_in[i].T.astype(BF16)
    ffn_cw = jnp.concatenate([ffn_conv_w[i], ffn_conv_b[i][None, :],
                              jnp.zeros((SUBLANES - FFN_CONV - 1, D_FF), F32)], axis=0)
    return {
        'g_mix_pre': row(g_mix_pre), 'g_mix_post': row(g_mix_post),
        'g_ffn_pre': row(g_ffn_pre), 'g_ffn_post': row(g_ffn_post),
        'g_attn_out': row(g_attn_out), 'g_ssm_out': row(g_ssm_out),
        'w_in_t': wi_t,
        'w_dt_t': jnp.pad(wi_t[D_MAIN_PROJ:], ((0, LANES - (wi_t.shape[0] - D_MAIN_PROJ)), (0, 0))),
        'dt_bias': pad_lanes(row(dt_bias)),
        'ssm_conv_w': ssm_conv_w[i].astype(F32), 'ssm_conv_b': row(ssm_conv_b),
        'a_log': pad_lanes(row(a_log)),
        'd_skip': jnp.repeat(d_skip[i].astype(F32), SSM_HEAD_DIM).reshape(1, D_SSM),
        'w_out_a': w_out[i][:D_ATT].astype(BF16), 'w_out_s': w_out[i][D_ATT:].astype(BF16),
        'w_gu': w_up[i].astype(BF16),
        'w_down': w_down[i].astype(BF16),
        'ffn_cw': ffn_cw,
    }


def kernel(x_prompt, x_sample, cache_k, cache_v, state_ssm, state_ssm_conv, state_ffn_conv, g_mix_pre, g_mix_post, w_in, ssm_conv_w, ssm_conv_b, dt_bias, a_log, d_skip, g_ssm_out, g_attn_out, w_out, g_ffn_pre, g_ffn_post, w_up, ffn_conv_w, ffn_conv_b, w_down):
    depth = w_in.shape[0]
    bp = x_prompt.shape[0]
    dtp = x_prompt.dtype
    zh = jnp.zeros((bp, N_SSM_HEADS, SSM_HEAD_DIM, SSM_STATE), dtp)
    zcs = jnp.zeros((bp, SSM_CONV - 1, CONV_DIM), dtp)
    zcf = jnp.zeros((bp, FFN_CONV - 1, D_FF), dtp)
    y_p, y_s = x_prompt, x_sample
    outs_p, outs_s = [], []
    for i in range(depth):
        p = _prep_params(i, g_mix_pre, g_mix_post, w_in, ssm_conv_w, ssm_conv_b, dt_bias, a_log, d_skip,
                         g_ssm_out, g_attn_out, w_out, g_ffn_pre, g_ffn_post, w_up, ffn_conv_w, ffn_conv_b,
                         w_down)
        rp = _layer(y_p, None, zh, zcs, zcf, p, True)
        rs = _layer(y_s, (cache_k, cache_v, i), state_ssm[i], state_ssm_conv[i], state_ffn_conv[i], p, False)
        y_p, y_s = rp[0], rs[0]
        outs_p.append(rp[1:])
        outs_s.append(rs[1:])
    stack = lambda outs, j: jnp.stack([o[j] for o in outs])
    return (y_p, y_s) + tuple(stack(outs_p, j) for j in range(5)) + tuple(stack(outs_s, j) for j in range(5))
```

```python
import functools
import math

import jax
import jax.numpy as jnp
from jax import lax
from jax.experimental import pallas as pl
from jax.experimental.pallas import tpu as pltpu

F32 = jnp.float32
BF16 = jnp.bfloat16

D_MODEL = 1024
D_ATT = 512
N_ATT_HEADS = 8
ATT_HEAD_DIM = 64
D_SSM = 512
N_SSM_HEADS = 8
SSM_HEAD_DIM = 64
SSM_STATE = 64
SSM_GROUPS = 2
SSM_CONV = 4
CONV_DIM = D_SSM + 2 * SSM_GROUPS * SSM_STATE
D_FF = 2816
FFN_CONV = 3
NORM_EPS = 1e-6
D_MAIN_PROJ = 3 * D_ATT + D_SSM + CONV_DIM
ATT_SCALE = ATT_HEAD_DIM ** -0.5
LOG2E = math.log2(math.e)
SB_EXHAUSTED = -120.0

LANES = 128
SUBLANES = 8
HEADS_PER_LANE_TILE = LANES // ATT_HEAD_DIM
N_PAIRS = N_SSM_HEADS // HEADS_PER_LANE_TILE

PROJ_TM = 1024
ATT_TQ = 256
ATT_TK = 128
ATT_CACHE_CHUNK = 256
SSD_Q = 128
FFN_TM = 512
FFN_F = 256
FFN_ROWS = 128
FFN_NC = D_FF // FFN_F
VMEM_LIMIT = 56 * 1024 * 1024


def _rmsnorm(x, g):
    y = x * lax.rsqrt(jnp.mean(x * x, axis=-1, keepdims=True) + NORM_EPS)
    return y * g


def _softplus(x):
    return jnp.maximum(x, 0.0) + jnp.log1p(jnp.exp(-jnp.abs(x)))


def _silu(x):
    return x * (1.0 / (1.0 + jnp.exp(-x)))


def _dot(a, b):
    return jnp.dot(a, b, preferred_element_type=F32)


def _dot_nt(a, b):
    return lax.dot_general(a, b, (((1,), (1,)), ((), ())), preferred_element_type=F32)


def _dot_tn(a, b):
    return lax.dot_general(a, b, (((0,), (0,)), ((), ())), preferred_element_type=F32)


def _split2(x):
    hi = x.astype(BF16)
    lo = (x - hi.astype(F32)).astype(BF16)
    return hi, lo


def _split3(x):
    hi = x.astype(BF16)
    r1 = x - hi.astype(F32)
    mid = r1.astype(BF16)
    lo = (r1 - mid.astype(F32)).astype(BF16)
    return hi, mid, lo


def _store_attention_tiles(qs_ref, vc_ref, qn, vb):
    tq, tk = ATT_TQ, ATT_TQ // 2
    hm = _head_lane_masks(LANES, BF16)
    for p in range(D_ATT // LANES):
        for t in range(qn.shape[0] // tq):
            qt = qn[t * tq:(t + 1) * tq, p * LANES:(p + 1) * LANES]
            vt = vb[t * tq:(t + 1) * tq, p * LANES:(p + 1) * LANES]
            for h in range(HEADS_PER_LANE_TILE):
                qs_ref[0, p, t, h * tq:(h + 1) * tq, :] = qt * hm[h]
                for n, half in enumerate((1, 0)):
                    r0 = (n * HEADS_PER_LANE_TILE + h) * tk
                    vc_ref[0, p, t, r0:r0 + tk, :] = vt[half * tk:(half + 1) * tk] * hm[h]


def _in_proj_kernel(x_ref, g_ref, wt_ref, wdt_ref, dtb_ref,
                    qb_ref, kb_ref, vb_ref, k_ref, v_ref, z_ref, xbc_ref, dt_ref, *, kv_transposed):
    h = _rmsnorm(x_ref[0], g_ref[...]).astype(BF16)

    def proj(lo, hi):
        return _dot_nt(h, wt_ref[lo:hi, :])

    q = proj(0, D_ATT)
    v = proj(2 * D_ATT, 3 * D_ATT)
    if kv_transposed:
        tm = v.shape[0]
        _store_attention_tiles(qb_ref, vb_ref, (q * -ATT_SCALE).astype(BF16), v.astype(BF16))
        kt = _dot_nt(wt_ref[D_ATT:2 * D_ATT, :], h)
        kb_ref[0] = kt.astype(BF16)
        k_ref[0] = kt.reshape(N_ATT_HEADS, ATT_HEAD_DIM, tm)
        v_ref[0] = v.T.reshape(N_ATT_HEADS, ATT_HEAD_DIM, tm)
    else:
        qb_ref[0] = (q * ATT_SCALE).astype(BF16)
        vb_ref[0] = v.astype(BF16)
        k = proj(D_ATT, 2 * D_ATT)
        kb_ref[0] = k.astype(BF16)
        k_ref[0] = _lanes_to_heads(k)
        v_ref[0] = _lanes_to_heads(v)
    z_ref[0] = proj(3 * D_ATT, 3 * D_ATT + D_SSM)
    xbc_ref[0] = proj(3 * D_ATT + D_SSM, D_MAIN_PROJ)
    dt_ref[0] = _softplus(_dot_nt(h, wdt_ref[...]) + dtb_ref[...])


def _in_proj(x, g, w_t, w_dt, dt_bias, kv_transposed):
    b, l, _ = x.shape
    tm = min(PROJ_TM, l)
    assert l % tm == 0
    row = lambda n: pl.BlockSpec((1, tm, n), lambda bi, i: (bi, i, 0))
    full = lambda a: pl.BlockSpec(a.shape, lambda bi, i: (0,) * a.ndim)
    sds = jax.ShapeDtypeStruct
    if kv_transposed:
        kb_shape, kb_spec = sds((b, D_ATT, l), BF16), pl.BlockSpec((1, D_ATT, tm), lambda bi, i: (bi, 0, i))
        kv_shape = sds((b, N_ATT_HEADS, ATT_HEAD_DIM, l), F32)
        kv_spec = pl.BlockSpec((1, N_ATT_HEADS, ATT_HEAD_DIM, tm), lambda bi, i: (bi, 0, 0, i))
        assert tm % ATT_TQ == 0
        n_pairs, tiles = D_ATT // LANES, tm // ATT_TQ
        qv_shape = sds((b, n_pairs, l // ATT_TQ, 2 * ATT_TQ, LANES), BF16)
        qv_spec = pl.BlockSpec((1, n_pairs, tiles, 2 * ATT_TQ, LANES), lambda bi, i: (bi, 0, i, 0, 0))
    else:
        kb_shape, kb_spec = sds((b, l, D_ATT), BF16), row(D_ATT)
        kv_shape = sds((b, l, N_ATT_HEADS, ATT_HEAD_DIM), F32)
        kv_spec = pl.BlockSpec((1, tm, N_ATT_HEADS, ATT_HEAD_DIM), lambda bi, i: (bi, i, 0, 0))
        qv_shape, qv_spec = sds((b, l, D_ATT), BF16), row(D_ATT)
    out_shape = (qv_shape, kb_shape, qv_shape, kv_shape, kv_shape,
                 sds((b, l, D_SSM), F32), sds((b, l, CONV_DIM), F32), sds((b, l, LANES), F32))
    return pl.pallas_call(
        functools.partial(_in_proj_kernel, kv_transposed=kv_transposed),
        grid=(b, l // tm),
        in_specs=[row(D_MODEL), full(g), full(w_t), full(w_dt), full(dt_bias)],
        out_specs=(qv_spec, kb_spec, qv_spec, kv_spec, kv_spec,
                   row(D_SSM), row(CONV_DIM), row(LANES)),
        out_shape=out_shape,
        compiler_params=pltpu.CompilerParams(
            dimension_semantics=("arbitrary", "arbitrary"), vmem_limit_bytes=VMEM_LIMIT),
        name="in_proj",
    )(x, g, w_t, w_dt, dt_bias)


def _sb_weights(s, c, u2, mask):
    lk = -(jnp.maximum(s, 0.0) + jnp.log(1.0 + jnp.exp(-jnp.abs(s))))
    if mask is not None:
        lk = jnp.where(mask, lk, 0.0)
    hi, lo = _split2(lk)
    r = _dot(jnp.concatenate([hi, lo], axis=1), u2) + c
    w = jnp.exp(s + r)
    if mask is not None:
        w = jnp.where(mask, w, 0.0)
    return w, c + jnp.sum(lk, axis=1, keepdims=True)


def _head_lane_masks(n_lanes, dtype):
    lane = lax.broadcasted_iota(jnp.int32, (1, n_lanes), 1)
    return [jnp.where((lane >= h * ATT_HEAD_DIM) & (lane < (h + 1) * ATT_HEAD_DIM), 1.0, 0.0).astype(dtype)
            for h in range(n_lanes // ATT_HEAD_DIM)]


def _attn_prompt_kernel(qs_ref, kt_ref, vc_ref, u2x_ref, o_ref, s_buf, hl_buf, mask_buf, out_buf, *, tq, nq):
    tk = tq // 2
    big = 1e30
    causal = lax.broadcasted_iota(jnp.int32, (tq, tq), 1) < lax.broadcasted_iota(jnp.int32, (tq, tq), 0)
    mask_buf[0] = jnp.full((tq, tq), -jnp.inf, F32)
    mask_buf[1] = jnp.where(causal, -jnp.inf, big)

    def advance(i, j, skip_rest):
        last = (j == 0) | skip_rest
        return jnp.where(last, i + 1, i), jnp.where(last, i + 1, j - 1)

    def exhausted_after(c_prev, row_sums, first):
        carry = jnp.where(first, 0.0, c_prev[:, 0:1]) + row_sums
        return jnp.max(carry) < SB_EXHAUSTED

    def row_off(idx):
        return pl.multiple_of(jnp.minimum(idx, nq - 1) * tq, tq)

    def stage_a(i, j, slot):
        kb = kt_ref[0, :, pl.ds(row_off(j), tq)]
        t = _dot(qs_ref[0, 0, jnp.minimum(i, nq - 1)], kb)
        floor = mask_buf[(i == j).astype(jnp.int32)]
        row_sums = []
        for h in range(HEADS_PER_LANE_TILE):
            th = jnp.maximum(t[h * tq:(h + 1) * tq], floor)
            s_buf[slot, h * tq:(h + 1) * tq, :] = th
            e = jnp.exp2(jnp.abs(th) * (-LOG2E))
            lk = jnp.minimum(th, 0.0) - jnp.log(1.0 + e)
            row_sums.append(jnp.sum(lk, axis=1, keepdims=True))
            hi, lo = _split2(lk)
            for half in range(2):
                r0 = half * 2 * tq + h * tq
                hl_buf[slot, r0:r0 + tq, 0:tk] = hi[:, half * tk:(half + 1) * tk]
                hl_buf[slot, r0:r0 + tq, tk:tq] = lo[:, half * tk:(half + 1) * tk]
        return jnp.concatenate(row_sums, axis=0)

    def stage_b(i, j, slot, c, acc):
        rr = _dot(hl_buf[slot], u2x_ref[...])
        first = i == j
        c = jnp.where(first, 0.0, c)
        acc = jnp.where(first, 0.0, acc)
        ws = []
        for half in (1, 0):
            r0 = half * 2 * tq
            r = rr[r0:r0 + 2 * tq, 0:tk] + c
            th = s_buf[slot, :, half * tk:(half + 1) * tk]
            w = jnp.exp(r - th).astype(BF16)
            c = c + rr[r0:r0 + 2 * tq, tk:tq]
            ws += [w[0:tq], w[tq:2 * tq]]
        acc = acc + _dot(jnp.concatenate(ws, axis=1), vc_ref[0, 0, jnp.minimum(j, nq - 1)])
        out_buf[jnp.minimum(i, nq)] = acc
        return c, acc

    def two_items(carry):
        ib, jb, ia, ja, c, acc = carry
        rs = stage_a(ia, ja, 1)
        c, acc = stage_b(ib, jb, 0, c, acc)
        i2, j2 = advance(ia, ja, exhausted_after(c, rs, ia == ja))
        rs = stage_a(i2, j2, 0)
        c, acc = stage_b(ia, ja, 1, c, acc)
        i3, j3 = advance(i2, j2, exhausted_after(c, rs, i2 == j2))
        return i2, j2, i3, j3, c, acc

    def body(carry):
        return two_items(two_items(two_items(two_items(carry))))

    z = jnp.int32(0)
    stage_a(z, z, 0)
    init = (z, z, z + 1, z + 1, jnp.zeros((2 * tq, tk), F32), jnp.zeros((tq, LANES), F32))
    lax.while_loop(lambda carry: carry[0] < nq, body, init)
    for i in range(nq):
        o_ref[0, i * tq:(i + 1) * tq, :] = out_buf[i]


def _u2_matrix(tk):
    j = jnp.arange(2 * tk)[:, None] % tk
    s = jnp.arange(tk)[None, :]
    return (j >= s).astype(BF16)


def _attn_prompt(qs, kt, vc):
    b, n_pairs, nq, _, _ = qs.shape
    tq = ATT_TQ
    l = nq * tq
    assert tq == 2 * ATT_TK and kt.shape == (b, D_ATT, l)
    u2x = jnp.concatenate([_u2_matrix(ATT_TK), jnp.ones((tq, ATT_TK), BF16)], axis=1)
    kernel = functools.partial(_attn_prompt_kernel, tq=tq, nq=nq)
    tiles = pl.BlockSpec((1, 1, nq, 2 * tq, LANES), lambda bi, hp: (bi, hp, 0, 0, 0))
    return pl.pallas_call(
        kernel,
        grid=(b, n_pairs),
        in_specs=[tiles, pl.BlockSpec((1, LANES, l), lambda bi, hp: (bi, hp, 0)), tiles,
                  pl.BlockSpec(u2x.shape, lambda bi, hp: (0, 0))],
        out_specs=pl.BlockSpec((1, l, LANES), lambda bi, hp: (bi, 0, hp)),
        out_shape=jax.ShapeDtypeStruct((b, l, D_ATT), F32),
        scratch_shapes=[pltpu.VMEM((2, 2 * tq, tq), F32), pltpu.VMEM((2, 4 * tq, tq), BF16),
                        pltpu.VMEM((2, tq, tq), F32), pltpu.VMEM((nq + 1, tq, LANES), F32)],
        compiler_params=pltpu.CompilerParams(
            dimension_semantics=("arbitrary", "arbitrary"), vmem_limit_bytes=VMEM_LIMIT),
        name="attn_prompt",
    )(qs, kt, vc, u2x)


def _lanes_to_heads(x):
    t = x.shape[0]
    parts = []
    for g in range(D_ATT // LANES):
        a = x[:, g * LANES:(g + 1) * LANES]
        b = pltpu.roll(a, ATT_HEAD_DIM, axis=1)
        parts += [a.reshape(t // SUBLANES, SUBLANES, LANES), b.reshape(t // SUBLANES, SUBLANES, LANES)]
    y = jnp.swapaxes(jnp.stack(parts, axis=1), 1, 2).reshape(t, N_ATT_HEADS, LANES)
    return y[:, :, 0:ATT_HEAD_DIM]


def _attn_sample_kernel(q_ref, kn_ref, vn_ref, ck_hbm, cv_hbm, u2_ref, o_ref, kbuf, vbuf, sem,
                        *, l, tk, chunk, nch, layer):
    bi = pl.program_id(0)
    m = N_ATT_HEADS * l
    hm = _head_lane_masks(D_ATT, BF16)
    q = q_ref[0]
    qs = jnp.concatenate([q * mk for mk in hm], axis=0)
    u2 = u2_ref[...]

    def copies(ch, slot):
        keys = pl.ds(pl.multiple_of((nch - 1 - ch) * chunk, chunk), chunk)
        return (pltpu.make_async_copy(ck_hbm.at[layer, bi, :, :, keys], kbuf.at[slot], sem.at[0, slot]),
                pltpu.make_async_copy(cv_hbm.at[layer, bi, :, :, keys], vbuf.at[slot], sem.at[1, slot]))

    def start(ch, slot):
        for cp in copies(ch, slot):
            cp.start()

    def wait(ch, slot):
        for cp in copies(ch, slot):
            cp.wait()

    start(0, 0)
    pad = jnp.zeros((tk - l, D_ATT), BF16)
    kn = jnp.concatenate([kn_ref[0], pad], axis=0)
    vn = jnp.concatenate([vn_ref[0], pad], axis=0)
    row = jnp.concatenate([lax.broadcasted_iota(jnp.int32, (l, tk), 0)] * N_ATT_HEADS, axis=0)
    col = lax.broadcasted_iota(jnp.int32, (m, tk), 1)
    w, c = _sb_weights(_dot_nt(qs, kn), jnp.zeros((m, 1), F32), u2, col < row)
    acc = _dot(w.astype(BF16), vn)

    def live(c):
        return jnp.max(c) >= SB_EXHAUSTED

    def body(carry):
        ch, c, acc = carry
        slot = ch % 2
        wait(ch, slot)
        kt = kbuf[slot].reshape(D_ATT, chunk).astype(BF16)
        vt = vbuf[slot].reshape(D_ATT, chunk).astype(BF16)

        @pl.when(ch + 1 < nch)
        def _():
            start(ch + 1, 1 - slot)

        for t in reversed(range(chunk // tk)):
            w, c = _sb_weights(_dot(qs, kt[:, t * tk:(t + 1) * tk]), c, u2, None)
            acc = acc + _dot_nt(w.astype(BF16), vt[:, t * tk:(t + 1) * tk])
        return ch + 1, c, acc

    ch, c, acc = lax.while_loop(lambda carry: (carry[0] < nch) & live(carry[1]), body, (jnp.int32(0), c, acc))

    @pl.when(ch < nch)
    def _():
        wait(ch, ch % 2)

    hmf = _head_lane_masks(D_ATT, F32)
    out = acc[0:l] * hmf[0]
    for h in range(1, N_ATT_HEADS):
        out = out + acc[h * l:(h + 1) * l] * hmf[h]
    o_ref[0] = out


def _attn_sample(qb, kb_new, vb_new, cache_k, cache_v, layer):
    b, l, _ = qb.shape
    past = cache_k.shape[2]
    cache_k = jnp.transpose(cache_k, (0, 1, 3, 4, 2))
    cache_v = jnp.transpose(cache_v, (0, 1, 3, 4, 2))
    tk = ATT_TK
    chunk = min(ATT_CACHE_CHUNK, past)
    assert past % chunk == 0 and chunk % tk == 0 and l <= tk and l % 16 == 0
    u2 = _u2_matrix(tk)
    kernel = functools.partial(_attn_sample_kernel, l=l, tk=tk, chunk=chunk, nch=past // chunk, layer=layer)
    new = pl.BlockSpec((1, l, D_ATT), lambda bi: (bi, 0, 0))
    hbm = pl.BlockSpec(memory_space=pl.ANY)
    return pl.pallas_call(
        kernel,
        grid=(b,),
        in_specs=[new, new, new, hbm, hbm, pl.BlockSpec(u2.shape, lambda bi: (0, 0))],
        out_specs=new,
        out_shape=jax.ShapeDtypeStruct((b, l, D_ATT), F32),
        scratch_shapes=[pltpu.VMEM((2, N_ATT_HEADS, ATT_HEAD_DIM, chunk), F32),
                        pltpu.VMEM((2, N_ATT_HEADS, ATT_HEAD_DIM, chunk), F32),
                        pltpu.SemaphoreType.DMA((2, 2))],
        compiler_params=pltpu.CompilerParams(
            dimension_semantics=("arbitrary",), vmem_limit_bytes=VMEM_LIMIT),
        name="attn_sample",
    )(qb, kb_new, vb_new, cache_k, cache_v, u2)


def _ssd_kernel(xbc_ref, z_ref, dt_ref, conv0_ref, s0_ref, cw_ref, cb_ref, alog_ref, dskip_ref, g_ref,
                tri_ref, y_ref, sfin_ref, cbuf, state, *, q):
    c = pl.program_id(1)

    @pl.when(c == 0)
    def _():
        cbuf[...] = conv0_ref[0]
        zero = jnp.zeros((SSM_HEAD_DIM, SSM_STATE), F32)
        for pr in range(N_PAIRS):
            top = jnp.concatenate([s0_ref[0, 2 * pr], zero], axis=1)
            bot = jnp.concatenate([zero, s0_ref[0, 2 * pr + 1]], axis=1)
            state[pr] = jnp.concatenate([top, bot], axis=0)

    dt = dt_ref[0]
    da = dt * (-jnp.exp(alog_ref[...]))
    tri = tri_ref[...]
    hi, mid, lo = _split3(da)
    acum = _dot(tri, hi) + _dot(tri, mid) + _dot(tri, lo)
    acum_t = acum.T
    dt_t = dt.T
    a_end = acum[q - 1:q, :]
    trow = lax.broadcasted_iota(jnp.int32, (q, q), 0)
    tcol = lax.broadcasted_iota(jnp.int32, (q, q), 1)
    causal = tcol <= trow
    bd_r = lax.broadcasted_iota(jnp.int32, (LANES, LANES), 0) < SSM_STATE
    bd_c = lax.broadcasted_iota(jnp.int32, (LANES, LANES), 1) < SSM_STATE
    block_diag = bd_r == bd_c

    x = xbc_ref[0]
    prev = cbuf[...]
    row8 = lax.broadcasted_iota(jnp.int32, (SUBLANES, CONV_DIM), 0)
    xc = cb_ref[...] + x * cw_ref[SSM_CONV - 1:SSM_CONV, :]
    for k in range(1, SSM_CONV):
        r = pltpu.roll(x, k, axis=0)
        head = r[0:SUBLANES]
        for j in range(k):
            head = jnp.where(row8 == j, prev[SUBLANES - k + j:SUBLANES - k + j + 1, :], head)
        r = jnp.concatenate([head, r[SUBLANES:]], axis=0)
        xc = xc + r * cw_ref[SSM_CONV - 1 - k:SSM_CONV - k, :]
    xc = _silu(xc)
    cbuf[...] = x[q - SUBLANES:q]

    lane = lax.broadcasted_iota(jnp.int32, (1, LANES), 1)
    lo_half = lane < SSM_STATE
    bmat = xc[:, D_SSM:D_SSM + LANES]
    cmat = xc[:, D_SSM + LANES:D_SSM + 2 * LANES]
    b_sw = pltpu.roll(bmat, SSM_STATE, axis=1)
    c_sw = pltpu.roll(cmat, SSM_STATE, axis=1)
    bdup = [jnp.where(lo_half, bmat, b_sw), jnp.where(lo_half, b_sw, bmat)]
    cdup = [jnp.where(lo_half, cmat, c_sw), jnp.where(lo_half, c_sw, cmat)]
    gmask = [jnp.where(lo_half, 1.0, 0.0), jnp.where(lo_half, 0.0, 1.0)]
    bmat_b = bmat.astype(BF16)
    cb_g = [_dot_nt((cmat * gmask[g]).astype(BF16), bmat_b) for g in range(SSM_GROUPS)]

    hm_b = [jnp.where(lo_half, 1.0, 0.0).astype(BF16), jnp.where(lo_half, 0.0, 1.0).astype(BF16)]

    pairs = range(N_PAIRS)
    group = [pr // (N_PAIRS // SSM_GROUPS) for pr in pairs]
    x_pair = [xc[:, pr * LANES:(pr + 1) * LANES] for pr in pairs]
    x_b = [v.astype(BF16) for v in x_pair]
    s_pair = [state[pr] for pr in pairs]
    lhs_diag, lhs_off, upd = [], [], []
    for pr in pairs:
        h0 = pr * HEADS_PER_LANE_TILE
        ms = []
        for h in (h0, h0 + 1):
            seg = acum[:, h:h + 1] - acum_t[h:h + 1, :]
            lmat = jnp.where(causal, jnp.exp(seg), 0.0)
            ms.append((cb_g[group[pr]] * lmat * dt_t[h:h + 1, :]).astype(BF16))
        acol = jnp.where(lo_half, acum[:, h0:h0 + 1], acum[:, h0 + 1:h0 + 2])
        dcol = jnp.where(lo_half, dt[:, h0:h0 + 1], dt[:, h0 + 1:h0 + 2])
        aend = jnp.where(lo_half, a_end[:, h0:h0 + 1], a_end[:, h0 + 1:h0 + 2])
        lhs_diag.append(jnp.concatenate(ms, axis=1))
        lhs_off.append((cdup[group[pr]] * jnp.exp(acol)).astype(BF16))
        upd.append((aend, (bdup[group[pr]] * (jnp.exp(aend - acol) * dcol)).astype(BF16)))
    y_diag = [_dot(lhs_diag[pr], jnp.concatenate([x_b[pr] * hm_b[0], x_b[pr] * hm_b[1]], axis=0)) for pr in pairs]
    y_off = [_dot_nt(lhs_off[pr], s_pair[pr].astype(BF16)) for pr in pairs]
    ys = []
    ssq = jnp.zeros((q, 1), F32)
    for pr in pairs:
        y = y_diag[pr] + y_off[pr] + dskip_ref[:, pr * LANES:(pr + 1) * LANES] * x_pair[pr]
        yz = y * _silu(z_ref[0, :, pr * LANES:(pr + 1) * LANES])
        ssq = ssq + jnp.sum(yz * yz, axis=1, keepdims=True)
        ys.append(yz)

    inv = lax.rsqrt(ssq * (1.0 / D_SSM) + NORM_EPS)
    for pr in pairs:
        aend, bw = upd[pr]
        state[pr] = s_pair[pr] * jnp.exp(aend) + jnp.where(block_diag, _dot_tn(x_b[pr], bw), 0.0)
    for pr in range(N_PAIRS):
        y_ref[0, :, pr * LANES:(pr + 1) * LANES] = (
            ys[pr] * inv * g_ref[:, pr * LANES:(pr + 1) * LANES]).astype(BF16)

    @pl.when(c == pl.num_programs(1) - 1)
    def _():
        for pr in range(N_PAIRS):
            sfin_ref[0, 2 * pr] = state[pr, 0:SSM_HEAD_DIM, 0:SSM_STATE]
            sfin_ref[0, 2 * pr + 1] = state[pr, SSM_HEAD_DIM:LANES, SSM_STATE:LANES]


def _ssd(xbc, z, dt, conv0, s0, conv_w, conv_b, a_log_pad, dskip_lanes, g_ssm):
    b, l, _ = xbc.shape
    q = min(SSD_Q, l)
    assert l % q == 0 and q % SUBLANES == 0
    tri = (jnp.arange(q)[None, :] <= jnp.arange(q)[:, None]).astype(BF16)
    seq = lambda n: pl.BlockSpec((1, q, n), lambda bi, ci: (bi, ci, 0))
    per_b = lambda a: pl.BlockSpec((1,) + a.shape[1:], lambda bi, ci: (bi,) + (0,) * (a.ndim - 1))
    full = lambda a: pl.BlockSpec(a.shape, lambda bi, ci: (0,) * a.ndim)
    kernel = functools.partial(_ssd_kernel, q=q)
    return pl.pallas_call(
        kernel,
        grid=(b, l // q),
        in_specs=[seq(CONV_DIM), seq(D_SSM), seq(LANES), per_b(conv0), per_b(s0),
                  full(conv_w), full(conv_b), full(a_log_pad), full(dskip_lanes), full(g_ssm), full(tri)],
        out_specs=(seq(D_SSM), per_b(s0)),
        out_shape=(jax.ShapeDtypeStruct((b, l, D_SSM), BF16), jax.ShapeDtypeStruct(s0.shape, F32)),
        scratch_shapes=[pltpu.VMEM((SUBLANES, CONV_DIM), F32),
                        pltpu.VMEM((N_PAIRS, LANES, LANES), F32)],
        compiler_params=pltpu.CompilerParams(
            dimension_semantics=("arbitrary", "arbitrary"), vmem_limit_bytes=VMEM_LIMIT),
        name="ssd",
    )(xbc, z, dt, conv0, s0, conv_w, conv_b, a_log_pad, dskip_lanes, g_ssm, tri)


def _gelu_tanh(x):
    return 0.5 * x * (1.0 + jnp.tanh(math.sqrt(2.0 / math.pi) * (x + 0.044715 * (x * x * x))))


def _out_ffn_kernel(x_ref, attn_ref, ys_ref, fc0_ref, ga_ref, gpost_ref, gpre_ref, gfpost_ref,
                    woa_ref, wos_ref, wgu_ref, wd_ref, cw_ref,
                    y_ref, fcn_ref, h2_buf, acc, carry, g_buf, u_buf, *, n_seq, lt):
    t = pl.program_id(1)

    @pl.when(t == 0)
    def _():
        carry[...] = fc0_ref[0]

    tm = x_ref.shape[0]
    pieces = [slice(r0, r0 + FFN_ROWS) for r0 in range(0, tm, FFN_ROWS)]
    an = [_rmsnorm(attn_ref[rows, :], ga_ref[...]).astype(BF16) for rows in pieces]
    m = [_dot(an[k], woa_ref[...]) + _dot(ys_ref[rows, :], wos_ref[...]) for k, rows in enumerate(pieces)]
    x1 = [x_ref[rows, :] + _rmsnorm(m[k], gpost_ref[...]) for k, rows in enumerate(pieces)]
    for k, rows in enumerate(pieces):
        y_ref[rows, :] = x1[k]
        h2_buf[rows, :] = _rmsnorm(x1[k], gpre_ref[...]).astype(BF16)
    row = lax.broadcasted_iota(jnp.int32, (lt, FFN_F), 0)

    def cols(ci, base=0):
        start = base + ci * FFN_F
        return pl.ds(start if isinstance(ci, int) else pl.multiple_of(start, FFN_F), FFN_F)

    def stage1(ci, slot):
        h2 = h2_buf[...]
        g_buf[slot] = _dot(h2, wgu_ref[:, cols(ci)])
        u_buf[slot] = _dot(h2, wgu_ref[:, cols(ci, D_FF)])

    def stage2(ci, slot):
        gate = g_buf[slot]
        cw = cw_ref[:, cols(ci)]
        acts = []
        for s in range(n_seq):
            gs = gate[s * lt:(s + 1) * lt]
            prev = carry[ci, s * SUBLANES:(s + 1) * SUBLANES, :]
            p1 = prev[SUBLANES - 1:SUBLANES, :]
            p2 = prev[SUBLANES - 2:SUBLANES - 1, :]
            g1 = jnp.where(row == 0, p1, pltpu.roll(gs, 1, axis=0))
            g2 = jnp.where(row == 0, p2, jnp.where(row == 1, p1, pltpu.roll(gs, 2, axis=0)))
            gc = cw[3:4, :] + g2 * cw[0:1, :] + g1 * cw[1:2, :] + gs * cw[2:3, :]
            carry[ci, s * SUBLANES:(s + 1) * SUBLANES, :] = gs[lt - SUBLANES:lt]
            acts.append(_gelu_tanh(gc))
        act = acts[0] if n_seq == 1 else jnp.concatenate(acts, axis=0)
        acc[...] += _dot((act * u_buf[slot]).astype(BF16), wd_ref[cols(ci), :])

    acc[...] = jnp.zeros_like(acc)
    stage1(0, 0)

    def body(p, _):
        c = 2 * p + 1
        stage1(c, 1)
        stage2(c - 1, 0)
        stage1(c + 1, 0)
        stage2(c, 1)
        return 0

    assert FFN_NC % 2 == 1
    lax.fori_loop(0, FFN_NC // 2, body, 0)
    stage2(FFN_NC - 1, 0)
    for r0 in range(0, tm, FFN_ROWS):
        rows = slice(r0, r0 + FFN_ROWS)
        y_ref[rows, :] = y_ref[rows, :] + _rmsnorm(acc[rows, :], gfpost_ref[...])
    fcn_ref[0] = carry[...]


def _out_ffn(x2d, attn2d, ys2d, fc0, n_seq, lt, gains, weights):
    t = x2d.shape[0]
    tm = n_seq * lt
    n_groups = fc0.shape[0]
    tiles = t // (tm * n_groups)
    assert tiles * tm * n_groups == t
    row = lambda n: pl.BlockSpec((tm, n), lambda gi, ti: (gi * tiles + ti, 0))
    full = lambda a: pl.BlockSpec(a.shape, lambda gi, ti: (0,) * a.ndim, pipeline_mode=pl.Buffered(1))
    fc_spec = pl.BlockSpec((1,) + fc0.shape[1:], lambda gi, ti: (gi, 0, 0, 0))
    kernel = functools.partial(_out_ffn_kernel, n_seq=n_seq, lt=lt)
    return pl.pallas_call(
        kernel,
        grid=(n_groups, tiles),
        in_specs=[row(D_MODEL), row(D_ATT), row(D_SSM), fc_spec] + [full(a) for a in gains]
                 + [full(a) for a in weights],
        out_specs=(row(D_MODEL), fc_spec),
        out_shape=(jax.ShapeDtypeStruct((t, D_MODEL), F32), jax.ShapeDtypeStruct(fc0.shape, F32)),
        scratch_shapes=[pltpu.VMEM((tm, D_MODEL), BF16), pltpu.VMEM((tm, D_MODEL), F32),
                        pltpu.VMEM(fc0.shape[1:], F32), pltpu.VMEM((2, tm, FFN_F), F32),
                        pltpu.VMEM((2, tm, FFN_F), F32)],
        compiler_params=pltpu.CompilerParams(
            dimension_semantics=("arbitrary", "arbitrary"), vmem_limit_bytes=VMEM_LIMIT),
        name="out_ffn",
    )(x2d, attn2d, ys2d, fc0, *gains, *weights)


def _ffn_state_to_chunks(st, n_seq):
    b = st.shape[0]
    s = st.reshape(b // n_seq, n_seq, FFN_CONV - 1, FFN_NC, FFN_F)
    s = jnp.pad(s, ((0, 0), (0, 0), (SUBLANES - (FFN_CONV - 1), 0), (0, 0), (0, 0)))
    return jnp.transpose(s, (0, 3, 1, 2, 4)).reshape(b // n_seq, FFN_NC, n_seq * SUBLANES, FFN_F)


def _chunks_to_ffn_state(ch, n_seq):
    g = ch.shape[0]
    s = ch.reshape(g, FFN_NC, n_seq, SUBLANES, FFN_F)[:, :, :, SUBLANES - (FFN_CONV - 1):, :]
    return jnp.transpose(s, (0, 2, 3, 1, 4)).reshape(g * n_seq, FFN_CONV - 1, D_FF)


def _layer(x, caches, ssm_h0, ssm_conv_prev, ffn_conv_prev, p, is_prompt):
    b, l, _ = x.shape
    t = b * l
    x2d = x.reshape(t, D_MODEL)
    r3 = lambda a: a.reshape(b, l, a.shape[-1])
    proj_w = (p['g_mix_pre'], p['w_in_t'], p['w_dt_t'], p['dt_bias'])
    if is_prompt:
        qb, kb, vb, kt, vt, z, xbc, dt = _in_proj(x, *proj_w, kv_transposed=True)
        attn = _attn_prompt(qb, kb, vb)
        k, v = jnp.transpose(kt, (0, 3, 1, 2)), jnp.transpose(vt, (0, 3, 1, 2))
    else:
        qb, kb, vb, k, v, z, xbc, dt = _in_proj(x2d[None], *proj_w, kv_transposed=False)
        attn = _attn_sample(r3(qb), r3(kb), r3(vb), *caches)

    conv0 = jnp.pad(ssm_conv_prev, ((0, 0), (SUBLANES - (SSM_CONV - 1), 0), (0, 0)))
    ys, s_fin = _ssd(r3(xbc), r3(z), r3(dt), conv0, ssm_h0,
                     p['ssm_conv_w'], p['ssm_conv_b'], p['a_log'], p['d_skip'], p['g_ssm_out'])
    ssm_conv_new = r3(xbc)[:, l - (SSM_CONV - 1):, :]

    if is_prompt:
        n_seq, lt = 1, min(FFN_TM, l)
    else:
        n_seq, lt = b, l
    fc0 = _ffn_state_to_chunks(ffn_conv_prev, n_seq)
    gains = (p['g_attn_out'], p['g_mix_post'], p['g_ffn_pre'], p['g_ffn_post'])
    weights = (p['w_out_a'], p['w_out_s'], p['w_gu'], p['w_down'], p['ffn_cw'])
    y2d, fcn = _out_ffn(x2d, attn.reshape(t, D_ATT), ys.reshape(t, D_SSM), fc0, n_seq, lt, gains, weights)
    return (y2d.reshape(b, l, D_MODEL), k.reshape(b, l, N_ATT_HEADS, ATT_HEAD_DIM),
            v.reshape(b, l, N_ATT_HEADS, ATT_HEAD_DIM), s_fin, ssm_conv_new,
            _chunks_to_ffn_state(fcn, n_seq))


def _prep_params(i, g_mix_pre, g_mix_post, w_in, ssm_conv_w, ssm_conv_b, dt_bias, a_log, d_skip,
                 g_ssm_out, g_attn_out, w_out, g_ffn_pre, g_ffn_post, w_up, ffn_conv_w, ffn_conv_b, w_down):
    row = lambda a: a[i].reshape(1, -1).astype(F32)
    pad_lanes = lambda a: jnp.pad(a, ((0, 0), (0, LANES - a.shape[1])))
    wi_t = w_in[i].T.astype(BF16)
    ffn_cw = jnp.concatenate([ffn_conv_w[i], ffn_conv_b[i][None, :],
                              jnp.zeros((SUBLANES - FFN_CONV - 1, D_FF), F32)], axis=0)
    return {
        'g_mix_pre': row(g_mix_pre), 'g_mix_post': row(g_mix_post),
        'g_ffn_pre': row(g_ffn_pre), 'g_ffn_post': row(g_ffn_post),
        'g_attn_out': row(g_attn_out), 'g_ssm_out': row(g_ssm_out),
        'w_in_t': wi_t,
        'w_dt_t': jnp.pad(wi_t[D_MAIN_PROJ:], ((0, LANES - (wi_t.shape[0] - D_MAIN_PROJ)), (0, 0))),
        'dt_bias': pad_lanes(row(dt_bias)),
        'ssm_conv_w': ssm_conv_w[i].astype(F32), 'ssm_conv_b': row(ssm_conv_b),
        'a_log': pad_lanes(row(a_log)),
        'd_skip': jnp.repeat(d_skip[i].astype(F32), SSM_HEAD_DIM).reshape(1, D_SSM),
        'w_out_a': w_out[i][:D_ATT].astype(BF16), 'w_out_s': w_out[i][D_ATT:].astype(BF16),
        'w_gu': w_up[i].astype(BF16),
        'w_down': w_down[i].astype(BF16),
        'ffn_cw': ffn_cw,
    }


def kernel(x_prompt, x_sample, cache_k, cache_v, state_ssm, state_ssm_conv, state_ffn_conv, g_mix_pre, g_mix_post, w_in, ssm_conv_w, ssm_conv_b, dt_bias, a_log, d_skip, g_ssm_out, g_attn_out, w_out, g_ffn_pre, g_ffn_post, w_up, ffn_conv_w, ffn_conv_b, w_down):
    depth = w_in.shape[0]
    bp = x_prompt.shape[0]
    dtp = x_prompt.dtype
    zh = jnp.zeros((bp, N_SSM_HEADS, SSM_HEAD_DIM, SSM_STATE), dtp)
    zcs = jnp.zeros((bp, SSM_CONV - 1, CONV_DIM), dtp)
    zcf = jnp.zeros((bp, FFN_CONV - 1, D_FF), dtp)
    y_p, y_s = x_prompt, x_sample
    outs_p, outs_s = [], []
    for i in range(depth):
        p = _prep_params(i, g_mix_pre, g_mix_post, w_in, ssm_conv_w, ssm_conv_b, dt_bias, a_log, d_skip,
                         g_ssm_out, g_attn_out, w_out, g_ffn_pre, g_ffn_post, w_up, ffn_conv_w, ffn_conv_b,
                         w_down)
        rp = _layer(y_p, None, zh, zcs, zcf, p, True)
        rs = _layer(y_s, (cache_k, cache_v, i), state_ssm[i], state_ssm_conv[i], state_ffn_conv[i], p, False)
        y_p, y_s = rp[0], rs[0]
        outs_p.append(rp[1:])
        outs_s.append(rs[1:])
    stack = lambda outs, j: jnp.stack([o[j] for o in outs])
    return (y_p, y_s) + tuple(stack(outs_p, j) for j in range(5)) + tuple(stack(outs_s, j) for j in range(5))
```
